```python
import jax, jax.numpy as jnp
from jax import lax
import numpy as np


D_MODEL = 2048
BATCH = 8
SEQ = 2048
DEPTH = 2

HG_HEADS = 8
HG_DIM = 128
HG_WIDTH = HG_HEADS * HG_DIM
HG_CHUNK = 64
ATT_GROUPS = ((128, 1), (512, 4), (2048, 16))
ATT_HEADS_PER_GROUP = 4
ATT_HEAD_DIM = 128
ATT_WIDTH = 3 * ATT_HEADS_PER_GROUP * ATT_HEAD_DIM
ATT_OUT_WIDTH = ATT_HEADS_PER_GROUP * ATT_HEAD_DIM
ROPE_THETA = 10000.0
N_EXPERTS = 64
N_GROUPS = 8
TOPK_GROUPS = 4
TOP_K = 8
EXPERT_FF = 512
SHARED_FF = 512
ROUTED_SCALE = 2.5
DN_ALPHA = (2 * DEPTH) ** 0.25
DN_BETA = (8 * DEPTH) ** -0.25
LN_EPS = 1e-5
NORM_EPS = 1e-6
IN_SPLITS = (HG_WIDTH, HG_WIDTH, HG_WIDTH, HG_WIDTH, ATT_WIDTH, ATT_WIDTH, ATT_WIDTH, D_MODEL, D_MODEL)
IN_WIDTH = 4 * HG_WIDTH + 3 * ATT_WIDTH + 2 * D_MODEL

kernel_name = "hgrn2_dilated_attn_moe_hybrid"


def layer_norm(x, g, b):
    xf = x.astype(jnp.float32)
    mu = jnp.mean(xf, -1, keepdims=True)
    var = jnp.mean(jnp.square(xf - mu), -1, keepdims=True)
    return ((xf - mu) * lax.rsqrt(var + LN_EPS) * g + b).astype(x.dtype)


def hgrn2_lower_bounds(lb_logits):
    p = jax.nn.softmax(lb_logits.astype(jnp.float32), axis=0)
    c = jnp.cumsum(p, axis=0)
    return c - c[:1]


def rope(x, positions):
    dh = x.shape[-1]
    half = dh // 2
    inv = ROPE_THETA ** (-jnp.arange(half, dtype=jnp.float32) * 2.0 / dh)
    ang = positions.astype(jnp.float32)[..., None] * inv
    cos = jnp.cos(ang)[:, :, None, :]
    sin = jnp.sin(ang)[:, :, None, :]
    xf = x.astype(jnp.float32)
    x1, x2 = xf[..., :half], xf[..., half:]
    return jnp.concatenate([x1 * cos - x2 * sin, x2 * cos + x1 * sin], axis=-1)


def banded_attention(q, k, v, n_back):
    n, h, length, dh = q.shape
    blk = n_back
    nb = -(-length // blk)
    pad = nb * blk - length
    def blocks(t):
        t = jnp.pad(t.astype(jnp.float32), ((0, 0), (0, 0), (0, pad), (0, 0)))
        return t.reshape(n, h, nb, blk, dh)
    def with_prev(t):
        prev = jnp.concatenate([jnp.zeros_like(t[:, :, :1]), t[:, :, :-1]], axis=2)
        return jnp.concatenate([prev, t], axis=3)
    qb = blocks(q) * (dh ** -0.5)
    kk = with_prev(blocks(k))
    vv = with_prev(blocks(v))
    s = jnp.einsum('nhbqd,nhbkd->nhbqk', qb, kk)
    qi = jnp.arange(blk)[:, None]
    kj = jnp.arange(2 * blk)[None, :]
    dist = blk + qi - kj
    key_pos = (jnp.arange(nb)[:, None, None] - 1) * blk + kj[None]
    valid = (dist >= 0) & (dist <= n_back) & (key_pos >= 0)
    s = jnp.where(valid, s, -jnp.inf)
    m = jnp.max(s, -1, keepdims=True)
    p = jnp.exp(s - m)
    den = jnp.sum(p, -1, keepdims=True)
    o = jnp.einsum('nhbqk,nhbkd->nhbqd', p, vv) / den
    lse = (m + jnp.log(den))[..., 0]
    o = o.reshape(n, h, nb * blk, dh)[:, :, :length]
    lse = lse.reshape(n, h, nb * blk)[:, :, :length]
    return o, lse


def dilated_group(q, k, v, window, dil):
    b, s, h, dh = q.shape
    length = s // dil
    def gather(t):
        return t.reshape(b, length, dil, h, dh).transpose(0, 2, 3, 1, 4).reshape(b * dil, h, length, dh)
    o, lse = banded_attention(gather(q), gather(k), gather(v), window // dil)
    o = o.reshape(b, dil, h, length, dh).transpose(0, 3, 1, 2, 4).reshape(b, s, h, dh)
    lse = lse.reshape(b, dil, h, length).transpose(0, 3, 1, 2).reshape(b, s, h)
    return o, lse


def dilated_attention_branch(aq, ak, av, positions):
    b, s, _ = aq.shape
    g, hg, dh = len(ATT_GROUPS), ATT_HEADS_PER_GROUP, ATT_HEAD_DIM
    q = rope(aq.reshape(b, s, g * hg, dh), positions).reshape(b, s, g, hg, dh)
    k = rope(ak.reshape(b, s, g * hg, dh), positions).reshape(b, s, g, hg, dh)
    v = av.astype(jnp.float32).reshape(b, s, g, hg, dh)
    outs, lses = [], []
    for gi, (window, dil) in enumerate(ATT_GROUPS):
        o, lse = dilated_group(q[:, :, gi], k[:, :, gi], v[:, :, gi], window, dil)
        outs.append(o)
        lses.append(lse)
    w = jax.nn.softmax(jnp.stack(lses, 0), axis=0)
    o = jnp.sum(w[..., None] * jnp.stack(outs, 0), axis=0)
    return o.reshape(b, s, ATT_OUT_WIDTH).astype(aq.dtype)


def hgrn2_chunk_scan(q, log_f, k, v):
    b, h, s, dk = q.shape
    dv = v.shape[-1]
    c = HG_CHUNK
    n = s // c
    def chunks(t):
        return t.reshape(b, h, n, c, t.shape[-1]).transpose(2, 0, 1, 3, 4)
    causal = jnp.tril(jnp.ones((c, c), dtype=bool))
    def step(state, inp):
        qc, gc, kc, vc = inp
        cum = jnp.cumsum(gc, axis=-2)
        inter = jnp.einsum('bhtk,bhkv->bhtv', qc * jnp.exp(cum), state)
        diff = cum[..., :, None, :] - cum[..., None, :, :]
        decay = jnp.exp(jnp.where(causal[:, :, None], diff, -jnp.inf))
        att = jnp.einsum('bhtsk,bhsk->bhts', qc[..., :, None, :] * decay, kc)
        intra = jnp.einsum('bhts,bhsv->bhtv', att, vc)
        end = cum[..., -1:, :]
        new_state = jnp.exp(end[..., 0, :])[..., None] * state + jnp.einsum('bhsk,bhsv->bhkv', kc * jnp.exp(end - cum), vc)
        return new_state, inter + intra
    init = jnp.zeros((b, h, dk, dv), jnp.float32)
    _, out = lax.scan(step, init, (chunks(q), chunks(log_f), chunks(k), chunks(v)))
    return out.transpose(1, 2, 0, 3, 4).reshape(b, h, s, dv)


def hgrn2_branch(hq, hf, hi, hg, lb, norm_g):
    b, s, _ = hq.shape
    def heads(t):
        return t.astype(jnp.float32).reshape(b, s, HG_HEADS, HG_DIM).transpose(0, 2, 1, 3)
    lb_h = lb.reshape(HG_HEADS, 1, HG_DIM)
    z = heads(hf)
    q = jax.nn.silu(heads(hq))
    log_f = jnp.logaddexp(jnp.log(lb_h), jnp.log1p(-lb_h) + jax.nn.log_sigmoid(z))
    k = (1.0 - lb_h) * jax.nn.sigmoid(-z)
    v = heads(hi)
    o = hgrn2_chunk_scan(q, log_f, k, v)
    o = o * lax.rsqrt(jnp.mean(jnp.square(o), -1, keepdims=True) + NORM_EPS) * norm_g.reshape(HG_HEADS, 1, HG_DIM)
    o = o.transpose(0, 2, 1, 3).reshape(b, s, HG_WIDTH)
    return (o * jax.nn.sigmoid(hg.astype(jnp.float32))).astype(hq.dtype)


def token_mixer(x, positions, w_in, lb, hg_norm_g, w_hg_proj, w_att_proj, w_out):
    h = x @ w_in
    offsets = [int(o) for o in np.cumsum(IN_SPLITS)[:-1]]
    hq, hf, hi, hg, aq, ak, av, gate_h, gate_a = jnp.split(h, offsets, axis=-1)
    y_h = hgrn2_branch(hq, hf, hi, hg, lb, hg_norm_g) @ w_hg_proj
    y_a = dilated_attention_branch(aq, ak, av, positions) @ w_att_proj
    merged = jax.nn.sigmoid(gate_h) * y_h + jax.nn.sigmoid(gate_a) * y_a
    return merged @ w_out


def moe_ffn(x, w_router, router_bias, w_e_gate, w_e_up, w_e_down, w_s_gate, w_s_up, w_s_down):
    b, s, d = x.shape
    t = b * s
    xt = x.reshape(t, d)
    scores = jax.nn.sigmoid(jnp.matmul(xt.astype(jnp.float32), w_router.astype(jnp.float32)))
    sel = scores + router_bias.astype(jnp.float32)
    grp = sel.reshape(t, N_GROUPS, N_EXPERTS // N_GROUPS)
    grp_score = jnp.sum(lax.top_k(grp, 2)[0], axis=-1)
    _, gidx = lax.top_k(grp_score, TOPK_GROUPS)
    rows = jnp.arange(t)[:, None]
    gmask = jnp.zeros((t, N_GROUPS), dtype=bool).at[rows, gidx].set(True)
    sel = jnp.where(gmask[:, :, None], grp, -jnp.inf).reshape(t, N_EXPERTS)
    _, eidx = lax.top_k(sel, TOP_K)
    w = jnp.take_along_axis(scores, eidx, axis=-1)
    w = w / jnp.sum(w, -1, keepdims=True) * ROUTED_SCALE
    gates = jnp.zeros((t, N_EXPERTS), jnp.float32).at[rows, eidx].set(w)
    def expert(acc, params):
        wg, wu, wd, g = params
        hid = jax.nn.silu(xt @ wg) * (xt @ wu)
        return acc + g[:, None] * (hid @ wd).astype(jnp.float32), None
    routed, _ = lax.scan(expert, jnp.zeros((t, d), jnp.float32), (w_e_gate, w_e_up, w_e_down, gates.T))
    shared = (jax.nn.silu(xt @ w_s_gate) * (xt @ w_s_up)) @ w_s_down
    return (routed + shared.astype(jnp.float32)).reshape(b, s, d).astype(x.dtype)


def setup_inputs(seed: int = 0) -> dict:
    key = jax.random.key(seed)
    ks = jax.random.split(key, 20)
    f32 = jnp.float32
    def nrm(k, shape, scale):
        return jax.random.normal(k, shape, f32) * scale
    return {
        "x": nrm(ks[0], (BATCH, SEQ, D_MODEL), 1.0),
        "positions": jnp.broadcast_to(jnp.arange(SEQ, dtype=jnp.int32), (BATCH, SEQ)),
        "w_in": nrm(ks[1], (DEPTH, D_MODEL, IN_WIDTH), D_MODEL ** -0.5),
        "lb_logits": nrm(ks[2], (DEPTH, HG_WIDTH), 0.5),
        "hg_norm_g": 1.0 + nrm(ks[3], (DEPTH, HG_WIDTH), 0.02),
        "w_hg_proj": nrm(ks[4], (DEPTH, HG_WIDTH, D_MODEL), HG_WIDTH ** -0.5),
        "w_att_proj": nrm(ks[5], (DEPTH, ATT_OUT_WIDTH, D_MODEL), ATT_OUT_WIDTH ** -0.5),
        "w_out": nrm(ks[6], (DEPTH, D_MODEL, D_MODEL), DN_BETA * D_MODEL ** -0.5),
        "ln1_g": 1.0 + nrm(ks[7], (DEPTH, D_MODEL), 0.02),
        "ln1_b": nrm(ks[8], (DEPTH, D_MODEL), 0.02),
        "w_router": nrm(ks[9], (DEPTH, D_MODEL, N_EXPERTS), D_MODEL ** -0.5),
        "router_bias": nrm(ks[10], (DEPTH, N_EXPERTS), 0.01),
        "w_e_gate": nrm(ks[11], (DEPTH, N_EXPERTS, D_MODEL, EXPERT_FF), D_MODEL ** -0.5),
        "w_e_up": nrm(ks[12], (DEPTH, N_EXPERTS, D_MODEL, EXPERT_FF), D_MODEL ** -0.5),
        "w_e_down": nrm(ks[13], (DEPTH, N_EXPERTS, EXPERT_FF, D_MODEL), DN_BETA * EXPERT_FF ** -0.5),
        "w_s_gate": nrm(ks[14], (DEPTH, D_MODEL, SHARED_FF), D_MODEL ** -0.5),
        "w_s_up": nrm(ks[15], (DEPTH, D_MODEL, SHARED_FF), D_MODEL ** -0.5),
        "w_s_down": nrm(ks[16], (DEPTH, SHARED_FF, D_MODEL), DN_BETA * SHARED_FF ** -0.5),
        "ln2_g": 1.0 + nrm(ks[17], (DEPTH, D_MODEL), 0.02),
        "ln2_b": nrm(ks[18], (DEPTH, D_MODEL), 0.02),
    }


def reference(x, positions, w_in, lb_logits, hg_norm_g, w_hg_proj, w_att_proj, w_out, ln1_g, ln1_b,
              w_router, router_bias, w_e_gate, w_e_up, w_e_down, w_s_gate, w_s_up, w_s_down, ln2_g, ln2_b):
    lower_bounds = hgrn2_lower_bounds(lb_logits)
    for layer in range(DEPTH):
        mix = token_mixer(x, positions, w_in[layer], lower_bounds[layer], hg_norm_g[layer],
                          w_hg_proj[layer], w_att_proj[layer], w_out[layer])
        x = layer_norm(DN_ALPHA * x + mix, ln1_g[layer], ln1_b[layer])
        ffn = moe_ffn(x, w_router[layer], router_bias[layer], w_e_gate[layer], w_e_up[layer], w_e_down[layer],
                      w_s_gate[layer], w_s_up[layer], w_s_down[layer])
        x = layer_norm(DN_ALPHA * x + ffn, ln2_g[layer], ln2_b[layer])
    return x
```

```python
import functools

import numpy as np
import jax
import jax.numpy as jnp
from jax import lax
from jax.experimental import pallas as pl
from jax.experimental.pallas import tpu as pltpu

F32 = jnp.float32
BF16 = jnp.bfloat16
U32 = jnp.uint32
I32 = jnp.int32

LANES = 128
VMEM_LIMIT_BYTES = 56 * 1024 * 1024

HG_HEADS = 8
HG_DIM = 128
HG_CHUNK = 64
HG_SUB = 16
ATT_GROUPS = ((128, 1), (512, 4), (2048, 16))
ATT_HEADS = 4
ATT_DIM = 128
ATT_BACK = 128
ROPE_THETA = 10000.0
N_EXPERTS = 64
N_GROUPS = 8
TOPK_GROUPS = 4
TOP_K = 8
ROUTED_SCALE = 2.5
LN_EPS = 1e-5
NORM_EPS = 1e-6

NT_DIMS = (((1,), (1,)), ((), ()))
TN_DIMS = (((0,), (0,)), ((), ()))


def _params(*sem):
    return pltpu.CompilerParams(dimension_semantics=sem, vmem_limit_bytes=VMEM_LIMIT_BYTES)


def _sigmoid(x):
    return 1.0 / (1.0 + jnp.exp(-x))


def _matmul_kernel(x_ref, w_ref, o_ref):
    o_ref[...] = jnp.dot(x_ref[...], w_ref[...], preferred_element_type=F32).astype(o_ref.dtype)


def _matmul(x, w, tm, tn, out_dtype):
    m, k = x.shape
    n = w.shape[1]
    return pl.pallas_call(
        _matmul_kernel,
        out_shape=jax.ShapeDtypeStruct((m, n), out_dtype),
        grid=(m // tm, n // tn),
        in_specs=[pl.BlockSpec((tm, k), lambda i, j: (i, 0)),
                  pl.BlockSpec((k, tn), lambda i, j: (0, j))],
        out_specs=pl.BlockSpec((tm, tn), lambda i, j: (i, j)),
        compiler_params=_params("parallel", "arbitrary"),
        name="in_proj",
    )(x, w)


def _rope_table_kernel(pos_ref, inv_ref, sign_ref, cos_ref, sin_ref):
    ang = pos_ref[...].astype(F32) * inv_ref[...]
    cos_ref[...] = jnp.cos(ang)
    sin_ref[...] = jnp.sin(ang) * sign_ref[...]


def _rope_tables(positions):
    t = positions.size
    half = ATT_DIM // 2
    inv_half = ROPE_THETA ** (-np.arange(half, dtype=np.float32) * np.float32(2.0) / np.float32(ATT_DIM))
    inv = jnp.asarray(np.concatenate([inv_half, inv_half]).astype(np.float32).reshape(1, ATT_DIM))
    sign = jnp.asarray(np.concatenate([-np.ones(half), np.ones(half)]).astype(np.float32).reshape(1, ATT_DIM))
    tm = min(t, 2048)
    return pl.pallas_call(
        _rope_table_kernel,
        out_shape=(jax.ShapeDtypeStruct((t, ATT_DIM), F32), jax.ShapeDtypeStruct((t, ATT_DIM), F32)),
        grid=(t // tm,),
        in_specs=[pl.BlockSpec((tm, 1), lambda i: (i, 0)),
                  pl.BlockSpec((1, ATT_DIM), lambda i: (0, 0)),
                  pl.BlockSpec((1, ATT_DIM), lambda i: (0, 0))],
        out_specs=(pl.BlockSpec((tm, ATT_DIM), lambda i: (i, 0)),
                   pl.BlockSpec((tm, ATT_DIM), lambda i: (i, 0))),
        compiler_params=_params("parallel"),
        name="rope_tables",
    )(positions.reshape(t, 1), inv, sign)


def _cumsum_rows(x):
    n = x.shape[0]
    row = lax.broadcasted_iota(I32, x.shape, 0)
    s = 1
    while s < n:
        x = x + jnp.where(row >= s, pltpu.roll(x, s, 0), 0.0)
        s *= 2
    return x


def _rows_from(cum, offsets):
    parts = []
    for o in offsets:
        if o is None:
            parts.append(jnp.zeros((HG_SUB, cum.shape[1]), F32))
        else:
            parts.append(jnp.broadcast_to(cum[o:o + 1, :], (HG_SUB, cum.shape[1])))
    return jnp.concatenate(parts, axis=0)


def _hgrn_kernel(hq_ref, hf_ref, hi_ref, hg_ref, lb_ref, l1m_ref, ng_ref, o_ref, st_ref, *, n_chunks):
    c = HG_CHUNK
    nsub = c // HG_SUB

    @pl.when(pl.program_id(2) == 0)
    def _():
        st_ref[...] = jnp.zeros_like(st_ref)

    lb = lb_ref[...]
    one_m_lb = 1.0 - lb
    log1m_lb = l1m_ref[...]
    norm_g = ng_ref[...]
    row = lax.broadcasted_iota(I32, (c, HG_DIM), 0)
    blk = row // HG_SUB
    ti = lax.broadcasted_iota(I32, (c, c), 0)
    si = lax.broadcasted_iota(I32, (c, c), 1)
    diag_mask = ((ti // HG_SUB) == (si // HG_SUB)) & (si <= ti)

    def chunk(ci, carry):
        r0 = pl.multiple_of(ci * c, c)
        z = hf_ref[pl.ds(r0, c), :].astype(F32)
        qraw = hq_ref[pl.ds(r0, c), :].astype(F32)
        v = hi_ref[pl.ds(r0, c), :]
        g = hg_ref[pl.ds(r0, c), :].astype(F32)

        e = jnp.exp(-jnp.abs(z))
        r = 1.0 / (1.0 + e)
        sig = jnp.where(z >= 0, r, e * r)
        sig_neg = jnp.where(z >= 0, e * r, r)
        log_sig = jnp.minimum(z, 0.0) - jnp.log(1.0 + e)
        log_f = jnp.maximum(jnp.log(lb + one_m_lb * sig), log1m_lb + log_sig)
        k = one_m_lb * sig_neg
        q = qraw * _sigmoid(qraw)

        cum = _cumsum_rows(log_f)
        start = _rows_from(cum, [None] + [HG_SUB * i - 1 for i in range(1, nsub)])
        mid = _rows_from(cum, [HG_SUB * i + HG_SUB // 2 - 1 for i in range(nsub)])
        end = cum[c - 1:c, :]

        qd = (q * jnp.exp(cum - mid)).astype(BF16)
        kd = (k * jnp.exp(mid - cum)).astype(BF16)
        att = jnp.where(diag_mask, lax.dot_general(qd, kd, NT_DIMS, preferred_element_type=F32), 0.0)

        qs = q * jnp.exp(cum - start)
        q_slots, k_slots = [], []
        for i in range(1, nsub):
            q_slots.append(jnp.where(blk == i, qs, 0.0).astype(BF16))
            n_rows = HG_SUB * i
            b_i = cum[n_rows - 1:n_rows, :]
            k_i = k[:n_rows] * jnp.exp(b_i - cum[:n_rows])
            k_slots.append(jnp.concatenate([k_i, jnp.zeros((c - n_rows, HG_DIM), F32)], axis=0).astype(BF16))
        att = att + lax.dot_general(jnp.concatenate(q_slots, axis=1), jnp.concatenate(k_slots, axis=1),
                                    NT_DIMS, preferred_element_type=F32)
        intra = jnp.dot(att.astype(BF16), v, preferred_element_type=F32)

        st = st_ref[...]
        inter = lax.dot_general((q * jnp.exp(cum)).astype(BF16), st.astype(BF16), NT_DIMS,
                                preferred_element_type=F32)
        k_end = (k * jnp.exp(end - cum)).astype(BF16)
        st_ref[...] = st * jnp.exp(end) + lax.dot_general(v, k_end, TN_DIMS, preferred_element_type=F32)

        o = inter + intra
        o = o * lax.rsqrt(jnp.mean(o * o, axis=-1, keepdims=True) + NORM_EPS) * norm_g
        o_ref[pl.ds(r0, c), :] = (o * _sigmoid(g)).astype(o_ref.dtype)
        return carry

    lax.fori_loop(0, n_chunks, chunk, 0)


def _hgrn(h, lb, log1m_lb, norm_g, batch, seq):
    t = h.shape[0]
    tb = min(seq, 512)
    nsb = seq // tb
    width = HG_HEADS * HG_DIM

    def col(off):
        return pl.BlockSpec((tb, HG_DIM), lambda b, hh, s: (b * nsb + s, off * HG_HEADS + hh))

    vec = pl.BlockSpec((1, HG_DIM), lambda b, hh, s: (0, hh))
    return pl.pallas_call(
        functools.partial(_hgrn_kernel, n_chunks=tb // HG_CHUNK),
        out_shape=jax.ShapeDtypeStruct((t, width), BF16),
        grid=(batch, HG_HEADS, nsb),
        in_specs=[col(0), col(1), col(2), col(3), vec, vec, vec],
        out_specs=pl.BlockSpec((tb, HG_DIM), lambda b, hh, s: (b * nsb + s, hh)),
        scratch_shapes=[pltpu.VMEM((HG_DIM, HG_DIM), F32)],
        compiler_params=_params("parallel", "parallel", "arbitrary"),
        name="hgrn2",
    )(h, h, h, h, lb, log1m_lb, norm_g)


def _attn_kernel(*refs, seq):
    qkv_refs = refs[:9]
    cos_ref, sin_ref, o_ref = refs[9:12]
    qf, kf, vf, og, lg = refs[12:]
    n_groups = len(ATT_GROUPS)
    scale = ATT_DIM ** -0.5
    rb = min(seq, 256)

    def rope_rows(i, carry):
        r0 = pl.multiple_of(i * rb, rb)
        cs = cos_ref[pl.ds(r0, rb), :]
        sn = sin_ref[pl.ds(r0, rb), :]
        for gi in range(n_groups):
            xq = qkv_refs[3 * gi][pl.ds(r0, rb), :].astype(F32)
            xk = qkv_refs[3 * gi + 1][pl.ds(r0, rb), :].astype(F32)
            qf[gi, pl.ds(r0, rb), :] = (xq * cs + pltpu.roll(xq, ATT_DIM // 2, 1) * sn) * scale
            kf[gi, pl.ds(r0, rb), :] = xk * cs + pltpu.roll(xk, ATT_DIM // 2, 1) * sn
            vf[gi, pl.ds(r0, rb), :] = qkv_refs[3 * gi + 2][pl.ds(r0, rb), :].astype(F32)
        return carry

    lax.fori_loop(0, seq // rb, rope_rows, 0)

    qb = ATT_BACK
    for gi, (_, dil) in enumerate(ATT_GROUPS):
        length = seq // dil
        nk = min(2 * qb, length)
        n_qblk = length // qb

        def block(it, carry, gi=gi, dil=dil, nk=nk, n_qblk=n_qblk):
            res = it // n_qblk
            iq = it % n_qblk
            q0 = iq * qb
            k0 = jnp.maximum(q0 - qb, 0)
            if dil == 1:
                q_rows = pl.ds(pl.multiple_of(q0, qb), qb)
                k_rows = pl.ds(pl.multiple_of(k0, qb), nk)
            else:
                q_rows = pl.ds(res + dil * q0, qb, stride=dil)
                k_rows = pl.ds(res + dil * k0, nk, stride=dil)
            q = qf[gi, q_rows, :].astype(BF16)
            k = kf[gi, k_rows, :].astype(BF16)
            v = vf[gi, k_rows, :].astype(BF16)
            s = lax.dot_general(q, k, NT_DIMS, preferred_element_type=F32)
            dist = (q0 + lax.broadcasted_iota(I32, (qb, nk), 0)) - (k0 + lax.broadcasted_iota(I32, (qb, nk), 1))
            s = jnp.where((dist >= 0) & (dist <= ATT_BACK), s, -jnp.inf)
            m = jnp.max(s, axis=-1, keepdims=True)
            p = jnp.exp(s - m)
            den = jnp.sum(p, axis=-1, keepdims=True)
            o = jnp.dot(p.astype(BF16), v, preferred_element_type=F32) / den
            og[gi, q_rows, :] = o
            lg[gi, q_rows, :] = jnp.broadcast_to(m + jnp.log(den), (qb, ATT_DIM))
            return carry

        lax.fori_loop(0, dil * n_qblk, block, 0)

    def merge_rows(i, carry):
        r0 = pl.multiple_of(i * rb, rb)
        ls = [lg[gi, pl.ds(r0, rb), :] for gi in range(n_groups)]
        m = functools.reduce(jnp.maximum, ls)
        ws = [jnp.exp(l - m) for l in ls]
        num = sum(w * og[gi, pl.ds(r0, rb), :] for gi, w in enumerate(ws))
        o_ref[pl.ds(r0, rb), :] = (num / sum(ws)).astype(o_ref.dtype)
        return carry

    lax.fori_loop(0, seq // rb, merge_rows, 0)


def _attention(h, cos, sin, batch, seq, col0):
    t = h.shape[0]
    n_groups = len(ATT_GROUPS)
    part = n_groups * ATT_HEADS * ATT_DIM
    blk0 = col0 // ATT_DIM
    in_specs = []
    for gi in range(n_groups):
        for p in range(3):
            off = blk0 + (p * part) // ATT_DIM + gi * ATT_HEADS
            in_specs.append(pl.BlockSpec((seq, ATT_DIM), lambda b, hh, off=off: (b, off + hh)))
    tab = pl.BlockSpec((seq, ATT_DIM), lambda b, hh: (b, 0))
    in_specs += [tab, tab]
    scr = pltpu.VMEM((n_groups, seq, ATT_DIM), F32)
    return pl.pallas_call(
        functools.partial(_attn_kernel, seq=seq),
        out_shape=jax.ShapeDtypeStruct((t, ATT_HEADS * ATT_DIM), BF16),
        grid=(batch, ATT_HEADS),
        in_specs=in_specs,
        out_specs=pl.BlockSpec((seq, ATT_DIM), lambda b, hh: (b, hh)),
        scratch_shapes=[scr, scr, scr, scr, scr],
        compiler_params=_params("parallel", "parallel"),
        name="dilated_attn",
    )(*([h] * 9), cos, sin)


def _layer_norm_rows(r, g, b):
    mu = jnp.mean(r, axis=-1, keepdims=True)
    d = r - mu
    var = jnp.mean(d * d, axis=-1, keepdims=True)
    return d * lax.rsqrt(var + LN_EPS) * g + b


def _pack_halves(y):
    n = y.shape[1] // 2
    hi = pltpu.bitcast(y[:, :n].astype(BF16).astype(F32), U32)
    lo = pltpu.bitcast(y[:, n:].astype(BF16).astype(F32), U32)
    return hi | (lo >> 16)


def _unpack_halves(w):
    hi = pltpu.bitcast(w & jnp.uint32(0xFFFF0000), F32)
    lo = pltpu.bitcast(w << 16, F32)
    return hi, lo


def _mix_out_kernel(x_ref, oh_ref, oa_ref, gh_ref, ga_ref, whg_ref, wap_ref, wout_ref, g_ref, b_ref,
                    x1_ref, x1p_ref, acc_ref, *, alpha, n_col):
    n = pl.program_id(1)
    y_h = jnp.dot(oh_ref[...], whg_ref[...], preferred_element_type=F32)
    y_a = jnp.dot(oa_ref[...], wap_ref[...], preferred_element_type=F32)
    merged = _sigmoid(gh_ref[...].astype(F32)) * y_h + _sigmoid(ga_ref[...].astype(F32)) * y_a
    part = jnp.dot(merged.astype(BF16), wout_ref[...], preferred_element_type=F32)

    @pl.when(n == 0)
    def _():
        acc_ref[...] = part

    @pl.when(n > 0)
    def _():
        acc_ref[...] += part

    @pl.when(n == n_col - 1)
    def _():
        y = _layer_norm_rows(alpha * x_ref[...] + acc_ref[...], g_ref[...], b_ref[...])
        x1_ref[...] = y
        x1p_ref[...] = _pack_halves(y)


def _mix_out(x, h, o_h, o_a, w_hg, w_ap, w_out, ln_g, ln_b, gate_col0, alpha):
    t, d = x.shape
    tm = min(t, 512)
    tn = 512
    n_col = d // tn
    gh0 = gate_col0 // tn
    ga0 = (gate_col0 + d) // tn
    return pl.pallas_call(
        functools.partial(_mix_out_kernel, alpha=alpha, n_col=n_col),
        out_shape=(jax.ShapeDtypeStruct((t, d), F32), jax.ShapeDtypeStruct((t, d // 2), U32)),
        grid=(t // tm, n_col),
        in_specs=[pl.BlockSpec((tm, d), lambda i, n: (i, 0)),
                  pl.BlockSpec((tm, o_h.shape[1]), lambda i, n: (i, 0)),
                  pl.BlockSpec((tm, o_a.shape[1]), lambda i, n: (i, 0)),
                  pl.BlockSpec((tm, tn), lambda i, n: (i, gh0 + n)),
                  pl.BlockSpec((tm, tn), lambda i, n: (i, ga0 + n)),
                  pl.BlockSpec((w_hg.shape[0], tn), lambda i, n: (0, n)),
                  pl.BlockSpec((w_ap.shape[0], tn), lambda i, n: (0, n)),
                  pl.BlockSpec((tn, d), lambda i, n: (n, 0)),
                  pl.BlockSpec((1, d), lambda i, n: (0, 0)),
                  pl.BlockSpec((1, d), lambda i, n: (0, 0))],
        out_specs=(pl.BlockSpec((tm, d), lambda i, n: (i, 0)),
                   pl.BlockSpec((tm, d // 2), lambda i, n: (i, 0))),
        scratch_shapes=[pltpu.VMEM((tm, d), F32)],
        compiler_params=_params("parallel", "arbitrary"),
        name="mix_out",
    )(x, o_h, o_a, h, h, w_hg, w_ap, w_out, ln_g, ln_b)


def _split_bf16(x):
    hi = x.astype(BF16)
    lo = (x - hi.astype(F32)).astype(BF16)
    return hi, lo


def _router_kernel(x_ref, wr_ref, bias_ref, tri_ref, low_ref, rank_ref, eid_ref, wgt_ref, cnt_ref,
                   sel_scr, carry_ref, *, tr):
    ne = N_EXPERTS
    per = ne // N_GROUPS

    @pl.when(pl.program_id(0) == 0)
    def _():
        carry_ref[...] = jnp.zeros_like(carry_ref)

    xh, xl = _split_bf16(x_ref[...])
    wh, wl = _split_bf16(wr_ref[...])
    logits = (lax.dot_general(wh, xh, NT_DIMS, preferred_element_type=F32)
              + lax.dot_general(wh, xl, NT_DIMS, preferred_element_type=F32)
              + lax.dot_general(wl, xh, NT_DIMS, preferred_element_type=F32))
    scores = _sigmoid(logits)
    sel = scores + bias_ref[...]

    grp = sel.reshape(N_GROUPS, per, tr)
    sub = lax.broadcasted_iota(I32, grp.shape, 1)
    m1 = jnp.max(grp, axis=1, keepdims=True)
    first = jnp.min(jnp.where(grp == m1, sub, per), axis=1, keepdims=True)
    m2 = jnp.max(jnp.where(sub == first, -jnp.inf, grp), axis=1, keepdims=True)
    gs = m1 + m2
    gidx = lax.broadcasted_iota(I32, gs.shape, 0)
    grank = jnp.zeros(gs.shape, I32)
    for j in range(N_GROUPS):
        other = gs[j:j + 1]
        grank += ((other > gs) | ((other == gs) & (j < gidx))).astype(I32)
    masked = jnp.where(grank < TOPK_GROUPS, grp, -jnp.inf).reshape(ne, tr)

    sel_scr[...] = masked
    eidx = lax.broadcasted_iota(I32, (ne, tr), 0)

    def rank_step(j, erank):
        other = sel_scr[pl.ds(j, 1), :]
        return erank + ((other > masked) | ((other == masked) & (j < eidx))).astype(I32)

    erank = lax.fori_loop(0, ne, rank_step, jnp.zeros((ne, tr), I32))
    chosen = erank < TOP_K
    w = jnp.where(chosen, scores, 0.0)
    gates = w / jnp.sum(w, axis=0, keepdims=True) * ROUTED_SCALE

    chosen_b = jnp.where(chosen, 1.0, 0.0).astype(BF16)
    incl = jnp.dot(chosen_b, tri_ref[...], preferred_element_type=F32)
    carry = carry_ref[...]
    rank_in_expert = (carry + incl - 1.0).astype(I32)
    carry_new = carry + incl[:, tr - 1:tr]
    carry_ref[...] = carry_new
    cnt_ref[...] = jnp.broadcast_to(carry_new, cnt_ref.shape).astype(I32)

    slot = jnp.dot(low_ref[...], chosen_b, preferred_element_type=F32).astype(I32)
    for j in range(TOP_K):
        pick = chosen & (slot == j)
        rank_ref[pl.ds(j, 1), :] = jnp.sum(jnp.where(pick, rank_in_expert, 0), axis=0, keepdims=True)
        eid_ref[pl.ds(j, 1), :] = jnp.sum(jnp.where(pick, eidx, 0), axis=0, keepdims=True)
        wgt_ref[pl.ds(j, 1), :] = jnp.sum(jnp.where(pick, gates, 0.0), axis=0, keepdims=True)


def _router(x1, w_router_t, bias_col):
    t, d = x1.shape
    ne = N_EXPERTS
    tr = min(t, 512)
    tri = jnp.asarray(np.triu(np.ones((tr, tr), np.float32)), BF16)
    low = jnp.asarray(np.tril(np.ones((ne, ne), np.float32), -1), BF16)
    slot_shape = jax.ShapeDtypeStruct((TOP_K, t), I32)
    slot_spec = pl.BlockSpec((TOP_K, tr), lambda i: (0, i))
    return pl.pallas_call(
        functools.partial(_router_kernel, tr=tr),
        out_shape=(slot_shape, slot_shape, jax.ShapeDtypeStruct((TOP_K, t), F32),
                   jax.ShapeDtypeStruct((ne, LANES), I32)),
        grid=(t // tr,),
        in_specs=[pl.BlockSpec((tr, d), lambda i: (i, 0)),
                  pl.BlockSpec((ne, d), lambda i: (0, 0)),
                  pl.BlockSpec((ne, 1), lambda i: (0, 0)),
                  pl.BlockSpec((tr, tr), lambda i: (0, 0)),
                  pl.BlockSpec((ne, ne), lambda i: (0, 0))],
        out_specs=(slot_spec, slot_spec, slot_spec, pl.BlockSpec((ne, LANES), lambda i: (0, 0))),
        scratch_shapes=[pltpu.VMEM((ne, tr), F32), pltpu.VMEM((ne, 1), F32)],
        compiler_params=_params("arbitrary"),
        name="router",
    )(x1, w_router_t, bias_col, tri, low)


def _dispatch_kernel(pos_ref, x_ref, buf_in_ref, buf_ref, sem, *, td):
    del buf_in_ref

    def issue(tok, carry):
        for j in range(TOP_K):
            pltpu.make_async_copy(x_ref.at[pl.ds(tok, 1), :],
                                  buf_ref.at[pl.ds(pos_ref[j, tok], 1), :], sem).start()
        return carry

    lax.fori_loop(0, td, issue, 0)
    for j in range(TOP_K):
        pltpu.make_async_copy(x_ref, buf_ref.at[pl.ds(0, td), :], sem).wait()


def _dispatch(pos, x1p, n_rows):
    t, dp = x1p.shape
    td = min(t, 256)
    buf = jnp.zeros((n_rows, dp), U32)
    return pl.pallas_call(
        functools.partial(_dispatch_kernel, td=td),
        out_shape=jax.ShapeDtypeStruct((n_rows, dp), U32),
        grid=(t // td,),
        in_specs=[pl.BlockSpec((TOP_K, td), lambda i: (0, i), memory_space=pltpu.SMEM),
                  pl.BlockSpec((td, dp), lambda i: (i, 0)),
                  pl.BlockSpec(memory_space=pl.ANY)],
        out_specs=pl.BlockSpec(memory_space=pl.ANY),
        scratch_shapes=[pltpu.SemaphoreType.DMA(())],
        input_output_aliases={2: 0},
        compiler_params=_params("arbitrary"),
        name="dispatch",
    )(pos, x1p, buf)


def _experts_kernel(te_ref, nv_ref, xs_ref, wg_ref, wu_ref, wd_ref, ys_ref):
    @pl.when(pl.program_id(0) < nv_ref[0])
    def _():
        half = xs_ref.shape[1]
        a, b = _unpack_halves(xs_ref[...])
        a = a.astype(BF16)
        b = b.astype(BF16)

        def proj(w_ref):
            return (jnp.dot(a, w_ref[0, :half, :], preferred_element_type=F32)
                    + jnp.dot(b, w_ref[0, half:, :], preferred_element_type=F32))

        gate = proj(wg_ref)
        hid = (gate * _sigmoid(gate) * proj(wu_ref)).astype(BF16)
        ys_ref[...] = _pack_halves(jnp.dot(hid, wd_ref[0], preferred_element_type=F32))

    @pl.when(pl.program_id(0) >= nv_ref[0])
    def _():
        ys_ref[...] = jnp.zeros_like(ys_ref)


def _experts(tile_expert, n_valid, xs, w_gate, w_up, w_down, tm):
    n_rows, dp = xs.shape
    ne, d, ff = w_gate.shape
    n_tiles = n_rows // tm

    def row_map(i, te, nv):
        return (jnp.minimum(i, nv[0] - 1), 0)

    def w_map(i, te, nv):
        return (te[jnp.minimum(i, nv[0] - 1)], 0, 0)

    return pl.pallas_call(
        _experts_kernel,
        out_shape=jax.ShapeDtypeStruct((n_rows, dp), U32),
        grid_spec=pltpu.PrefetchScalarGridSpec(
            num_scalar_prefetch=2,
            grid=(n_tiles,),
            in_specs=[pl.BlockSpec((tm, dp), row_map),
                      pl.BlockSpec((1, d, ff), w_map),
                      pl.BlockSpec((1, d, ff), w_map),
                      pl.BlockSpec((1, ff, d), w_map)],
            out_specs=pl.BlockSpec((tm, dp), lambda i, te, nv: (i, 0))),
        compiler_params=_params("arbitrary"),
        name="experts",
    )(tile_expert, n_valid, xs, w_gate, w_up, w_down)


def _combine_kernel(pos_ref, x_ref, wk_ref, ys_ref, wsg_ref, wsu_ref, wsd_ref, g_ref, b_ref,
                    x2_ref, x2b_ref, ybuf, sem, *, tc, alpha):
    def issue(tok, carry):
        for j in range(TOP_K):
            pltpu.make_async_copy(ys_ref.at[pl.ds(pos_ref[j, tok], 1), :],
                                  ybuf.at[j, pl.ds(tok, 1), :], sem).start()
        return carry

    lax.fori_loop(0, tc, issue, 0)

    x = x_ref[...]
    xb = x.astype(BF16)
    gate = jnp.dot(xb, wsg_ref[...], preferred_element_type=F32)
    up = jnp.dot(xb, wsu_ref[...], preferred_element_type=F32)
    hid = (gate * _sigmoid(gate) * up).astype(BF16)
    shared = jnp.dot(hid, wsd_ref[...], preferred_element_type=F32)

    for j in range(TOP_K):
        pltpu.make_async_copy(ys_ref.at[pl.ds(0, tc), :], ybuf.at[j], sem).wait()

    half = ybuf.shape[2]
    wk = wk_ref[...]
    acc_hi = jnp.zeros((tc, half), F32)
    acc_lo = jnp.zeros((tc, half), F32)
    for j in range(TOP_K):
        hi, lo = _unpack_halves(ybuf[j])
        wj = wk[:, j:j + 1]
        acc_hi += wj * hi
        acc_lo += wj * lo
    ffn = jnp.concatenate([acc_hi, acc_lo], axis=1) + shared
    y = _layer_norm_rows(alpha * x + ffn, g_ref[...], b_ref[...])
    x2_ref[...] = y
    x2b_ref[...] = y.astype(BF16)


def _combine(pos, x1, wk, ys, ws_gate, ws_up, ws_down, ln_g, ln_b, alpha):
    t, d = x1.shape
    dp = ys.shape[1]
    ff = ws_gate.shape[1]
    tc = min(t, 256)
    return pl.pallas_call(
        functools.partial(_combine_kernel, tc=tc, alpha=alpha),
        out_shape=(jax.ShapeDtypeStruct((t, d), F32), jax.ShapeDtypeStruct((t, d), BF16)),
        grid=(t // tc,),
        in_specs=[pl.BlockSpec((TOP_K, tc), lambda i: (0, i), memory_space=pltpu.SMEM),
                  pl.BlockSpec((tc, d), lambda i: (i, 0)),
                  pl.BlockSpec((tc, TOP_K), lambda i: (i, 0)),
                  pl.BlockSpec(memory_space=pl.ANY),
                  pl.BlockSpec((d, ff), lambda i: (0, 0)),
                  pl.BlockSpec((d, ff), lambda i: (0, 0)),
                  pl.BlockSpec((ff, d), lambda i: (0, 0)),
                  pl.BlockSpec((1, d), lambda i: (0, 0)),
                  pl.BlockSpec((1, d), lambda i: (0, 0))],
        out_specs=(pl.BlockSpec((tc, d), lambda i: (i, 0)),
                   pl.BlockSpec((tc, d), lambda i: (i, 0))),
        scratch_shapes=[pltpu.VMEM((TOP_K, tc, dp), U32), pltpu.SemaphoreType.DMA(())],
        compiler_params=_params("arbitrary"),
        name="combine",
    )(pos, x1, wk, ys, ws_gate, ws_up, ws_down, ln_g, ln_b)


def _moe(x1, x1p, w_router_t, bias_col, w_gate, w_up, w_down, ws_gate, ws_up, ws_down, ln_g, ln_b, alpha):
    t = x1.shape[0]
    tm = 256
    rank, eid, wgt, cnt = _router(x1, w_router_t, bias_col)
    counts = cnt[:, 0]
    tiles_per = (counts + tm - 1) // tm
    tile_end = jnp.cumsum(tiles_per)
    tile_start = tile_end - tiles_per
    n_tiles = (t * TOP_K) // tm + N_EXPERTS
    tile_expert = jnp.minimum(jnp.searchsorted(tile_end, jnp.arange(n_tiles, dtype=I32), side="right"),
                              N_EXPERTS - 1).astype(I32)
    n_valid = tile_end[-1:].astype(I32)
    pos = (tile_start * tm)[eid] + rank
    xs = _dispatch(pos, x1p, n_tiles * tm)
    ys = _experts(tile_expert, n_valid, xs, w_gate, w_up, w_down, tm)
    return _combine(pos, x1, wgt.T, ys, ws_gate, ws_up, ws_down, ln_g, ln_b, alpha)


def kernel(x, positions, w_in, lb_logits, hg_norm_g, w_hg_proj, w_att_proj, w_out, ln1_g, ln1_b,
           w_router, router_bias, w_e_gate, w_e_up, w_e_down, w_s_gate, w_s_up, w_s_down, ln2_g, ln2_b):
    batch, seq, d = x.shape
    depth = w_in.shape[0]
    t = batch * seq
    alpha = float((2 * depth) ** 0.25)
    hg_width = HG_HEADS * HG_DIM
    att_col0 = 4 * hg_width
    gate_col0 = att_col0 + 3 * len(ATT_GROUPS) * ATT_HEADS * ATT_DIM

    p = jax.nn.softmax(lb_logits.astype(F32), axis=0)
    cs = jnp.cumsum(p, axis=0)
    lower = cs - cs[:1]
    log1m_lower = jnp.log1p(-lower)

    cos, sin = _rope_tables(positions)
    xf = x.reshape(t, d)
    xb = xf.astype(BF16)
    in_width = w_in.shape[2]
    tn = 1280 if in_width % 1280 == 0 else 512
    for layer in range(depth):
        h = _matmul(xb, w_in[layer].astype(BF16), min(t, 1024), tn, BF16)
        o_h = _hgrn(h, lower[layer].reshape(1, -1), log1m_lower[layer].reshape(1, -1),
                    hg_norm_g[layer].reshape(1, -1), batch, seq)
        o_a = _attention(h, cos, sin, batch, seq, att_col0)
        x1, x1p = _mix_out(xf, h, o_h, o_a, w_hg_proj[layer].astype(BF16), w_att_proj[layer].astype(BF16),
                           w_out[layer].astype(BF16), ln1_g[layer].reshape(1, d), ln1_b[layer].reshape(1, d),
                           gate_col0, alpha)
        xf, xb = _moe(x1, x1p, w_router[layer].T, router_bias[layer].reshape(-1, 1),
                      w_e_gate[layer].astype(BF16), w_e_up[layer].astype(BF16), w_e_down[layer].astype(BF16),
                      w_s_gate[layer].astype(BF16), w_s_up[layer].astype(BF16), w_s_down[layer].astype(BF16),
                      ln2_g[layer].reshape(1, d), ln2_b[layer].reshape(1, d), alpha)
    return xf.reshape(batch, seq, d)
```

```python
import functools

import numpy as np
import jax
import jax.numpy as jnp
from jax import lax
from jax.experimental import pallas as pl
from jax.experimental.pallas import tpu as pltpu

F32 = jnp.float32
BF16 = jnp.bfloat16
U32 = jnp.uint32
I32 = jnp.int32

LANES = 128
VMEM_LIMIT_BYTES = 56 * 1024 * 1024

HG_HEADS = 8
HG_DIM = 128
HG_CHUNK = 64
HG_SUB = 16
ATT_GROUPS = ((128, 1), (512, 4), (2048, 16))
ATT_HEADS = 4
ATT_DIM = 128
ATT_BACK = 128
ROPE_THETA = 10000.0
N_EXPERTS = 64
N_GROUPS = 8
TOPK_GROUPS = 4
TOP_K = 8
ROUTED_SCALE = 2.5
LN_EPS = 1e-5
NORM_EPS = 1e-6

NT_DIMS = (((1,), (1,)), ((), ()))
TN_DIMS = (((0,), (0,)), ((), ()))


def _params(*sem):
    return pltpu.CompilerParams(dimension_semantics=sem, vmem_limit_bytes=VMEM_LIMIT_BYTES)


def _sigmoid(x):
    return 1.0 / (1.0 + jnp.exp(-x))


def _matmul_kernel(x_ref, w_ref, o_ref, wb_ref):
    @pl.when(pl.program_id(1) == 0)
    def _():
        wb_ref[...] = w_ref[0].astype(BF16)

    o_ref[...] = jnp.dot(x_ref[...], wb_ref[...], preferred_element_type=F32).astype(o_ref.dtype)


def _matmul(x, w, layer, tm, tn, out_dtype):
    m, k = x.shape
    n = w.shape[2]
    return pl.pallas_call(
        _matmul_kernel,
        out_shape=jax.ShapeDtypeStruct((m, n), out_dtype),
        grid=(n // tn, m // tm),
        in_specs=[pl.BlockSpec((tm, k), lambda j, i: (i, 0)),
                  pl.BlockSpec((1, k, tn), lambda j, i: (layer, 0, j))],
        out_specs=pl.BlockSpec((tm, tn), lambda j, i: (i, j)),
        scratch_shapes=[pltpu.VMEM((k, tn), BF16)],
        compiler_params=_params("parallel", "arbitrary"),
        name="in_proj",
    )(x, w)


def _rope_table_kernel(pos_ref, inv_ref, sign_ref, cos_ref, sin_ref):
    ang = pos_ref[...].astype(F32) * inv_ref[...]
    cos_ref[...] = jnp.cos(ang)
    sin_ref[...] = jnp.sin(ang) * sign_ref[...]


def _rope_tables(positions):
    t = positions.size
    half = ATT_DIM // 2
    inv_half = ROPE_THETA ** (-np.arange(half, dtype=np.float32) * np.float32(2.0) / np.float32(ATT_DIM))
    inv = jnp.asarray(np.concatenate([inv_half, inv_half]).astype(np.float32).reshape(1, ATT_DIM))
    sign = jnp.asarray(np.concatenate([-np.ones(half), np.ones(half)]).astype(np.float32).reshape(1, ATT_DIM))
    tm = min(t, 2048)
    return pl.pallas_call(
        _rope_table_kernel,
        out_shape=(jax.ShapeDtypeStruct((t, ATT_DIM), F32), jax.ShapeDtypeStruct((t, ATT_DIM), F32)),
        grid=(t // tm,),
        in_specs=[pl.BlockSpec((tm, 1), lambda i: (i, 0)),
                  pl.BlockSpec((1, ATT_DIM), lambda i: (0, 0)),
                  pl.BlockSpec((1, ATT_DIM), lambda i: (0, 0))],
        out_specs=(pl.BlockSpec((tm, ATT_DIM), lambda i: (i, 0)),
                   pl.BlockSpec((tm, ATT_DIM), lambda i: (i, 0))),
        compiler_params=_params("parallel"),
        name="rope_tables",
    )(positions.reshape(t, 1), inv, sign)


def _cumsum_rows(x):
    n = x.shape[0]
    row = lax.broadcasted_iota(I32, x.shape, 0)
    s = 1
    while s < n:
        x = x + jnp.where(row >= s, pltpu.roll(x, s, 0), 0.0)
        s *= 2
    return x


def _rows_from(cum, offsets):
    parts = []
    for o in offsets:
        if o is None:
            parts.append(jnp.zeros((HG_SUB, cum.shape[1]), F32))
        else:
            parts.append(jnp.broadcast_to(cum[o:o + 1, :], (HG_SUB, cum.shape[1])))
    return jnp.concatenate(parts, axis=0)


def _hgrn_kernel(hq_ref, hf_ref, hi_ref, hg_ref, lb_ref, l1m_ref, ng_ref, o_ref, st_ref, *, n_chunks):
    c = HG_CHUNK
    nsub = c // HG_SUB

    @pl.when(pl.program_id(2) == 0)
    def _():
        st_ref[...] = jnp.zeros_like(st_ref)

    lb = lb_ref[...]
    one_m_lb = 1.0 - lb
    log1m_lb = l1m_ref[...]
    norm_g = ng_ref[...]
    row = lax.broadcasted_iota(I32, (c, HG_DIM), 0)
    blk = row // HG_SUB
    ti = lax.broadcasted_iota(I32, (c, c), 0)
    si = lax.broadcasted_iota(I32, (c, c), 1)
    diag_mask = ((ti // HG_SUB) == (si // HG_SUB)) & (si <= ti)

    def chunk(ci, carry):
        r0 = pl.multiple_of(ci * c, c)
        z = hf_ref[pl.ds(r0, c), :].astype(F32)
        qraw = hq_ref[pl.ds(r0, c), :].astype(F32)
        v = hi_ref[pl.ds(r0, c), :]
        g = hg_ref[pl.ds(r0, c), :].astype(F32)

        e = jnp.exp(-jnp.abs(z))
        r = 1.0 / (1.0 + e)
        sig = jnp.where(z >= 0, r, e * r)
        sig_neg = jnp.where(z >= 0, e * r, r)
        log_sig = jnp.minimum(z, 0.0) - jnp.log(1.0 + e)
        log_f = jnp.maximum(jnp.log(lb + one_m_lb * sig), log1m_lb + log_sig)
        k = one_m_lb * sig_neg
        q = qraw * _sigmoid(qraw)

        cum = _cumsum_rows(log_f)
        start = _rows_from(cum, [None] + [HG_SUB * i - 1 for i in range(1, nsub)])
        mid = _rows_from(cum, [HG_SUB * i + HG_SUB // 2 - 1 for i in range(nsub)])
        end = cum[c - 1:c, :]

        qd = (q * jnp.exp(cum - mid)).astype(BF16)
        kd = (k * jnp.exp(mid - cum)).astype(BF16)
        att = jnp.where(diag_mask, lax.dot_general(qd, kd, NT_DIMS, preferred_element_type=F32), 0.0)

        qs = q * jnp.exp(cum - start)
        q_slots, k_slots = [], []
        for i in range(1, nsub):
            q_slots.append(jnp.where(blk == i, qs, 0.0).astype(BF16))
            n_rows = HG_SUB * i
            b_i = cum[n_rows - 1:n_rows, :]
            k_i = k[:n_rows] * jnp.exp(b_i - cum[:n_rows])
            k_slots.append(jnp.concatenate([k_i, jnp.zeros((c - n_rows, HG_DIM), F32)], axis=0).astype(BF16))
        att = att + lax.dot_general(jnp.concatenate(q_slots, axis=1), jnp.concatenate(k_slots, axis=1),
                                    NT_DIMS, preferred_element_type=F32)
        intra = jnp.dot(att.astype(BF16), v, preferred_element_type=F32)

        st = st_ref[...]
        inter = lax.dot_general((q * jnp.exp(cum)).astype(BF16), st.astype(BF16), NT_DIMS,
                                preferred_element_type=F32)
        k_end = (k * jnp.exp(end - cum)).astype(BF16)
        st_ref[...] = st * jnp.exp(end) + lax.dot_general(v, k_end, TN_DIMS, preferred_element_type=F32)

        o = inter + intra
        o = o * lax.rsqrt(jnp.mean(o * o, axis=-1, keepdims=True) + NORM_EPS) * norm_g
        o_ref[pl.ds(r0, c), :] = (o * _sigmoid(g)).astype(o_ref.dtype)
        return carry

    lax.fori_loop(0, n_chunks, chunk, 0)


def _hgrn(h, lb, log1m_lb, norm_g, batch, seq):
    t = h.shape[0]
    tb = min(seq, 512)
    nsb = seq // tb
    width = HG_HEADS * HG_DIM

    def col(off):
        return pl.BlockSpec((tb, HG_DIM), lambda b, hh, s: (b * nsb + s, off * HG_HEADS + hh))

    vec = pl.BlockSpec((1, HG_DIM), lambda b, hh, s: (0, hh))
    return pl.pallas_call(
        functools.partial(_hgrn_kernel, n_chunks=tb // HG_CHUNK),
        out_shape=jax.ShapeDtypeStruct((t, width), BF16),
        grid=(batch, HG_HEADS, nsb),
        in_specs=[col(0), col(1), col(2), col(3), vec, vec, vec],
        out_specs=pl.BlockSpec((tb, HG_DIM), lambda b, hh, s: (b * nsb + s, hh)),
        scratch_shapes=[pltpu.VMEM((HG_DIM, HG_DIM), F32)],
        compiler_params=_params("parallel", "parallel", "arbitrary"),
        name="hgrn2",
    )(h, h, h, h, lb, log1m_lb, norm_g)


def _attn_kernel(*refs, seq):
    qkv_refs = refs[:9]
    cos_ref, sin_ref, o_ref = refs[9:12]
    qf, kf, vf, og, lg = refs[12:]
    n_groups = len(ATT_GROUPS)
    scale = ATT_DIM ** -0.5
    rb = min(seq, 256)

    def rope_rows(i, carry):
        r0 = pl.multiple_of(i * rb, rb)
        cs = cos_ref[pl.ds(r0, rb), :]
        sn = sin_ref[pl.ds(r0, rb), :]
        for gi in range(n_groups):
            xq = qkv_refs[3 * gi][pl.ds(r0, rb), :].astype(F32)
            xk = qkv_refs[3 * gi + 1][pl.ds(r0, rb), :].astype(F32)
            qf[gi, pl.ds(r0, rb), :] = (xq * cs + pltpu.roll(xq, ATT_DIM // 2, 1) * sn) * scale
            kf[gi, pl.ds(r0, rb), :] = xk * cs + pltpu.roll(xk, ATT_DIM // 2, 1) * sn
            vf[gi, pl.ds(r0, rb), :] = qkv_refs[3 * gi + 2][pl.ds(r0, rb), :].astype(F32)
        return carry

    lax.fori_loop(0, seq // rb, rope_rows, 0)

    qb = ATT_BACK
    for gi, (_, dil) in enumerate(ATT_GROUPS):
        length = seq // dil
        nk = min(2 * qb, length)
        n_qblk = length // qb

        def block(it, carry, gi=gi, dil=dil, nk=nk, n_qblk=n_qblk):
            res = it // n_qblk
            iq = it % n_qblk
            q0 = iq * qb
            k0 = jnp.maximum(q0 - qb, 0)
            if dil == 1:
                q_rows = pl.ds(pl.multiple_of(q0, qb), qb)
                k_rows = pl.ds(pl.multiple_of(k0, qb), nk)
            else:
                q_rows = pl.ds(res + dil * q0, qb, stride=dil)
                k_rows = pl.ds(res + dil * k0, nk, stride=dil)
            q = qf[gi, q_rows, :].astype(BF16)
            k = kf[gi, k_rows, :].astype(BF16)
            v = vf[gi, k_rows, :].astype(BF16)
            s = lax.dot_general(q, k, NT_DIMS, preferred_element_type=F32)
            dist = (q0 + lax.broadcasted_iota(I32, (qb, nk), 0)) - (k0 + lax.broadcasted_iota(I32, (qb, nk), 1))
            s = jnp.where((dist >= 0) & (dist <= ATT_BACK), s, -jnp.inf)
            m = jnp.max(s, axis=-1, keepdims=True)
            p = jnp.exp(s - m)
            den = jnp.sum(p, axis=-1, keepdims=True)
            o = jnp.dot(p.astype(BF16), v, preferred_element_type=F32) / den
            og[gi, q_rows, :] = o
            lg[gi, q_rows, :] = jnp.broadcast_to(m + jnp.log(den), (qb, ATT_DIM))
            return carry

        lax.fori_loop(0, dil * n_qblk, block, 0)

    def merge_rows(i, carry):
        r0 = pl.multiple_of(i * rb, rb)
        ls = [lg[gi, pl.ds(r0, rb), :] for gi in range(n_groups)]
        m = functools.reduce(jnp.maximum, ls)
        ws = [jnp.exp(l - m) for l in ls]
        num = sum(w * og[gi, pl.ds(r0, rb), :] for gi, w in enumerate(ws))
        o_ref[pl.ds(r0, rb), :] = (num / sum(ws)).astype(o_ref.dtype)
        return carry

    lax.fori_loop(0, seq // rb, merge_rows, 0)


def _attention(h, cos, sin, batch, seq, col0):
    t = h.shape[0]
    n_groups = len(ATT_GROUPS)
    part = n_groups * ATT_HEADS * ATT_DIM
    blk0 = col0 // ATT_DIM
    in_specs = []
    for gi in range(n_groups):
        for p in range(3):
            off = blk0 + (p * part) // ATT_DIM + gi * ATT_HEADS
            in_specs.append(pl.BlockSpec((seq, ATT_DIM), lambda b, hh, off=off: (b, off + hh)))
    tab = pl.BlockSpec((seq, ATT_DIM), lambda b, hh: (b, 0))
    in_specs += [tab, tab]
    scr = pltpu.VMEM((n_groups, seq, ATT_DIM), F32)
    return pl.pallas_call(
        functools.partial(_attn_kernel, seq=seq),
        out_shape=jax.ShapeDtypeStruct((t, ATT_HEADS * ATT_DIM), BF16),
        grid=(batch, ATT_HEADS),
        in_specs=in_specs,
        out_specs=pl.BlockSpec((seq, ATT_DIM), lambda b, hh: (b, hh)),
        scratch_shapes=[scr, scr, scr, scr, scr],
        compiler_params=_params("parallel", "parallel"),
        name="dilated_attn",
    )(*([h] * 9), cos, sin)


def _layer_norm_rows(r, g, b):
    mu = jnp.mean(r, axis=-1, keepdims=True)
    d = r - mu
    var = jnp.mean(d * d, axis=-1, keepdims=True)
    return d * lax.rsqrt(var + LN_EPS) * g + b


def _pack_halves(y):
    n = y.shape[1] // 2
    hi = pltpu.bitcast(y[:, :n].astype(BF16).astype(F32), U32)
    lo = pltpu.bitcast(y[:, n:].astype(BF16).astype(F32), U32)
    return hi | (lo >> 16)


def _unpack_halves(w):
    hi = pltpu.bitcast(w & jnp.uint32(0xFFFF0000), F32)
    lo = pltpu.bitcast(w << 16, F32)
    return hi, lo


def _mix_out_kernel(x_ref, oh_ref, oa_ref, gh_ref, ga_ref, whg_ref, wap_ref, wout_ref, g_ref, b_ref,
                    x1_ref, x1p_ref, acc_ref, *, alpha, n_col):
    n = pl.program_id(1)
    y_h = jnp.dot(oh_ref[...], whg_ref[...], preferred_element_type=F32)
    y_a = jnp.dot(oa_ref[...], wap_ref[...], preferred_element_type=F32)
    merged = _sigmoid(gh_ref[...].astype(F32)) * y_h + _sigmoid(ga_ref[...].astype(F32)) * y_a
    part = jnp.dot(merged.astype(BF16), wout_ref[...], preferred_element_type=F32)

    @pl.when(n == 0)
    def _():
        acc_ref[...] = part

    @pl.when(n > 0)
    def _():
        acc_ref[...] += part

    @pl.when(n == n_col - 1)
    def _():
        y = _layer_norm_rows(alpha * x_ref[...] + acc_ref[...], g_ref[...], b_ref[...])
        x1_ref[...] = y
        x1p_ref[...] = _pack_halves(y)


def _mix_out(x, h, o_h, o_a, w_hg, w_ap, w_out, ln_g, ln_b, gate_col0, alpha):
    t, d = x.shape
    tm = min(t, 512)
    tn = 512
    n_col = d // tn
    gh0 = gate_col0 // tn
    ga0 = (gate_col0 + d) // tn
    return pl.pallas_call(
        functools.partial(_mix_out_kernel, alpha=alpha, n_col=n_col),
        out_shape=(jax.ShapeDtypeStruct((t, d), F32), jax.ShapeDtypeStruct((t, d // 2), U32)),
        grid=(t // tm, n_col),
        in_specs=[pl.BlockSpec((tm, d), lambda i, n: (i, 0)),
                  pl.BlockSpec((tm, o_h.shape[1]), lambda i, n: (i, 0)),
                  pl.BlockSpec((tm, o_a.shape[1]), lambda i, n: (i, 0)),
                  pl.BlockSpec((tm, tn), lambda i, n: (i, gh0 + n)),
                  pl.BlockSpec((tm, tn), lambda i, n: (i, ga0 + n)),
                  pl.BlockSpec((w_hg.shape[0], tn), lambda i, n: (0, n)),
                  pl.BlockSpec((w_ap.shape[0], tn), lambda i, n: (0, n)),
                  pl.BlockSpec((tn, d), lambda i, n: (n, 0)),
                  pl.BlockSpec((1, d), lambda i, n: (0, 0)),
                  pl.BlockSpec((1, d), lambda i, n: (0, 0))],
        out_specs=(pl.BlockSpec((tm, d), lambda i, n: (i, 0)),
                   pl.BlockSpec((tm, d // 2), lambda i, n: (i, 0))),
        scratch_shapes=[pltpu.VMEM((tm, d), F32)],
        compiler_params=_params("parallel", "arbitrary"),
        name="mix_out",
    )(x, o_h, o_a, h, h, w_hg, w_ap, w_out, ln_g, ln_b)


def _split_bf16(x):
    hi = x.astype(BF16)
    lo = (x - hi.astype(F32)).astype(BF16)
    return hi, lo


def _router_kernel(x_ref, wr_ref, bias_ref, tri_ref, low_ref, rank_ref, eid_ref, wgt_ref, cnt_ref,
                   sel_scr, carry_ref, *, tr):
    ne = N_EXPERTS
    per = ne // N_GROUPS

    @pl.when(pl.program_id(0) == 0)
    def _():
        carry_ref[...] = jnp.zeros_like(carry_ref)

    xh, xl = _split_bf16(x_ref[...])
    wh, wl = _split_bf16(wr_ref[...])
    logits = (lax.dot_general(wh, xh, NT_DIMS, preferred_element_type=F32)
              + lax.dot_general(wh, xl, NT_DIMS, preferred_element_type=F32)
              + lax.dot_general(wl, xh, NT_DIMS, preferred_element_type=F32))
    scores = _sigmoid(logits)
    sel = scores + bias_ref[...]

    grp = sel.reshape(N_GROUPS, per, tr)
    sub = lax.broadcasted_iota(I32, grp.shape, 1)
    m1 = jnp.max(grp, axis=1, keepdims=True)
    first = jnp.min(jnp.where(grp == m1, sub, per), axis=1, keepdims=True)
    m2 = jnp.max(jnp.where(sub == first, -jnp.inf, grp), axis=1, keepdims=True)
    gs = m1 + m2
    gidx = lax.broadcasted_iota(I32, gs.shape, 0)
    grank = jnp.zeros(gs.shape, I32)
    for j in range(N_GROUPS):
        other = gs[j:j + 1]
        grank += ((other > gs) | ((other == gs) & (j < gidx))).astype(I32)
    masked = jnp.where(grank < TOPK_GROUPS, grp, -jnp.inf).reshape(ne, tr)

    sel_scr[...] = masked
    eidx = lax.broadcasted_iota(I32, (ne, tr), 0)

    def rank_step(j, erank):
        other = sel_scr[pl.ds(j, 1), :]
        return erank + ((other > masked) | ((other == masked) & (j < eidx))).astype(I32)

    erank = lax.fori_loop(0, ne, rank_step, jnp.zeros((ne, tr), I32))
    chosen = erank < TOP_K
    w = jnp.where(chosen, scores, 0.0)
    gates = w / jnp.sum(w, axis=0, keepdims=True) * ROUTED_SCALE

    chosen_b = jnp.where(chosen, 1.0, 0.0).astype(BF16)
    incl = jnp.dot(chosen_b, tri_ref[...], preferred_element_type=F32)
    carry = carry_ref[...]
    rank_in_expert = (carry + incl - 1.0).astype(I32)
    carry_new = carry + incl[:, tr - 1:tr]
    carry_ref[...] = carry_new
    cnt_ref[...] = jnp.broadcast_to(carry_new, cnt_ref.shape).astype(I32)

    slot = jnp.dot(low_ref[...], chosen_b, preferred_element_type=F32).astype(I32)
    for j in range(TOP_K):
        pick = chosen & (slot == j)
        rank_ref[pl.ds(j, 1), :] = jnp.sum(jnp.where(pick, rank_in_expert, 0), axis=0, keepdims=True)
        eid_ref[pl.ds(j, 1), :] = jnp.sum(jnp.where(pick, eidx, 0), axis=0, keepdims=True)
        wgt_ref[pl.ds(j, 1), :] = jnp.sum(jnp.where(pick, gates, 0.0), axis=0, keepdims=True)


def _router(x1, w_router_t, bias_col):
    t, d = x1.shape
    ne = N_EXPERTS
    tr = min(t, 512)
    tri = jnp.asarray(np.triu(np.ones((tr, tr), np.float32)), BF16)
    low = jnp.asarray(np.tril(np.ones((ne, ne), np.float32), -1), BF16)
    slot_shape = jax.ShapeDtypeStruct((TOP_K, t), I32)
    slot_spec = pl.BlockSpec((TOP_K, tr), lambda i: (0, i))
    return pl.pallas_call(
        functools.partial(_router_kernel, tr=tr),
        out_shape=(slot_shape, slot_shape, jax.ShapeDtypeStruct((TOP_K, t), F32),
                   jax.ShapeDtypeStruct((ne, LANES), I32)),
        grid=(t // tr,),
        in_specs=[pl.BlockSpec((tr, d), lambda i: (i, 0)),
                  pl.BlockSpec((ne, d), lambda i: (0, 0)),
                  pl.BlockSpec((ne, 1), lambda i: (0, 0)),
                  pl.BlockSpec((tr, tr), lambda i: (0, 0)),
                  pl.BlockSpec((ne, ne), lambda i: (0, 0))],
        out_specs=(slot_spec, slot_spec, slot_spec, pl.BlockSpec((ne, LANES), lambda i: (0, 0))),
        scratch_shapes=[pltpu.VMEM((ne, tr), F32), pltpu.VMEM((ne, 1), F32)],
        compiler_params=_params("arbitrary"),
        name="router",
    )(x1, w_router_t, bias_col, tri, low)


def _dispatch_kernel(start_ref, eid_ref, rank_ref, x_ref, buf_ref, sem, *, td):
    def issue(tok, carry):
        for j in range(TOP_K):
            row = start_ref[eid_ref[j, tok]] + rank_ref[j, tok]
            pltpu.make_async_copy(x_ref.at[pl.ds(tok, 1), :], buf_ref.at[pl.ds(row, 1), :], sem).start()
        return carry

    lax.fori_loop(0, td, issue, 0)
    for j in range(TOP_K):
        pltpu.make_async_copy(x_ref, buf_ref.at[pl.ds(0, td), :], sem).wait()


def _dispatch(expert_start, eid, rank, x1p):
    t, dp = x1p.shape
    td = min(t, 256)
    slot_spec = pl.BlockSpec((TOP_K, td), lambda i, st: (0, i), memory_space=pltpu.SMEM)
    return pl.pallas_call(
        functools.partial(_dispatch_kernel, td=td),
        out_shape=jax.ShapeDtypeStruct((t * TOP_K, dp), U32),
        grid_spec=pltpu.PrefetchScalarGridSpec(
            num_scalar_prefetch=1,
            grid=(t // td,),
            in_specs=[slot_spec, slot_spec, pl.BlockSpec((td, dp), lambda i, st: (i, 0))],
            out_specs=pl.BlockSpec(memory_space=pl.ANY),
            scratch_shapes=[pltpu.SemaphoreType.DMA(())]),
        compiler_params=_params("arbitrary"),
        name="dispatch",
    )(expert_start, eid, rank, x1p)


def _experts_kernel(tile_ref, exp_ref, lo_ref, hi_ref, nv_ref, xs_ref, wg_ref, wu_ref, wd_ref, ys_ref,
                    wgb, wub, wdb):
    it = pl.program_id(0)
    valid = it < nv_ref[0]

    @pl.when(valid & ((it == 0) | (exp_ref[it] != exp_ref[jnp.maximum(it - 1, 0)])))
    def _():
        wgb[...] = wg_ref[0, 0].astype(BF16)
        wub[...] = wu_ref[0, 0].astype(BF16)
        wdb[...] = wd_ref[0, 0].astype(BF16)

    @pl.when(valid)
    def _():
        tm, half = xs_ref.shape
        a, b = _unpack_halves(xs_ref[...])
        a = a.astype(BF16)
        b = b.astype(BF16)

        def proj(w):
            return (jnp.dot(a, w[:half, :], preferred_element_type=F32)
                    + jnp.dot(b, w[half:, :], preferred_element_type=F32))

        gate = proj(wgb)
        hid = (gate * _sigmoid(gate) * proj(wub)).astype(BF16)
        y = _pack_halves(jnp.dot(hid, wdb[...], preferred_element_type=F32))
        lo = lo_ref[it]

        @pl.when(lo == 0)
        def _():
            ys_ref[...] = y

        @pl.when(lo > 0)
        def _():
            row = lax.broadcasted_iota(I32, (tm, half), 0)
            ys_ref[...] = jnp.where((row >= lo) & (row < hi_ref[it]), y, ys_ref[...])


def _experts(items, xs, w_gate, w_up, w_down, layer, tm):
    n_rows, dp = xs.shape
    _, ne, d, ff = w_gate.shape
    n_items = items[0].shape[0]

    def row_map(i, tl, ex, lo, hi, nv):
        return (tl[jnp.minimum(i, nv[0] - 1)], 0)

    def w_map(i, tl, ex, lo, hi, nv):
        return (layer, ex[jnp.minimum(i, nv[0] - 1)], 0, 0)

    return pl.pallas_call(
        _experts_kernel,
        out_shape=jax.ShapeDtypeStruct((n_rows, dp), U32),
        grid_spec=pltpu.PrefetchScalarGridSpec(
            num_scalar_prefetch=5,
            grid=(n_items,),
            in_specs=[pl.BlockSpec((tm, dp), row_map),
                      pl.BlockSpec((1, 1, d, ff), w_map),
                      pl.BlockSpec((1, 1, d, ff), w_map),
                      pl.BlockSpec((1, 1, ff, d), w_map)],
            out_specs=pl.BlockSpec((tm, dp), row_map),
            scratch_shapes=[pltpu.VMEM((d, ff), BF16), pltpu.VMEM((d, ff), BF16), pltpu.VMEM((ff, d), BF16)]),
        compiler_params=_params("arbitrary"),
        name="experts",
    )(*items, xs, w_gate, w_up, w_down)


def _combine_kernel(start_ref, eid_ref, rank_ref, x_ref, wk_ref, ys_ref, wsg_ref, wsu_ref, wsd_ref, g_ref, b_ref,
                    x2_ref, x2b_ref, ybuf, sem, *, tc, alpha):
    def issue(tok, carry):
        for j in range(TOP_K):
            row = start_ref[eid_ref[j, tok]] + rank_ref[j, tok]
            pltpu.make_async_copy(ys_ref.at[pl.ds(row, 1), :], ybuf.at[j, pl.ds(tok, 1), :], sem).start()
        return carry

    lax.fori_loop(0, tc, issue, 0)

    x = x_ref[...]
    xb = x.astype(BF16)
    gate = jnp.dot(xb, wsg_ref[...], preferred_element_type=F32)
    up = jnp.dot(xb, wsu_ref[...], preferred_element_type=F32)
    hid = (gate * _sigmoid(gate) * up).astype(BF16)
    shared = jnp.dot(hid, wsd_ref[...], preferred_element_type=F32)

    for j in range(TOP_K):
        pltpu.make_async_copy(ys_ref.at[pl.ds(0, tc), :], ybuf.at[j], sem).wait()

    half = ybuf.shape[2]
    wk = wk_ref[...]
    acc_hi = jnp.zeros((tc, half), F32)
    acc_lo = jnp.zeros((tc, half), F32)
    for j in range(TOP_K):
        hi, lo = _unpack_halves(ybuf[j])
        wj = wk[:, j:j + 1]
        acc_hi += wj * hi
        acc_lo += wj * lo
    ffn = jnp.concatenate([acc_hi, acc_lo], axis=1) + shared
    y = _layer_norm_rows(alpha * x + ffn, g_ref[...], b_ref[...])
    x2_ref[...] = y
    x2b_ref[...] = y.astype(BF16)


def _combine(expert_start, eid, rank, x1, wk, ys, ws_gate, ws_up, ws_down, ln_g, ln_b, alpha):
    t, d = x1.shape
    dp = ys.shape[1]
    ff = ws_gate.shape[1]
    tc = min(t, 256)
    slot_spec = pl.BlockSpec((TOP_K, tc), lambda i, st: (0, i), memory_space=pltpu.SMEM)
    const = lambda i, st: (0, 0)
    rows = lambda i, st: (i, 0)
    return pl.pallas_call(
        functools.partial(_combine_kernel, tc=tc, alpha=alpha),
        out_shape=(jax.ShapeDtypeStruct((t, d), F32), jax.ShapeDtypeStruct((t, d), BF16)),
        grid_spec=pltpu.PrefetchScalarGridSpec(
            num_scalar_prefetch=1,
            grid=(t // tc,),
            in_specs=[slot_spec, slot_spec,
                      pl.BlockSpec((tc, d), rows),
                      pl.BlockSpec((tc, TOP_K), rows),
                      pl.BlockSpec(memory_space=pl.ANY),
                      pl.BlockSpec((d, ff), const),
                      pl.BlockSpec((d, ff), const),
                      pl.BlockSpec((ff, d), const),
                      pl.BlockSpec((1, d), const),
                      pl.BlockSpec((1, d), const)],
            out_specs=(pl.BlockSpec((tc, d), rows), pl.BlockSpec((tc, d), rows)),
            scratch_shapes=[pltpu.VMEM((TOP_K, tc, dp), U32), pltpu.SemaphoreType.DMA(())]),
        compiler_params=_params("arbitrary"),
        name="combine",
    )(expert_start, eid, rank, x1, wk, ys, ws_gate, ws_up, ws_down, ln_g, ln_b)


def _expert_work_items(counts, n_rows, tm):
    ne = counts.shape[0]
    end = jnp.cumsum(counts)
    start = end - counts
    first_tile = start // tm
    n_it = jnp.where(counts > 0, (end - 1) // tm - first_tile + 1, 0)
    it_end = jnp.cumsum(n_it)
    it_start = it_end - n_it
    n_items = n_rows // tm + ne
    k = jnp.arange(n_items, dtype=I32)
    expert = jnp.minimum(jnp.sum((it_end[None, :] <= k[:, None]).astype(I32), axis=1), ne - 1)
    onehot = expert[:, None] == jnp.arange(ne, dtype=I32)[None, :]
    pick = lambda v: jnp.sum(jnp.where(onehot, v[None, :], 0), axis=1)
    tile = jnp.clip(pick(first_tile) + k - pick(it_start), 0, n_rows // tm - 1)
    lo = jnp.maximum(pick(start) - tile * tm, 0)
    hi = jnp.minimum(pick(end) - tile * tm, tm)
    return start.astype(I32), (tile.astype(I32), expert.astype(I32), lo.astype(I32), hi.astype(I32),
                               it_end[-1:].astype(I32))


def _moe(x1, x1p, w_router_t, bias_col, w_gate, w_up, w_down, layer, ws_gate, ws_up, ws_down, ln_g, ln_b, alpha):
    t = x1.shape[0]
    tm = 256
    rank, eid, wgt, cnt = _router(x1, w_router_t, bias_col)
    expert_start, items = _expert_work_items(cnt[:, 0], t * TOP_K, tm)
    xs = _dispatch(expert_start, eid, rank, x1p)
    ys = _experts(items, xs, w_gate, w_up, w_down, layer, tm)
    return _combine(expert_start, eid, rank, x1, wgt.T, ys, ws_gate, ws_up, ws_down, ln_g, ln_b, alpha)


def kernel(x, positions, w_in, lb_logits, hg_norm_g, w_hg_proj, w_att_proj, w_out, ln1_g, ln1_b,
           w_router, router_bias, w_e_gate, w_e_up, w_e_down, w_s_gate, w_s_up, w_s_down, ln2_g, ln2_b):
    batch, seq, d = x.shape
    depth = w_in.shape[0]
    t = batch * seq
    alpha = float((2 * depth) ** 0.25)
    hg_width = HG_HEADS * HG_DIM
    att_col0 = 4 * hg_width
    gate_col0 = att_col0 + 3 * len(ATT_GROUPS) * ATT_HEADS * ATT_DIM

    p = jax.nn.softmax(lb_logits.astype(F32), axis=0)
    cs = jnp.cumsum(p, axis=0)
    lower = cs - cs[:1]
    log1m_lower = jnp.log1p(-lower)

    cos, sin = _rope_tables(positions)
    xf = x.reshape(t, d)
    xb = xf.astype(BF16)
    in_width = w_in.shape[2]
    tn = 1280 if in_width % 1280 == 0 else 512
    for layer in range(depth):
        h = _matmul(xb, w_in, layer, min(t, 1024), tn, BF16)
        o_h = _hgrn(h, lower[layer].reshape(1, -1), log1m_lower[layer].reshape(1, -1),
                    hg_norm_g[layer].reshape(1, -1), batch, seq)
        o_a = _attention(h, cos, sin, batch, seq, att_col0)
        x1, x1p = _mix_out(xf, h, o_h, o_a, w_hg_proj[layer].astype(BF16), w_att_proj[layer].astype(BF16),
                           w_out[layer].astype(BF16), ln1_g[layer].reshape(1, d), ln1_b[layer].reshape(1, d),
                           gate_col0, alpha)
        xf, xb = _moe(x1, x1p, w_router[layer].T, router_bias[layer].reshape(-1, 1),
                      w_e_gate, w_e_up, w_e_down, layer,
                      w_s_gate[layer].astype(BF16), w_s_up[layer].astype(BF16), w_s_down[layer].astype(BF16),
                      ln2_g[layer].reshape(1, d), ln2_b[layer].reshape(1, d), alpha)
    return xf.reshape(batch, seq, d)
```

```python
import functools

import numpy as np
import jax
import jax.numpy as jnp
from jax import lax
from jax.experimental import pallas as pl
from jax.experimental.pallas import tpu as pltpu

F32 = jnp.float32
BF16 = jnp.bfloat16
U32 = jnp.uint32
I32 = jnp.int32

LANES = 128
VMEM_LIMIT_BYTES = 56 * 1024 * 1024

HG_HEADS = 8
HG_DIM = 128
HG_CHUNK = 64
HG_SUB = 16
ATT_GROUPS = ((128, 1), (512, 4), (2048, 16))
ATT_HEADS = 4
ATT_DIM = 128
ATT_BACK = 128
ROPE_THETA = 10000.0
N_EXPERTS = 64
N_GROUPS = 8
TOPK_GROUPS = 4
TOP_K = 8
ROUTED_SCALE = 2.5
LN_EPS = 1e-5
NORM_EPS = 1e-6

NT_DIMS = (((1,), (1,)), ((), ()))
TN_DIMS = (((0,), (0,)), ((), ()))


def _params(*sem):
    return pltpu.CompilerParams(dimension_semantics=sem, vmem_limit_bytes=VMEM_LIMIT_BYTES)


def _sigmoid(x):
    return 1.0 / (1.0 + jnp.exp(-x))


def _matmul_kernel(x_ref, w_ref, o_ref, wb_ref):
    @pl.when(pl.program_id(1) == 0)
    def _():
        wb_ref[...] = w_ref[0].astype(BF16)

    o_ref[...] = jnp.dot(x_ref[...], wb_ref[...], preferred_element_type=F32).astype(o_ref.dtype)


def _matmul(x, w, layer, tm, tn, out_dtype):
    m, k = x.shape
    n = w.shape[2]
    return pl.pallas_call(
        _matmul_kernel,
        out_shape=jax.ShapeDtypeStruct((m, n), out_dtype),
        grid=(n // tn, m // tm),
        in_specs=[pl.BlockSpec((tm, k), lambda j, i: (i, 0)),
                  pl.BlockSpec((1, k, tn), lambda j, i: (layer, 0, j))],
        out_specs=pl.BlockSpec((tm, tn), lambda j, i: (i, j)),
        scratch_shapes=[pltpu.VMEM((k, tn), BF16)],
        compiler_params=_params("parallel", "arbitrary"),
        name="in_proj",
    )(x, w)


def _rope_table_kernel(pos_ref, inv_ref, sign_ref, cos_ref, sin_ref):
    ang = pos_ref[...].astype(F32) * inv_ref[...]
    cos_ref[...] = jnp.cos(ang)
    sin_ref[...] = jnp.sin(ang) * sign_ref[...]


def _rope_tables(positions):
    t = positions.size
    half = ATT_DIM // 2
    inv_half = ROPE_THETA ** (-np.arange(half, dtype=np.float32) * np.float32(2.0) / np.float32(ATT_DIM))
    inv = jnp.asarray(np.concatenate([inv_half, inv_half]).astype(np.float32).reshape(1, ATT_DIM))
    sign = jnp.asarray(np.concatenate([-np.ones(half), np.ones(half)]).astype(np.float32).reshape(1, ATT_DIM))
    tm = min(t, 2048)
    return pl.pallas_call(
        _rope_table_kernel,
        out_shape=(jax.ShapeDtypeStruct((t, ATT_DIM), F32), jax.ShapeDtypeStruct((t, ATT_DIM), F32)),
        grid=(t // tm,),
        in_specs=[pl.BlockSpec((tm, 1), lambda i: (i, 0)),
                  pl.BlockSpec((1, ATT_DIM), lambda i: (0, 0)),
                  pl.BlockSpec((1, ATT_DIM), lambda i: (0, 0))],
        out_specs=(pl.BlockSpec((tm, ATT_DIM), lambda i: (i, 0)),
                   pl.BlockSpec((tm, ATT_DIM), lambda i: (i, 0))),
        compiler_params=_params("parallel"),
        name="rope_tables",
    )(positions.reshape(t, 1), inv, sign)


def _cumsum_rows(x):
    n = x.shape[0]
    row = lax.broadcasted_iota(I32, x.shape, 0)
    s = 1
    while s < n:
        x = x + jnp.where(row >= s, pltpu.roll(x, s, 0), 0.0)
        s *= 2
    return x


def _rows_from(cum, offsets):
    parts = []
    for o in offsets:
        if o is None:
            parts.append(jnp.zeros((HG_SUB, cum.shape[1]), F32))
        else:
            parts.append(jnp.broadcast_to(cum[o:o + 1, :], (HG_SUB, cum.shape[1])))
    return jnp.concatenate(parts, axis=0)


def _hgrn_kernel(hq_ref, hf_ref, hi_ref, hg_ref, lb_ref, l1m_ref, ng_ref, o_ref, st_ref, *, n_chunks):
    c = HG_CHUNK
    nsub = c // HG_SUB

    @pl.when(pl.program_id(2) == 0)
    def _():
        st_ref[...] = jnp.zeros_like(st_ref)

    lb = lb_ref[...]
    one_m_lb = 1.0 - lb
    log1m_lb = l1m_ref[...]
    norm_g = ng_ref[...]
    row = lax.broadcasted_iota(I32, (c, HG_DIM), 0)
    blk = row // HG_SUB
    ti = lax.broadcasted_iota(I32, (c, c), 0)
    si = lax.broadcasted_iota(I32, (c, c), 1)
    diag_mask = ((ti // HG_SUB) == (si // HG_SUB)) & (si <= ti)

    def chunk(ci, carry):
        r0 = pl.multiple_of(ci * c, c)
        z = hf_ref[pl.ds(r0, c), :].astype(F32)
        qraw = hq_ref[pl.ds(r0, c), :].astype(F32)
        v = hi_ref[pl.ds(r0, c), :]
        g = hg_ref[pl.ds(r0, c), :].astype(F32)

        e = jnp.exp(-jnp.abs(z))
        r = 1.0 / (1.0 + e)
        sig = jnp.where(z >= 0, r, e * r)
        sig_neg = jnp.where(z >= 0, e * r, r)
        log_sig = jnp.minimum(z, 0.0) - jnp.log(1.0 + e)
        log_f = jnp.maximum(jnp.log(lb + one_m_lb * sig), log1m_lb + log_sig)
        k = one_m_lb * sig_neg
        q = qraw * _sigmoid(qraw)

        cum = _cumsum_rows(log_f)
        start = _rows_from(cum, [None] + [HG_SUB * i - 1 for i in range(1, nsub)])
        mid = _rows_from(cum, [HG_SUB * i + HG_SUB // 2 - 1 for i in range(nsub)])
        end = cum[c - 1:c, :]

        qd = (q * jnp.exp(cum - mid)).astype(BF16)
        kd = (k * jnp.exp(mid - cum)).astype(BF16)
        att = jnp.where(diag_mask, lax.dot_general(qd, kd, NT_DIMS, preferred_element_type=F32), 0.0)

        qs = q * jnp.exp(cum - start)
        q_slots, k_slots = [], []
        for i in range(1, nsub):
            q_slots.append(jnp.where(blk == i, qs, 0.0).astype(BF16))
            n_rows = HG_SUB * i
            b_i = cum[n_rows - 1:n_rows, :]
            k_i = k[:n_rows] * jnp.exp(b_i - cum[:n_rows])
            k_slots.append(jnp.concatenate([k_i, jnp.zeros((c - n_rows, HG_DIM), F32)], axis=0).astype(BF16))
        att = att + lax.dot_general(jnp.concatenate(q_slots, axis=1), jnp.concatenate(k_slots, axis=1),
                                    NT_DIMS, preferred_element_type=F32)
        intra = jnp.dot(att.astype(BF16), v, preferred_element_type=F32)

        st = st_ref[...]
        inter = lax.dot_general((q * jnp.exp(cum)).astype(BF16), st.astype(BF16), NT_DIMS,
                                preferred_element_type=F32)
        k_end = (k * jnp.exp(end - cum)).astype(BF16)
        st_ref[...] = st * jnp.exp(end) + lax.dot_general(v, k_end, TN_DIMS, preferred_element_type=F32)

        o = inter + intra
        o = o * lax.rsqrt(jnp.mean(o * o, axis=-1, keepdims=True) + NORM_EPS) * norm_g
        o_ref[pl.ds(r0, c), :] = (o * _sigmoid(g)).astype(o_ref.dtype)
        return carry

    lax.fori_loop(0, n_chunks, chunk, 0, unroll=8)


def _hgrn(h, lb, log1m_lb, norm_g, batch, seq):
    t = h.shape[0]
    tb = min(seq, 512)
    nsb = seq // tb
    width = HG_HEADS * HG_DIM

    def col(off):
        return pl.BlockSpec((tb, HG_DIM), lambda b, hh, s: (b * nsb + s, off * HG_HEADS + hh))

    vec = pl.BlockSpec((1, HG_DIM), lambda b, hh, s: (0, hh))
    return pl.pallas_call(
        functools.partial(_hgrn_kernel, n_chunks=tb // HG_CHUNK),
        out_shape=jax.ShapeDtypeStruct((t, width), BF16),
        grid=(batch, HG_HEADS, nsb),
        in_specs=[col(0), col(1), col(2), col(3), vec, vec, vec],
        out_specs=pl.BlockSpec((tb, HG_DIM), lambda b, hh, s: (b * nsb + s, hh)),
        scratch_shapes=[pltpu.VMEM((HG_DIM, HG_DIM), F32)],
        compiler_params=_params("parallel", "parallel", "arbitrary"),
        name="hgrn2",
    )(h, h, h, h, lb, log1m_lb, norm_g)


def _attn_kernel(*refs, seq):
    qkv_refs = refs[:9]
    cos_ref, sin_ref, o_ref = refs[9:12]
    qf, kf, vf, og, lg = refs[12:]
    n_groups = len(ATT_GROUPS)
    scale = ATT_DIM ** -0.5
    rb = min(seq, 256)

    def rope_rows(i, carry):
        r0 = pl.multiple_of(i * rb, rb)
        cs = cos_ref[pl.ds(r0, rb), :]
        sn = sin_ref[pl.ds(r0, rb), :]
        for gi in range(n_groups):
            xq = qkv_refs[3 * gi][pl.ds(r0, rb), :].astype(F32)
            xk = qkv_refs[3 * gi + 1][pl.ds(r0, rb), :].astype(F32)
            qf[gi, pl.ds(r0, rb), :] = (xq * cs + pltpu.roll(xq, ATT_DIM // 2, 1) * sn) * scale
            kf[gi, pl.ds(r0, rb), :] = xk * cs + pltpu.roll(xk, ATT_DIM // 2, 1) * sn
            vf[gi, pl.ds(r0, rb), :] = qkv_refs[3 * gi + 2][pl.ds(r0, rb), :].astype(F32)
        return carry

    lax.fori_loop(0, seq // rb, rope_rows, 0)

    qb = ATT_BACK
    for gi, (_, dil) in enumerate(ATT_GROUPS):
        length = seq // dil
        nk = min(2 * qb, length)
        n_qblk = length // qb

        def block(it, carry, gi=gi, dil=dil, nk=nk, n_qblk=n_qblk):
            res = it // n_qblk
            iq = it % n_qblk
            q0 = iq * qb
            k0 = jnp.maximum(q0 - qb, 0)
            if dil == 1:
                q_rows = pl.ds(pl.multiple_of(q0, qb), qb)
                k_rows = pl.ds(pl.multiple_of(k0, qb), nk)
            else:
                q_rows = pl.ds(res + dil * q0, qb, stride=dil)
                k_rows = pl.ds(res + dil * k0, nk, stride=dil)
            q = qf[gi, q_rows, :].astype(BF16)
            k = kf[gi, k_rows, :].astype(BF16)
            v = vf[gi, k_rows, :].astype(BF16)
            s = lax.dot_general(q, k, NT_DIMS, preferred_element_type=F32)
            dist = (q0 + lax.broadcasted_iota(I32, (qb, nk), 0)) - (k0 + lax.broadcasted_iota(I32, (qb, nk), 1))
            s = jnp.where((dist >= 0) & (dist <= ATT_BACK), s, -jnp.inf)
            m = jnp.max(s, axis=-1, keepdims=True)
            p = jnp.exp(s - m)
            den = jnp.sum(p, axis=-1, keepdims=True)
            o = jnp.dot(p.astype(BF16), v, preferred_element_type=F32) / den
            og[gi, q_rows, :] = o
            lg[gi, q_rows, :] = jnp.broadcast_to(m + jnp.log(den), (qb, ATT_DIM))
            return carry

        lax.fori_loop(0, dil * n_qblk, block, 0, unroll=4)

    def merge_rows(i, carry):
        r0 = pl.multiple_of(i * rb, rb)
        ls = [lg[gi, pl.ds(r0, rb), :] for gi in range(n_groups)]
        m = functools.reduce(jnp.maximum, ls)
        ws = [jnp.exp(l - m) for l in ls]
        num = sum(w * og[gi, pl.ds(r0, rb), :] for gi, w in enumerate(ws))
        o_ref[pl.ds(r0, rb), :] = (num / sum(ws)).astype(o_ref.dtype)
        return carry

    lax.fori_loop(0, seq // rb, merge_rows, 0)


def _attention(h, cos, sin, batch, seq, col0):
    t = h.shape[0]
    n_groups = len(ATT_GROUPS)
    part = n_groups * ATT_HEADS * ATT_DIM
    blk0 = col0 // ATT_DIM
    in_specs = []
    for gi in range(n_groups):
        for p in range(3):
            off = blk0 + (p * part) // ATT_DIM + gi * ATT_HEADS
            in_specs.append(pl.BlockSpec((seq, ATT_DIM), lambda b, hh, off=off: (b, off + hh)))
    tab = pl.BlockSpec((seq, ATT_DIM), lambda b, hh: (b, 0))
    in_specs += [tab, tab]
    scr = pltpu.VMEM((n_groups, seq, ATT_DIM), F32)
    return pl.pallas_call(
        functools.partial(_attn_kernel, seq=seq),
        out_shape=jax.ShapeDtypeStruct((t, ATT_HEADS * ATT_DIM), BF16),
        grid=(batch, ATT_HEADS),
        in_specs=in_specs,
        out_specs=pl.BlockSpec((seq, ATT_DIM), lambda b, hh: (b, hh)),
        scratch_shapes=[scr, scr, scr, scr, scr],
        compiler_params=_params("parallel", "parallel"),
        name="dilated_attn",
    )(*([h] * 9), cos, sin)


def _layer_norm_rows(r, g, b):
    mu = jnp.mean(r, axis=-1, keepdims=True)
    d = r - mu
    var = jnp.mean(d * d, axis=-1, keepdims=True)
    return d * lax.rsqrt(var + LN_EPS) * g + b


def _pack_halves(y):
    n = y.shape[1] // 2
    hi = pltpu.bitcast(y[:, :n].astype(BF16).astype(F32), U32)
    lo = pltpu.bitcast(y[:, n:].astype(BF16).astype(F32), U32)
    return hi | (lo >> 16)


def _unpack_halves(w):
    hi = pltpu.bitcast(w & jnp.uint32(0xFFFF0000), F32)
    lo = pltpu.bitcast(w << 16, F32)
    return hi, lo


def _mix_out_kernel(x_ref, oh_ref, oa_ref, gh_ref, ga_ref, whg_ref, wap_ref, wout_ref, g_ref, b_ref,
                    x1_ref, x1p_ref, acc_ref, *, alpha, n_col):
    n = pl.program_id(1)
    y_h = jnp.dot(oh_ref[...], whg_ref[...], preferred_element_type=F32)
    y_a = jnp.dot(oa_ref[...], wap_ref[...], preferred_element_type=F32)
    merged = _sigmoid(gh_ref[...].astype(F32)) * y_h + _sigmoid(ga_ref[...].astype(F32)) * y_a
    part = jnp.dot(merged.astype(BF16), wout_ref[...], preferred_element_type=F32)

    @pl.when(n == 0)
    def _():
        acc_ref[...] = part

    @pl.when(n > 0)
    def _():
        acc_ref[...] += part

    @pl.when(n == n_col - 1)
    def _():
        y = _layer_norm_rows(alpha * x_ref[...] + acc_ref[...], g_ref[...], b_ref[...])
        x1_ref[...] = y
        x1p_ref[...] = _pack_halves(y)


def _mix_out(x, h, o_h, o_a, w_hg, w_ap, w_out, ln_g, ln_b, gate_col0, alpha):
    t, d = x.shape
    tm = min(t, 512)
    tn = 512
    n_col = d // tn
    gh0 = gate_col0 // tn
    ga0 = (gate_col0 + d) // tn
    return pl.pallas_call(
        functools.partial(_mix_out_kernel, alpha=alpha, n_col=n_col),
        out_shape=(jax.ShapeDtypeStruct((t, d), F32), jax.ShapeDtypeStruct((t, d // 2), U32)),
        grid=(t // tm, n_col),
        in_specs=[pl.BlockSpec((tm, d), lambda i, n: (i, 0)),
                  pl.BlockSpec((tm, o_h.shape[1]), lambda i, n: (i, 0)),
                  pl.BlockSpec((tm, o_a.shape[1]), lambda i, n: (i, 0)),
                  pl.BlockSpec((tm, tn), lambda i, n: (i, gh0 + n)),
                  pl.BlockSpec((tm, tn), lambda i, n: (i, ga0 + n)),
                  pl.BlockSpec((w_hg.shape[0], tn), lambda i, n: (0, n)),
                  pl.BlockSpec((w_ap.shape[0], tn), lambda i, n: (0, n)),
                  pl.BlockSpec((tn, d), lambda i, n: (n, 0)),
                  pl.BlockSpec((1, d), lambda i, n: (0, 0)),
                  pl.BlockSpec((1, d), lambda i, n: (0, 0))],
        out_specs=(pl.BlockSpec((tm, d), lambda i, n: (i, 0)),
                   pl.BlockSpec((tm, d // 2), lambda i, n: (i, 0))),
        scratch_shapes=[pltpu.VMEM((tm, d), F32)],
        compiler_params=_params("parallel", "arbitrary"),
        name="mix_out",
    )(x, o_h, o_a, h, h, w_hg, w_ap, w_out, ln_g, ln_b)


def _split_bf16(x):
    hi = x.astype(BF16)
    lo = (x - hi.astype(F32)).astype(BF16)
    return hi, lo


def _router_kernel(x_ref, wr_ref, bias_ref, tri_ref, low_ref, rank_ref, eid_ref, wgt_ref, cnt_ref,
                   sel_scr, carry_ref, *, tr):
    ne = N_EXPERTS
    per = ne // N_GROUPS

    @pl.when(pl.program_id(0) == 0)
    def _():
        carry_ref[...] = jnp.zeros_like(carry_ref)

    xh, xl = _split_bf16(x_ref[...])
    wh, wl = _split_bf16(wr_ref[...])
    logits = (lax.dot_general(wh, xh, NT_DIMS, preferred_element_type=F32)
              + lax.dot_general(wh, xl, NT_DIMS, preferred_element_type=F32)
              + lax.dot_general(wl, xh, NT_DIMS, preferred_element_type=F32))
    scores = _sigmoid(logits)
    sel = scores + bias_ref[...]

    grp = sel.reshape(N_GROUPS, per, tr)
    sub = lax.broadcasted_iota(I32, grp.shape, 1)
    m1 = jnp.max(grp, axis=1, keepdims=True)
    first = jnp.min(jnp.where(grp == m1, sub, per), axis=1, keepdims=True)
    m2 = jnp.max(jnp.where(sub == first, -jnp.inf, grp), axis=1, keepdims=True)
    gs = m1 + m2
    gidx = lax.broadcasted_iota(I32, gs.shape, 0)
    grank = jnp.zeros(gs.shape, I32)
    for j in range(N_GROUPS):
        other = gs[j:j + 1]
        grank += ((other > gs) | ((other == gs) & (j < gidx))).astype(I32)
    masked = jnp.where(grank < TOPK_GROUPS, grp, -jnp.inf).reshape(ne, tr)

    sel_scr[...] = masked
    eidx = lax.broadcasted_iota(I32, (ne, tr), 0)

    def rank_step(j, erank):
        other = sel_scr[pl.ds(j, 1), :]
        return erank + ((other > masked) | ((other == masked) & (j < eidx))).astype(I32)

    erank = lax.fori_loop(0, ne, rank_step, jnp.zeros((ne, tr), I32))
    chosen = erank < TOP_K
    w = jnp.where(chosen, scores, 0.0)
    gates = w / jnp.sum(w, axis=0, keepdims=True) * ROUTED_SCALE

    chosen_b = jnp.where(chosen, 1.0, 0.0).astype(BF16)
    incl = jnp.dot(chosen_b, tri_ref[...], preferred_element_type=F32)
    carry = carry_ref[...]
    rank_in_expert = (carry + incl - 1.0).astype(I32)
    carry_new = carry + incl[:, tr - 1:tr]
    carry_ref[...] = carry_new
    cnt_ref[...] = jnp.broadcast_to(carry_new, cnt_ref.shape).astype(I32)

    slot = jnp.dot(low_ref[...], chosen_b, preferred_element_type=F32).astype(I32)
    for j in range(TOP_K):
        pick = chosen & (slot == j)
        rank_ref[pl.ds(j, 1), :] = jnp.sum(jnp.where(pick, rank_in_expert, 0), axis=0, keepdims=True)
        eid_ref[pl.ds(j, 1), :] = jnp.sum(jnp.where(pick, eidx, 0), axis=0, keepdims=True)
        wgt_ref[pl.ds(j, 1), :] = jnp.sum(jnp.where(pick, gates, 0.0), axis=0, keepdims=True)


def _router(x1, w_router_t, bias_col):
    t, d = x1.shape
    ne = N_EXPERTS
    tr = min(t, 512)
    tri = jnp.asarray(np.triu(np.ones((tr, tr), np.float32)), BF16)
    low = jnp.asarray(np.tril(np.ones((ne, ne), np.float32), -1), BF16)
    slot_shape = jax.ShapeDtypeStruct((TOP_K, t), I32)
    slot_spec = pl.BlockSpec((TOP_K, tr), lambda i: (0, i))
    return pl.pallas_call(
        functools.partial(_router_kernel, tr=tr),
        out_shape=(slot_shape, slot_shape, jax.ShapeDtypeStruct((TOP_K, t), F32),
                   jax.ShapeDtypeStruct((ne, LANES), I32)),
        grid=(t // tr,),
        in_specs=[pl.BlockSpec((tr, d), lambda i: (i, 0)),
                  pl.BlockSpec((ne, d), lambda i: (0, 0)),
                  pl.BlockSpec((ne, 1), lambda i: (0, 0)),
                  pl.BlockSpec((tr, tr), lambda i: (0, 0)),
                  pl.BlockSpec((ne, ne), lambda i: (0, 0))],
        out_specs=(slot_spec, slot_spec, slot_spec, pl.BlockSpec((ne, LANES), lambda i: (0, 0))),
        scratch_shapes=[pltpu.VMEM((ne, tr), F32), pltpu.VMEM((ne, 1), F32)],
        compiler_params=_params("arbitrary"),
        name="router",
    )(x1, w_router_t, bias_col, tri, low)


def _dispatch_kernel(pos_ref, x_ref, buf_ref, sem, *, td):
    def issue(tok, carry):
        for j in range(TOP_K):
            pltpu.make_async_copy(x_ref.at[pl.ds(tok, 1), :],
                                  buf_ref.at[pl.ds(pos_ref[j, tok], 1), :], sem).start()
        return carry

    lax.fori_loop(0, td, issue, 0)
    for j in range(TOP_K):
        pltpu.make_async_copy(x_ref, buf_ref.at[pl.ds(0, td), :], sem).wait()


def _dispatch(pos, x1p):
    t, dp = x1p.shape
    td = min(t, 256)
    return pl.pallas_call(
        functools.partial(_dispatch_kernel, td=td),
        out_shape=jax.ShapeDtypeStruct((t * TOP_K, dp), U32),
        grid=(t // td,),
        in_specs=[pl.BlockSpec((TOP_K, td), lambda i: (0, i), memory_space=pltpu.SMEM),
                  pl.BlockSpec((td, dp), lambda i: (i, 0))],
        out_specs=pl.BlockSpec(memory_space=pl.ANY),
        scratch_shapes=[pltpu.SemaphoreType.DMA(())],
        compiler_params=_params("arbitrary"),
        name="dispatch",
    )(pos, x1p)


def _experts_kernel(tile_ref, exp_ref, lo_ref, hi_ref, nv_ref, xs_ref, wg_ref, wu_ref, wd_ref, ys_ref,
                    wgb, wub, wdb):
    it = pl.program_id(0)
    valid = it < nv_ref[0]

    @pl.when(valid & ((it == 0) | (exp_ref[it] != exp_ref[jnp.maximum(it - 1, 0)])))
    def _():
        wgb[...] = wg_ref[0, 0].astype(BF16)
        wub[...] = wu_ref[0, 0].astype(BF16)
        wdb[...] = wd_ref[0, 0].astype(BF16)

    @pl.when(valid)
    def _():
        tm, half = xs_ref.shape
        a, b = _unpack_halves(xs_ref[...])
        x = jnp.concatenate([a.astype(BF16), b.astype(BF16)], axis=1)
        gate = jnp.dot(x, wgb[...], preferred_element_type=F32)
        up = jnp.dot(x, wub[...], preferred_element_type=F32)
        hid = (gate * _sigmoid(gate) * up).astype(BF16)
        y = _pack_halves(jnp.dot(hid, wdb[...], preferred_element_type=F32))
        lo = lo_ref[it]

        @pl.when(lo == 0)
        def _():
            ys_ref[...] = y

        @pl.when(lo > 0)
        def _():
            row = lax.broadcasted_iota(I32, (tm, half), 0)
            ys_ref[...] = jnp.where((row >= lo) & (row < hi_ref[it]), y, ys_ref[...])


def _experts(items, xs, w_gate, w_up, w_down, layer, tm):
    n_rows, dp = xs.shape
    _, ne, d, ff = w_gate.shape
    n_items = items[0].shape[0]

    def row_map(i, tl, ex, lo, hi, nv):
        return (tl[jnp.minimum(i, nv[0] - 1)], 0)

    def w_map(i, tl, ex, lo, hi, nv):
        return (layer, ex[jnp.minimum(i, nv[0] - 1)], 0, 0)

    return pl.pallas_call(
        _experts_kernel,
        out_shape=jax.ShapeDtypeStruct((n_rows, dp), U32),
        grid_spec=pltpu.PrefetchScalarGridSpec(
            num_scalar_prefetch=5,
            grid=(n_items,),
            in_specs=[pl.BlockSpec((tm, dp), row_map),
                      pl.BlockSpec((1, 1, d, ff), w_map),
                      pl.BlockSpec((1, 1, d, ff), w_map),
                      pl.BlockSpec((1, 1, ff, d), w_map)],
            out_specs=pl.BlockSpec((tm, dp), row_map),
            scratch_shapes=[pltpu.VMEM((d, ff), BF16), pltpu.VMEM((d, ff), BF16), pltpu.VMEM((ff, d), BF16)]),
        compiler_params=_params("arbitrary"),
        name="experts",
    )(*items, xs, w_gate, w_up, w_down)


def _combine_kernel(pos_ref, x_ref, wk_ref, ys_ref, wsg_ref, wsu_ref, wsd_ref, g_ref, b_ref,
                    x2_ref, x2b_ref, ybuf, sem, *, tc, alpha):
    def issue(tok, carry):
        for j in range(TOP_K):
            pltpu.make_async_copy(ys_ref.at[pl.ds(pos_ref[j, tok], 1), :],
                                  ybuf.at[j, pl.ds(tok, 1), :], sem).start()
        return carry

    lax.fori_loop(0, tc, issue, 0)

    x = x_ref[...]
    xb = x.astype(BF16)
    gate = jnp.dot(xb, wsg_ref[...], preferred_element_type=F32)
    up = jnp.dot(xb, wsu_ref[...], preferred_element_type=F32)
    hid = (gate * _sigmoid(gate) * up).astype(BF16)
    shared = jnp.dot(hid, wsd_ref[...], preferred_element_type=F32)

    for j in range(TOP_K):
        pltpu.make_async_copy(ys_ref.at[pl.ds(0, tc), :], ybuf.at[j], sem).wait()

    half = ybuf.shape[2]
    wk = wk_ref[...]
    acc_hi = jnp.zeros((tc, half), F32)
    acc_lo = jnp.zeros((tc, half), F32)
    for j in range(TOP_K):
        hi, lo = _unpack_halves(ybuf[j])
        wj = wk[:, j:j + 1]
        acc_hi += wj * hi
        acc_lo += wj * lo
    ffn = jnp.concatenate([acc_hi, acc_lo], axis=1) + shared
    y = _layer_norm_rows(alpha * x + ffn, g_ref[...], b_ref[...])
    x2_ref[...] = y
    x2b_ref[...] = y.astype(BF16)


def _combine(pos, x1, wk, ys, ws_gate, ws_up, ws_down, ln_g, ln_b, alpha):
    t, d = x1.shape
    dp = ys.shape[1]
    ff = ws_gate.shape[1]
    tc = min(t, 256)
    const = lambda i: (0, 0)
    rows = lambda i: (i, 0)
    return pl.pallas_call(
        functools.partial(_combine_kernel, tc=tc, alpha=alpha),
        out_shape=(jax.ShapeDtypeStruct((t, d), F32), jax.ShapeDtypeStruct((t, d), BF16)),
        grid=(t // tc,),
        in_specs=[pl.BlockSpec((TOP_K, tc), lambda i: (0, i), memory_space=pltpu.SMEM),
                  pl.BlockSpec((tc, d), rows),
                  pl.BlockSpec((tc, TOP_K), rows),
                  pl.BlockSpec(memory_space=pl.ANY),
                  pl.BlockSpec((d, ff), const),
                  pl.BlockSpec((d, ff), const),
                  pl.BlockSpec((ff, d), const),
                  pl.BlockSpec((1, d), const),
                  pl.BlockSpec((1, d), const)],
        out_specs=(pl.BlockSpec((tc, d), rows), pl.BlockSpec((tc, d), rows)),
        scratch_shapes=[pltpu.VMEM((TOP_K, tc, dp), U32), pltpu.SemaphoreType.DMA(())],
        compiler_params=_params("arbitrary"),
        name="combine",
    )(pos, x1, wk, ys, ws_gate, ws_up, ws_down, ln_g, ln_b)


def _expert_work_items(counts, n_rows, tm):
    ne = counts.shape[0]
    end = jnp.cumsum(counts)
    start = end - counts
    first_tile = start // tm
    n_it = jnp.where(counts > 0, (end - 1) // tm - first_tile + 1, 0)
    it_end = jnp.cumsum(n_it)
    it_start = it_end - n_it
    n_items = n_rows // tm + ne
    k = jnp.arange(n_items, dtype=I32)
    expert = jnp.minimum(jnp.sum((it_end[None, :] <= k[:, None]).astype(I32), axis=1), ne - 1)
    onehot = expert[:, None] == jnp.arange(ne, dtype=I32)[None, :]
    pick = lambda v: jnp.sum(jnp.where(onehot, v[None, :], 0), axis=1)
    tile = jnp.clip(pick(first_tile) + k - pick(it_start), 0, n_rows // tm - 1)
    lo = jnp.maximum(pick(start) - tile * tm, 0)
    hi = jnp.minimum(pick(end) - tile * tm, tm)
    return start.astype(I32), (tile.astype(I32), expert.astype(I32), lo.astype(I32), hi.astype(I32),
                               it_end[-1:].astype(I32))


def _moe(x1, x1p, w_router_t, bias_col, w_gate, w_up, w_down, layer, ws_gate, ws_up, ws_down, ln_g, ln_b, alpha):
    t = x1.shape[0]
    tm = 256
    rank, eid, wgt, cnt = _router(x1, w_router_t, bias_col)
    expert_start, items = _expert_work_items(cnt[:, 0], t * TOP_K, tm)
    pos = rank
    for e in range(N_EXPERTS):
        pos = pos + jnp.where(eid == e, expert_start[e], 0)
    xs = _dispatch(pos, x1p)
    ys = _experts(items, xs, w_gate, w_up, w_down, layer, tm)
    return _combine(pos, x1, wgt.T, ys, ws_gate, ws_up, ws_down, ln_g, ln_b, alpha)


def kernel(x, positions, w_in, lb_logits, hg_norm_g, w_hg_proj, w_att_proj, w_out, ln1_g, ln1_b,
           w_router, router_bias, w_e_gate, w_e_up, w_e_down, w_s_gate, w_s_up, w_s_down, ln2_g, ln2_b):
    batch, seq, d = x.shape
    depth = w_in.shape[0]
    t = batch * seq
    alpha = float((2 * depth) ** 0.25)
    hg_width = HG_HEADS * HG_DIM
    att_col0 = 4 * hg_width
    gate_col0 = att_col0 + 3 * len(ATT_GROUPS) * ATT_HEADS * ATT_DIM

    p = jax.nn.softmax(lb_logits.astype(F32), axis=0)
    cs = jnp.cumsum(p, axis=0)
    lower = cs - cs[:1]
    log1m_lower = jnp.log1p(-lower)

    cos, sin = _rope_tables(positions)
    xf = x.reshape(t, d)
    xb = xf.astype(BF16)
    in_width = w_in.shape[2]
    tn = 1280 if in_width % 1280 == 0 else 512
    for layer in range(depth):
        h = _matmul(xb, w_in, layer, min(t, 1024), tn, BF16)
        o_h = _hgrn(h, lower[layer].reshape(1, -1), log1m_lower[layer].reshape(1, -1),
                    hg_norm_g[layer].reshape(1, -1), batch, seq)
        o_a = _attention(h, cos, sin, batch, seq, att_col0)
        x1, x1p = _mix_out(xf, h, o_h, o_a, w_hg_proj[layer].astype(BF16), w_att_proj[layer].astype(BF16),
                           w_out[layer].astype(BF16), ln1_g[layer].reshape(1, d), ln1_b[layer].reshape(1, d),
                           gate_col0, alpha)
        xf, xb = _moe(x1, x1p, w_router[layer].T, router_bias[layer].reshape(-1, 1),
                      w_e_gate, w_e_up, w_e_down, layer,
                      w_s_gate[layer].astype(BF16), w_s_up[layer].astype(BF16), w_s_down[layer].astype(BF16),
                      ln2_g[layer].reshape(1, d), ln2_b[layer].reshape(1, d), alpha)
    return xf.reshape(batch, seq, d)
```

```python
import functools

import numpy as np
import jax
import jax.numpy as jnp
from jax import lax
from jax.experimental import pallas as pl
from jax.experimental.pallas import tpu as pltpu
from jax.experimental.pallas import tpu_sc as plsc

F32 = jnp.float32
BF16 = jnp.bfloat16
U32 = jnp.uint32
I32 = jnp.int32

LANES = 128
VMEM_LIMIT_BYTES = 56 * 1024 * 1024
SC_CORES = 2
SC_SUBCORES = 16
SC_WORKERS = SC_CORES * SC_SUBCORES
SC_SCATTER_ROWS = 32
SC_GATHER_ROWS = 16
SC_GATHER_BUFFERS = 4

HG_HEADS = 8
HG_DIM = 128
HG_CHUNK = 64
HG_SUB = 16
ATT_GROUPS = ((128, 1), (512, 4), (2048, 16))
ATT_HEADS = 4
ATT_DIM = 128
ATT_BACK = 128
ROPE_THETA = 10000.0
N_EXPERTS = 64
N_GROUPS = 8
TOPK_GROUPS = 4
TOP_K = 8
ROUTED_SCALE = 2.5
LN_EPS = 1e-5
NORM_EPS = 1e-6

NT_DIMS = (((1,), (1,)), ((), ()))
TN_DIMS = (((0,), (0,)), ((), ()))


def _params(*sem):
    return pltpu.CompilerParams(dimension_semantics=sem, vmem_limit_bytes=VMEM_LIMIT_BYTES)


def _sigmoid(x):
    return 1.0 / (1.0 + jnp.exp(-x))


def _matmul_kernel(x_ref, w_ref, o_ref, wb_ref):
    @pl.when(pl.program_id(1) == 0)
    def _():
        wb_ref[...] = w_ref[0].astype(BF16)

    o_ref[...] = jnp.dot(x_ref[...], wb_ref[...], preferred_element_type=F32).astype(o_ref.dtype)


def _matmul(x, w, layer, tm, tn, out_dtype):
    m, k = x.shape
    n = w.shape[2]
    return pl.pallas_call(
        _matmul_kernel,
        out_shape=jax.ShapeDtypeStruct((m, n), out_dtype),
        grid=(n // tn, m // tm),
        in_specs=[pl.BlockSpec((tm, k), lambda j, i: (i, 0)),
                  pl.BlockSpec((1, k, tn), lambda j, i: (layer, 0, j))],
        out_specs=pl.BlockSpec((tm, tn), lambda j, i: (i, j)),
        scratch_shapes=[pltpu.VMEM((k, tn), BF16)],
        compiler_params=_params("parallel", "arbitrary"),
        name="in_proj",
    )(x, w)


def _rope_table_kernel(pos_ref, inv_ref, sign_ref, cos_ref, sin_ref):
    ang = pos_ref[...].astype(F32) * inv_ref[...]
    cos_ref[...] = jnp.cos(ang)
    sin_ref[...] = jnp.sin(ang) * sign_ref[...]


def _rope_tables(positions):
    t = positions.size
    half = ATT_DIM // 2
    inv_half = ROPE_THETA ** (-np.arange(half, dtype=np.float32) * np.float32(2.0) / np.float32(ATT_DIM))
    inv = jnp.asarray(np.concatenate([inv_half, inv_half]).astype(np.float32).reshape(1, ATT_DIM))
    sign = jnp.asarray(np.concatenate([-np.ones(half), np.ones(half)]).astype(np.float32).reshape(1, ATT_DIM))
    tm = min(t, 2048)
    return pl.pallas_call(
        _rope_table_kernel,
        out_shape=(jax.ShapeDtypeStruct((t, ATT_DIM), F32), jax.ShapeDtypeStruct((t, ATT_DIM), F32)),
        grid=(t // tm,),
        in_specs=[pl.BlockSpec((tm, 1), lambda i: (i, 0)),
                  pl.BlockSpec((1, ATT_DIM), lambda i: (0, 0)),
                  pl.BlockSpec((1, ATT_DIM), lambda i: (0, 0))],
        out_specs=(pl.BlockSpec((tm, ATT_DIM), lambda i: (i, 0)),
                   pl.BlockSpec((tm, ATT_DIM), lambda i: (i, 0))),
        compiler_params=_params("parallel"),
        name="rope_tables",
    )(positions.reshape(t, 1), inv, sign)


def _cumsum_rows(x):
    n = x.shape[0]
    row = lax.broadcasted_iota(I32, x.shape, 0)
    s = 1
    while s < n:
        x = x + jnp.where(row >= s, pltpu.roll(x, s, 0), 0.0)
        s *= 2
    return x


def _rows_from(cum, offsets):
    parts = []
    for o in offsets:
        if o is None:
            parts.append(jnp.zeros((HG_SUB, cum.shape[1]), F32))
        else:
            parts.append(jnp.broadcast_to(cum[o:o + 1, :], (HG_SUB, cum.shape[1])))
    return jnp.concatenate(parts, axis=0)


def _hgrn_kernel(hq_ref, hf_ref, hi_ref, hg_ref, lb_ref, l1m_ref, ng_ref, o_ref, st_ref, *, n_chunks):
    c = HG_CHUNK
    nsub = c // HG_SUB

    @pl.when(pl.program_id(2) == 0)
    def _():
        st_ref[...] = jnp.zeros_like(st_ref)

    lb = lb_ref[...]
    one_m_lb = 1.0 - lb
    log1m_lb = l1m_ref[...]
    norm_g = ng_ref[...]
    row = lax.broadcasted_iota(I32, (c, HG_DIM), 0)
    blk = row // HG_SUB
    ti = lax.broadcasted_iota(I32, (c, c), 0)
    si = lax.broadcasted_iota(I32, (c, c), 1)
    diag_mask = ((ti // HG_SUB) == (si // HG_SUB)) & (si <= ti)

    def chunk(ci, carry):
        r0 = pl.multiple_of(ci * c, c)
        z = hf_ref[pl.ds(r0, c), :].astype(F32)
        qraw = hq_ref[pl.ds(r0, c), :].astype(F32)
        v = hi_ref[pl.ds(r0, c), :]
        g = hg_ref[pl.ds(r0, c), :].astype(F32)

        e = jnp.exp(-jnp.abs(z))
        r = 1.0 / (1.0 + e)
        sig = jnp.where(z >= 0, r, e * r)
        sig_neg = jnp.where(z >= 0, e * r, r)
        log_sig = jnp.minimum(z, 0.0) - jnp.log(1.0 + e)
        log_f = jnp.maximum(jnp.log(lb + one_m_lb * sig), log1m_lb + log_sig)
        k = one_m_lb * sig_neg
        q = qraw * _sigmoid(qraw)

        cum = _cumsum_rows(log_f)
        start = _rows_from(cum, [None] + [HG_SUB * i - 1 for i in range(1, nsub)])
        mid = _rows_from(cum, [HG_SUB * i + HG_SUB // 2 - 1 for i in range(nsub)])
        end = cum[c - 1:c, :]

        qd = (q * jnp.exp(cum - mid)).astype(BF16)
        kd = (k * jnp.exp(mid - cum)).astype(BF16)
        att = jnp.where(diag_mask, lax.dot_general(qd, kd, NT_DIMS, preferred_element_type=F32), 0.0)

        qs = q * jnp.exp(cum - start)
        q_slots, k_slots = [], []
        for i in range(1, nsub):
            q_slots.append(jnp.where(blk == i, qs, 0.0).astype(BF16))
            n_rows = HG_SUB * i
            b_i = cum[n_rows - 1:n_rows, :]
            k_i = k[:n_rows] * jnp.exp(b_i - cum[:n_rows])
            k_slots.append(jnp.concatenate([k_i, jnp.zeros((c - n_rows, HG_DIM), F32)], axis=0).astype(BF16))
        att = att + lax.dot_general(jnp.concatenate(q_slots, axis=1), jnp.concatenate(k_slots, axis=1),
                                    NT_DIMS, preferred_element_type=F32)
        intra = jnp.dot(att.astype(BF16), v, preferred_element_type=F32)

        st = st_ref[...]
        inter = lax.dot_general((q * jnp.exp(cum)).astype(BF16), st.astype(BF16), NT_DIMS,
                                preferred_element_type=F32)
        k_end = (k * jnp.exp(end - cum)).astype(BF16)
        st_ref[...] = st * jnp.exp(end) + lax.dot_general(v, k_end, TN_DIMS, preferred_element_type=F32)

        o = inter + intra
        o = o * lax.rsqrt(jnp.mean(o * o, axis=-1, keepdims=True) + NORM_EPS) * norm_g
        o_ref[pl.ds(r0, c), :] = (o * _sigmoid(g)).astype(o_ref.dtype)
        return carry

    lax.fori_loop(0, n_chunks, chunk, 0, unroll=8)


def _hgrn(h, lb, log1m_lb, norm_g, batch, seq):
    t = h.shape[0]
    tb = min(seq, 512)
    nsb = seq // tb
    width = HG_HEADS * HG_DIM

    def col(off):
        return pl.BlockSpec((tb, HG_DIM), lambda b, hh, s: (b * nsb + s, off * HG_HEADS + hh))

    vec = pl.BlockSpec((1, HG_DIM), lambda b, hh, s: (0, hh))
    return pl.pallas_call(
        functools.partial(_hgrn_kernel, n_chunks=tb // HG_CHUNK),
        out_shape=jax.ShapeDtypeStruct((t, width), BF16),
        grid=(batch, HG_HEADS, nsb),
        in_specs=[col(0), col(1), col(2), col(3), vec, vec, vec],
        out_specs=pl.BlockSpec((tb, HG_DIM), lambda b, hh, s: (b * nsb + s, hh)),
        scratch_shapes=[pltpu.VMEM((HG_DIM, HG_DIM), F32)],
        compiler_params=_params("parallel", "parallel", "arbitrary"),
        name="hgrn2",
    )(h, h, h, h, lb, log1m_lb, norm_g)


def _attn_kernel(*refs, seq):
    qkv_refs = refs[:9]
    cos_ref, sin_ref, o_ref = refs[9:12]
    qf, kf, vf, og, lg = refs[12:]
    n_groups = len(ATT_GROUPS)
    scale = ATT_DIM ** -0.5
    rb = min(seq, 256)

    def rope_rows(i, carry):
        r0 = pl.multiple_of(i * rb, rb)
        cs = cos_ref[pl.ds(r0, rb), :]
        sn = sin_ref[pl.ds(r0, rb), :]
        for gi in range(n_groups):
            xq = qkv_refs[3 * gi][pl.ds(r0, rb), :].astype(F32)
            xk = qkv_refs[3 * gi + 1][pl.ds(r0, rb), :].astype(F32)
            qf[gi, pl.ds(r0, rb), :] = (xq * cs + pltpu.roll(xq, ATT_DIM // 2, 1) * sn) * scale
            kf[gi, pl.ds(r0, rb), :] = xk * cs + pltpu.roll(xk, ATT_DIM // 2, 1) * sn
            vf[gi, pl.ds(r0, rb), :] = qkv_refs[3 * gi + 2][pl.ds(r0, rb), :].astype(F32)
        return carry

    lax.fori_loop(0, seq // rb, rope_rows, 0)

    qb = ATT_BACK
    for gi, (_, dil) in enumerate(ATT_GROUPS):
        length = seq // dil
        nk = min(2 * qb, length)
        n_qblk = length // qb

        def block(it, carry, gi=gi, dil=dil, nk=nk, n_qblk=n_qblk):
            res = it // n_qblk
            iq = it % n_qblk
            q0 = iq * qb
            k0 = jnp.maximum(q0 - qb, 0)
            if dil == 1:
                q_rows = pl.ds(pl.multiple_of(q0, qb), qb)
                k_rows = pl.ds(pl.multiple_of(k0, qb), nk)
            else:
                q_rows = pl.ds(res + dil * q0, qb, stride=dil)
                k_rows = pl.ds(res + dil * k0, nk, stride=dil)
            q = qf[gi, q_rows, :].astype(BF16)
            k = kf[gi, k_rows, :].astype(BF16)
            v = vf[gi, k_rows, :].astype(BF16)
            s = lax.dot_general(q, k, NT_DIMS, preferred_element_type=F32)
            dist = (q0 + lax.broadcasted_iota(I32, (qb, nk), 0)) - (k0 + lax.broadcasted_iota(I32, (qb, nk), 1))
            s = jnp.where((dist >= 0) & (dist <= ATT_BACK), s, -jnp.inf)
            m = jnp.max(s, axis=-1, keepdims=True)
            p = jnp.exp(s - m)
            den = jnp.sum(p, axis=-1, keepdims=True)
            o = jnp.dot(p.astype(BF16), v, preferred_element_type=F32) / den
            og[gi, q_rows, :] = o
            lg[gi, q_rows, :] = jnp.broadcast_to(m + jnp.log(den), (qb, ATT_DIM))
            return carry

        lax.fori_loop(0, dil * n_qblk, block, 0, unroll=4)

    def merge_rows(i, carry):
        r0 = pl.multiple_of(i * rb, rb)
        ls = [lg[gi, pl.ds(r0, rb), :] for gi in range(n_groups)]
        m = functools.reduce(jnp.maximum, ls)
        ws = [jnp.exp(l - m) for l in ls]
        num = sum(w * og[gi, pl.ds(r0, rb), :] for gi, w in enumerate(ws))
        o_ref[pl.ds(r0, rb), :] = (num / sum(ws)).astype(o_ref.dtype)
        return carry

    lax.fori_loop(0, seq // rb, merge_rows, 0)


def _attention(h, cos, sin, batch, seq, col0):
    t = h.shape[0]
    n_groups = len(ATT_GROUPS)
    part = n_groups * ATT_HEADS * ATT_DIM
    blk0 = col0 // ATT_DIM
    in_specs = []
    for gi in range(n_groups):
        for p in range(3):
            off = blk0 + (p * part) // ATT_DIM + gi * ATT_HEADS
            in_specs.append(pl.BlockSpec((seq, ATT_DIM), lambda b, hh, off=off: (b, off + hh)))
    tab = pl.BlockSpec((seq, ATT_DIM), lambda b, hh: (b, 0))
    in_specs += [tab, tab]
    scr = pltpu.VMEM((n_groups, seq, ATT_DIM), F32)
    return pl.pallas_call(
        functools.partial(_attn_kernel, seq=seq),
        out_shape=jax.ShapeDtypeStruct((t, ATT_HEADS * ATT_DIM), BF16),
        grid=(batch, ATT_HEADS),
        in_specs=in_specs,
        out_specs=pl.BlockSpec((seq, ATT_DIM), lambda b, hh: (b, hh)),
        scratch_shapes=[scr, scr, scr, scr, scr],
        compiler_params=_params("parallel", "parallel"),
        name="dilated_attn",
    )(*([h] * 9), cos, sin)


def _layer_norm_rows(r, g, b):
    mu = jnp.mean(r, axis=-1, keepdims=True)
    d = r - mu
    var = jnp.mean(d * d, axis=-1, keepdims=True)
    return d * lax.rsqrt(var + LN_EPS) * g + b


def _pack_halves(y):
    n = y.shape[1] // 2
    hi = pltpu.bitcast(y[:, :n].astype(BF16).astype(F32), U32)
    lo = pltpu.bitcast(y[:, n:].astype(BF16).astype(F32), U32)
    return hi | (lo >> 16)


def _unpack_halves(w):
    hi = pltpu.bitcast(w & jnp.uint32(0xFFFF0000), F32)
    lo = pltpu.bitcast(w << 16, F32)
    return hi, lo


def _mix_out_kernel(*refs, alpha, n_col):
    x_ref, oh_ref, oa_ref = refs[:3]
    gh_refs = refs[3:3 + n_col]
    ga_refs = refs[3 + n_col:3 + 2 * n_col]
    whg_ref, wap_ref, wout_ref, g_ref, b_ref, x1_ref, x1p_ref = refs[3 + 2 * n_col:]
    tn = gh_refs[0].shape[1]
    oh = oh_ref[...]
    oa = oa_ref[...]
    merged = []
    for n in range(n_col):
        cols = slice(n * tn, (n + 1) * tn)
        y_h = jnp.dot(oh, whg_ref[:, cols], preferred_element_type=F32)
        y_a = jnp.dot(oa, wap_ref[:, cols], preferred_element_type=F32)
        m = _sigmoid(gh_refs[n][...].astype(F32)) * y_h + _sigmoid(ga_refs[n][...].astype(F32)) * y_a
        merged.append(m.astype(BF16))
    mix = jnp.dot(jnp.concatenate(merged, axis=1), wout_ref[...], preferred_element_type=F32)
    y = _layer_norm_rows(alpha * x_ref[...] + mix, g_ref[...], b_ref[...])
    x1_ref[...] = y
    x1p_ref[...] = _pack_halves(y)


def _mix_out(x, h, o_h, o_a, w_hg, w_ap, w_out, ln_g, ln_b, gate_col0, alpha):
    t, d = x.shape
    tm = min(t, 256)
    tn = 512
    n_col = d // tn
    gh0 = gate_col0 // tn
    ga0 = (gate_col0 + d) // tn
    rows = lambda i: (i, 0)
    const = lambda i: (0, 0)
    resident = dict(index_map=const, pipeline_mode=pl.Buffered(1))
    gate_specs = [pl.BlockSpec((tm, tn), lambda i, c=c0 + n: (i, c)) for c0 in (gh0, ga0) for n in range(n_col)]
    return pl.pallas_call(
        functools.partial(_mix_out_kernel, alpha=alpha, n_col=n_col),
        out_shape=(jax.ShapeDtypeStruct((t, d), F32), jax.ShapeDtypeStruct((t, d // 2), U32)),
        grid=(t // tm,),
        in_specs=[pl.BlockSpec((tm, d), rows),
                  pl.BlockSpec((tm, o_h.shape[1]), rows),
                  pl.BlockSpec((tm, o_a.shape[1]), rows),
                  *gate_specs,
                  pl.BlockSpec(w_hg.shape, **resident),
                  pl.BlockSpec(w_ap.shape, **resident),
                  pl.BlockSpec(w_out.shape, **resident),
                  pl.BlockSpec((1, d), const),
                  pl.BlockSpec((1, d), const)],
        out_specs=(pl.BlockSpec((tm, d), rows), pl.BlockSpec((tm, d // 2), rows)),
        compiler_params=_params("parallel"),
        name="mix_out",
    )(x, o_h, o_a, *([h] * (2 * n_col)), w_hg, w_ap, w_out, ln_g, ln_b)


def _split_bf16(x):
    hi = x.astype(BF16)
    lo = (x - hi.astype(F32)).astype(BF16)
    return hi, lo


def _router_kernel(x_ref, wr_ref, bias_ref, tri_ref, low_ref, rank_ref, eid_ref, wgt_ref, cnt_ref,
                   sel_scr, carry_ref, *, tr):
    ne = N_EXPERTS
    per = ne // N_GROUPS

    @pl.when(pl.program_id(0) == 0)
    def _():
        carry_ref[...] = jnp.zeros_like(carry_ref)

    xh, xl = _split_bf16(x_ref[...])
    wh, wl = _split_bf16(wr_ref[...])
    logits = (lax.dot_general(wh, xh, NT_DIMS, preferred_element_type=F32)
              + lax.dot_general(wh, xl, NT_DIMS, preferred_element_type=F32)
              + lax.dot_general(wl, xh, NT_DIMS, preferred_element_type=F32))
    scores = _sigmoid(logits)
    sel = scores + bias_ref[...]

    grp = sel.reshape(N_GROUPS, per, tr)
    sub = lax.broadcasted_iota(I32, grp.shape, 1)
    m1 = jnp.max(grp, axis=1, keepdims=True)
    first = jnp.min(jnp.where(grp == m1, sub, per), axis=1, keepdims=True)
    m2 = jnp.max(jnp.where(sub == first, -jnp.inf, grp), axis=1, keepdims=True)
    gs = m1 + m2
    gidx = lax.broadcasted_iota(I32, gs.shape, 0)
    grank = jnp.zeros(gs.shape, I32)
    for j in range(N_GROUPS):
        other = gs[j:j + 1]
        grank += ((other > gs) | ((other == gs) & (j < gidx))).astype(I32)
    masked = jnp.where(grank < TOPK_GROUPS, grp, -jnp.inf).reshape(ne, tr)

    sel_scr[...] = masked
    eidx = lax.broadcasted_iota(I32, (ne, tr), 0)

    def rank_step(j, erank):
        other = sel_scr[pl.ds(j, 1), :]
        return erank + ((other > masked) | ((other == masked) & (j < eidx))).astype(I32)

    erank = lax.fori_loop(0, ne, rank_step, jnp.zeros((ne, tr), I32))
    chosen = erank < TOP_K
    w = jnp.where(chosen, scores, 0.0)
    gates = w / jnp.sum(w, axis=0, keepdims=True) * ROUTED_SCALE

    chosen_b = jnp.where(chosen, 1.0, 0.0).astype(BF16)
    incl = jnp.dot(chosen_b, tri_ref[...], preferred_element_type=F32)
    carry = carry_ref[...]
    rank_in_expert = (carry + incl - 1.0).astype(I32)
    carry_new = carry + incl[:, tr - 1:tr]
    carry_ref[...] = carry_new
    cnt_ref[...] = jnp.broadcast_to(carry_new, cnt_ref.shape).astype(I32)

    slot = jnp.dot(low_ref[...], chosen_b, preferred_element_type=F32).astype(I32)
    for j in range(TOP_K):
        pick = chosen & (slot == j)
        rank_ref[pl.ds(j, 1), :] = jnp.sum(jnp.where(pick, rank_in_expert, 0), axis=0, keepdims=True)
        eid_ref[pl.ds(j, 1), :] = jnp.sum(jnp.where(pick, eidx, 0), axis=0, keepdims=True)
        wgt_ref[pl.ds(j, 1), :] = jnp.sum(jnp.where(pick, gates, 0.0), axis=0, keepdims=True)


def _router(x1, w_router_t, bias_col):
    t, d = x1.shape
    ne = N_EXPERTS
    tr = min(t, 512)
    tri = jnp.asarray(np.triu(np.ones((tr, tr), np.float32)), BF16)
    low = jnp.asarray(np.tril(np.ones((ne, ne), np.float32), -1), BF16)
    slot_shape = jax.ShapeDtypeStruct((TOP_K, t), I32)
    slot_spec = pl.BlockSpec((TOP_K, tr), lambda i: (0, i))
    return pl.pallas_call(
        functools.partial(_router_kernel, tr=tr),
        out_shape=(slot_shape, slot_shape, jax.ShapeDtypeStruct((TOP_K, t), F32),
                   jax.ShapeDtypeStruct((ne, LANES), I32)),
        grid=(t // tr,),
        in_specs=[pl.BlockSpec((tr, d), lambda i: (i, 0)),
                  pl.BlockSpec((ne, d), lambda i: (0, 0)),
                  pl.BlockSpec((ne, 1), lambda i: (0, 0)),
                  pl.BlockSpec((tr, tr), lambda i: (0, 0)),
                  pl.BlockSpec((ne, ne), lambda i: (0, 0))],
        out_specs=(slot_spec, slot_spec, slot_spec, pl.BlockSpec((ne, LANES), lambda i: (0, 0))),
        scratch_shapes=[pltpu.VMEM((ne, tr), F32), pltpu.VMEM((ne, 1), F32)],
        compiler_params=_params("arbitrary"),
        name="router",
    )(x1, w_router_t, bias_col, tri, low)


def _sc_worker_id():
    return lax.axis_index("s") * SC_CORES + lax.axis_index("c")


def _dispatch(pos, x1p):
    t, dp = x1p.shape
    t_per_w = t // SC_WORKERS
    chunk = min(SC_SCATTER_ROWS, t_per_w // 2)
    n_chunks = t_per_w // chunk
    mesh = plsc.VectorSubcoreMesh(core_axis_name="c", subcore_axis_name="s")

    @functools.partial(
        pl.kernel, mesh=mesh,
        out_type=jax.ShapeDtypeStruct((t * TOP_K, dp), x1p.dtype),
        scratch_types=[pltpu.VMEM((n_chunks * TOP_K, chunk), I32),
                       pltpu.VMEM((2, chunk, dp), x1p.dtype),
                       pltpu.SemaphoreType.DMA((2,)),
                       pltpu.SemaphoreType.DMA((2,))],
    )
    def scatter_rows(x_hbm, pos_hbm, out_hbm, idx_v, rows_v, load_sem, scat_sem):
        wid = _sc_worker_id()
        base = wid * t_per_w
        pltpu.sync_copy(pos_hbm.at[wid], idx_v)

        def load(c, b):
            return pltpu.make_async_copy(x_hbm.at[pl.ds(base + c * chunk, chunk)], rows_v.at[b], load_sem.at[b])

        def scat(c, b, j):
            return pltpu.make_async_copy(rows_v.at[b], out_hbm.at[idx_v.at[c * TOP_K + j]], scat_sem.at[b])

        load(0, 0).start()

        @pl.loop(0, n_chunks, step=2)
        def _(c0):
            for b in range(2):
                c = c0 + b
                load(c, b).wait()
                for j in range(TOP_K):
                    scat(c, b, j).start()

                @pl.when(c + 1 < n_chunks)
                def _():
                    @pl.when(c >= 1)
                    def _():
                        for j in range(TOP_K):
                            scat(c - 1, 1 - b, j).wait()
                    load(c + 1, 1 - b).start()

        for b in range(2):
            for j in range(TOP_K):
                scat(n_chunks - 2 + b, b, j).wait()

    idx = pos.reshape(TOP_K, SC_WORKERS, n_chunks, chunk).transpose(1, 2, 0, 3)
    return scatter_rows(x1p, idx.reshape(SC_WORKERS, n_chunks * TOP_K, chunk))


def _gather_rows(table, idx):
    n_rows = idx.shape[0]
    dp = table.shape[1]
    r_per_w = n_rows // SC_WORKERS
    chunk, nbuf = SC_GATHER_ROWS, SC_GATHER_BUFFERS
    n_chunks = r_per_w // chunk
    mesh = plsc.VectorSubcoreMesh(core_axis_name="c", subcore_axis_name="s")

    @functools.partial(
        pl.kernel, mesh=mesh,
        out_type=jax.ShapeDtypeStruct((n_rows, dp), table.dtype),
        scratch_types=[pltpu.VMEM((n_chunks, chunk), I32),
                       pltpu.VMEM((nbuf, chunk, dp), table.dtype),
                       pltpu.SemaphoreType.DMA((nbuf,)),
                       pltpu.SemaphoreType.DMA((nbuf,))],
    )
    def gather_rows(table_hbm, idx_hbm, out_hbm, idx_v, rows_v, gat_sem, store_sem):
        wid = _sc_worker_id()
        base = wid * r_per_w
        pltpu.sync_copy(idx_hbm.at[wid], idx_v)

        def gather(c, b):
            return pltpu.make_async_copy(table_hbm.at[idx_v.at[c]], rows_v.at[b], gat_sem.at[b])

        def store(c, b):
            return pltpu.make_async_copy(rows_v.at[b], out_hbm.at[pl.ds(base + c * chunk, chunk)], store_sem.at[b])

        for b in range(nbuf - 1):
            gather(b, b).start()

        @pl.loop(0, n_chunks, step=nbuf)
        def _(c0):
            for b in range(nbuf):
                c = c0 + b
                gather(c, b).wait()
                store(c, b).start()
                nb = (b + nbuf - 1) % nbuf

                @pl.when(c + nbuf - 1 < n_chunks)
                def _():
                    @pl.when(c >= 1)
                    def _():
                        store(c - 1, nb).wait()
                    gather(c + nbuf - 1, nb).start()

        for b in range(nbuf):
            store(n_chunks - nbuf + b, b).wait()

    return gather_rows(table, idx.reshape(SC_WORKERS, n_chunks, chunk))


def _experts_kernel(tile_ref, exp_ref, lo_ref, hi_ref, nv_ref, xs_ref, wg_ref, wu_ref, wd_ref, ys_ref,
                    wgb, wub, wdb):
    it = pl.program_id(0)
    valid = it < nv_ref[0]

    @pl.when(valid & ((it == 0) | (exp_ref[it] != exp_ref[jnp.maximum(it - 1, 0)])))
    def _():
        wgb[...] = wg_ref[0, 0].astype(BF16)
        wub[...] = wu_ref[0, 0].astype(BF16)
        wdb[...] = wd_ref[0, 0].astype(BF16)

    @pl.when(valid)
    def _():
        tm, half = xs_ref.shape
        a, b = _unpack_halves(xs_ref[...])
        x = jnp.concatenate([a.astype(BF16), b.astype(BF16)], axis=1)
        gate = jnp.dot(x, wgb[...], preferred_element_type=F32)
        up = jnp.dot(x, wub[...], preferred_element_type=F32)
        hid = (gate * _sigmoid(gate) * up).astype(BF16)
        y = _pack_halves(jnp.dot(hid, wdb[...], preferred_element_type=F32))
        lo = lo_ref[it]

        @pl.when(lo == 0)
        def _():
            ys_ref[...] = y

        @pl.when(lo > 0)
        def _():
            row = lax.broadcasted_iota(I32, (tm, half), 0)
            ys_ref[...] = jnp.where((row >= lo) & (row < hi_ref[it]), y, ys_ref[...])


def _experts(items, xs, w_gate, w_up, w_down, layer, tm):
    n_rows, dp = xs.shape
    _, ne, d, ff = w_gate.shape
    n_items = items[0].shape[0]

    def row_map(i, tl, ex, lo, hi, nv):
        return (tl[jnp.minimum(i, nv[0] - 1)], 0)

    def w_map(i, tl, ex, lo, hi, nv):
        return (layer, ex[jnp.minimum(i, nv[0] - 1)], 0, 0)

    return pl.pallas_call(
        _experts_kernel,
        out_shape=jax.ShapeDtypeStruct((n_rows, dp), U32),
        grid_spec=pltpu.PrefetchScalarGridSpec(
            num_scalar_prefetch=5,
            grid=(n_items,),
            in_specs=[pl.BlockSpec((tm, dp), row_map),
                      pl.BlockSpec((1, 1, d, ff), w_map),
                      pl.BlockSpec((1, 1, d, ff), w_map),
                      pl.BlockSpec((1, 1, ff, d), w_map)],
            out_specs=pl.BlockSpec((tm, dp), row_map),
            scratch_shapes=[pltpu.VMEM((d, ff), BF16), pltpu.VMEM((d, ff), BF16), pltpu.VMEM((ff, d), BF16)]),
        compiler_params=_params("arbitrary"),
        name="experts",
    )(*items, xs, w_gate, w_up, w_down)


def _combine_kernel(x_ref, wk_ref, yk_ref, wsg_ref, wsu_ref, wsd_ref, g_ref, b_ref, x2_ref, x2b_ref, *, alpha):
    x = x_ref[...]
    xb = x.astype(BF16)
    gate = jnp.dot(xb, wsg_ref[...], preferred_element_type=F32)
    up = jnp.dot(xb, wsu_ref[...], preferred_element_type=F32)
    hid = (gate * _sigmoid(gate) * up).astype(BF16)
    shared = jnp.dot(hid, wsd_ref[...], preferred_element_type=F32)

    tc, half = yk_ref.shape[1:]
    wk = wk_ref[...]
    acc_hi = jnp.zeros((tc, half), F32)
    acc_lo = jnp.zeros((tc, half), F32)
    for j in range(TOP_K):
        hi, lo = _unpack_halves(yk_ref[j])
        wj = wk[:, j:j + 1]
        acc_hi += wj * hi
        acc_lo += wj * lo
    ffn = jnp.concatenate([acc_hi, acc_lo], axis=1) + shared
    y = _layer_norm_rows(alpha * x + ffn, g_ref[...], b_ref[...])
    x2_ref[...] = y
    x2b_ref[...] = y.astype(BF16)


def _combine(x1, wk, yk, ws_gate, ws_up, ws_down, ln_g, ln_b, alpha):
    t, d = x1.shape
    dp = yk.shape[2]
    ff = ws_gate.shape[1]
    tc = min(t, 256)
    const = lambda i: (0, 0)
    rows = lambda i: (i, 0)
    return pl.pallas_call(
        functools.partial(_combine_kernel, alpha=alpha),
        out_shape=(jax.ShapeDtypeStruct((t, d), F32), jax.ShapeDtypeStruct((t, d), BF16)),
        grid=(t // tc,),
        in_specs=[pl.BlockSpec((tc, d), rows),
                  pl.BlockSpec((tc, TOP_K), rows),
                  pl.BlockSpec((TOP_K, tc, dp), lambda i: (0, i, 0)),
                  pl.BlockSpec((d, ff), const),
                  pl.BlockSpec((d, ff), const),
                  pl.BlockSpec((ff, d), const),
                  pl.BlockSpec((1, d), const),
                  pl.BlockSpec((1, d), const)],
        out_specs=(pl.BlockSpec((tc, d), rows), pl.BlockSpec((tc, d), rows)),
        compiler_params=_params("parallel"),
        name="combine",
    )(x1, wk, yk, ws_gate, ws_up, ws_down, ln_g, ln_b)


def _expert_work_items(counts, n_rows, tm):
    ne = counts.shape[0]
    end = jnp.cumsum(counts)
    start = end - counts
    first_tile = start // tm
    n_it = jnp.where(counts > 0, (end - 1) // tm - first_tile + 1, 0)
    it_end = jnp.cumsum(n_it)
    it_start = it_end - n_it
    n_items = n_rows // tm + ne
    k = jnp.arange(n_items, dtype=I32)
    expert = jnp.minimum(jnp.sum((it_end[None, :] <= k[:, None]).astype(I32), axis=1), ne - 1)
    onehot = expert[:, None] == jnp.arange(ne, dtype=I32)[None, :]
    pick = lambda v: jnp.sum(jnp.where(onehot, v[None, :], 0), axis=1)
    tile = jnp.clip(pick(first_tile) + k - pick(it_start), 0, n_rows // tm - 1)
    lo = jnp.maximum(pick(start) - tile * tm, 0)
    hi = jnp.minimum(pick(end) - tile * tm, tm)
    return start.astype(I32), (tile.astype(I32), expert.astype(I32), lo.astype(I32), hi.astype(I32),
                               it_end[-1:].astype(I32))


def _moe(x1, x1p, w_router_t, bias_col, w_gate, w_up, w_down, layer, ws_gate, ws_up, ws_down, ln_g, ln_b, alpha):
    t = x1.shape[0]
    tm = 256
    rank, eid, wgt, cnt = _router(x1, w_router_t, bias_col)
    expert_start, items = _expert_work_items(cnt[:, 0], t * TOP_K, tm)
    pos = rank
    for e in range(N_EXPERTS):
        pos = pos + jnp.where(eid == e, expert_start[e], 0)
    xs = _dispatch(pos, x1p)
    ys = _experts(items, xs, w_gate, w_up, w_down, layer, tm)
    yk = _gather_rows(ys, pos.reshape(-1)).reshape(TOP_K, t, -1)
    return _combine(x1, wgt.T, yk, ws_gate, ws_up, ws_down, ln_g, ln_b, alpha)


def kernel(x, positions, w_in, lb_logits, hg_norm_g, w_hg_proj, w_att_proj, w_out, ln1_g, ln1_b,
           w_router, router_bias, w_e_gate, w_e_up, w_e_down, w_s_gate, w_s_up, w_s_down, ln2_g, ln2_b):
    batch, seq, d = x.shape
    depth = w_in.shape[0]
    t = batch * seq
    alpha = float((2 * depth) ** 0.25)
    hg_width = HG_HEADS * HG_DIM
    att_col0 = 4 * hg_width
    gate_col0 = att_col0 + 3 * len(ATT_GROUPS) * ATT_HEADS * ATT_DIM

    p = jax.nn.softmax(lb_logits.astype(F32), axis=0)
    cs = jnp.cumsum(p, axis=0)
    lower = cs - cs[:1]
    log1m_lower = jnp.log1p(-lower)

    cos, sin = _rope_tables(positions)
    xf = x.reshape(t, d)
    xb = xf.astype(BF16)
    in_width = w_in.shape[2]
    tn = 1280 if in_width % 1280 == 0 else 512
    for layer in range(depth):
        h = _matmul(xb, w_in, layer, min(t, 1024), tn, BF16)
        o_h = _hgrn(h, lower[layer].reshape(1, -1), log1m_lower[layer].reshape(1, -1),
                    hg_norm_g[layer].reshape(1, -1), batch, seq)
        o_a = _attention(h, cos, sin, batch, seq, att_col0)
        x1, x1p = _mix_out(xf, h, o_h, o_a, w_hg_proj[layer].astype(BF16), w_att_proj[layer].astype(BF16),
                           w_out[layer].astype(BF16), ln1_g[layer].reshape(1, d), ln1_b[layer].reshape(1, d),
                           gate_col0, alpha)
        xf, xb = _moe(x1, x1p, w_router[layer].T, router_bias[layer].reshape(-1, 1),
                      w_e_gate, w_e_up, w_e_down, layer,
                      w_s_gate[layer].astype(BF16), w_s_up[layer].astype(BF16), w_s_down[layer].astype(BF16),
                      ln2_g[layer].reshape(1, d), ln2_b[layer].reshape(1, d), alpha)
    return xf.reshape(batch, seq, d)
```

```python
import functools

import numpy as np
import jax
import jax.numpy as jnp
from jax import lax
from jax.experimental import pallas as pl
from jax.experimental.pallas import tpu as pltpu
from jax.experimental.pallas import tpu_sc as plsc

F32 = jnp.float32
BF16 = jnp.bfloat16
I32 = jnp.int32

LANES = 128
VMEM_LIMIT_BYTES = 56 * 1024 * 1024
SC_CORES = 2
SC_SUBCORES = 16
SC_WORKERS = SC_CORES * SC_SUBCORES
SC_SCATTER_ROWS = 32
SC_GATHER_ROWS = 16
SC_GATHER_BUFFERS = 4
MOE_TOKEN_RANGES = 2

HG_HEADS = 8
HG_DIM = 128
HG_CHUNK = 64
HG_SUB = 16
ATT_GROUPS = ((128, 1), (512, 4), (2048, 16))
ATT_HEADS = 4
ATT_DIM = 128
ATT_BACK = 128
ROPE_THETA = 10000.0
N_EXPERTS = 64
N_GROUPS = 8
TOPK_GROUPS = 4
TOP_K = 8
ROUTED_SCALE = 2.5
LN_EPS = 1e-5
NORM_EPS = 1e-6

NT_DIMS = (((1,), (1,)), ((), ()))
TN_DIMS = (((0,), (0,)), ((), ()))


def _params(*sem):
    return pltpu.CompilerParams(dimension_semantics=sem, vmem_limit_bytes=VMEM_LIMIT_BYTES)


def _sigmoid(x):
    return 1.0 / (1.0 + jnp.exp(-x))


def _matmul_kernel(x_ref, w_ref, o_ref, wb_ref):
    @pl.when(pl.program_id(1) == 0)
    def _():
        wb_ref[...] = w_ref[0].astype(BF16)

    o_ref[...] = jnp.dot(x_ref[...], wb_ref[...], preferred_element_type=F32).astype(o_ref.dtype)


def _matmul(x, w, layer, tm, tn, out_dtype):
    m, k = x.shape
    n = w.shape[2]
    return pl.pallas_call(
        _matmul_kernel,
        out_shape=jax.ShapeDtypeStruct((m, n), out_dtype),
        grid=(n // tn, m // tm),
        in_specs=[pl.BlockSpec((tm, k), lambda j, i: (i, 0)),
                  pl.BlockSpec((1, k, tn), lambda j, i: (layer, 0, j))],
        out_specs=pl.BlockSpec((tm, tn), lambda j, i: (i, j)),
        scratch_shapes=[pltpu.VMEM((k, tn), BF16)],
        compiler_params=_params("parallel", "arbitrary"),
        name="in_proj",
    )(x, w)


def _rope_table_kernel(pos_ref, inv_ref, sign_ref, cos_ref, sin_ref):
    ang = pos_ref[...].astype(F32) * inv_ref[...]
    cos_ref[...] = jnp.cos(ang)
    sin_ref[...] = jnp.sin(ang) * sign_ref[...]


def _rope_tables(positions):
    t = positions.size
    half = ATT_DIM // 2
    inv_half = ROPE_THETA ** (-np.arange(half, dtype=np.float32) * np.float32(2.0) / np.float32(ATT_DIM))
    inv = jnp.asarray(np.concatenate([inv_half, inv_half]).astype(np.float32).reshape(1, ATT_DIM))
    sign = jnp.asarray(np.concatenate([-np.ones(half), np.ones(half)]).astype(np.float32).reshape(1, ATT_DIM))
    tm = min(t, 2048)
    return pl.pallas_call(
        _rope_table_kernel,
        out_shape=(jax.ShapeDtypeStruct((t, ATT_DIM), F32), jax.ShapeDtypeStruct((t, ATT_DIM), F32)),
        grid=(t // tm,),
        in_specs=[pl.BlockSpec((tm, 1), lambda i: (i, 0)),
                  pl.BlockSpec((1, ATT_DIM), lambda i: (0, 0)),
                  pl.BlockSpec((1, ATT_DIM), lambda i: (0, 0))],
        out_specs=(pl.BlockSpec((tm, ATT_DIM), lambda i: (i, 0)),
                   pl.BlockSpec((tm, ATT_DIM), lambda i: (i, 0))),
        compiler_params=_params("parallel"),
        name="rope_tables",
    )(positions.reshape(t, 1), inv, sign)


def _cumsum_rows(x):
    n = x.shape[0]
    row = lax.broadcasted_iota(I32, x.shape, 0)
    s = 1
    while s < n:
        x = x + jnp.where(row >= s, pltpu.roll(x, s, 0), 0.0)
        s *= 2
    return x


def _rows_from(cum, offsets):
    parts = []
    for o in offsets:
        if o is None:
            parts.append(jnp.zeros((HG_SUB, cum.shape[1]), F32))
        else:
            parts.append(jnp.broadcast_to(cum[o:o + 1, :], (HG_SUB, cum.shape[1])))
    return jnp.concatenate(parts, axis=0)


def _hgrn_kernel(hq_ref, hf_ref, hi_ref, hg_ref, lb_ref, l1m_ref, ng_ref, o_ref, st_ref, *, n_chunks):
    c = HG_CHUNK
    nsub = c // HG_SUB

    @pl.when(pl.program_id(2) == 0)
    def _():
        st_ref[...] = jnp.zeros_like(st_ref)

    lb = lb_ref[...]
    one_m_lb = 1.0 - lb
    log1m_lb = l1m_ref[...]
    norm_g = ng_ref[...]
    row = lax.broadcasted_iota(I32, (c, HG_DIM), 0)
    blk = row // HG_SUB
    ti = lax.broadcasted_iota(I32, (c, c), 0)
    si = lax.broadcasted_iota(I32, (c, c), 1)
    diag_mask = ((ti // HG_SUB) == (si // HG_SUB)) & (si <= ti)

    def chunk(ci, carry):
        r0 = pl.multiple_of(ci * c, c)
        z = hf_ref[pl.ds(r0, c), :].astype(F32)
        qraw = hq_ref[pl.ds(r0, c), :].astype(F32)
        v = hi_ref[pl.ds(r0, c), :]
        g = hg_ref[pl.ds(r0, c), :].astype(F32)

        e = jnp.exp(-jnp.abs(z))
        r = 1.0 / (1.0 + e)
        sig = jnp.where(z >= 0, r, e * r)
        sig_neg = jnp.where(z >= 0, e * r, r)
        log_sig = jnp.minimum(z, 0.0) - jnp.log(1.0 + e)
        log_f = jnp.maximum(jnp.log(lb + one_m_lb * sig), log1m_lb + log_sig)
        k = one_m_lb * sig_neg
        q = qraw * _sigmoid(qraw)

        cum = _cumsum_rows(log_f)
        start = _rows_from(cum, [None] + [HG_SUB * i - 1 for i in range(1, nsub)])
        mid = _rows_from(cum, [HG_SUB * i + HG_SUB // 2 - 1 for i in range(nsub)])
        end = cum[c - 1:c, :]

        qd = (q * jnp.exp(cum - mid)).astype(BF16)
        kd = (k * jnp.exp(mid - cum)).astype(BF16)
        att = jnp.where(diag_mask, lax.dot_general(qd, kd, NT_DIMS, preferred_element_type=F32), 0.0)

        qs = q * jnp.exp(cum - start)
        q_slots, k_slots = [], []
        for i in range(1, nsub):
            q_slots.append(jnp.where(blk == i, qs, 0.0).astype(BF16))
            n_rows = HG_SUB * i
            b_i = cum[n_rows - 1:n_rows, :]
            k_i = k[:n_rows] * jnp.exp(b_i - cum[:n_rows])
            k_slots.append(jnp.concatenate([k_i, jnp.zeros((c - n_rows, HG_DIM), F32)], axis=0).astype(BF16))
        att = att + lax.dot_general(jnp.concatenate(q_slots, axis=1), jnp.concatenate(k_slots, axis=1),
                                    NT_DIMS, preferred_element_type=F32)
        intra = jnp.dot(att.astype(BF16), v, preferred_element_type=F32)

        st = st_ref[...]
        inter = lax.dot_general((q * jnp.exp(cum)).astype(BF16), st.astype(BF16), NT_DIMS,
                                preferred_element_type=F32)
        k_end = (k * jnp.exp(end - cum)).astype(BF16)
        st_ref[...] = st * jnp.exp(end) + lax.dot_general(v, k_end, TN_DIMS, preferred_element_type=F32)

        o = inter + intra
        o = o * lax.rsqrt(jnp.mean(o * o, axis=-1, keepdims=True) + NORM_EPS) * norm_g
        o_ref[pl.ds(r0, c), :] = (o * _sigmoid(g)).astype(o_ref.dtype)
        return carry

    lax.fori_loop(0, n_chunks, chunk, 0, unroll=8)


def _hgrn(h, lb, log1m_lb, norm_g, batch, seq):
    t = h.shape[0]
    tb = min(seq, 512)
    nsb = seq // tb
    width = HG_HEADS * HG_DIM

    def col(off):
        return pl.BlockSpec((tb, HG_DIM), lambda b, hh, s: (b * nsb + s, off * HG_HEADS + hh))

    vec = pl.BlockSpec((1, HG_DIM), lambda b, hh, s: (0, hh))
    return pl.pallas_call(
        functools.partial(_hgrn_kernel, n_chunks=tb // HG_CHUNK),
        out_shape=jax.ShapeDtypeStruct((t, width), BF16),
        grid=(batch, HG_HEADS, nsb),
        in_specs=[col(0), col(1), col(2), col(3), vec, vec, vec],
        out_specs=pl.BlockSpec((tb, HG_DIM), lambda b, hh, s: (b * nsb + s, hh)),
        scratch_shapes=[pltpu.VMEM((HG_DIM, HG_DIM), F32)],
        compiler_params=_params("parallel", "parallel", "arbitrary"),
        name="hgrn2",
    )(h, h, h, h, lb, log1m_lb, norm_g)


def _attn_kernel(*refs, seq):
    qkv_refs = refs[:9]
    cos_ref, sin_ref, o_ref = refs[9:12]
    qf, kf, vf, og, lg = refs[12:]
    n_groups = len(ATT_GROUPS)
    scale = ATT_DIM ** -0.5
    rb = min(seq, 256)

    def rope_rows(i, carry):
        r0 = pl.multiple_of(i * rb, rb)
        cs = cos_ref[pl.ds(r0, rb), :]
        sn = sin_ref[pl.ds(r0, rb), :]
        for gi in range(n_groups):
            xq = qkv_refs[3 * gi][pl.ds(r0, rb), :].astype(F32)
            xk = qkv_refs[3 * gi + 1][pl.ds(r0, rb), :].astype(F32)
            qf[gi, pl.ds(r0, rb), :] = (xq * cs + pltpu.roll(xq, ATT_DIM // 2, 1) * sn) * scale
            kf[gi, pl.ds(r0, rb), :] = xk * cs + pltpu.roll(xk, ATT_DIM // 2, 1) * sn
            vf[gi, pl.ds(r0, rb), :] = qkv_refs[3 * gi + 2][pl.ds(r0, rb), :].astype(F32)
        return carry

    lax.fori_loop(0, seq // rb, rope_rows, 0)

    qb = ATT_BACK
    for gi, (_, dil) in enumerate(ATT_GROUPS):
        length = seq // dil
        nk = min(2 * qb, length)
        n_qblk = length // qb

        def block(it, carry, gi=gi, dil=dil, nk=nk, n_qblk=n_qblk):
            res = it // n_qblk
            iq = it % n_qblk
            q0 = iq * qb
            k0 = jnp.maximum(q0 - qb, 0)
            if dil == 1:
                q_rows = pl.ds(pl.multiple_of(q0, qb), qb)
                k_rows = pl.ds(pl.multiple_of(k0, qb), nk)
            else:
                q_rows = pl.ds(res + dil * q0, qb, stride=dil)
                k_rows = pl.ds(res + dil * k0, nk, stride=dil)
            q = qf[gi, q_rows, :].astype(BF16)
            k = kf[gi, k_rows, :].astype(BF16)
            v = vf[gi, k_rows, :].astype(BF16)
            s = lax.dot_general(q, k, NT_DIMS, preferred_element_type=F32)
            dist = (q0 + lax.broadcasted_iota(I32, (qb, nk), 0)) - (k0 + lax.broadcasted_iota(I32, (qb, nk), 1))
            s = jnp.where((dist >= 0) & (dist <= ATT_BACK), s, -jnp.inf)
            m = jnp.max(s, axis=-1, keepdims=True)
            p = jnp.exp(s - m)
            den = jnp.sum(p, axis=-1, keepdims=True)
            o = jnp.dot(p.astype(BF16), v, preferred_element_type=F32) / den
            og[gi, q_rows, :] = o
            lg[gi, q_rows, :] = jnp.broadcast_to(m + jnp.log(den), (qb, ATT_DIM))
            return carry

        lax.fori_loop(0, dil * n_qblk, block, 0, unroll=4)

    def merge_rows(i, carry):
        r0 = pl.multiple_of(i * rb, rb)
        ls = [lg[gi, pl.ds(r0, rb), :] for gi in range(n_groups)]
        m = functools.reduce(jnp.maximum, ls)
        ws = [jnp.exp(l - m) for l in ls]
        num = sum(w * og[gi, pl.ds(r0, rb), :] for gi, w in enumerate(ws))
        o_ref[pl.ds(r0, rb), :] = (num / sum(ws)).astype(o_ref.dtype)
        return carry

    lax.fori_loop(0, seq // rb, merge_rows, 0)


def _attention(h, cos, sin, batch, seq, col0):
    t = h.shape[0]
    n_groups = len(ATT_GROUPS)
    part = n_groups * ATT_HEADS * ATT_DIM
    blk0 = col0 // ATT_DIM
    in_specs = []
    for gi in range(n_groups):
        for p in range(3):
            off = blk0 + (p * part) // ATT_DIM + gi * ATT_HEADS
            in_specs.append(pl.BlockSpec((seq, ATT_DIM), lambda b, hh, off=off: (b, off + hh)))
    tab = pl.BlockSpec((seq, ATT_DIM), lambda b, hh: (b, 0))
    in_specs += [tab, tab]
    scr = pltpu.VMEM((n_groups, seq, ATT_DIM), F32)
    return pl.pallas_call(
        functools.partial(_attn_kernel, seq=seq),
        out_shape=jax.ShapeDtypeStruct((t, ATT_HEADS * ATT_DIM), BF16),
        grid=(batch, ATT_HEADS),
        in_specs=in_specs,
        out_specs=pl.BlockSpec((seq, ATT_DIM), lambda b, hh: (b, hh)),
        scratch_shapes=[scr, scr, scr, scr, scr],
        compiler_params=_params("parallel", "parallel"),
        name="dilated_attn",
    )(*([h] * 9), cos, sin)


def _layer_norm_rows(r, g, b):
    mu = jnp.mean(r, axis=-1, keepdims=True)
    d = r - mu
    var = jnp.mean(d * d, axis=-1, keepdims=True)
    return d * lax.rsqrt(var + LN_EPS) * g + b


def _pack_halves(y):
    n = y.shape[1] // 2
    return pltpu.pack_elementwise([y[:, :n], y[:, n:]], packed_dtype=BF16)


def _unpack_halves(w):
    first = pltpu.unpack_elementwise(w, index=0, packed_dtype=BF16, unpacked_dtype=F32)
    second = pltpu.unpack_elementwise(w, index=1, packed_dtype=BF16, unpacked_dtype=F32)
    return first, second


def _mix_out_kernel(*refs, alpha, n_col):
    x_ref, oh_ref, oa_ref = refs[:3]
    gh_refs = refs[3:3 + n_col]
    ga_refs = refs[3 + n_col:3 + 2 * n_col]
    whg_ref, wap_ref, wout_ref, g_ref, b_ref, x1_ref, x1p_ref = refs[3 + 2 * n_col:]
    tn = gh_refs[0].shape[1]
    oh = oh_ref[...]
    oa = oa_ref[...]
    merged = []
    for n in range(n_col):
        cols = slice(n * tn, (n + 1) * tn)
        y_h = jnp.dot(oh, whg_ref[:, cols], preferred_element_type=F32)
        y_a = jnp.dot(oa, wap_ref[:, cols], preferred_element_type=F32)
        m = _sigmoid(gh_refs[n][...].astype(F32)) * y_h + _sigmoid(ga_refs[n][...].astype(F32)) * y_a
        merged.append(m.astype(BF16))
    mix = jnp.dot(jnp.concatenate(merged, axis=1), wout_ref[...], preferred_element_type=F32)
    y = _layer_norm_rows(alpha * x_ref[...] + mix, g_ref[...], b_ref[...])
    x1_ref[...] = y
    x1p_ref[...] = _pack_halves(y)


def _mix_out(x, h, o_h, o_a, w_hg, w_ap, w_out, ln_g, ln_b, gate_col0, alpha):
    t, d = x.shape
    tm = min(t, 256)
    tn = 512
    n_col = d // tn
    gh0 = gate_col0 // tn
    ga0 = (gate_col0 + d) // tn
    rows = lambda i: (i, 0)
    const = lambda i: (0, 0)
    resident = dict(index_map=const, pipeline_mode=pl.Buffered(1))
    gate_specs = [pl.BlockSpec((tm, tn), lambda i, c=c0 + n: (i, c)) for c0 in (gh0, ga0) for n in range(n_col)]
    return pl.pallas_call(
        functools.partial(_mix_out_kernel, alpha=alpha, n_col=n_col),
        out_shape=(jax.ShapeDtypeStruct((t, d), F32), jax.ShapeDtypeStruct((t, d // 2), I32)),
        grid=(t // tm,),
        in_specs=[pl.BlockSpec((tm, d), rows),
                  pl.BlockSpec((tm, o_h.shape[1]), rows),
                  pl.BlockSpec((tm, o_a.shape[1]), rows),
                  *gate_specs,
                  pl.BlockSpec(w_hg.shape, **resident),
                  pl.BlockSpec(w_ap.shape, **resident),
                  pl.BlockSpec(w_out.shape, **resident),
                  pl.BlockSpec((1, d), const),
                  pl.BlockSpec((1, d), const)],
        out_specs=(pl.BlockSpec((tm, d), rows), pl.BlockSpec((tm, d // 2), rows)),
        compiler_params=_params("parallel"),
        name="mix_out",
    )(x, o_h, o_a, *([h] * (2 * n_col)), w_hg, w_ap, w_out, ln_g, ln_b)


def _split_bf16(x):
    hi = x.astype(BF16)
    lo = (x - hi.astype(F32)).astype(BF16)
    return hi, lo


def _router_kernel(x_ref, wr_ref, bias_ref, tri_ref, low_ref, rank_ref, eid_ref, wgt_ref, cnt_ref,
                   sel_scr, carry_ref, *, tr):
    ne = N_EXPERTS
    per = ne // N_GROUPS

    @pl.when(pl.program_id(0) == 0)
    def _():
        carry_ref[...] = jnp.zeros_like(carry_ref)

    xh, xl = _split_bf16(x_ref[...])
    wh, wl = _split_bf16(wr_ref[...])
    logits = (lax.dot_general(wh, xh, NT_DIMS, preferred_element_type=F32)
              + lax.dot_general(wh, xl, NT_DIMS, preferred_element_type=F32)
              + lax.dot_general(wl, xh, NT_DIMS, preferred_element_type=F32))
    scores = _sigmoid(logits)
    sel = scores + bias_ref[...]

    grp = sel.reshape(N_GROUPS, per, tr)
    sub = lax.broadcasted_iota(I32, grp.shape, 1)
    m1 = jnp.max(grp, axis=1, keepdims=True)
    first = jnp.min(jnp.where(grp == m1, sub, per), axis=1, keepdims=True)
    m2 = jnp.max(jnp.where(sub == first, -jnp.inf, grp), axis=1, keepdims=True)
    gs = m1 + m2
    gidx = lax.broadcasted_iota(I32, gs.shape, 0)
    grank = jnp.zeros(gs.shape, I32)
    for j in range(N_GROUPS):
        other = gs[j:j + 1]
        grank += ((other > gs) | ((other == gs) & (j < gidx))).astype(I32)
    masked = jnp.where(grank < TOPK_GROUPS, grp, -jnp.inf).reshape(ne, tr)

    sel_scr[...] = masked
    eidx = lax.broadcasted_iota(I32, (ne, tr), 0)

    def rank_step(j, erank):
        other = sel_scr[pl.ds(j, 1), :]
        return erank + ((other > masked) | ((other == masked) & (j < eidx))).astype(I32)

    erank = lax.fori_loop(0, ne, rank_step, jnp.zeros((ne, tr), I32))
    chosen = erank < TOP_K
    w = jnp.where(chosen, scores, 0.0)
    gates = w / jnp.sum(w, axis=0, keepdims=True) * ROUTED_SCALE

    chosen_b = jnp.where(chosen, 1.0, 0.0).astype(BF16)
    incl = jnp.dot(chosen_b, tri_ref[...], preferred_element_type=F32)
    carry = carry_ref[...]
    rank_in_expert = (carry + incl - 1.0).astype(I32)
    carry_new = carry + incl[:, tr - 1:tr]
    carry_ref[...] = carry_new
    cnt_ref[...] = jnp.broadcast_to(carry_new, cnt_ref.shape).astype(I32)

    slot = jnp.dot(low_ref[...], chosen_b, preferred_element_type=F32).astype(I32)
    for j in range(TOP_K):
        pick = chosen & (slot == j)
        rank_ref[pl.ds(j, 1), :] = jnp.sum(jnp.where(pick, rank_in_expert, 0), axis=0, keepdims=True)
        eid_ref[pl.ds(j, 1), :] = jnp.sum(jnp.where(pick, eidx, 0), axis=0, keepdims=True)
        wgt_ref[pl.ds(j, 1), :] = jnp.sum(jnp.where(pick, gates, 0.0), axis=0, keepdims=True)


def _router(x1, w_router_t, bias_col, row0, t):
    d = x1.shape[1]
    ne = N_EXPERTS
    tr = min(t, 512)
    blk0 = row0 // tr
    tri = jnp.asarray(np.triu(np.ones((tr, tr), np.float32)), BF16)
    low = jnp.asarray(np.tril(np.ones((ne, ne), np.float32), -1), BF16)
    slot_shape = jax.ShapeDtypeStruct((TOP_K, t), I32)
    slot_spec = pl.BlockSpec((TOP_K, tr), lambda i: (0, i))
    return pl.pallas_call(
        functools.partial(_router_kernel, tr=tr),
        out_shape=(slot_shape, slot_shape, jax.ShapeDtypeStruct((TOP_K, t), F32),
                   jax.ShapeDtypeStruct((ne, LANES), I32)),
        grid=(t // tr,),
        in_specs=[pl.BlockSpec((tr, d), lambda i: (blk0 + i, 0)),
                  pl.BlockSpec((ne, d), lambda i: (0, 0)),
                  pl.BlockSpec((ne, 1), lambda i: (0, 0)),
                  pl.BlockSpec((tr, tr), lambda i: (0, 0)),
                  pl.BlockSpec((ne, ne), lambda i: (0, 0))],
        out_specs=(slot_spec, slot_spec, slot_spec, pl.BlockSpec((ne, LANES), lambda i: (0, 0))),
        scratch_shapes=[pltpu.VMEM((ne, tr), F32), pltpu.VMEM((ne, 1), F32)],
        compiler_params=_params("arbitrary"),
        name="router",
    )(x1, w_router_t, bias_col, tri, low)


def _sc_worker_id():
    return lax.axis_index("s") * SC_CORES + lax.axis_index("c")


def _dispatch(pos, x1p, row0):
    t = pos.shape[1]
    dp = x1p.shape[1]
    t_per_w = t // SC_WORKERS
    chunk = min(SC_SCATTER_ROWS, t_per_w // 2)
    n_chunks = t_per_w // chunk
    mesh = plsc.VectorSubcoreMesh(core_axis_name="c", subcore_axis_name="s")

    @functools.partial(
        pl.kernel, mesh=mesh,
        out_type=jax.ShapeDtypeStruct((t * TOP_K, dp), x1p.dtype),
        scratch_types=[pltpu.VMEM((n_chunks * TOP_K, chunk), I32),
                       pltpu.VMEM((2, chunk, dp), x1p.dtype),
                       pltpu.SemaphoreType.DMA((2,)),
                       pltpu.SemaphoreType.DMA((2,))],
    )
    def scatter_rows(x_hbm, pos_hbm, out_hbm, idx_v, rows_v, load_sem, scat_sem):
        wid = _sc_worker_id()
        base = row0 + wid * t_per_w
        pltpu.sync_copy(pos_hbm.at[wid], idx_v)

        def load(c, b):
            return pltpu.make_async_copy(x_hbm.at[pl.ds(base + c * chunk, chunk)], rows_v.at[b], load_sem.at[b])

        def scat(c, b, j):
            return pltpu.make_async_copy(rows_v.at[b], out_hbm.at[idx_v.at[c * TOP_K + j]], scat_sem.at[b])

        load(0, 0).start()

        @pl.loop(0, n_chunks, step=2)
        def _(c0):
            for b in range(2):
                c = c0 + b
                load(c, b).wait()
                for j in range(TOP_K):
                    scat(c, b, j).start()

                @pl.when(c + 1 < n_chunks)
                def _():
                    @pl.when(c >= 1)
                    def _():
                        for j in range(TOP_K):
                            scat(c - 1, 1 - b, j).wait()
                    load(c + 1, 1 - b).start()

        for b in range(2):
            for j in range(TOP_K):
                scat(n_chunks - 2 + b, b, j).wait()

    idx = pos.reshape(TOP_K, SC_WORKERS, n_chunks, chunk).transpose(1, 2, 0, 3)
    return scatter_rows(x1p, idx.reshape(SC_WORKERS, n_chunks * TOP_K, chunk))


def _gather_rows(table, idx):
    n_rows = idx.shape[0]
    dp = table.shape[1]
    r_per_w = n_rows // SC_WORKERS
    chunk, nbuf = SC_GATHER_ROWS, SC_GATHER_BUFFERS
    n_chunks = r_per_w // chunk
    mesh = plsc.VectorSubcoreMesh(core_axis_name="c", subcore_axis_name="s")

    @functools.partial(
        pl.kernel, mesh=mesh,
        out_type=jax.ShapeDtypeStruct((n_rows, dp), table.dtype),
        scratch_types=[pltpu.VMEM((n_chunks, chunk), I32),
                       pltpu.VMEM((nbuf, chunk, dp), table.dtype),
                       pltpu.SemaphoreType.DMA((nbuf,)),
                       pltpu.SemaphoreType.DMA((nbuf,))],
    )
    def gather_rows(table_hbm, idx_hbm, out_hbm, idx_v, rows_v, gat_sem, store_sem):
        wid = _sc_worker_id()
        base = wid * r_per_w
        pltpu.sync_copy(idx_hbm.at[wid], idx_v)

        def gather(c, b):
            return pltpu.make_async_copy(table_hbm.at[idx_v.at[c]], rows_v.at[b], gat_sem.at[b])

        def store(c, b):
            return pltpu.make_async_copy(rows_v.at[b], out_hbm.at[pl.ds(base + c * chunk, chunk)], store_sem.at[b])

        for b in range(nbuf - 1):
            gather(b, b).start()

        @pl.loop(0, n_chunks, step=nbuf)
        def _(c0):
            for b in range(nbuf):
                c = c0 + b
                gather(c, b).wait()
                store(c, b).start()
                nb = (b + nbuf - 1) % nbuf

                @pl.when(c + nbuf - 1 < n_chunks)
                def _():
                    @pl.when(c >= 1)
                    def _():
                        store(c - 1, nb).wait()
                    gather(c + nbuf - 1, nb).start()

        for b in range(nbuf):
            store(n_chunks - nbuf + b, b).wait()

    return gather_rows(table, idx.reshape(SC_WORKERS, n_chunks, chunk))


def _experts_kernel(tile_ref, exp_ref, lo_ref, hi_ref, nv_ref, xs_ref, wg_ref, wu_ref, wd_ref, ys_ref,
                    wgb, wub, wdb):
    it = pl.program_id(0)
    valid = it < nv_ref[0]

    @pl.when(valid & ((it == 0) | (exp_ref[it] != exp_ref[jnp.maximum(it - 1, 0)])))
    def _():
        wgb[...] = wg_ref[0, 0].astype(BF16)
        wub[...] = wu_ref[0, 0].astype(BF16)
        wdb[...] = wd_ref[0, 0].astype(BF16)

    @pl.when(valid)
    def _():
        tm, half = xs_ref.shape
        a, b = _unpack_halves(xs_ref[...])
        x = jnp.concatenate([a.astype(BF16), b.astype(BF16)], axis=1)
        gate = jnp.dot(x, wgb[...], preferred_element_type=F32)
        up = jnp.dot(x, wub[...], preferred_element_type=F32)
        hid = (gate * _sigmoid(gate) * up).astype(BF16)
        y = _pack_halves(jnp.dot(hid, wdb[...], preferred_element_type=F32))
        lo = lo_ref[it]

        @pl.when(lo == 0)
        def _():
            ys_ref[...] = y

        @pl.when(lo > 0)
        def _():
            row = lax.broadcasted_iota(I32, (tm, half), 0)
            ys_ref[...] = jnp.where((row >= lo) & (row < hi_ref[it]), y, ys_ref[...])


def _experts(items, xs, w_gate, w_up, w_down, layer, tm):
    n_rows, dp = xs.shape
    _, ne, d, ff = w_gate.shape
    n_items = items[0].shape[0]

    def row_map(i, tl, ex, lo, hi, nv):
        return (tl[jnp.minimum(i, nv[0] - 1)], 0)

    def w_map(i, tl, ex, lo, hi, nv):
        return (layer, ex[jnp.minimum(i, nv[0] - 1)], 0, 0)

    return pl.pallas_call(
        _experts_kernel,
        out_shape=jax.ShapeDtypeStruct((n_rows, dp), I32),
        grid_spec=pltpu.PrefetchScalarGridSpec(
            num_scalar_prefetch=5,
            grid=(n_items,),
            in_specs=[pl.BlockSpec((tm, dp), row_map),
                      pl.BlockSpec((1, 1, d, ff), w_map),
                      pl.BlockSpec((1, 1, d, ff), w_map),
                      pl.BlockSpec((1, 1, ff, d), w_map)],
            out_specs=pl.BlockSpec((tm, dp), row_map),
            scratch_shapes=[pltpu.VMEM((d, ff), BF16), pltpu.VMEM((d, ff), BF16), pltpu.VMEM((ff, d), BF16)]),
        compiler_params=_params("arbitrary"),
        name="experts",
    )(*items, xs, w_gate, w_up, w_down)


def _combine_kernel(*refs, alpha):
    x_ref, wk_ref, yk_ref, wsg_ref, wsu_ref, wsd_ref, g_ref, b_ref = refs[:8]
    x2_ref, x2b_ref = refs[-2:]
    x = x_ref[...]
    xb = x.astype(BF16)
    gate = jnp.dot(xb, wsg_ref[...], preferred_element_type=F32)
    up = jnp.dot(xb, wsu_ref[...], preferred_element_type=F32)
    hid = (gate * _sigmoid(gate) * up).astype(BF16)
    shared = jnp.dot(hid, wsd_ref[...], preferred_element_type=F32)

    tc, half = yk_ref.shape[1:]
    wk = wk_ref[...]
    acc_hi = jnp.zeros((tc, half), F32)
    acc_lo = jnp.zeros((tc, half), F32)
    for j in range(TOP_K):
        hi, lo = _unpack_halves(yk_ref[j])
        wj = wk[:, j:j + 1]
        acc_hi += wj * hi
        acc_lo += wj * lo
    ffn = jnp.concatenate([acc_hi, acc_lo], axis=1) + shared
    y = _layer_norm_rows(alpha * x + ffn, g_ref[...], b_ref[...])
    x2_ref[...] = y
    x2b_ref[...] = y.astype(BF16)


def _combine(x1, wk, yk, ws_gate, ws_up, ws_down, ln_g, ln_b, alpha, row0, prev):
    t_all, d = x1.shape
    t, dp = yk.shape[1:]
    ff = ws_gate.shape[1]
    tc = min(t, 256)
    blk0 = row0 // tc
    const = lambda i: (0, 0)
    rows = lambda i: (i, 0)
    rows_all = lambda i: (blk0 + i, 0)
    in_specs = [pl.BlockSpec((tc, d), rows_all),
                pl.BlockSpec((tc, TOP_K), rows),
                pl.BlockSpec((TOP_K, tc, dp), lambda i: (0, i, 0)),
                pl.BlockSpec((d, ff), const),
                pl.BlockSpec((d, ff), const),
                pl.BlockSpec((ff, d), const),
                pl.BlockSpec((1, d), const),
                pl.BlockSpec((1, d), const)]
    args = [x1, wk, yk, ws_gate, ws_up, ws_down, ln_g, ln_b]
    aliases = {}
    if prev is not None:
        in_specs += [pl.BlockSpec(memory_space=pl.ANY)] * 2
        aliases = {len(args): 0, len(args) + 1: 1}
        args += list(prev)
    return pl.pallas_call(
        functools.partial(_combine_kernel, alpha=alpha),
        out_shape=(jax.ShapeDtypeStruct((t_all, d), F32), jax.ShapeDtypeStruct((t_all, d), BF16)),
        grid=(t // tc,),
        in_specs=in_specs,
        out_specs=(pl.BlockSpec((tc, d), rows_all), pl.BlockSpec((tc, d), rows_all)),
        input_output_aliases=aliases,
        compiler_params=_params("parallel"),
        name="combine",
    )(*args)


def _expert_work_items(counts, n_rows, tm):
    ne = counts.shape[0]
    end = jnp.cumsum(counts)
    start = end - counts
    first_tile = start // tm
    n_it = jnp.where(counts > 0, (end - 1) // tm - first_tile + 1, 0)
    it_end = jnp.cumsum(n_it)
    it_start = it_end - n_it
    n_items = n_rows // tm + ne
    k = jnp.arange(n_items, dtype=I32)
    expert = jnp.minimum(jnp.sum((it_end[None, :] <= k[:, None]).astype(I32), axis=1), ne - 1)
    onehot = expert[:, None] == jnp.arange(ne, dtype=I32)[None, :]
    pick = lambda v: jnp.sum(jnp.where(onehot, v[None, :], 0), axis=1)
    tile = jnp.clip(pick(first_tile) + k - pick(it_start), 0, n_rows // tm - 1)
    lo = jnp.maximum(pick(start) - tile * tm, 0)
    hi = jnp.minimum(pick(end) - tile * tm, tm)
    return start.astype(I32), (tile.astype(I32), expert.astype(I32), lo.astype(I32), hi.astype(I32),
                               it_end[-1:].astype(I32))


def _moe(x1, x1p, w_router_t, bias_col, w_gate, w_up, w_down, layer, ws_gate, ws_up, ws_down, ln_g, ln_b, alpha):
    t_all = x1.shape[0]
    tm = 256
    t = t_all // MOE_TOKEN_RANGES
    out = None
    for part in range(MOE_TOKEN_RANGES):
        row0 = part * t
        rank, eid, wgt, cnt = _router(x1, w_router_t, bias_col, row0, t)
        expert_start, items = _expert_work_items(cnt[:, 0], t * TOP_K, tm)
        pos = rank
        for e in range(N_EXPERTS):
            pos = pos + jnp.where(eid == e, expert_start[e], 0)
        xs = _dispatch(pos, x1p, row0)
        ys = _experts(items, xs, w_gate, w_up, w_down, layer, tm)
        yk = _gather_rows(ys, pos.reshape(-1)).reshape(TOP_K, t, -1)
        out = _combine(x1, wgt.T, yk, ws_gate, ws_up, ws_down, ln_g, ln_b, alpha, row0, out)
    return out


def kernel(x, positions, w_in, lb_logits, hg_norm_g, w_hg_proj, w_att_proj, w_out, ln1_g, ln1_b,
           w_router, router_bias, w_e_gate, w_e_up, w_e_down, w_s_gate, w_s_up, w_s_down, ln2_g, ln2_b):
    batch, seq, d = x.shape
    depth = w_in.shape[0]
    t = batch * seq
    alpha = float((2 * depth) ** 0.25)
    hg_width = HG_HEADS * HG_DIM
    att_col0 = 4 * hg_width
    gate_col0 = att_col0 + 3 * len(ATT_GROUPS) * ATT_HEADS * ATT_DIM

    p = jax.nn.softmax(lb_logits.astype(F32), axis=0)
    cs = jnp.cumsum(p, axis=0)
    lower = cs - cs[:1]
    log1m_lower = jnp.log1p(-lower)

    cos, sin = _rope_tables(positions)
    xf = x.reshape(t, d)
    xb = xf.astype(BF16)
    in_width = w_in.shape[2]
    tn = 1280 if in_width % 1280 == 0 else 512
    for layer in range(depth):
        h = _matmul(xb, w_in, layer, min(t, 1024), tn, BF16)
        o_h = _hgrn(h, lower[layer].reshape(1, -1), log1m_lower[layer].reshape(1, -1),
                    hg_norm_g[layer].reshape(1, -1), batch, seq)
        o_a = _attention(h, cos, sin, batch, seq, att_col0)
        x1, x1p = _mix_out(xf, h, o_h, o_a, w_hg_proj[layer].astype(BF16), w_att_proj[layer].astype(BF16),
                           w_out[layer].astype(BF16), ln1_g[layer].reshape(1, d), ln1_b[layer].reshape(1, d),
                           gate_col0, alpha)
        xf, xb = _moe(x1, x1p, w_router[layer].T, router_bias[layer].reshape(-1, 1),
                      w_e_gate, w_e_up, w_e_down, layer,
                      w_s_gate[layer].astype(BF16), w_s_up[layer].astype(BF16), w_s_down[layer].astype(BF16),
                      ln2_g[layer].reshape(1, d), ln2_b[layer].reshape(1, d), alpha)
    return xf.reshape(batch, seq, d)
```

```python
import functools

import numpy as np
import jax
import jax.numpy as jnp
from jax import lax
from jax.experimental import pallas as pl
from jax.experimental.pallas import tpu as pltpu
from jax.experimental.pallas import tpu_sc as plsc

F32 = jnp.float32
BF16 = jnp.bfloat16
I32 = jnp.int32

LANES = 128
VMEM_LIMIT_BYTES = 56 * 1024 * 1024
SC_CORES = 2
SC_SUBCORES = 16
SC_WORKERS = SC_CORES * SC_SUBCORES
SC_SCATTER_ROWS = 32
SC_GATHER_ROWS = 16
SC_GATHER_BUFFERS = 4
MOE_TOKEN_RANGES = 1

HG_HEADS = 8
HG_DIM = 128
HG_CHUNK = 64
HG_SUB = 16
ATT_GROUPS = ((128, 1), (512, 4), (2048, 16))
ATT_HEADS = 4
ATT_DIM = 128
ATT_BACK = 128
ROPE_THETA = 10000.0
N_EXPERTS = 64
N_GROUPS = 8
TOPK_GROUPS = 4
TOP_K = 8
ROUTED_SCALE = 2.5
LN_EPS = 1e-5
NORM_EPS = 1e-6

NT_DIMS = (((1,), (1,)), ((), ()))
TN_DIMS = (((0,), (0,)), ((), ()))


def _params(*sem):
    return pltpu.CompilerParams(dimension_semantics=sem, vmem_limit_bytes=VMEM_LIMIT_BYTES)


def _sigmoid(x):
    return 1.0 / (1.0 + jnp.exp(-x))


def _matmul_kernel(x_ref, w_ref, o_ref, wb_ref):
    @pl.when(pl.program_id(1) == 0)
    def _():
        wb_ref[...] = w_ref[0].astype(BF16)

    o_ref[...] = jnp.dot(x_ref[...], wb_ref[...], preferred_element_type=F32).astype(o_ref.dtype)


def _matmul(x, w, layer, tm, tn, out_dtype):
    m, k = x.shape
    n = w.shape[2]
    return pl.pallas_call(
        _matmul_kernel,
        out_shape=jax.ShapeDtypeStruct((m, n), out_dtype),
        grid=(n // tn, m // tm),
        in_specs=[pl.BlockSpec((tm, k), lambda j, i: (i, 0)),
                  pl.BlockSpec((1, k, tn), lambda j, i: (layer, 0, j))],
        out_specs=pl.BlockSpec((tm, tn), lambda j, i: (i, j)),
        scratch_shapes=[pltpu.VMEM((k, tn), BF16)],
        compiler_params=_params("parallel", "arbitrary"),
        name="in_proj",
    )(x, w)


def _rope_table_kernel(pos_ref, inv_ref, sign_ref, cos_ref, sin_ref):
    ang = pos_ref[...].astype(F32) * inv_ref[...]
    cos_ref[...] = jnp.cos(ang)
    sin_ref[...] = jnp.sin(ang) * sign_ref[...]


def _rope_tables(positions):
    t = positions.size
    half = ATT_DIM // 2
    inv_half = ROPE_THETA ** (-np.arange(half, dtype=np.float32) * np.float32(2.0) / np.float32(ATT_DIM))
    inv = jnp.asarray(np.concatenate([inv_half, inv_half]).astype(np.float32).reshape(1, ATT_DIM))
    sign = jnp.asarray(np.concatenate([-np.ones(half), np.ones(half)]).astype(np.float32).reshape(1, ATT_DIM))
    tm = min(t, 2048)
    return pl.pallas_call(
        _rope_table_kernel,
        out_shape=(jax.ShapeDtypeStruct((t, ATT_DIM), F32), jax.ShapeDtypeStruct((t, ATT_DIM), F32)),
        grid=(t // tm,),
        in_specs=[pl.BlockSpec((tm, 1), lambda i: (i, 0)),
                  pl.BlockSpec((1, ATT_DIM), lambda i: (0, 0)),
                  pl.BlockSpec((1, ATT_DIM), lambda i: (0, 0))],
        out_specs=(pl.BlockSpec((tm, ATT_DIM), lambda i: (i, 0)),
                   pl.BlockSpec((tm, ATT_DIM), lambda i: (i, 0))),
        compiler_params=_params("parallel"),
        name="rope_tables",
    )(positions.reshape(t, 1), inv, sign)


def _cumsum_rows(x):
    n = x.shape[0]
    row = lax.broadcasted_iota(I32, x.shape, 0)
    s = 1
    while s < n:
        x = x + jnp.where(row >= s, pltpu.roll(x, s, 0), 0.0)
        s *= 2
    return x


def _rows_from(cum, offsets):
    parts = []
    for o in offsets:
        if o is None:
            parts.append(jnp.zeros((HG_SUB, cum.shape[1]), F32))
        else:
            parts.append(jnp.broadcast_to(cum[o:o + 1, :], (HG_SUB, cum.shape[1])))
    return jnp.concatenate(parts, axis=0)


def _hgrn_kernel(hq_ref, hf_ref, hi_ref, hg_ref, lb_ref, l1m_ref, ng_ref, o_ref, st_ref, *, n_chunks):
    c = HG_CHUNK
    nsub = c // HG_SUB

    @pl.when(pl.program_id(2) == 0)
    def _():
        st_ref[...] = jnp.zeros_like(st_ref)

    lb = lb_ref[...]
    one_m_lb = 1.0 - lb
    log1m_lb = l1m_ref[...]
    norm_g = ng_ref[...]
    row = lax.broadcasted_iota(I32, (c, HG_DIM), 0)
    blk = row // HG_SUB
    ti = lax.broadcasted_iota(I32, (c, c), 0)
    si = lax.broadcasted_iota(I32, (c, c), 1)
    diag_mask = ((ti // HG_SUB) == (si // HG_SUB)) & (si <= ti)

    def chunk(ci, carry):
        r0 = pl.multiple_of(ci * c, c)
        z = hf_ref[pl.ds(r0, c), :].astype(F32)
        qraw = hq_ref[pl.ds(r0, c), :].astype(F32)
        v = hi_ref[pl.ds(r0, c), :]
        g = hg_ref[pl.ds(r0, c), :].astype(F32)

        e = jnp.exp(-jnp.abs(z))
        r = 1.0 / (1.0 + e)
        sig = jnp.where(z >= 0, r, e * r)
        sig_neg = jnp.where(z >= 0, e * r, r)
        log_sig = jnp.minimum(z, 0.0) - jnp.log(1.0 + e)
        log_f = jnp.maximum(jnp.log(lb + one_m_lb * sig), log1m_lb + log_sig)
        k = one_m_lb * sig_neg
        q = qraw * _sigmoid(qraw)

        cum = _cumsum_rows(log_f)
        start = _rows_from(cum, [None] + [HG_SUB * i - 1 for i in range(1, nsub)])
        mid = _rows_from(cum, [HG_SUB * i + HG_SUB // 2 - 1 for i in range(nsub)])
        end = cum[c - 1:c, :]

        qd = (q * jnp.exp(cum - mid)).astype(BF16)
        kd = (k * jnp.exp(mid - cum)).astype(BF16)
        att = jnp.where(diag_mask, lax.dot_general(qd, kd, NT_DIMS, preferred_element_type=F32), 0.0)

        qs = q * jnp.exp(cum - start)
        q_slots, k_slots = [], []
        for i in range(1, nsub):
            q_slots.append(jnp.where(blk == i, qs, 0.0).astype(BF16))
            n_rows = HG_SUB * i
            b_i = cum[n_rows - 1:n_rows, :]
            k_i = k[:n_rows] * jnp.exp(b_i - cum[:n_rows])
            k_slots.append(jnp.concatenate([k_i, jnp.zeros((c - n_rows, HG_DIM), F32)], axis=0).astype(BF16))
        att = att + lax.dot_general(jnp.concatenate(q_slots, axis=1), jnp.concatenate(k_slots, axis=1),
                                    NT_DIMS, preferred_element_type=F32)
        intra = jnp.dot(att.astype(BF16), v, preferred_element_type=F32)

        st = st_ref[...]
        inter = lax.dot_general((q * jnp.exp(cum)).astype(BF16), st.astype(BF16), NT_DIMS,
                                preferred_element_type=F32)
        k_end = (k * jnp.exp(end - cum)).astype(BF16)
        st_ref[...] = st * jnp.exp(end) + lax.dot_general(v, k_end, TN_DIMS, preferred_element_type=F32)

        o = inter + intra
        o = o * lax.rsqrt(jnp.mean(o * o, axis=-1, keepdims=True) + NORM_EPS) * norm_g
        o_ref[pl.ds(r0, c), :] = (o * _sigmoid(g)).astype(o_ref.dtype)
        return carry

    lax.fori_loop(0, n_chunks, chunk, 0, unroll=8)


def _hgrn(h, lb, log1m_lb, norm_g, batch, seq):
    t = h.shape[0]
    tb = min(seq, 512)
    nsb = seq // tb
    width = HG_HEADS * HG_DIM

    def col(off):
        return pl.BlockSpec((tb, HG_DIM), lambda b, hh, s: (b * nsb + s, off * HG_HEADS + hh))

    vec = pl.BlockSpec((1, HG_DIM), lambda b, hh, s: (0, hh))
    return pl.pallas_call(
        functools.partial(_hgrn_kernel, n_chunks=tb // HG_CHUNK),
        out_shape=jax.ShapeDtypeStruct((t, width), BF16),
        grid=(batch, HG_HEADS, nsb),
        in_specs=[col(0), col(1), col(2), col(3), vec, vec, vec],
        out_specs=pl.BlockSpec((tb, HG_DIM), lambda b, hh, s: (b * nsb + s, hh)),
        scratch_shapes=[pltpu.VMEM((HG_DIM, HG_DIM), F32)],
        compiler_params=_params("parallel", "parallel", "arbitrary"),
        name="hgrn2",
    )(h, h, h, h, lb, log1m_lb, norm_g)


def _attn_kernel(*refs, seq):
    qkv_refs = refs[:9]
    cos_ref, sin_ref, o_ref = refs[9:12]
    qf, kf, vf, og, lg = refs[12:]
    n_groups = len(ATT_GROUPS)
    scale = ATT_DIM ** -0.5
    rb = min(seq, 256)

    def rope_rows(i, carry):
        r0 = pl.multiple_of(i * rb, rb)
        cs = cos_ref[pl.ds(r0, rb), :]
        sn = sin_ref[pl.ds(r0, rb), :]
        for gi in range(n_groups):
            xq = qkv_refs[3 * gi][pl.ds(r0, rb), :].astype(F32)
            xk = qkv_refs[3 * gi + 1][pl.ds(r0, rb), :].astype(F32)
            qf[gi, pl.ds(r0, rb), :] = (xq * cs + pltpu.roll(xq, ATT_DIM // 2, 1) * sn) * scale
            kf[gi, pl.ds(r0, rb), :] = xk * cs + pltpu.roll(xk, ATT_DIM // 2, 1) * sn
            vf[gi, pl.ds(r0, rb), :] = qkv_refs[3 * gi + 2][pl.ds(r0, rb), :].astype(F32)
        return carry

    lax.fori_loop(0, seq // rb, rope_rows, 0)

    qb = ATT_BACK
    for gi, (_, dil) in enumerate(ATT_GROUPS):
        length = seq // dil
        nk = min(2 * qb, length)
        n_qblk = length // qb

        def block(it, carry, gi=gi, dil=dil, nk=nk, n_qblk=n_qblk):
            res = it // n_qblk
            iq = it % n_qblk
            q0 = iq * qb
            k0 = jnp.maximum(q0 - qb, 0)
            if dil == 1:
                q_rows = pl.ds(pl.multiple_of(q0, qb), qb)
                k_rows = pl.ds(pl.multiple_of(k0, qb), nk)
            else:
                q_rows = pl.ds(res + dil * q0, qb, stride=dil)
                k_rows = pl.ds(res + dil * k0, nk, stride=dil)
            q = qf[gi, q_rows, :].astype(BF16)
            k = kf[gi, k_rows, :].astype(BF16)
            v = vf[gi, k_rows, :].astype(BF16)
            s = lax.dot_general(q, k, NT_DIMS, preferred_element_type=F32)
            dist = (q0 + lax.broadcasted_iota(I32, (qb, nk), 0)) - (k0 + lax.broadcasted_iota(I32, (qb, nk), 1))
            s = jnp.where((dist >= 0) & (dist <= ATT_BACK), s, -jnp.inf)
            m = jnp.max(s, axis=-1, keepdims=True)
            p = jnp.exp(s - m)
            den = jnp.sum(p, axis=-1, keepdims=True)
            o = jnp.dot(p.astype(BF16), v, preferred_element_type=F32) / den
            og[gi, q_rows, :] = o
            lg[gi, q_rows, :] = jnp.broadcast_to(m + jnp.log(den), (qb, ATT_DIM))
            return carry

        lax.fori_loop(0, dil * n_qblk, block, 0, unroll=4)

    def merge_rows(i, carry):
        r0 = pl.multiple_of(i * rb, rb)
        ls = [lg[gi, pl.ds(r0, rb), :] for gi in range(n_groups)]
        m = functools.reduce(jnp.maximum, ls)
        ws = [jnp.exp(l - m) for l in ls]
        num = sum(w * og[gi, pl.ds(r0, rb), :] for gi, w in enumerate(ws))
        o_ref[pl.ds(r0, rb), :] = (num / sum(ws)).astype(o_ref.dtype)
        return carry

    lax.fori_loop(0, seq // rb, merge_rows, 0)


def _attention(h, cos, sin, batch, seq, col0):
    t = h.shape[0]
    n_groups = len(ATT_GROUPS)
    part = n_groups * ATT_HEADS * ATT_DIM
    blk0 = col0 // ATT_DIM
    in_specs = []
    for gi in range(n_groups):
        for p in range(3):
            off = blk0 + (p * part) // ATT_DIM + gi * ATT_HEADS
            in_specs.append(pl.BlockSpec((seq, ATT_DIM), lambda b, hh, off=off: (b, off + hh)))
    tab = pl.BlockSpec((seq, ATT_DIM), lambda b, hh: (b, 0))
    in_specs += [tab, tab]
    scr = pltpu.VMEM((n_groups, seq, ATT_DIM), F32)
    return pl.pallas_call(
        functools.partial(_attn_kernel, seq=seq),
        out_shape=jax.ShapeDtypeStruct((t, ATT_HEADS * ATT_DIM), BF16),
        grid=(batch, ATT_HEADS),
        in_specs=in_specs,
        out_specs=pl.BlockSpec((seq, ATT_DIM), lambda b, hh: (b, hh)),
        scratch_shapes=[scr, scr, scr, scr, scr],
        compiler_params=_params("parallel", "parallel"),
        name="dilated_attn",
    )(*([h] * 9), cos, sin)


def _layer_norm_rows(r, g, b):
    mu = jnp.mean(r, axis=-1, keepdims=True)
    d = r - mu
    var = jnp.mean(d * d, axis=-1, keepdims=True)
    return d * lax.rsqrt(var + LN_EPS) * g + b


def _pack_halves(y):
    n = y.shape[1] // 2
    return pltpu.pack_elementwise([y[:, :n], y[:, n:]], packed_dtype=BF16)


def _unpack_halves(w):
    first = pltpu.unpack_elementwise(w, index=0, packed_dtype=BF16, unpacked_dtype=F32)
    second = pltpu.unpack_elementwise(w, index=1, packed_dtype=BF16, unpacked_dtype=F32)
    return first, second


def _mix_out_kernel(*refs, alpha, n_col):
    x_ref, oh_ref, oa_ref = refs[:3]
    gh_refs = refs[3:3 + n_col]
    ga_refs = refs[3 + n_col:3 + 2 * n_col]
    whg_ref, wap_ref, wout_ref, g_ref, b_ref, x1_ref, x1p_ref = refs[3 + 2 * n_col:]
    tn = gh_refs[0].shape[1]
    oh = oh_ref[...]
    oa = oa_ref[...]
    merged = []
    for n in range(n_col):
        cols = slice(n * tn, (n + 1) * tn)
        y_h = jnp.dot(oh, whg_ref[:, cols], preferred_element_type=F32)
        y_a = jnp.dot(oa, wap_ref[:, cols], preferred_element_type=F32)
        m = _sigmoid(gh_refs[n][...].astype(F32)) * y_h + _sigmoid(ga_refs[n][...].astype(F32)) * y_a
        merged.append(m.astype(BF16))
    mix = jnp.dot(jnp.concatenate(merged, axis=1), wout_ref[...], preferred_element_type=F32)
    y = _layer_norm_rows(alpha * x_ref[...] + mix, g_ref[...], b_ref[...])
    x1_ref[...] = y
    x1p_ref[...] = _pack_halves(y)


def _mix_out(x, h, o_h, o_a, w_hg, w_ap, w_out, ln_g, ln_b, gate_col0, alpha):
    t, d = x.shape
    tm = min(t, 256)
    tn = 512
    n_col = d // tn
    gh0 = gate_col0 // tn
    ga0 = (gate_col0 + d) // tn
    rows = lambda i: (i, 0)
    const = lambda i: (0, 0)
    resident = dict(index_map=const, pipeline_mode=pl.Buffered(1))
    gate_specs = [pl.BlockSpec((tm, tn), lambda i, c=c0 + n: (i, c)) for c0 in (gh0, ga0) for n in range(n_col)]
    return pl.pallas_call(
        functools.partial(_mix_out_kernel, alpha=alpha, n_col=n_col),
        out_shape=(jax.ShapeDtypeStruct((t, d), F32), jax.ShapeDtypeStruct((t, d // 2), I32)),
        grid=(t // tm,),
        in_specs=[pl.BlockSpec((tm, d), rows),
                  pl.BlockSpec((tm, o_h.shape[1]), rows),
                  pl.BlockSpec((tm, o_a.shape[1]), rows),
                  *gate_specs,
                  pl.BlockSpec(w_hg.shape, **resident),
                  pl.BlockSpec(w_ap.shape, **resident),
                  pl.BlockSpec(w_out.shape, **resident),
                  pl.BlockSpec((1, d), const),
                  pl.BlockSpec((1, d), const)],
        out_specs=(pl.BlockSpec((tm, d), rows), pl.BlockSpec((tm, d // 2), rows)),
        compiler_params=_params("parallel"),
        name="mix_out",
    )(x, o_h, o_a, *([h] * (2 * n_col)), w_hg, w_ap, w_out, ln_g, ln_b)


def _split_bf16(x):
    hi = x.astype(BF16)
    lo = (x - hi.astype(F32)).astype(BF16)
    return hi, lo


def _router_kernel(x_ref, wr_ref, bias_ref, tri_ref, low_ref, rank_ref, eid_ref, wgt_ref, cnt_ref,
                   carry_ref, *, tr):
    ne = N_EXPERTS
    per = ne // N_GROUPS

    @pl.when(pl.program_id(0) == 0)
    def _():
        carry_ref[...] = jnp.zeros_like(carry_ref)

    xh, xl = _split_bf16(x_ref[...])
    wh, wl = _split_bf16(wr_ref[...])
    logits = (lax.dot_general(wh, xh, NT_DIMS, preferred_element_type=F32)
              + lax.dot_general(wh, xl, NT_DIMS, preferred_element_type=F32)
              + lax.dot_general(wl, xh, NT_DIMS, preferred_element_type=F32))
    scores = _sigmoid(logits)
    sel = scores + bias_ref[...]

    grp = sel.reshape(N_GROUPS, per, tr)
    sub = lax.broadcasted_iota(I32, grp.shape, 1)
    m1 = jnp.max(grp, axis=1, keepdims=True)
    first = jnp.min(jnp.where(grp == m1, sub, per), axis=1, keepdims=True)
    m2 = jnp.max(jnp.where(sub == first, -jnp.inf, grp), axis=1, keepdims=True)
    gs = m1 + m2
    gidx = lax.broadcasted_iota(I32, gs.shape, 0)
    grank = jnp.zeros(gs.shape, I32)
    for j in range(N_GROUPS):
        other = gs[j:j + 1]
        grank += ((other > gs) | ((other == gs) & (j < gidx))).astype(I32)
    masked = jnp.where(grank < TOPK_GROUPS, grp, -jnp.inf).reshape(ne, tr)

    eidx = lax.broadcasted_iota(I32, (ne, tr), 0)
    work = masked
    picked = jnp.zeros((ne, tr), F32)
    for _ in range(TOP_K):
        top = jnp.max(work, axis=0, keepdims=True)
        first = jnp.min(jnp.where(work == top, eidx, ne), axis=0, keepdims=True)
        hit = eidx == first
        picked = jnp.where(hit, 1.0, picked)
        work = jnp.where(hit, -jnp.inf, work)
    chosen = picked > 0.0
    w = jnp.where(chosen, scores, 0.0)
    gates = w / jnp.sum(w, axis=0, keepdims=True) * ROUTED_SCALE

    chosen_b = jnp.where(chosen, 1.0, 0.0).astype(BF16)
    incl = jnp.dot(chosen_b, tri_ref[...], preferred_element_type=F32)
    carry = carry_ref[...]
    rank_in_expert = (carry + incl - 1.0).astype(I32)
    carry_new = carry + incl[:, tr - 1:tr]
    carry_ref[...] = carry_new
    cnt_ref[...] = jnp.broadcast_to(carry_new, cnt_ref.shape).astype(I32)

    slot = jnp.dot(low_ref[...], chosen_b, preferred_element_type=F32).astype(I32)
    for j in range(TOP_K):
        pick = chosen & (slot == j)
        rank_ref[pl.ds(j, 1), :] = jnp.sum(jnp.where(pick, rank_in_expert, 0), axis=0, keepdims=True)
        eid_ref[pl.ds(j, 1), :] = jnp.sum(jnp.where(pick, eidx, 0), axis=0, keepdims=True)
        wgt_ref[pl.ds(j, 1), :] = jnp.sum(jnp.where(pick, gates, 0.0), axis=0, keepdims=True)


def _router(x1, w_router_t, bias_col, row0, t):
    d = x1.shape[1]
    ne = N_EXPERTS
    tr = min(t, 512)
    blk0 = row0 // tr
    tri = jnp.asarray(np.triu(np.ones((tr, tr), np.float32)), BF16)
    low = jnp.asarray(np.tril(np.ones((ne, ne), np.float32), -1), BF16)
    slot_shape = jax.ShapeDtypeStruct((TOP_K, t), I32)
    slot_spec = pl.BlockSpec((TOP_K, tr), lambda i: (0, i))
    return pl.pallas_call(
        functools.partial(_router_kernel, tr=tr),
        out_shape=(slot_shape, slot_shape, jax.ShapeDtypeStruct((TOP_K, t), F32),
                   jax.ShapeDtypeStruct((ne, LANES), I32)),
        grid=(t // tr,),
        in_specs=[pl.BlockSpec((tr, d), lambda i: (blk0 + i, 0)),
                  pl.BlockSpec((ne, d), lambda i: (0, 0)),
                  pl.BlockSpec((ne, 1), lambda i: (0, 0)),
                  pl.BlockSpec((tr, tr), lambda i: (0, 0)),
                  pl.BlockSpec((ne, ne), lambda i: (0, 0))],
        out_specs=(slot_spec, slot_spec, slot_spec, pl.BlockSpec((ne, LANES), lambda i: (0, 0))),
        scratch_shapes=[pltpu.VMEM((ne, 1), F32)],
        compiler_params=_params("arbitrary"),
        name="router",
    )(x1, w_router_t, bias_col, tri, low)


def _sc_worker_id():
    return lax.axis_index("s") * SC_CORES + lax.axis_index("c")


def _dispatch(pos, x1p, row0):
    t = pos.shape[1]
    dp = x1p.shape[1]
    t_per_w = t // SC_WORKERS
    chunk = min(SC_SCATTER_ROWS, t_per_w // 2)
    n_chunks = t_per_w // chunk
    mesh = plsc.VectorSubcoreMesh(core_axis_name="c", subcore_axis_name="s")

    @functools.partial(
        pl.kernel, mesh=mesh,
        out_type=jax.ShapeDtypeStruct((t * TOP_K, dp), x1p.dtype),
        scratch_types=[pltpu.VMEM((n_chunks * TOP_K, chunk), I32),
                       pltpu.VMEM((2, chunk, dp), x1p.dtype),
                       pltpu.SemaphoreType.DMA((2,)),
                       pltpu.SemaphoreType.DMA((2,))],
    )
    def scatter_rows(x_hbm, pos_hbm, out_hbm, idx_v, rows_v, load_sem, scat_sem):
        wid = _sc_worker_id()
        base = row0 + wid * t_per_w
        pltpu.sync_copy(pos_hbm.at[wid], idx_v)

        def load(c, b):
            return pltpu.make_async_copy(x_hbm.at[pl.ds(base + c * chunk, chunk)], rows_v.at[b], load_sem.at[b])

        def scat(c, b, j):
            return pltpu.make_async_copy(rows_v.at[b], out_hbm.at[idx_v.at[c * TOP_K + j]], scat_sem.at[b])

        load(0, 0).start()

        @pl.loop(0, n_chunks, step=2)
        def _(c0):
            for b in range(2):
                c = c0 + b
                load(c, b).wait()
                for j in range(TOP_K):
                    scat(c, b, j).start()

                @pl.when(c + 1 < n_chunks)
                def _():
                    @pl.when(c >= 1)
                    def _():
                        for j in range(TOP_K):
                            scat(c - 1, 1 - b, j).wait()
                    load(c + 1, 1 - b).start()

        for b in range(2):
            for j in range(TOP_K):
                scat(n_chunks - 2 + b, b, j).wait()

    idx = pos.reshape(TOP_K, SC_WORKERS, n_chunks, chunk).transpose(1, 2, 0, 3)
    return scatter_rows(x1p, idx.reshape(SC_WORKERS, n_chunks * TOP_K, chunk))


def _gather_rows(table, idx):
    n_rows = idx.shape[0]
    dp = table.shape[1]
    r_per_w = n_rows // SC_WORKERS
    chunk, nbuf = SC_GATHER_ROWS, SC_GATHER_BUFFERS
    n_chunks = r_per_w // chunk
    mesh = plsc.VectorSubcoreMesh(core_axis_name="c", subcore_axis_name="s")

    @functools.partial(
        pl.kernel, mesh=mesh,
        out_type=jax.ShapeDtypeStruct((n_rows, dp), table.dtype),
        scratch_types=[pltpu.VMEM((n_chunks, chunk), I32),
                       pltpu.VMEM((nbuf, chunk, dp), table.dtype),
                       pltpu.SemaphoreType.DMA((nbuf,)),
                       pltpu.SemaphoreType.DMA((nbuf,))],
    )
    def gather_rows(table_hbm, idx_hbm, out_hbm, idx_v, rows_v, gat_sem, store_sem):
        wid = _sc_worker_id()
        base = wid * r_per_w
        pltpu.sync_copy(idx_hbm.at[wid], idx_v)

        def gather(c, b):
            return pltpu.make_async_copy(table_hbm.at[idx_v.at[c]], rows_v.at[b], gat_sem.at[b])

        def store(c, b):
            return pltpu.make_async_copy(rows_v.at[b], out_hbm.at[pl.ds(base + c * chunk, chunk)], store_sem.at[b])

        for b in range(nbuf - 1):
            gather(b, b).start()

        @pl.loop(0, n_chunks, step=nbuf)
        def _(c0):
            for b in range(nbuf):
                c = c0 + b
                gather(c, b).wait()
                store(c, b).start()
                nb = (b + nbuf - 1) % nbuf

                @pl.when(c + nbuf - 1 < n_chunks)
                def _():
                    @pl.when(c >= 1)
                    def _():
                        store(c - 1, nb).wait()
                    gather(c + nbuf - 1, nb).start()

        for b in range(nbuf):
            store(n_chunks - nbuf + b, b).wait()

    return gather_rows(table, idx.reshape(SC_WORKERS, n_chunks, chunk))


def _experts_kernel(tile_ref, exp_ref, lo_ref, hi_ref, slot_ref, nv_ref, xs_ref, wg_ref, wu_ref, wd_ref, ys_ref,
                    wgb, wub, wdb, hid_ref, ybuf_ref):
    it = pl.program_id(0)
    nv = nv_ref[0]
    cur = jnp.minimum(it, nv - 1)
    prev = jnp.maximum(it - 1, 0)

    @pl.when(it == 0)
    def _():
        hid_ref[...] = jnp.zeros_like(hid_ref)
        wdb[...] = jnp.zeros_like(wdb)
        ybuf_ref[...] = jnp.zeros_like(ybuf_ref)

    @pl.when((it < nv) & ((it == 0) | (exp_ref[cur] != exp_ref[prev])))
    def _():
        wgb[...] = wg_ref[0, 0].astype(BF16)
        wub[...] = wu_ref[0, 0].astype(BF16)
        wdb[slot_ref[cur]] = wd_ref[0, 0].astype(BF16)

    tm, half = xs_ref.shape
    y = _pack_halves(jnp.dot(hid_ref[(it + 1) % 2], wdb[slot_ref[prev]], preferred_element_type=F32))
    a, b = _unpack_halves(xs_ref[...])
    x = jnp.concatenate([a.astype(BF16), b.astype(BF16)], axis=1)
    gate = jnp.dot(x, wgb[...], preferred_element_type=F32)
    up = jnp.dot(x, wub[...], preferred_element_type=F32)
    hid_ref[it % 2] = (gate * _sigmoid(gate) * up).astype(BF16)

    done = (it >= 1) & (it <= nv)
    lo = jnp.where(done, lo_ref[prev], 0)
    hi = jnp.where(done, hi_ref[prev], 0)
    row = lax.broadcasted_iota(I32, (tm, half), 0)
    merged = jnp.where((row >= lo) & (row < hi), y, ybuf_ref[...])
    ybuf_ref[...] = merged
    ys_ref[...] = merged


def _experts(items, xs, w_gate, w_up, w_down, layer, tm):
    n_rows, dp = xs.shape
    _, ne, d, ff = w_gate.shape
    n_items = items[0].shape[0]

    def cur_map(i, tl, ex, lo, hi, sl, nv):
        return (tl[jnp.minimum(i, nv[0] - 1)], 0)

    def prev_map(i, tl, ex, lo, hi, sl, nv):
        return (tl[jnp.clip(i - 1, 0, nv[0] - 1)], 0)

    def w_map(i, tl, ex, lo, hi, sl, nv):
        return (layer, ex[jnp.minimum(i, nv[0] - 1)], 0, 0)

    return pl.pallas_call(
        _experts_kernel,
        out_shape=jax.ShapeDtypeStruct((n_rows, dp), I32),
        grid_spec=pltpu.PrefetchScalarGridSpec(
            num_scalar_prefetch=6,
            grid=(n_items + 1,),
            in_specs=[pl.BlockSpec((tm, dp), cur_map),
                      pl.BlockSpec((1, 1, d, ff), w_map),
                      pl.BlockSpec((1, 1, d, ff), w_map),
                      pl.BlockSpec((1, 1, ff, d), w_map)],
            out_specs=pl.BlockSpec((tm, dp), prev_map),
            scratch_shapes=[pltpu.VMEM((d, ff), BF16), pltpu.VMEM((d, ff), BF16), pltpu.VMEM((2, ff, d), BF16),
                            pltpu.VMEM((2, tm, ff), BF16), pltpu.VMEM((tm, dp), I32)]),
        compiler_params=_params("arbitrary"),
        name="experts",
    )(*items, xs, w_gate, w_up, w_down)


def _combine_kernel(*refs, alpha):
    x_ref, wk_ref, yk_ref, wsg_ref, wsu_ref, wsd_ref, g_ref, b_ref = refs[:8]
    x2_ref, x2b_ref = refs[-2:]
    x = x_ref[...]
    xb = x.astype(BF16)
    gate = jnp.dot(xb, wsg_ref[...], preferred_element_type=F32)
    up = jnp.dot(xb, wsu_ref[...], preferred_element_type=F32)
    hid = (gate * _sigmoid(gate) * up).astype(BF16)
    shared = jnp.dot(hid, wsd_ref[...], preferred_element_type=F32)

    tc, half = yk_ref.shape[1:]
    wk = wk_ref[...]
    acc_hi = jnp.zeros((tc, half), F32)
    acc_lo = jnp.zeros((tc, half), F32)
    for j in range(TOP_K):
        hi, lo = _unpack_halves(yk_ref[j])
        wj = wk[:, j:j + 1]
        acc_hi += wj * hi
        acc_lo += wj * lo
    ffn = jnp.concatenate([acc_hi, acc_lo], axis=1) + shared
    y = _layer_norm_rows(alpha * x + ffn, g_ref[...], b_ref[...])
    x2_ref[...] = y
    x2b_ref[...] = y.astype(BF16)


def _combine(x1, wk, yk, ws_gate, ws_up, ws_down, ln_g, ln_b, alpha, row0, prev):
    t_all, d = x1.shape
    t, dp = yk.shape[1:]
    ff = ws_gate.shape[1]
    tc = min(t, 256)
    blk0 = row0 // tc
    const = lambda i: (0, 0)
    rows = lambda i: (i, 0)
    rows_all = lambda i: (blk0 + i, 0)
    in_specs = [pl.BlockSpec((tc, d), rows_all),
                pl.BlockSpec((tc, TOP_K), rows),
                pl.BlockSpec((TOP_K, tc, dp), lambda i: (0, i, 0)),
                pl.BlockSpec((d, ff), const),
                pl.BlockSpec((d, ff), const),
                pl.BlockSpec((ff, d), const),
                pl.BlockSpec((1, d), const),
                pl.BlockSpec((1, d), const)]
    args = [x1, wk, yk, ws_gate, ws_up, ws_down, ln_g, ln_b]
    aliases = {}
    if prev is not None:
        in_specs += [pl.BlockSpec(memory_space=pl.ANY)] * 2
        aliases = {len(args): 0, len(args) + 1: 1}
        args += list(prev)
    return pl.pallas_call(
        functools.partial(_combine_kernel, alpha=alpha),
        out_shape=(jax.ShapeDtypeStruct((t_all, d), F32), jax.ShapeDtypeStruct((t_all, d), BF16)),
        grid=(t // tc,),
        in_specs=in_specs,
        out_specs=(pl.BlockSpec((tc, d), rows_all), pl.BlockSpec((tc, d), rows_all)),
        input_output_aliases=aliases,
        compiler_params=_params("parallel"),
        name="combine",
    )(*args)


def _expert_work_items(counts, n_rows, tm):
    ne = counts.shape[0]
    end = jnp.cumsum(counts)
    start = end - counts
    first_tile = start // tm
    n_it = jnp.where(counts > 0, (end - 1) // tm - first_tile + 1, 0)
    it_end = jnp.cumsum(n_it)
    it_start = it_end - n_it
    n_items = n_rows // tm + ne
    k = jnp.arange(n_items, dtype=I32)
    expert = jnp.minimum(jnp.sum((it_end[None, :] <= k[:, None]).astype(I32), axis=1), ne - 1)
    onehot = expert[:, None] == jnp.arange(ne, dtype=I32)[None, :]
    pick = lambda v: jnp.sum(jnp.where(onehot, v[None, :], 0), axis=1)
    tile = jnp.clip(pick(first_tile) + k - pick(it_start), 0, n_rows // tm - 1)
    lo = jnp.maximum(pick(start) - tile * tm, 0)
    hi = jnp.minimum(pick(end) - tile * tm, tm)
    changed = jnp.concatenate([jnp.zeros((1,), I32), (expert[1:] != expert[:-1]).astype(I32)])
    slot = jnp.cumsum(changed) % 2
    return start.astype(I32), (tile.astype(I32), expert.astype(I32), lo.astype(I32), hi.astype(I32),
                               slot.astype(I32), it_end[-1:].astype(I32))


def _moe(x1, x1p, w_router_t, bias_col, w_gate, w_up, w_down, layer, ws_gate, ws_up, ws_down, ln_g, ln_b, alpha):
    t_all = x1.shape[0]
    tm = 256
    t = t_all // MOE_TOKEN_RANGES
    out = None
    for part in range(MOE_TOKEN_RANGES):
        row0 = part * t
        rank, eid, wgt, cnt = _router(x1, w_router_t, bias_col, row0, t)
        expert_start, items = _expert_work_items(cnt[:, 0], t * TOP_K, tm)
        pos = rank
        for e in range(N_EXPERTS):
            pos = pos + jnp.where(eid == e, expert_start[e], 0)
        xs = _dispatch(pos, x1p, row0)
        ys = _experts(items, xs, w_gate, w_up, w_down, layer, tm)
        yk = _gather_rows(ys, pos.reshape(-1)).reshape(TOP_K, t, -1)
        out = _combine(x1, wgt.T, yk, ws_gate, ws_up, ws_down, ln_g, ln_b, alpha, row0, out)
    return out


def kernel(x, positions, w_in, lb_logits, hg_norm_g, w_hg_proj, w_att_proj, w_out, ln1_g, ln1_b,
           w_router, router_bias, w_e_gate, w_e_up, w_e_down, w_s_gate, w_s_up, w_s_down, ln2_g, ln2_b):
    batch, seq, d = x.shape
    depth = w_in.shape[0]
    t = batch * seq
    alpha = float((2 * depth) ** 0.25)
    hg_width = HG_HEADS * HG_DIM
    att_col0 = 4 * hg_width
    gate_col0 = att_col0 + 3 * len(ATT_GROUPS) * ATT_HEADS * ATT_DIM

    p = jax.nn.softmax(lb_logits.astype(F32), axis=0)
    cs = jnp.cumsum(p, axis=0)
    lower = cs - cs[:1]
    log1m_lower = jnp.log1p(-lower)

    cos, sin = _rope_tables(positions)
    xf = x.reshape(t, d)
    xb = xf.astype(BF16)
    in_width = w_in.shape[2]
    tn = 1280 if in_width % 1280 == 0 else 512
    for layer in range(depth):
        h = _matmul(xb, w_in, layer, min(t, 1024), tn, BF16)
        o_h = _hgrn(h, lower[layer].reshape(1, -1), log1m_lower[layer].reshape(1, -1),
                    hg_norm_g[layer].reshape(1, -1), batch, seq)
        o_a = _attention(h, cos, sin, batch, seq, att_col0)
        x1, x1p = _mix_out(xf, h, o_h, o_a, w_hg_proj[layer].astype(BF16), w_att_proj[layer].astype(BF16),
                           w_out[layer].astype(BF16), ln1_g[layer].reshape(1, d), ln1_b[layer].reshape(1, d),
                           gate_col0, alpha)
        xf, xb = _moe(x1, x1p, w_router[layer].T, router_bias[layer].reshape(-1, 1),
                      w_e_gate, w_e_up, w_e_down, layer,
                      w_s_gate[layer].astype(BF16), w_s_up[layer].astype(BF16), w_s_down[layer].astype(BF16),
                      ln2_g[layer].reshape(1, d), ln2_b[layer].reshape(1, d), alpha)
    return xf.reshape(batch, seq, d)
```

```python
import functools

import numpy as np
import jax
import jax.numpy as jnp
from jax import lax
from jax.experimental import pallas as pl
from jax.experimental.pallas import tpu as pltpu
from jax.experimental.pallas import tpu_sc as plsc

F32 = jnp.float32
BF16 = jnp.bfloat16
I32 = jnp.int32

LANES = 128
VMEM_LIMIT_BYTES = 56 * 1024 * 1024
SC_CORES = 2
SC_SUBCORES = 16
SC_WORKERS = SC_CORES * SC_SUBCORES
SC_SCATTER_ROWS = 32
SC_GATHER_ROWS = 16
SC_GATHER_BUFFERS = 4
MOE_TOKEN_RANGES = 1
EXPERT_CAST_STEPS = 16

HG_HEADS = 8
HG_DIM = 128
HG_CHUNK = 64
HG_SUB = 16
ATT_GROUPS = ((128, 1), (512, 4), (2048, 16))
ATT_HEADS = 4
ATT_DIM = 128
ATT_BACK = 128
ROPE_THETA = 10000.0
N_EXPERTS = 64
N_GROUPS = 8
TOPK_GROUPS = 4
TOP_K = 8
ROUTED_SCALE = 2.5
LN_EPS = 1e-5
NORM_EPS = 1e-6

NT_DIMS = (((1,), (1,)), ((), ()))
TN_DIMS = (((0,), (0,)), ((), ()))


def _params(*sem):
    return pltpu.CompilerParams(dimension_semantics=sem, vmem_limit_bytes=VMEM_LIMIT_BYTES)


def _sigmoid(x):
    return 1.0 / (1.0 + jnp.exp(-x))


def _matmul_kernel(x_ref, w_ref, o_ref, wb_ref):
    @pl.when(pl.program_id(1) == 0)
    def _():
        wb_ref[...] = w_ref[0].astype(BF16)

    o_ref[...] = jnp.dot(x_ref[...], wb_ref[...], preferred_element_type=F32).astype(o_ref.dtype)


def _matmul(x, w, layer, tm, tn, out_dtype):
    m, k = x.shape
    n = w.shape[2]
    return pl.pallas_call(
        _matmul_kernel,
        out_shape=jax.ShapeDtypeStruct((m, n), out_dtype),
        grid=(n // tn, m // tm),
        in_specs=[pl.BlockSpec((tm, k), lambda j, i: (i, 0)),
                  pl.BlockSpec((1, k, tn), lambda j, i: (layer, 0, j))],
        out_specs=pl.BlockSpec((tm, tn), lambda j, i: (i, j)),
        scratch_shapes=[pltpu.VMEM((k, tn), BF16)],
        compiler_params=_params("parallel", "arbitrary"),
        name="in_proj",
    )(x, w)


def _rope_table_kernel(pos_ref, inv_ref, sign_ref, cos_ref, sin_ref):
    ang = pos_ref[...].astype(F32) * inv_ref[...]
    cos_ref[...] = jnp.cos(ang)
    sin_ref[...] = jnp.sin(ang) * sign_ref[...]


def _rope_tables(positions):
    t = positions.size
    half = ATT_DIM // 2
    inv_half = ROPE_THETA ** (-np.arange(half, dtype=np.float32) * np.float32(2.0) / np.float32(ATT_DIM))
    inv = jnp.asarray(np.concatenate([inv_half, inv_half]).astype(np.float32).reshape(1, ATT_DIM))
    sign = jnp.asarray(np.concatenate([-np.ones(half), np.ones(half)]).astype(np.float32).reshape(1, ATT_DIM))
    tm = min(t, 2048)
    return pl.pallas_call(
        _rope_table_kernel,
        out_shape=(jax.ShapeDtypeStruct((t, ATT_DIM), F32), jax.ShapeDtypeStruct((t, ATT_DIM), F32)),
        grid=(t // tm,),
        in_specs=[pl.BlockSpec((tm, 1), lambda i: (i, 0)),
                  pl.BlockSpec((1, ATT_DIM), lambda i: (0, 0)),
                  pl.BlockSpec((1, ATT_DIM), lambda i: (0, 0))],
        out_specs=(pl.BlockSpec((tm, ATT_DIM), lambda i: (i, 0)),
                   pl.BlockSpec((tm, ATT_DIM), lambda i: (i, 0))),
        compiler_params=_params("parallel"),
        name="rope_tables",
    )(positions.reshape(t, 1), inv, sign)


def _cumsum_rows(x):
    n = x.shape[0]
    row = lax.broadcasted_iota(I32, x.shape, 0)
    s = 1
    while s < n:
        x = x + jnp.where(row >= s, pltpu.roll(x, s, 0), 0.0)
        s *= 2
    return x


def _rows_from(cum, offsets):
    parts = []
    for o in offsets:
        if o is None:
            parts.append(jnp.zeros((HG_SUB, cum.shape[1]), F32))
        else:
            parts.append(jnp.broadcast_to(cum[o:o + 1, :], (HG_SUB, cum.shape[1])))
    return jnp.concatenate(parts, axis=0)


def _hgrn_kernel(hq_ref, hf_ref, hi_ref, hg_ref, lb_ref, l1m_ref, ng_ref, o_ref, st_ref, *, n_chunks):
    c = HG_CHUNK
    nsub = c // HG_SUB

    @pl.when(pl.program_id(2) == 0)
    def _():
        st_ref[...] = jnp.zeros_like(st_ref)

    lb = lb_ref[...]
    one_m_lb = 1.0 - lb
    log1m_lb = l1m_ref[...]
    norm_g = ng_ref[...]
    row = lax.broadcasted_iota(I32, (c, HG_DIM), 0)
    blk = row // HG_SUB
    ti = lax.broadcasted_iota(I32, (c, c), 0)
    si = lax.broadcasted_iota(I32, (c, c), 1)
    diag_mask = ((ti // HG_SUB) == (si // HG_SUB)) & (si <= ti)

    def chunk(ci, carry):
        r0 = pl.multiple_of(ci * c, c)
        z = hf_ref[pl.ds(r0, c), :].astype(F32)
        qraw = hq_ref[pl.ds(r0, c), :].astype(F32)
        v = hi_ref[pl.ds(r0, c), :]
        g = hg_ref[pl.ds(r0, c), :].astype(F32)

        e = jnp.exp(-jnp.abs(z))
        r = 1.0 / (1.0 + e)
        sig = jnp.where(z >= 0, r, e * r)
        sig_neg = jnp.where(z >= 0, e * r, r)
        log_sig = jnp.minimum(z, 0.0) - jnp.log(1.0 + e)
        log_f = jnp.maximum(jnp.log(lb + one_m_lb * sig), log1m_lb + log_sig)
        k = one_m_lb * sig_neg
        q = qraw * _sigmoid(qraw)

        cum = _cumsum_rows(log_f)
        start = _rows_from(cum, [None] + [HG_SUB * i - 1 for i in range(1, nsub)])
        mid = _rows_from(cum, [HG_SUB * i + HG_SUB // 2 - 1 for i in range(nsub)])
        end = cum[c - 1:c, :]

        qd = (q * jnp.exp(cum - mid)).astype(BF16)
        kd = (k * jnp.exp(mid - cum)).astype(BF16)
        att = jnp.where(diag_mask, lax.dot_general(qd, kd, NT_DIMS, preferred_element_type=F32), 0.0)

        qs = q * jnp.exp(cum - start)
        q_slots, k_slots = [], []
        for i in range(1, nsub):
            q_slots.append(jnp.where(blk == i, qs, 0.0).astype(BF16))
            n_rows = HG_SUB * i
            b_i = cum[n_rows - 1:n_rows, :]
            k_i = k[:n_rows] * jnp.exp(b_i - cum[:n_rows])
            k_slots.append(jnp.concatenate([k_i, jnp.zeros((c - n_rows, HG_DIM), F32)], axis=0).astype(BF16))
        att = att + lax.dot_general(jnp.concatenate(q_slots, axis=1), jnp.concatenate(k_slots, axis=1),
                                    NT_DIMS, preferred_element_type=F32)
        intra = jnp.dot(att.astype(BF16), v, preferred_element_type=F32)

        st = st_ref[...]
        inter = lax.dot_general((q * jnp.exp(cum)).astype(BF16), st.astype(BF16), NT_DIMS,
                                preferred_element_type=F32)
        k_end = (k * jnp.exp(end - cum)).astype(BF16)
        st_ref[...] = st * jnp.exp(end) + lax.dot_general(v, k_end, TN_DIMS, preferred_element_type=F32)

        o = inter + intra
        o = o * lax.rsqrt(jnp.mean(o * o, axis=-1, keepdims=True) + NORM_EPS) * norm_g
        o_ref[pl.ds(r0, c), :] = (o * _sigmoid(g)).astype(o_ref.dtype)
        return carry

    lax.fori_loop(0, n_chunks, chunk, 0, unroll=8)


def _hgrn(h, lb, log1m_lb, norm_g, batch, seq):
    t = h.shape[0]
    tb = min(seq, 2048)
    nsb = seq // tb
    width = HG_HEADS * HG_DIM

    def col(off):
        return pl.BlockSpec((tb, HG_DIM), lambda b, hh, s: (b * nsb + s, off * HG_HEADS + hh))

    vec = pl.BlockSpec((1, HG_DIM), lambda b, hh, s: (0, hh))
    return pl.pallas_call(
        functools.partial(_hgrn_kernel, n_chunks=tb // HG_CHUNK),
        out_shape=jax.ShapeDtypeStruct((t, width), BF16),
        grid=(batch, HG_HEADS, nsb),
        in_specs=[col(0), col(1), col(2), col(3), vec, vec, vec],
        out_specs=pl.BlockSpec((tb, HG_DIM), lambda b, hh, s: (b * nsb + s, hh)),
        scratch_shapes=[pltpu.VMEM((HG_DIM, HG_DIM), F32)],
        compiler_params=_params("parallel", "parallel", "arbitrary"),
        name="hgrn2",
    )(h, h, h, h, lb, log1m_lb, norm_g)


def _attn_kernel(*refs, seq):
    qkv_refs = refs[:9]
    cos_ref, sin_ref, o_ref = refs[9:12]
    qf, kf, vf, og, lg = refs[12:]
    n_groups = len(ATT_GROUPS)
    scale = ATT_DIM ** -0.5
    rb = min(seq, 256)

    def rope_rows(i, carry):
        r0 = pl.multiple_of(i * rb, rb)
        cs = cos_ref[pl.ds(r0, rb), :]
        sn = sin_ref[pl.ds(r0, rb), :]
        for gi in range(n_groups):
            xq = qkv_refs[3 * gi][pl.ds(r0, rb), :].astype(F32)
            xk = qkv_refs[3 * gi + 1][pl.ds(r0, rb), :].astype(F32)
            qf[gi, pl.ds(r0, rb), :] = (xq * cs + pltpu.roll(xq, ATT_DIM // 2, 1) * sn) * scale
            kf[gi, pl.ds(r0, rb), :] = xk * cs + pltpu.roll(xk, ATT_DIM // 2, 1) * sn
            vf[gi, pl.ds(r0, rb), :] = qkv_refs[3 * gi + 2][pl.ds(r0, rb), :].astype(F32)
        return carry

    lax.fori_loop(0, seq // rb, rope_rows, 0)

    qb = ATT_BACK
    for gi, (_, dil) in enumerate(ATT_GROUPS):
        length = seq // dil
        nk = min(2 * qb, length)
        n_qblk = length // qb

        def block(it, carry, gi=gi, dil=dil, nk=nk, n_qblk=n_qblk):
            res = it // n_qblk
            iq = it % n_qblk
            q0 = iq * qb
            k0 = jnp.maximum(q0 - qb, 0)
            if dil == 1:
                q_rows = pl.ds(pl.multiple_of(q0, qb), qb)
                k_rows = pl.ds(pl.multiple_of(k0, qb), nk)
            else:
                q_rows = pl.ds(res + dil * q0, qb, stride=dil)
                k_rows = pl.ds(res + dil * k0, nk, stride=dil)
            q = qf[gi, q_rows, :].astype(BF16)
            k = kf[gi, k_rows, :].astype(BF16)
            v = vf[gi, k_rows, :].astype(BF16)
            s = lax.dot_general(q, k, NT_DIMS, preferred_element_type=F32)
            dist = (q0 + lax.broadcasted_iota(I32, (qb, nk), 0)) - (k0 + lax.broadcasted_iota(I32, (qb, nk), 1))
            s = jnp.where((dist >= 0) & (dist <= ATT_BACK), s, -jnp.inf)
            m = jnp.max(s, axis=-1, keepdims=True)
            p = jnp.exp(s - m)
            den = jnp.sum(p, axis=-1, keepdims=True)
            o = jnp.dot(p.astype(BF16), v, preferred_element_type=F32) / den
            og[gi, q_rows, :] = o
            lg[gi, q_rows, :] = jnp.broadcast_to(m + jnp.log(den), (qb, ATT_DIM))
            return carry

        lax.fori_loop(0, dil * n_qblk, block, 0, unroll=8)

    def merge_rows(i, carry):
        r0 = pl.multiple_of(i * rb, rb)
        ls = [lg[gi, pl.ds(r0, rb), :] for gi in range(n_groups)]
        m = functools.reduce(jnp.maximum, ls)
        ws = [jnp.exp(l - m) for l in ls]
        num = sum(w * og[gi, pl.ds(r0, rb), :] for gi, w in enumerate(ws))
        o_ref[pl.ds(r0, rb), :] = (num / sum(ws)).astype(o_ref.dtype)
        return carry

    lax.fori_loop(0, seq // rb, merge_rows, 0)


def _attention(h, cos, sin, batch, seq, col0):
    t = h.shape[0]
    n_groups = len(ATT_GROUPS)
    part = n_groups * ATT_HEADS * ATT_DIM
    blk0 = col0 // ATT_DIM
    in_specs = []
    for gi in range(n_groups):
        for p in range(3):
            off = blk0 + (p * part) // ATT_DIM + gi * ATT_HEADS
            in_specs.append(pl.BlockSpec((seq, ATT_DIM), lambda b, hh, off=off: (b, off + hh)))
    tab = pl.BlockSpec((seq, ATT_DIM), lambda b, hh: (b, 0))
    in_specs += [tab, tab]
    scr = pltpu.VMEM((n_groups, seq, ATT_DIM), F32)
    return pl.pallas_call(
        functools.partial(_attn_kernel, seq=seq),
        out_shape=jax.ShapeDtypeStruct((t, ATT_HEADS * ATT_DIM), BF16),
        grid=(batch, ATT_HEADS),
        in_specs=in_specs,
        out_specs=pl.BlockSpec((seq, ATT_DIM), lambda b, hh: (b, hh)),
        scratch_shapes=[scr] * 5,
        compiler_params=_params("parallel", "parallel"),
        name="dilated_attn",
    )(*([h] * 9), cos, sin)


def _layer_norm_rows(r, g, b):
    mu = jnp.mean(r, axis=-1, keepdims=True)
    d = r - mu
    var = jnp.mean(d * d, axis=-1, keepdims=True)
    return d * lax.rsqrt(var + LN_EPS) * g + b


def _pack_halves(y):
    n = y.shape[1] // 2
    return pltpu.pack_elementwise([y[:, :n], y[:, n:]], packed_dtype=BF16)


def _unpack_halves(w):
    first = pltpu.unpack_elementwise(w, index=0, packed_dtype=BF16, unpacked_dtype=F32)
    second = pltpu.unpack_elementwise(w, index=1, packed_dtype=BF16, unpacked_dtype=F32)
    return first, second


def _mix_out_kernel(*refs, alpha, n_col):
    x_ref, oh_ref, oa_ref = refs[:3]
    gh_refs = refs[3:3 + n_col]
    ga_refs = refs[3 + n_col:3 + 2 * n_col]
    whg_ref, wap_ref, wout_ref, g_ref, b_ref, x1_ref, x1p_ref = refs[3 + 2 * n_col:]
    tn = gh_refs[0].shape[1]
    oh = oh_ref[...]
    oa = oa_ref[...]
    merged = []
    for n in range(n_col):
        cols = slice(n * tn, (n + 1) * tn)
        y_h = jnp.dot(oh, whg_ref[:, cols], preferred_element_type=F32)
        y_a = jnp.dot(oa, wap_ref[:, cols], preferred_element_type=F32)
        m = _sigmoid(gh_refs[n][...].astype(F32)) * y_h + _sigmoid(ga_refs[n][...].astype(F32)) * y_a
        merged.append(m.astype(BF16))
    mix = jnp.dot(jnp.concatenate(merged, axis=1), wout_ref[...], preferred_element_type=F32)
    y = _layer_norm_rows(alpha * x_ref[...] + mix, g_ref[...], b_ref[...])
    x1_ref[...] = y
    x1p_ref[...] = _pack_halves(y)


def _mix_out(x, h, o_h, o_a, w_hg, w_ap, w_out, ln_g, ln_b, gate_col0, alpha):
    t, d = x.shape
    tm = min(t, 256)
    tn = 512
    n_col = d // tn
    gh0 = gate_col0 // tn
    ga0 = (gate_col0 + d) // tn
    rows = lambda i: (i, 0)
    const = lambda i: (0, 0)
    resident = dict(index_map=const, pipeline_mode=pl.Buffered(1))
    gate_specs = [pl.BlockSpec((tm, tn), lambda i, c=c0 + n: (i, c)) for c0 in (gh0, ga0) for n in range(n_col)]
    return pl.pallas_call(
        functools.partial(_mix_out_kernel, alpha=alpha, n_col=n_col),
        out_shape=(jax.ShapeDtypeStruct((t, d), F32), jax.ShapeDtypeStruct((t, d // 2), I32)),
        grid=(t // tm,),
        in_specs=[pl.BlockSpec((tm, d), rows),
                  pl.BlockSpec((tm, o_h.shape[1]), rows),
                  pl.BlockSpec((tm, o_a.shape[1]), rows),
                  *gate_specs,
                  pl.BlockSpec(w_hg.shape, **resident),
                  pl.BlockSpec(w_ap.shape, **resident),
                  pl.BlockSpec(w_out.shape, **resident),
                  pl.BlockSpec((1, d), const),
                  pl.BlockSpec((1, d), const)],
        out_specs=(pl.BlockSpec((tm, d), rows), pl.BlockSpec((tm, d // 2), rows)),
        compiler_params=_params("parallel"),
        name="mix_out",
    )(x, o_h, o_a, *([h] * (2 * n_col)), w_hg, w_ap, w_out, ln_g, ln_b)


def _split_bf16(x):
    hi = x.astype(BF16)
    lo = (x - hi.astype(F32)).astype(BF16)
    return hi, lo


def _router_kernel(x_ref, wr_ref, bias_ref, tri_ref, low_ref, rank_ref, eid_ref, wgt_ref, cnt_ref,
                   carry_ref, *, tr):
    ne = N_EXPERTS
    per = ne // N_GROUPS

    @pl.when(pl.program_id(0) == 0)
    def _():
        carry_ref[...] = jnp.zeros_like(carry_ref)

    xh, xl = _split_bf16(x_ref[...])
    wh, wl = _split_bf16(wr_ref[...])
    logits = (lax.dot_general(wh, xh, NT_DIMS, preferred_element_type=F32)
              + lax.dot_general(wh, xl, NT_DIMS, preferred_element_type=F32)
              + lax.dot_general(wl, xh, NT_DIMS, preferred_element_type=F32))
    scores = _sigmoid(logits)
    sel = scores + bias_ref[...]

    grp = sel.reshape(N_GROUPS, per, tr)
    sub = lax.broadcasted_iota(I32, grp.shape, 1)
    m1 = jnp.max(grp, axis=1, keepdims=True)
    first = jnp.min(jnp.where(grp == m1, sub, per), axis=1, keepdims=True)
    m2 = jnp.max(jnp.where(sub == first, -jnp.inf, grp), axis=1, keepdims=True)
    gs = m1 + m2
    gidx = lax.broadcasted_iota(I32, gs.shape, 0)
    grank = jnp.zeros(gs.shape, I32)
    for j in range(N_GROUPS):
        other = gs[j:j + 1]
        grank += ((other > gs) | ((other == gs) & (j < gidx))).astype(I32)
    masked = jnp.where(grank < TOPK_GROUPS, grp, -jnp.inf).reshape(ne, tr)

    eidx = lax.broadcasted_iota(I32, (ne, tr), 0)
    work = masked
    picked = jnp.zeros((ne, tr), F32)
    for _ in range(TOP_K):
        top = jnp.max(work, axis=0, keepdims=True)
        first = jnp.min(jnp.where(work == top, eidx, ne), axis=0, keepdims=True)
        hit = eidx == first
        picked = jnp.where(hit, 1.0, picked)
        work = jnp.where(hit, -jnp.inf, work)
    chosen = picked > 0.0
    w = jnp.where(chosen, scores, 0.0)
    gates = w / jnp.sum(w, axis=0, keepdims=True) * ROUTED_SCALE

    chosen_b = jnp.where(chosen, 1.0, 0.0).astype(BF16)
    incl = jnp.dot(chosen_b, tri_ref[...], preferred_element_type=F32)
    carry = carry_ref[...]
    rank_in_expert = (carry + incl - 1.0).astype(I32)
    carry_new = carry + incl[:, tr - 1:tr]
    carry_ref[...] = carry_new
    cnt_ref[...] = jnp.broadcast_to(carry_new, cnt_ref.shape).astype(I32)

    slot = jnp.dot(low_ref[...], chosen_b, preferred_element_type=F32).astype(I32)
    for j in range(TOP_K):
        pick = chosen & (slot == j)
        rank_ref[pl.ds(j, 1), :] = jnp.sum(jnp.where(pick, rank_in_expert, 0), axis=0, keepdims=True)
        eid_ref[pl.ds(j, 1), :] = jnp.sum(jnp.where(pick, eidx, 0), axis=0, keepdims=True)
        wgt_ref[pl.ds(j, 1), :] = jnp.sum(jnp.where(pick, gates, 0.0), axis=0, keepdims=True)


def _router(x1, w_router_t, bias_col, row0, t):
    d = x1.shape[1]
    ne = N_EXPERTS
    tr = min(t, 512)
    blk0 = row0 // tr
    tri = jnp.asarray(np.triu(np.ones((tr, tr), np.float32)), BF16)
    low = jnp.asarray(np.tril(np.ones((ne, ne), np.float32), -1), BF16)
    slot_shape = jax.ShapeDtypeStruct((TOP_K, t), I32)
    slot_spec = pl.BlockSpec((TOP_K, tr), lambda i: (0, i))
    return pl.pallas_call(
        functools.partial(_router_kernel, tr=tr),
        out_shape=(slot_shape, slot_shape, jax.ShapeDtypeStruct((TOP_K, t), F32),
                   jax.ShapeDtypeStruct((ne, LANES), I32)),
        grid=(t // tr,),
        in_specs=[pl.BlockSpec((tr, d), lambda i: (blk0 + i, 0)),
                  pl.BlockSpec((ne, d), lambda i: (0, 0)),
                  pl.BlockSpec((ne, 1), lambda i: (0, 0)),
                  pl.BlockSpec((tr, tr), lambda i: (0, 0)),
                  pl.BlockSpec((ne, ne), lambda i: (0, 0))],
        out_specs=(slot_spec, slot_spec, slot_spec, pl.BlockSpec((ne, LANES), lambda i: (0, 0))),
        scratch_shapes=[pltpu.VMEM((ne, 1), F32)],
        compiler_params=_params("arbitrary"),
        name="router",
    )(x1, w_router_t, bias_col, tri, low)


def _sc_worker_id():
    return lax.axis_index("s") * SC_CORES + lax.axis_index("c")


def _dispatch(pos, x1p, row0):
    t = pos.shape[1]
    dp = x1p.shape[1]
    t_per_w = t // SC_WORKERS
    chunk = min(SC_SCATTER_ROWS, t_per_w // 2)
    n_chunks = t_per_w // chunk
    mesh = plsc.VectorSubcoreMesh(core_axis_name="c", subcore_axis_name="s")

    @functools.partial(
        pl.kernel, mesh=mesh,
        out_type=jax.ShapeDtypeStruct((t * TOP_K, dp), x1p.dtype),
        scratch_types=[pltpu.VMEM((n_chunks * TOP_K, chunk), I32),
                       pltpu.VMEM((2, chunk, dp), x1p.dtype),
                       pltpu.SemaphoreType.DMA((2,)),
                       pltpu.SemaphoreType.DMA((2,))],
    )
    def scatter_rows(x_hbm, pos_hbm, out_hbm, idx_v, rows_v, load_sem, scat_sem):
        wid = _sc_worker_id()
        base = row0 + wid * t_per_w
        pltpu.sync_copy(pos_hbm.at[wid], idx_v)

        def load(c, b):
            return pltpu.make_async_copy(x_hbm.at[pl.ds(base + c * chunk, chunk)], rows_v.at[b], load_sem.at[b])

        def scat(c, b, j):
            return pltpu.make_async_copy(rows_v.at[b], out_hbm.at[idx_v.at[c * TOP_K + j]], scat_sem.at[b])

        load(0, 0).start()

        @pl.loop(0, n_chunks, step=2)
        def _(c0):
            for b in range(2):
                c = c0 + b
                load(c, b).wait()
                for j in range(TOP_K):
                    scat(c, b, j).start()

                @pl.when(c + 1 < n_chunks)
                def _():
                    @pl.when(c >= 1)
                    def _():
                        for j in range(TOP_K):
                            scat(c - 1, 1 - b, j).wait()
                    load(c + 1, 1 - b).start()

        for b in range(2):
            for j in range(TOP_K):
                scat(n_chunks - 2 + b, b, j).wait()

    idx = pos.reshape(TOP_K, SC_WORKERS, n_chunks, chunk).transpose(1, 2, 0, 3)
    return scatter_rows(x1p, idx.reshape(SC_WORKERS, n_chunks * TOP_K, chunk))


def _gather_rows(table, idx):
    n_rows = idx.shape[0]
    dp = table.shape[1]
    r_per_w = n_rows // SC_WORKERS
    chunk, nbuf = SC_GATHER_ROWS, SC_GATHER_BUFFERS
    n_chunks = r_per_w // chunk
    mesh = plsc.VectorSubcoreMesh(core_axis_name="c", subcore_axis_name="s")

    @functools.partial(
        pl.kernel, mesh=mesh,
        out_type=jax.ShapeDtypeStruct((n_rows, dp), table.dtype),
        scratch_types=[pltpu.VMEM((n_chunks, chunk), I32),
                       pltpu.VMEM((nbuf, chunk, dp), table.dtype),
                       pltpu.SemaphoreType.DMA((nbuf,)),
                       pltpu.SemaphoreType.DMA((nbuf,))],
    )
    def gather_rows(table_hbm, idx_hbm, out_hbm, idx_v, rows_v, gat_sem, store_sem):
        wid = _sc_worker_id()
        base = wid * r_per_w
        pltpu.sync_copy(idx_hbm.at[wid], idx_v)

        def gather(c, b):
            return pltpu.make_async_copy(table_hbm.at[idx_v.at[c]], rows_v.at[b], gat_sem.at[b])

        def store(c, b):
            return pltpu.make_async_copy(rows_v.at[b], out_hbm.at[pl.ds(base + c * chunk, chunk)], store_sem.at[b])

        for b in range(nbuf - 1):
            gather(b, b).start()

        @pl.loop(0, n_chunks, step=nbuf)
        def _(c0):
            for b in range(nbuf):
                c = c0 + b
                gather(c, b).wait()
                store(c, b).start()
                nb = (b + nbuf - 1) % nbuf

                @pl.when(c + nbuf - 1 < n_chunks)
                def _():
                    @pl.when(c >= 1)
                    def _():
                        store(c - 1, nb).wait()
                    gather(c + nbuf - 1, nb).start()

        for b in range(nbuf):
            store(n_chunks - nbuf + b, b).wait()

    return gather_rows(table, idx.reshape(SC_WORKERS, n_chunks, chunk))


def _experts_kernel(tile_ref, exp_ref, lo_ref, hi_ref, slot_ref, next_ref, nv_ref, xs_ref, wg_hbm, wu_hbm, wd_hbm,
                    ys_ref, wgf, wuf, wdf, wsem, wgb, wub, wdb, hid_ref, ybuf_ref, *, layer):
    it = pl.program_id(0)
    nv = nv_ref[0]
    cur = jnp.minimum(it, nv - 1)
    prev = jnp.maximum(it - 1, 0)

    def weight_copies(e):
        return [pltpu.make_async_copy(src.at[layer, e], dst, wsem.at[k])
                for k, (src, dst) in enumerate(((wg_hbm, wgf), (wu_hbm, wuf), (wd_hbm, wdf)))]

    @pl.when(it == 0)
    def _():
        hid_ref[...] = jnp.zeros_like(hid_ref)
        wdb[...] = jnp.zeros_like(wdb)
        ybuf_ref[...] = jnp.zeros_like(ybuf_ref)
        for cp in weight_copies(exp_ref[0]):
            cp.start()

    @pl.when((it < nv) & ((it == 0) | (exp_ref[cur] != exp_ref[prev])))
    def _():
        for cp in weight_copies(exp_ref[cur]):
            cp.wait()
        slot = slot_ref[cur]
        rows_g = wgf.shape[0] // EXPERT_CAST_STEPS
        rows_d = wdf.shape[0] // EXPERT_CAST_STEPS

        def cast_rows(i, carry):
            rg = pl.ds(pl.multiple_of(i * rows_g, rows_g), rows_g)
            rd = pl.ds(pl.multiple_of(i * rows_d, rows_d), rows_d)
            wgb[rg, :] = wgf[rg, :].astype(BF16)
            wub[rg, :] = wuf[rg, :].astype(BF16)
            wdb[slot, rd, :] = wdf[rd, :].astype(BF16)
            return carry

        lax.fori_loop(0, EXPERT_CAST_STEPS, cast_rows, 0)

        @pl.when(next_ref[cur] >= 0)
        def _():
            for cp in weight_copies(next_ref[cur]):
                cp.start()

    tm, half = xs_ref.shape
    y = _pack_halves(jnp.dot(hid_ref[(it + 1) % 2], wdb[slot_ref[prev]], preferred_element_type=F32))
    a, b = _unpack_halves(xs_ref[...])
    x = jnp.concatenate([a.astype(BF16), b.astype(BF16)], axis=1)
    gate = jnp.dot(x, wgb[...], preferred_element_type=F32)
    up = jnp.dot(x, wub[...], preferred_element_type=F32)
    hid_ref[it % 2] = (gate * _sigmoid(gate) * up).astype(BF16)

    done = (it >= 1) & (it <= nv)
    lo = jnp.where(done, lo_ref[prev], 0)
    hi = jnp.where(done, hi_ref[prev], 0)
    row = lax.broadcasted_iota(I32, (tm, half), 0)
    merged = jnp.where((row >= lo) & (row < hi), y, ybuf_ref[...])
    ybuf_ref[...] = merged
    ys_ref[...] = merged


def _experts(items, xs, w_gate, w_up, w_down, layer, tm):
    n_rows, dp = xs.shape
    _, ne, d, ff = w_gate.shape
    n_items = items[0].shape[0]

    def cur_map(i, tl, ex, lo, hi, sl, nx, nv):
        return (tl[jnp.minimum(i, nv[0] - 1)], 0)

    def prev_map(i, tl, ex, lo, hi, sl, nx, nv):
        return (tl[jnp.clip(i - 1, 0, nv[0] - 1)], 0)

    hbm = pl.BlockSpec(memory_space=pl.ANY)
    return pl.pallas_call(
        functools.partial(_experts_kernel, layer=layer),
        out_shape=jax.ShapeDtypeStruct((n_rows, dp), I32),
        grid_spec=pltpu.PrefetchScalarGridSpec(
            num_scalar_prefetch=7,
            grid=(n_items + 1,),
            in_specs=[pl.BlockSpec((tm, dp), cur_map), hbm, hbm, hbm],
            out_specs=pl.BlockSpec((tm, dp), prev_map),
            scratch_shapes=[pltpu.VMEM((d, ff), F32), pltpu.VMEM((d, ff), F32), pltpu.VMEM((ff, d), F32),
                            pltpu.SemaphoreType.DMA((3,)),
                            pltpu.VMEM((d, ff), BF16), pltpu.VMEM((d, ff), BF16), pltpu.VMEM((2, ff, d), BF16),
                            pltpu.VMEM((2, tm, ff), BF16), pltpu.VMEM((tm, dp), I32)]),
        compiler_params=_params("arbitrary"),
        name="experts",
    )(*items, xs, w_gate, w_up, w_down)


def _combine_kernel(*refs, alpha):
    x_ref, wk_ref, yk_ref, wsg_ref, wsu_ref, wsd_ref, g_ref, b_ref = refs[:8]
    x2_ref, x2b_ref = refs[-2:]
    x = x_ref[...]
    xb = x.astype(BF16)
    gate = jnp.dot(xb, wsg_ref[...], preferred_element_type=F32)
    up = jnp.dot(xb, wsu_ref[...], preferred_element_type=F32)
    hid = (gate * _sigmoid(gate) * up).astype(BF16)
    shared = jnp.dot(hid, wsd_ref[...], preferred_element_type=F32)

    tc, half = yk_ref.shape[1:]
    wk = wk_ref[...]
    acc_hi = jnp.zeros((tc, half), F32)
    acc_lo = jnp.zeros((tc, half), F32)
    for j in range(TOP_K):
        hi, lo = _unpack_halves(yk_ref[j])
        wj = wk[:, j:j + 1]
        acc_hi += wj * hi
        acc_lo += wj * lo
    ffn = jnp.concatenate([acc_hi, acc_lo], axis=1) + shared
    y = _layer_norm_rows(alpha * x + ffn, g_ref[...], b_ref[...])
    x2_ref[...] = y
    x2b_ref[...] = y.astype(BF16)


def _combine(x1, wk, yk, ws_gate, ws_up, ws_down, ln_g, ln_b, alpha, row0, prev):
    t_all, d = x1.shape
    t, dp = yk.shape[1:]
    ff = ws_gate.shape[1]
    tc = min(t, 256)
    blk0 = row0 // tc
    const = lambda i: (0, 0)
    rows = lambda i: (i, 0)
    rows_all = lambda i: (blk0 + i, 0)
    in_specs = [pl.BlockSpec((tc, d), rows_all),
                pl.BlockSpec((tc, TOP_K), rows),
                pl.BlockSpec((TOP_K, tc, dp), lambda i: (0, i, 0)),
                pl.BlockSpec((d, ff), const),
                pl.BlockSpec((d, ff), const),
                pl.BlockSpec((ff, d), const),
                pl.BlockSpec((1, d), const),
                pl.BlockSpec((1, d), const)]
    args = [x1, wk, yk, ws_gate, ws_up, ws_down, ln_g, ln_b]
    aliases = {}
    if prev is not None:
        in_specs += [pl.BlockSpec(memory_space=pl.ANY)] * 2
        aliases = {len(args): 0, len(args) + 1: 1}
        args += list(prev)
    return pl.pallas_call(
        functools.partial(_combine_kernel, alpha=alpha),
        out_shape=(jax.ShapeDtypeStruct((t_all, d), F32), jax.ShapeDtypeStruct((t_all, d), BF16)),
        grid=(t // tc,),
        in_specs=in_specs,
        out_specs=(pl.BlockSpec((tc, d), rows_all), pl.BlockSpec((tc, d), rows_all)),
        input_output_aliases=aliases,
        compiler_params=_params("parallel"),
        name="combine",
    )(*args)


def _expert_work_items(counts, n_rows, tm):
    ne = counts.shape[0]
    end = jnp.cumsum(counts)
    start = end - counts
    first_tile = start // tm
    n_it = jnp.where(counts > 0, (end - 1) // tm - first_tile + 1, 0)
    it_end = jnp.cumsum(n_it)
    it_start = it_end - n_it
    n_items = n_rows // tm + ne
    k = jnp.arange(n_items, dtype=I32)
    expert = jnp.minimum(jnp.sum((it_end[None, :] <= k[:, None]).astype(I32), axis=1), ne - 1)
    onehot = expert[:, None] == jnp.arange(ne, dtype=I32)[None, :]
    pick = lambda v: jnp.sum(jnp.where(onehot, v[None, :], 0), axis=1)
    tile = jnp.clip(pick(first_tile) + k - pick(it_start), 0, n_rows // tm - 1)
    lo = jnp.maximum(pick(start) - tile * tm, 0)
    hi = jnp.minimum(pick(end) - tile * tm, tm)
    changed = jnp.concatenate([jnp.zeros((1,), I32), (expert[1:] != expert[:-1]).astype(I32)])
    slot = jnp.cumsum(changed) % 2
    later = (k[None, :] < it_end[-1]) & (expert[None, :] > expert[:, None])
    nxt = jnp.min(jnp.where(later, expert[None, :], ne), axis=1)
    nxt = jnp.where(nxt < ne, nxt, -1)
    return start.astype(I32), (tile.astype(I32), expert.astype(I32), lo.astype(I32), hi.astype(I32),
                               slot.astype(I32), nxt.astype(I32), it_end[-1:].astype(I32))


def _moe(x1, x1p, w_router_t, bias_col, w_gate, w_up, w_down, layer, ws_gate, ws_up, ws_down, ln_g, ln_b, alpha):
    t_all = x1.shape[0]
    tm = 256
    t = t_all // MOE_TOKEN_RANGES
    out = None
    for part in range(MOE_TOKEN_RANGES):
        row0 = part * t
        rank, eid, wgt, cnt = _router(x1, w_router_t, bias_col, row0, t)
        expert_start, items = _expert_work_items(cnt[:, 0], t * TOP_K, tm)
        pos = rank
        for e in range(N_EXPERTS):
            pos = pos + jnp.where(eid == e, expert_start[e], 0)
        xs = _dispatch(pos, x1p, row0)
        ys = _experts(items, xs, w_gate, w_up, w_down, layer, tm)
        yk = _gather_rows(ys, pos.reshape(-1)).reshape(TOP_K, t, -1)
        out = _combine(x1, wgt.T, yk, ws_gate, ws_up, ws_down, ln_g, ln_b, alpha, row0, out)
    return out


def kernel(x, positions, w_in, lb_logits, hg_norm_g, w_hg_proj, w_att_proj, w_out, ln1_g, ln1_b,
           w_router, router_bias, w_e_gate, w_e_up, w_e_down, w_s_gate, w_s_up, w_s_down, ln2_g, ln2_b):
    batch, seq, d = x.shape
    depth = w_in.shape[0]
    t = batch * seq
    alpha = float((2 * depth) ** 0.25)
    hg_width = HG_HEADS * HG_DIM
    att_col0 = 4 * hg_width
    gate_col0 = att_col0 + 3 * len(ATT_GROUPS) * ATT_HEADS * ATT_DIM

    p = jax.nn.softmax(lb_logits.astype(F32), axis=0)
    cs = jnp.cumsum(p, axis=0)
    lower = cs - cs[:1]
    log1m_lower = jnp.log1p(-lower)

    cos, sin = _rope_tables(positions)
    xf = x.reshape(t, d)
    xb = xf.astype(BF16)
    in_width = w_in.shape[2]
    tn = 1280 if in_width % 1280 == 0 else 512
    for layer in range(depth):
        h = _matmul(xb, w_in, layer, min(t, 1024), tn, BF16)
        o_h = _hgrn(h, lower[layer].reshape(1, -1), log1m_lower[layer].reshape(1, -1),
                    hg_norm_g[layer].reshape(1, -1), batch, seq)
        o_a = _attention(h, cos, sin, batch, seq, att_col0)
        x1, x1p = _mix_out(xf, h, o_h, o_a, w_hg_proj[layer].astype(BF16), w_att_proj[layer].astype(BF16),
                           w_out[layer].astype(BF16), ln1_g[layer].reshape(1, d), ln1_b[layer].reshape(1, d),
                           gate_col0, alpha)
        xf, xb = _moe(x1, x1p, w_router[layer].T, router_bias[layer].reshape(-1, 1),
                      w_e_gate, w_e_up, w_e_down, layer,
                      w_s_gate[layer].astype(BF16), w_s_up[layer].astype(BF16), w_s_down[layer].astype(BF16),
                      ln2_g[layer].reshape(1, d), ln2_b[layer].reshape(1, d), alpha)
    return xf.reshape(batch, seq, d)
```

```python
import functools

import numpy as np
import jax
import jax.numpy as jnp
from jax import lax
from jax.experimental import pallas as pl
from jax.experimental.pallas import tpu as pltpu
from jax.experimental.pallas import tpu_sc as plsc

F32 = jnp.float32
BF16 = jnp.bfloat16
I32 = jnp.int32

LANES = 128
VMEM_LIMIT_BYTES = 56 * 1024 * 1024
SC_CORES = 2
SC_SUBCORES = 16
SC_WORKERS = SC_CORES * SC_SUBCORES
SC_SCATTER_ROWS = 32
SC_GATHER_ROWS = 16
SC_GATHER_BUFFERS = 4
BATCH_PIPELINES = 2
EXPERT_CAST_STEPS = 16

HG_HEADS = 8
HG_DIM = 128
HG_CHUNK = 64
HG_SUB = 16
ATT_GROUPS = ((128, 1), (512, 4), (2048, 16))
ATT_HEADS = 4
ATT_DIM = 128
ATT_BACK = 128
ROPE_THETA = 10000.0
N_EXPERTS = 64
N_GROUPS = 8
TOPK_GROUPS = 4
TOP_K = 8
ROUTED_SCALE = 2.5
LN_EPS = 1e-5
NORM_EPS = 1e-6

NT_DIMS = (((1,), (1,)), ((), ()))
TN_DIMS = (((0,), (0,)), ((), ()))


def _params(*sem):
    return pltpu.CompilerParams(dimension_semantics=sem, vmem_limit_bytes=VMEM_LIMIT_BYTES)


def _sigmoid(x):
    return 1.0 / (1.0 + jnp.exp(-x))


def _matmul_kernel(x_ref, w_ref, o_ref, wb_ref):
    @pl.when(pl.program_id(1) == 0)
    def _():
        wb_ref[...] = w_ref[0].astype(BF16)

    o_ref[...] = jnp.dot(x_ref[...], wb_ref[...], preferred_element_type=F32).astype(o_ref.dtype)


def _matmul(x, w, layer, tm, tn, out_dtype):
    m, k = x.shape
    n = w.shape[2]
    return pl.pallas_call(
        _matmul_kernel,
        out_shape=jax.ShapeDtypeStruct((m, n), out_dtype),
        grid=(n // tn, m // tm),
        in_specs=[pl.BlockSpec((tm, k), lambda j, i: (i, 0)),
                  pl.BlockSpec((1, k, tn), lambda j, i: (layer, 0, j))],
        out_specs=pl.BlockSpec((tm, tn), lambda j, i: (i, j)),
        scratch_shapes=[pltpu.VMEM((k, tn), BF16)],
        compiler_params=_params("parallel", "arbitrary"),
        name="in_proj",
    )(x, w)


def _rope_table_kernel(pos_ref, inv_ref, sign_ref, cos_ref, sin_ref):
    ang = pos_ref[...].astype(F32) * inv_ref[...]
    cos_ref[...] = jnp.cos(ang)
    sin_ref[...] = jnp.sin(ang) * sign_ref[...]


def _rope_tables(positions):
    t = positions.size
    half = ATT_DIM // 2
    inv_half = ROPE_THETA ** (-np.arange(half, dtype=np.float32) * np.float32(2.0) / np.float32(ATT_DIM))
    inv = jnp.asarray(np.concatenate([inv_half, inv_half]).astype(np.float32).reshape(1, ATT_DIM))
    sign = jnp.asarray(np.concatenate([-np.ones(half), np.ones(half)]).astype(np.float32).reshape(1, ATT_DIM))
    tm = min(t, 2048)
    return pl.pallas_call(
        _rope_table_kernel,
        out_shape=(jax.ShapeDtypeStruct((t, ATT_DIM), F32), jax.ShapeDtypeStruct((t, ATT_DIM), F32)),
        grid=(t // tm,),
        in_specs=[pl.BlockSpec((tm, 1), lambda i: (i, 0)),
                  pl.BlockSpec((1, ATT_DIM), lambda i: (0, 0)),
                  pl.BlockSpec((1, ATT_DIM), lambda i: (0, 0))],
        out_specs=(pl.BlockSpec((tm, ATT_DIM), lambda i: (i, 0)),
                   pl.BlockSpec((tm, ATT_DIM), lambda i: (i, 0))),
        compiler_params=_params("parallel"),
        name="rope_tables",
    )(positions.reshape(t, 1), inv, sign)


def _cumsum_rows(x):
    n = x.shape[0]
    row = lax.broadcasted_iota(I32, x.shape, 0)
    s = 1
    while s < n:
        x = x + jnp.where(row >= s, pltpu.roll(x, s, 0), 0.0)
        s *= 2
    return x


def _rows_from(cum, offsets):
    parts = []
    for o in offsets:
        if o is None:
            parts.append(jnp.zeros((HG_SUB, cum.shape[1]), F32))
        else:
            parts.append(jnp.broadcast_to(cum[o:o + 1, :], (HG_SUB, cum.shape[1])))
    return jnp.concatenate(parts, axis=0)


def _hgrn_kernel(hq_ref, hf_ref, hi_ref, hg_ref, lb_ref, l1m_ref, ng_ref, o_ref, st_ref, *, n_chunks):
    c = HG_CHUNK
    nsub = c // HG_SUB

    @pl.when(pl.program_id(2) == 0)
    def _():
        st_ref[...] = jnp.zeros_like(st_ref)

    lb = lb_ref[...]
    one_m_lb = 1.0 - lb
    log1m_lb = l1m_ref[...]
    norm_g = ng_ref[...]
    row = lax.broadcasted_iota(I32, (c, HG_DIM), 0)
    blk = row // HG_SUB
    ti = lax.broadcasted_iota(I32, (c, c), 0)
    si = lax.broadcasted_iota(I32, (c, c), 1)
    diag_mask = ((ti // HG_SUB) == (si // HG_SUB)) & (si <= ti)

    def chunk(ci, carry):
        r0 = pl.multiple_of(ci * c, c)
        z = hf_ref[pl.ds(r0, c), :].astype(F32)
        qraw = hq_ref[pl.ds(r0, c), :].astype(F32)
        v = hi_ref[pl.ds(r0, c), :]
        g = hg_ref[pl.ds(r0, c), :].astype(F32)

        e = jnp.exp(-jnp.abs(z))
        r = 1.0 / (1.0 + e)
        sig = jnp.where(z >= 0, r, e * r)
        sig_neg = jnp.where(z >= 0, e * r, r)
        log_sig = jnp.minimum(z, 0.0) - jnp.log(1.0 + e)
        log_f = jnp.maximum(jnp.log(lb + one_m_lb * sig), log1m_lb + log_sig)
        k = one_m_lb * sig_neg
        q = qraw * _sigmoid(qraw)

        cum = _cumsum_rows(log_f)
        start = _rows_from(cum, [None] + [HG_SUB * i - 1 for i in range(1, nsub)])
        mid = _rows_from(cum, [HG_SUB * i + HG_SUB // 2 - 1 for i in range(nsub)])
        end = cum[c - 1:c, :]

        qd = (q * jnp.exp(cum - mid)).astype(BF16)
        kd = (k * jnp.exp(mid - cum)).astype(BF16)
        att = jnp.where(diag_mask, lax.dot_general(qd, kd, NT_DIMS, preferred_element_type=F32), 0.0)

        qs = q * jnp.exp(cum - start)
        q_slots, k_slots = [], []
        for i in range(1, nsub):
            q_slots.append(jnp.where(blk == i, qs, 0.0).astype(BF16))
            n_rows = HG_SUB * i
            b_i = cum[n_rows - 1:n_rows, :]
            k_i = k[:n_rows] * jnp.exp(b_i - cum[:n_rows])
            k_slots.append(jnp.concatenate([k_i, jnp.zeros((c - n_rows, HG_DIM), F32)], axis=0).astype(BF16))
        att = att + lax.dot_general(jnp.concatenate(q_slots, axis=1), jnp.concatenate(k_slots, axis=1),
                                    NT_DIMS, preferred_element_type=F32)
        intra = jnp.dot(att.astype(BF16), v, preferred_element_type=F32)

        st = st_ref[...]
        inter = lax.dot_general((q * jnp.exp(cum)).astype(BF16), st.astype(BF16), NT_DIMS,
                                preferred_element_type=F32)
        k_end = (k * jnp.exp(end - cum)).astype(BF16)
        st_ref[...] = st * jnp.exp(end) + lax.dot_general(v, k_end, TN_DIMS, preferred_element_type=F32)

        o = inter + intra
        o = o * lax.rsqrt(jnp.mean(o * o, axis=-1, keepdims=True) + NORM_EPS) * norm_g
        o_ref[pl.ds(r0, c), :] = (o * _sigmoid(g)).astype(o_ref.dtype)
        return carry

    lax.fori_loop(0, n_chunks, chunk, 0, unroll=8)


def _hgrn(h, lb, log1m_lb, norm_g, batch, seq):
    t = h.shape[0]
    tb = min(seq, 2048)
    nsb = seq // tb
    width = HG_HEADS * HG_DIM

    def col(off):
        return pl.BlockSpec((tb, HG_DIM), lambda b, hh, s: (b * nsb + s, off * HG_HEADS + hh))

    vec = pl.BlockSpec((1, HG_DIM), lambda b, hh, s: (0, hh))
    return pl.pallas_call(
        functools.partial(_hgrn_kernel, n_chunks=tb // HG_CHUNK),
        out_shape=jax.ShapeDtypeStruct((t, width), BF16),
        grid=(batch, HG_HEADS, nsb),
        in_specs=[col(0), col(1), col(2), col(3), vec, vec, vec],
        out_specs=pl.BlockSpec((tb, HG_DIM), lambda b, hh, s: (b * nsb + s, hh)),
        scratch_shapes=[pltpu.VMEM((HG_DIM, HG_DIM), F32)],
        compiler_params=_params("parallel", "parallel", "arbitrary"),
        name="hgrn2",
    )(h, h, h, h, lb, log1m_lb, norm_g)


def _attn_kernel(*refs, seq):
    qkv_refs = refs[:9]
    cos_ref, sin_ref, o_ref = refs[9:12]
    qf, kf, vf, og, lg = refs[12:]
    n_groups = len(ATT_GROUPS)
    scale = ATT_DIM ** -0.5
    rb = min(seq, 256)

    def rope_rows(i, carry):
        r0 = pl.multiple_of(i * rb, rb)
        cs = cos_ref[pl.ds(r0, rb), :]
        sn = sin_ref[pl.ds(r0, rb), :]
        for gi in range(n_groups):
            xq = qkv_refs[3 * gi][pl.ds(r0, rb), :].astype(F32)
            xk = qkv_refs[3 * gi + 1][pl.ds(r0, rb), :].astype(F32)
            qf[gi, pl.ds(r0, rb), :] = (xq * cs + pltpu.roll(xq, ATT_DIM // 2, 1) * sn) * scale
            kf[gi, pl.ds(r0, rb), :] = xk * cs + pltpu.roll(xk, ATT_DIM // 2, 1) * sn
            vf[gi, pl.ds(r0, rb), :] = qkv_refs[3 * gi + 2][pl.ds(r0, rb), :].astype(F32)
        return carry

    lax.fori_loop(0, seq // rb, rope_rows, 0)

    qb = ATT_BACK
    for gi, (_, dil) in enumerate(ATT_GROUPS):
        length = seq // dil
        nk = min(2 * qb, length)
        n_qblk = length // qb

        def block(it, carry, gi=gi, dil=dil, nk=nk, n_qblk=n_qblk):
            res = it // n_qblk
            iq = it % n_qblk
            q0 = iq * qb
            k0 = jnp.maximum(q0 - qb, 0)
            if dil == 1:
                q_rows = pl.ds(pl.multiple_of(q0, qb), qb)
                k_rows = pl.ds(pl.multiple_of(k0, qb), nk)
            else:
                q_rows = pl.ds(res + dil * q0, qb, stride=dil)
                k_rows = pl.ds(res + dil * k0, nk, stride=dil)
            q = qf[gi, q_rows, :].astype(BF16)
            k = kf[gi, k_rows, :].astype(BF16)
            v = vf[gi, k_rows, :].astype(BF16)
            s = lax.dot_general(q, k, NT_DIMS, preferred_element_type=F32)
            dist = (q0 + lax.broadcasted_iota(I32, (qb, nk), 0)) - (k0 + lax.broadcasted_iota(I32, (qb, nk), 1))
            s = jnp.where((dist >= 0) & (dist <= ATT_BACK), s, -jnp.inf)
            m = jnp.max(s, axis=-1, keepdims=True)
            p = jnp.exp(s - m)
            den = jnp.sum(p, axis=-1, keepdims=True)
            o = jnp.dot(p.astype(BF16), v, preferred_element_type=F32) / den
            og[gi, q_rows, :] = o
            lg[gi, q_rows, :] = jnp.broadcast_to(m + jnp.log(den), (qb, ATT_DIM))
            return carry

        lax.fori_loop(0, dil * n_qblk, block, 0, unroll=8)

    def merge_rows(i, carry):
        r0 = pl.multiple_of(i * rb, rb)
        ls = [lg[gi, pl.ds(r0, rb), :] for gi in range(n_groups)]
        m = functools.reduce(jnp.maximum, ls)
        ws = [jnp.exp(l - m) for l in ls]
        num = sum(w * og[gi, pl.ds(r0, rb), :] for gi, w in enumerate(ws))
        o_ref[pl.ds(r0, rb), :] = (num / sum(ws)).astype(o_ref.dtype)
        return carry

    lax.fori_loop(0, seq // rb, merge_rows, 0)


def _attention(h, cos, sin, batch, seq, col0, batch0):
    t = h.shape[0]
    n_groups = len(ATT_GROUPS)
    part = n_groups * ATT_HEADS * ATT_DIM
    blk0 = col0 // ATT_DIM
    in_specs = []
    for gi in range(n_groups):
        for p in range(3):
            off = blk0 + (p * part) // ATT_DIM + gi * ATT_HEADS
            in_specs.append(pl.BlockSpec((seq, ATT_DIM), lambda b, hh, off=off: (b, off + hh)))
    tab = pl.BlockSpec((seq, ATT_DIM), lambda b, hh: (batch0 + b, 0))
    in_specs += [tab, tab]
    scr = pltpu.VMEM((n_groups, seq, ATT_DIM), F32)
    return pl.pallas_call(
        functools.partial(_attn_kernel, seq=seq),
        out_shape=jax.ShapeDtypeStruct((t, ATT_HEADS * ATT_DIM), BF16),
        grid=(batch, ATT_HEADS),
        in_specs=in_specs,
        out_specs=pl.BlockSpec((seq, ATT_DIM), lambda b, hh: (b, hh)),
        scratch_shapes=[scr] * 5,
        compiler_params=_params("parallel", "parallel"),
        name="dilated_attn",
    )(*([h] * 9), cos, sin)


def _layer_norm_rows(r, g, b):
    mu = jnp.mean(r, axis=-1, keepdims=True)
    d = r - mu
    var = jnp.mean(d * d, axis=-1, keepdims=True)
    return d * lax.rsqrt(var + LN_EPS) * g + b


def _pack_halves(y):
    n = y.shape[1] // 2
    return pltpu.pack_elementwise([y[:, :n], y[:, n:]], packed_dtype=BF16)


def _unpack_halves(w):
    first = pltpu.unpack_elementwise(w, index=0, packed_dtype=BF16, unpacked_dtype=F32)
    second = pltpu.unpack_elementwise(w, index=1, packed_dtype=BF16, unpacked_dtype=F32)
    return first, second


def _mix_out_kernel(*refs, alpha, n_col):
    x_ref, oh_ref, oa_ref = refs[:3]
    gh_refs = refs[3:3 + n_col]
    ga_refs = refs[3 + n_col:3 + 2 * n_col]
    whg_ref, wap_ref, wout_ref, g_ref, b_ref, x1_ref, x1p_ref = refs[3 + 2 * n_col:]
    tn = gh_refs[0].shape[1]
    oh = oh_ref[...]
    oa = oa_ref[...]
    merged = []
    for n in range(n_col):
        cols = slice(n * tn, (n + 1) * tn)
        y_h = jnp.dot(oh, whg_ref[:, cols], preferred_element_type=F32)
        y_a = jnp.dot(oa, wap_ref[:, cols], preferred_element_type=F32)
        m = _sigmoid(gh_refs[n][...].astype(F32)) * y_h + _sigmoid(ga_refs[n][...].astype(F32)) * y_a
        merged.append(m.astype(BF16))
    mix = jnp.dot(jnp.concatenate(merged, axis=1), wout_ref[...], preferred_element_type=F32)
    y = _layer_norm_rows(alpha * x_ref[...] + mix, g_ref[...], b_ref[...])
    x1_ref[...] = y
    x1p_ref[...] = _pack_halves(y)


def _mix_out(x, x_row0, h, o_h, o_a, w_hg, w_ap, w_out, ln_g, ln_b, gate_col0, alpha):
    t = h.shape[0]
    d = x.shape[1]
    tm = min(t, 256)
    x_blk0 = x_row0 // tm
    tn = 512
    n_col = d // tn
    gh0 = gate_col0 // tn
    ga0 = (gate_col0 + d) // tn
    rows = lambda i: (i, 0)
    const = lambda i: (0, 0)
    resident = dict(index_map=const, pipeline_mode=pl.Buffered(1))
    gate_specs = [pl.BlockSpec((tm, tn), lambda i, c=c0 + n: (i, c)) for c0 in (gh0, ga0) for n in range(n_col)]
    return pl.pallas_call(
        functools.partial(_mix_out_kernel, alpha=alpha, n_col=n_col),
        out_shape=(jax.ShapeDtypeStruct((t, d), F32), jax.ShapeDtypeStruct((t, d // 2), I32)),
        grid=(t // tm,),
        in_specs=[pl.BlockSpec((tm, d), lambda i: (x_blk0 + i, 0)),
                  pl.BlockSpec((tm, o_h.shape[1]), rows),
                  pl.BlockSpec((tm, o_a.shape[1]), rows),
                  *gate_specs,
                  pl.BlockSpec(w_hg.shape, **resident),
                  pl.BlockSpec(w_ap.shape, **resident),
                  pl.BlockSpec(w_out.shape, **resident),
                  pl.BlockSpec((1, d), const),
                  pl.BlockSpec((1, d), const)],
        out_specs=(pl.BlockSpec((tm, d), rows), pl.BlockSpec((tm, d // 2), rows)),
        compiler_params=_params("parallel"),
        name="mix_out",
    )(x, o_h, o_a, *([h] * (2 * n_col)), w_hg, w_ap, w_out, ln_g, ln_b)


def _split_bf16(x):
    hi = x.astype(BF16)
    lo = (x - hi.astype(F32)).astype(BF16)
    return hi, lo


def _router_kernel(x_ref, wr_ref, bias_ref, tri_ref, low_ref, rank_ref, eid_ref, wgt_ref, cnt_ref,
                   carry_ref, *, tr):
    ne = N_EXPERTS
    per = ne // N_GROUPS

    @pl.when(pl.program_id(0) == 0)
    def _():
        carry_ref[...] = jnp.zeros_like(carry_ref)

    xh, xl = _split_bf16(x_ref[...])
    wh, wl = _split_bf16(wr_ref[...])
    logits = (lax.dot_general(wh, xh, NT_DIMS, preferred_element_type=F32)
              + lax.dot_general(wh, xl, NT_DIMS, preferred_element_type=F32)
              + lax.dot_general(wl, xh, NT_DIMS, preferred_element_type=F32))
    scores = _sigmoid(logits)
    sel = scores + bias_ref[...]

    grp = sel.reshape(N_GROUPS, per, tr)
    sub = lax.broadcasted_iota(I32, grp.shape, 1)
    m1 = jnp.max(grp, axis=1, keepdims=True)
    first = jnp.min(jnp.where(grp == m1, sub, per), axis=1, keepdims=True)
    m2 = jnp.max(jnp.where(sub == first, -jnp.inf, grp), axis=1, keepdims=True)
    gs = m1 + m2
    gidx = lax.broadcasted_iota(I32, gs.shape, 0)
    grank = jnp.zeros(gs.shape, I32)
    for j in range(N_GROUPS):
        other = gs[j:j + 1]
        grank += ((other > gs) | ((other == gs) & (j < gidx))).astype(I32)
    masked = jnp.where(grank < TOPK_GROUPS, grp, -jnp.inf).reshape(ne, tr)

    eidx = lax.broadcasted_iota(I32, (ne, tr), 0)
    work = masked
    picked = jnp.zeros((ne, tr), F32)
    for _ in range(TOP_K):
        top = jnp.max(work, axis=0, keepdims=True)
        first = jnp.min(jnp.where(work == top, eidx, ne), axis=0, keepdims=True)
        hit = eidx == first
        picked = jnp.where(hit, 1.0, picked)
        work = jnp.where(hit, -jnp.inf, work)
    chosen = picked > 0.0
    w = jnp.where(chosen, scores, 0.0)
    gates = w / jnp.sum(w, axis=0, keepdims=True) * ROUTED_SCALE

    chosen_b = jnp.where(chosen, 1.0, 0.0).astype(BF16)
    incl = jnp.dot(chosen_b, tri_ref[...], preferred_element_type=F32)
    carry = carry_ref[...]
    rank_in_expert = (carry + incl - 1.0).astype(I32)
    carry_new = carry + incl[:, tr - 1:tr]
    carry_ref[...] = carry_new
    cnt_ref[...] = jnp.broadcast_to(carry_new, cnt_ref.shape).astype(I32)

    slot = jnp.dot(low_ref[...], chosen_b, preferred_element_type=F32).astype(I32)
    for j in range(TOP_K):
        pick = chosen & (slot == j)
        rank_ref[pl.ds(j, 1), :] = jnp.sum(jnp.where(pick, rank_in_expert, 0), axis=0, keepdims=True)
        eid_ref[pl.ds(j, 1), :] = jnp.sum(jnp.where(pick, eidx, 0), axis=0, keepdims=True)
        wgt_ref[pl.ds(j, 1), :] = jnp.sum(jnp.where(pick, gates, 0.0), axis=0, keepdims=True)


def _router(x1, w_router_t, bias_col, row0, t):
    d = x1.shape[1]
    ne = N_EXPERTS
    tr = min(t, 512)
    blk0 = row0 // tr
    tri = jnp.asarray(np.triu(np.ones((tr, tr), np.float32)), BF16)
    low = jnp.asarray(np.tril(np.ones((ne, ne), np.float32), -1), BF16)
    slot_shape = jax.ShapeDtypeStruct((TOP_K, t), I32)
    slot_spec = pl.BlockSpec((TOP_K, tr), lambda i: (0, i))
    return pl.pallas_call(
        functools.partial(_router_kernel, tr=tr),
        out_shape=(slot_shape, slot_shape, jax.ShapeDtypeStruct((TOP_K, t), F32),
                   jax.ShapeDtypeStruct((ne, LANES), I32)),
        grid=(t // tr,),
        in_specs=[pl.BlockSpec((tr, d), lambda i: (blk0 + i, 0)),
                  pl.BlockSpec((ne, d), lambda i: (0, 0)),
                  pl.BlockSpec((ne, 1), lambda i: (0, 0)),
                  pl.BlockSpec((tr, tr), lambda i: (0, 0)),
                  pl.BlockSpec((ne, ne), lambda i: (0, 0))],
        out_specs=(slot_spec, slot_spec, slot_spec, pl.BlockSpec((ne, LANES), lambda i: (0, 0))),
        scratch_shapes=[pltpu.VMEM((ne, 1), F32)],
        compiler_params=_params("arbitrary"),
        name="router",
    )(x1, w_router_t, bias_col, tri, low)


def _sc_worker_id():
    return lax.axis_index("s") * SC_CORES + lax.axis_index("c")


def _dispatch(pos, x1p, row0):
    t = pos.shape[1]
    dp = x1p.shape[1]
    t_per_w = t // SC_WORKERS
    chunk = min(SC_SCATTER_ROWS, t_per_w // 2)
    n_chunks = t_per_w // chunk
    mesh = plsc.VectorSubcoreMesh(core_axis_name="c", subcore_axis_name="s")

    @functools.partial(
        pl.kernel, mesh=mesh,
        out_type=jax.ShapeDtypeStruct((t * TOP_K, dp), x1p.dtype),
        scratch_types=[pltpu.VMEM((n_chunks * TOP_K, chunk), I32),
                       pltpu.VMEM((2, chunk, dp), x1p.dtype),
                       pltpu.SemaphoreType.DMA((2,)),
                       pltpu.SemaphoreType.DMA((2,))],
    )
    def scatter_rows(x_hbm, pos_hbm, out_hbm, idx_v, rows_v, load_sem, scat_sem):
        wid = _sc_worker_id()
        base = row0 + wid * t_per_w
        pltpu.sync_copy(pos_hbm.at[wid], idx_v)

        def load(c, b):
            return pltpu.make_async_copy(x_hbm.at[pl.ds(base + c * chunk, chunk)], rows_v.at[b], load_sem.at[b])

        def scat(c, b, j):
            return pltpu.make_async_copy(rows_v.at[b], out_hbm.at[idx_v.at[c * TOP_K + j]], scat_sem.at[b])

        load(0, 0).start()

        @pl.loop(0, n_chunks, step=2)
        def _(c0):
            for b in range(2):
                c = c0 + b
                load(c, b).wait()
                for j in range(TOP_K):
                    scat(c, b, j).start()

                @pl.when(c + 1 < n_chunks)
                def _():
                    @pl.when(c >= 1)
                    def _():
                        for j in range(TOP_K):
                            scat(c - 1, 1 - b, j).wait()
                    load(c + 1, 1 - b).start()

        for b in range(2):
            for j in range(TOP_K):
                scat(n_chunks - 2 + b, b, j).wait()

    idx = pos.reshape(TOP_K, SC_WORKERS, n_chunks, chunk).transpose(1, 2, 0, 3)
    return scatter_rows(x1p, idx.reshape(SC_WORKERS, n_chunks * TOP_K, chunk))


def _gather_rows(table, idx):
    n_rows = idx.shape[0]
    dp = table.shape[1]
    r_per_w = n_rows // SC_WORKERS
    chunk, nbuf = SC_GATHER_ROWS, SC_GATHER_BUFFERS
    n_chunks = r_per_w // chunk
    mesh = plsc.VectorSubcoreMesh(core_axis_name="c", subcore_axis_name="s")

    @functools.partial(
        pl.kernel, mesh=mesh,
        out_type=jax.ShapeDtypeStruct((n_rows, dp), table.dtype),
        scratch_types=[pltpu.VMEM((n_chunks, chunk), I32),
                       pltpu.VMEM((nbuf, chunk, dp), table.dtype),
                       pltpu.SemaphoreType.DMA((nbuf,)),
                       pltpu.SemaphoreType.DMA((nbuf,))],
    )
    def gather_rows(table_hbm, idx_hbm, out_hbm, idx_v, rows_v, gat_sem, store_sem):
        wid = _sc_worker_id()
        base = wid * r_per_w
        pltpu.sync_copy(idx_hbm.at[wid], idx_v)

        def gather(c, b):
            return pltpu.make_async_copy(table_hbm.at[idx_v.at[c]], rows_v.at[b], gat_sem.at[b])

        def store(c, b):
            return pltpu.make_async_copy(rows_v.at[b], out_hbm.at[pl.ds(base + c * chunk, chunk)], store_sem.at[b])

        for b in range(nbuf - 1):
            gather(b, b).start()

        @pl.loop(0, n_chunks, step=nbuf)
        def _(c0):
            for b in range(nbuf):
                c = c0 + b
                gather(c, b).wait()
                store(c, b).start()
                nb = (b + nbuf - 1) % nbuf

                @pl.when(c + nbuf - 1 < n_chunks)
                def _():
                    @pl.when(c >= 1)
                    def _():
                        store(c - 1, nb).wait()
                    gather(c + nbuf - 1, nb).start()

        for b in range(nbuf):
            store(n_chunks - nbuf + b, b).wait()

    return gather_rows(table, idx.reshape(SC_WORKERS, n_chunks, chunk))


def _experts_kernel(tile_ref, exp_ref, lo_ref, hi_ref, slot_ref, next_ref, nv_ref, xs_ref, wg_hbm, wu_hbm, wd_hbm,
                    ys_ref, wgf, wuf, wdf, wsem, wgb, wub, wdb, hid_ref, ybuf_ref, *, layer):
    it = pl.program_id(0)
    nv = nv_ref[0]
    cur = jnp.minimum(it, nv - 1)
    prev = jnp.maximum(it - 1, 0)

    def weight_copies(e):
        return [pltpu.make_async_copy(src.at[layer, e], dst, wsem.at[k])
                for k, (src, dst) in enumerate(((wg_hbm, wgf), (wu_hbm, wuf), (wd_hbm, wdf)))]

    @pl.when(it == 0)
    def _():
        hid_ref[...] = jnp.zeros_like(hid_ref)
        wdb[...] = jnp.zeros_like(wdb)
        ybuf_ref[...] = jnp.zeros_like(ybuf_ref)
        for cp in weight_copies(exp_ref[0]):
            cp.start()

    @pl.when((it < nv) & ((it == 0) | (exp_ref[cur] != exp_ref[prev])))
    def _():
        for cp in weight_copies(exp_ref[cur]):
            cp.wait()
        slot = slot_ref[cur]
        rows_g = wgf.shape[0] // EXPERT_CAST_STEPS
        rows_d = wdf.shape[0] // EXPERT_CAST_STEPS

        def cast_rows(i, carry):
            rg = pl.ds(pl.multiple_of(i * rows_g, rows_g), rows_g)
            rd = pl.ds(pl.multiple_of(i * rows_d, rows_d), rows_d)
            wgb[rg, :] = wgf[rg, :].astype(BF16)
            wub[rg, :] = wuf[rg, :].astype(BF16)
            wdb[slot, rd, :] = wdf[rd, :].astype(BF16)
            return carry

        lax.fori_loop(0, EXPERT_CAST_STEPS, cast_rows, 0)

        @pl.when(next_ref[cur] >= 0)
        def _():
            for cp in weight_copies(next_ref[cur]):
                cp.start()

    tm, half = xs_ref.shape
    y = _pack_halves(jnp.dot(hid_ref[(it + 1) % 2], wdb[slot_ref[prev]], preferred_element_type=F32))
    a, b = _unpack_halves(xs_ref[...])
    x = jnp.concatenate([a.astype(BF16), b.astype(BF16)], axis=1)
    gate = jnp.dot(x, wgb[...], preferred_element_type=F32)
    up = jnp.dot(x, wub[...], preferred_element_type=F32)
    hid_ref[it % 2] = (gate * _sigmoid(gate) * up).astype(BF16)

    done = (it >= 1) & (it <= nv)
    lo = jnp.where(done, lo_ref[prev], 0)
    hi = jnp.where(done, hi_ref[prev], 0)
    row = lax.broadcasted_iota(I32, (tm, half), 0)
    merged = jnp.where((row >= lo) & (row < hi), y, ybuf_ref[...])
    ybuf_ref[...] = merged
    ys_ref[...] = merged


def _experts(items, xs, w_gate, w_up, w_down, layer, tm):
    n_rows, dp = xs.shape
    _, ne, d, ff = w_gate.shape
    n_items = items[0].shape[0]

    def cur_map(i, tl, ex, lo, hi, sl, nx, nv):
        return (tl[jnp.minimum(i, nv[0] - 1)], 0)

    def prev_map(i, tl, ex, lo, hi, sl, nx, nv):
        return (tl[jnp.clip(i - 1, 0, nv[0] - 1)], 0)

    hbm = pl.BlockSpec(memory_space=pl.ANY)
    return pl.pallas_call(
        functools.partial(_experts_kernel, layer=layer),
        out_shape=jax.ShapeDtypeStruct((n_rows, dp), I32),
        grid_spec=pltpu.PrefetchScalarGridSpec(
            num_scalar_prefetch=7,
            grid=(n_items + 1,),
            in_specs=[pl.BlockSpec((tm, dp), cur_map), hbm, hbm, hbm],
            out_specs=pl.BlockSpec((tm, dp), prev_map),
            scratch_shapes=[pltpu.VMEM((d, ff), F32), pltpu.VMEM((d, ff), F32), pltpu.VMEM((ff, d), F32),
                            pltpu.SemaphoreType.DMA((3,)),
                            pltpu.VMEM((d, ff), BF16), pltpu.VMEM((d, ff), BF16), pltpu.VMEM((2, ff, d), BF16),
                            pltpu.VMEM((2, tm, ff), BF16), pltpu.VMEM((tm, dp), I32)]),
        compiler_params=_params("arbitrary"),
        name="experts",
    )(*items, xs, w_gate, w_up, w_down)


def _combine_kernel(*refs, alpha):
    x_ref, wk_ref, yk_ref, wsg_ref, wsu_ref, wsd_ref, g_ref, b_ref = refs[:8]
    x2_ref, x2b_ref = refs[-2:]
    x = x_ref[...]
    xb = x.astype(BF16)
    gate = jnp.dot(xb, wsg_ref[...], preferred_element_type=F32)
    up = jnp.dot(xb, wsu_ref[...], preferred_element_type=F32)
    hid = (gate * _sigmoid(gate) * up).astype(BF16)
    shared = jnp.dot(hid, wsd_ref[...], preferred_element_type=F32)

    tc, half = yk_ref.shape[1:]
    wk = wk_ref[...]
    acc_hi = jnp.zeros((tc, half), F32)
    acc_lo = jnp.zeros((tc, half), F32)
    for j in range(TOP_K):
        hi, lo = _unpack_halves(yk_ref[j])
        wj = wk[:, j:j + 1]
        acc_hi += wj * hi
        acc_lo += wj * lo
    ffn = jnp.concatenate([acc_hi, acc_lo], axis=1) + shared
    y = _layer_norm_rows(alpha * x + ffn, g_ref[...], b_ref[...])
    x2_ref[...] = y
    x2b_ref[...] = y.astype(BF16)


def _combine(x1, wk, yk, ws_gate, ws_up, ws_down, ln_g, ln_b, alpha, out_rows, out_row0, prev):
    t, d = x1.shape
    t_all = out_rows
    dp = yk.shape[2]
    ff = ws_gate.shape[1]
    tc = min(t, 256)
    blk0 = out_row0 // tc
    const = lambda i: (0, 0)
    rows = lambda i: (i, 0)
    rows_all = lambda i: (blk0 + i, 0)
    in_specs = [pl.BlockSpec((tc, d), rows),
                pl.BlockSpec((tc, TOP_K), rows),
                pl.BlockSpec((TOP_K, tc, dp), lambda i: (0, i, 0)),
                pl.BlockSpec((d, ff), const),
                pl.BlockSpec((d, ff), const),
                pl.BlockSpec((ff, d), const),
                pl.BlockSpec((1, d), const),
                pl.BlockSpec((1, d), const)]
    args = [x1, wk, yk, ws_gate, ws_up, ws_down, ln_g, ln_b]
    aliases = {}
    if prev is not None:
        in_specs += [pl.BlockSpec(memory_space=pl.ANY)] * 2
        aliases = {len(args): 0, len(args) + 1: 1}
        args += list(prev)
    return pl.pallas_call(
        functools.partial(_combine_kernel, alpha=alpha),
        out_shape=(jax.ShapeDtypeStruct((t_all, d), F32), jax.ShapeDtypeStruct((t_all, d), BF16)),
        grid=(t // tc,),
        in_specs=in_specs,
        out_specs=(pl.BlockSpec((tc, d), rows_all), pl.BlockSpec((tc, d), rows_all)),
        input_output_aliases=aliases,
        compiler_params=_params("parallel"),
        name="combine",
    )(*args)


def _expert_work_items(counts, n_rows, tm):
    ne = counts.shape[0]
    end = jnp.cumsum(counts)
    start = end - counts
    first_tile = start // tm
    n_it = jnp.where(counts > 0, (end - 1) // tm - first_tile + 1, 0)
    it_end = jnp.cumsum(n_it)
    it_start = it_end - n_it
    n_items = n_rows // tm + ne
    k = jnp.arange(n_items, dtype=I32)
    expert = jnp.minimum(jnp.sum((it_end[None, :] <= k[:, None]).astype(I32), axis=1), ne - 1)
    onehot = expert[:, None] == jnp.arange(ne, dtype=I32)[None, :]
    pick = lambda v: jnp.sum(jnp.where(onehot, v[None, :], 0), axis=1)
    tile = jnp.clip(pick(first_tile) + k - pick(it_start), 0, n_rows // tm - 1)
    lo = jnp.maximum(pick(start) - tile * tm, 0)
    hi = jnp.minimum(pick(end) - tile * tm, tm)
    changed = jnp.concatenate([jnp.zeros((1,), I32), (expert[1:] != expert[:-1]).astype(I32)])
    slot = jnp.cumsum(changed) % 2
    later = (k[None, :] < it_end[-1]) & (expert[None, :] > expert[:, None])
    nxt = jnp.min(jnp.where(later, expert[None, :], ne), axis=1)
    nxt = jnp.where(nxt < ne, nxt, -1)
    return start.astype(I32), (tile.astype(I32), expert.astype(I32), lo.astype(I32), hi.astype(I32),
                               slot.astype(I32), nxt.astype(I32), it_end[-1:].astype(I32))


def _moe(x1, x1p, w_router_t, bias_col, w_gate, w_up, w_down, layer, ws_gate, ws_up, ws_down, ln_g, ln_b, alpha,
         out_rows, out_row0, prev):
    t = x1.shape[0]
    tm = 256
    rank, eid, wgt, cnt = _router(x1, w_router_t, bias_col, 0, t)
    expert_start, items = _expert_work_items(cnt[:, 0], t * TOP_K, tm)
    pos = rank
    for e in range(N_EXPERTS):
        pos = pos + jnp.where(eid == e, expert_start[e], 0)
    xs = _dispatch(pos, x1p, 0)
    ys = _experts(items, xs, w_gate, w_up, w_down, layer, tm)
    yk = _gather_rows(ys, pos.reshape(-1)).reshape(TOP_K, t, -1)
    return _combine(x1, wgt.T, yk, ws_gate, ws_up, ws_down, ln_g, ln_b, alpha, out_rows, out_row0, prev)


def kernel(x, positions, w_in, lb_logits, hg_norm_g, w_hg_proj, w_att_proj, w_out, ln1_g, ln1_b,
           w_router, router_bias, w_e_gate, w_e_up, w_e_down, w_s_gate, w_s_up, w_s_down, ln2_g, ln2_b):
    batch, seq, d = x.shape
    depth = w_in.shape[0]
    t = batch * seq
    alpha = float((2 * depth) ** 0.25)
    hg_width = HG_HEADS * HG_DIM
    att_col0 = 4 * hg_width
    gate_col0 = att_col0 + 3 * len(ATT_GROUPS) * ATT_HEADS * ATT_DIM

    p = jax.nn.softmax(lb_logits.astype(F32), axis=0)
    cs = jnp.cumsum(p, axis=0)
    lower = cs - cs[:1]
    log1m_lower = jnp.log1p(-lower)

    cos, sin = _rope_tables(positions)
    x_all = x.reshape(t, d)
    in_width = w_in.shape[2]
    tn = 1280 if in_width % 1280 == 0 else 512

    n_parts = BATCH_PIPELINES if batch % BATCH_PIPELINES == 0 else 1
    bp = batch // n_parts
    tp = bp * seq
    state = [((x_all, part * tp), x_all[part * tp:(part + 1) * tp].astype(BF16)) for part in range(n_parts)]
    final = None
    for layer in range(depth):
        last = layer == depth - 1
        for part in range(n_parts):
            (xf, x_row0), xb = state[part]
            h = _matmul(xb, w_in, layer, min(tp, 1024), tn, BF16)
            o_h = _hgrn(h, lower[layer].reshape(1, -1), log1m_lower[layer].reshape(1, -1),
                        hg_norm_g[layer].reshape(1, -1), bp, seq)
            o_a = _attention(h, cos, sin, bp, seq, att_col0, part * bp)
            x1, x1p = _mix_out(xf, x_row0, h, o_h, o_a, w_hg_proj[layer].astype(BF16),
                               w_att_proj[layer].astype(BF16), w_out[layer].astype(BF16),
                               ln1_g[layer].reshape(1, d), ln1_b[layer].reshape(1, d), gate_col0, alpha)
            out = _moe(x1, x1p, w_router[layer].T, router_bias[layer].reshape(-1, 1),
                       w_e_gate, w_e_up, w_e_down, layer,
                       w_s_gate[layer].astype(BF16), w_s_up[layer].astype(BF16), w_s_down[layer].astype(BF16),
                       ln2_g[layer].reshape(1, d), ln2_b[layer].reshape(1, d), alpha,
                       t if last else tp, part * tp if last else 0, final if last else None)
            if last:
                final = out
            else:
                state[part] = ((out[0], 0), out[1])
    return final[0].reshape(batch, seq, d)
```

```python
import functools

import numpy as np
import jax
import jax.numpy as jnp
from jax import lax
from jax.experimental import pallas as pl
from jax.experimental.pallas import tpu as pltpu
from jax.experimental.pallas import tpu_sc as plsc

F32 = jnp.float32
BF16 = jnp.bfloat16
I32 = jnp.int32

LANES = 128
VMEM_LIMIT_BYTES = 56 * 1024 * 1024
SC_CORES = 2
SC_SUBCORES = 16
SC_WORKERS = SC_CORES * SC_SUBCORES
SC_SCATTER_ROWS = 32
SC_GATHER_ROWS = 16
SC_GATHER_BUFFERS = 4
BATCH_PIPELINES = 1
EXPERT_CAST_STEPS = 16

HG_HEADS = 8
HG_DIM = 128
HG_CHUNK = 64
HG_SUB = 16
ATT_GROUPS = ((128, 1), (512, 4), (2048, 16))
ATT_HEADS = 4
ATT_DIM = 128
ATT_BACK = 128
ROPE_THETA = 10000.0
N_EXPERTS = 64
N_GROUPS = 8
TOPK_GROUPS = 4
TOP_K = 8
ROUTED_SCALE = 2.5
LN_EPS = 1e-5
NORM_EPS = 1e-6

NT_DIMS = (((1,), (1,)), ((), ()))
TN_DIMS = (((0,), (0,)), ((), ()))


def _params(*sem):
    return pltpu.CompilerParams(dimension_semantics=sem, vmem_limit_bytes=VMEM_LIMIT_BYTES)


def _sigmoid(x):
    return 1.0 / (1.0 + jnp.exp(-x))


def _matmul_kernel(x_ref, w_ref, o_ref, wb_ref):
    @pl.when(pl.program_id(1) == 0)
    def _():
        wb_ref[...] = w_ref[0].astype(BF16)

    o_ref[...] = jnp.dot(x_ref[...], wb_ref[...], preferred_element_type=F32).astype(o_ref.dtype)


def _matmul(x, w, layer, tm, tn, out_dtype):
    m, k = x.shape
    n = w.shape[2]
    return pl.pallas_call(
        _matmul_kernel,
        out_shape=jax.ShapeDtypeStruct((m, n), out_dtype),
        grid=(n // tn, m // tm),
        in_specs=[pl.BlockSpec((tm, k), lambda j, i: (i, 0)),
                  pl.BlockSpec((1, k, tn), lambda j, i: (layer, 0, j))],
        out_specs=pl.BlockSpec((tm, tn), lambda j, i: (i, j)),
        scratch_shapes=[pltpu.VMEM((k, tn), BF16)],
        compiler_params=_params("parallel", "arbitrary"),
        name="in_proj",
    )(x, w)


def _rope_table_kernel(pos_ref, inv_ref, sign_ref, cos_ref, sin_ref):
    ang = pos_ref[...].astype(F32) * inv_ref[...]
    cos_ref[...] = jnp.cos(ang)
    sin_ref[...] = jnp.sin(ang) * sign_ref[...]


def _rope_tables(positions):
    t = positions.size
    half = ATT_DIM // 2
    inv_half = ROPE_THETA ** (-np.arange(half, dtype=np.float32) * np.float32(2.0) / np.float32(ATT_DIM))
    inv = jnp.asarray(np.concatenate([inv_half, inv_half]).astype(np.float32).reshape(1, ATT_DIM))
    sign = jnp.asarray(np.concatenate([-np.ones(half), np.ones(half)]).astype(np.float32).reshape(1, ATT_DIM))
    tm = min(t, 2048)
    return pl.pallas_call(
        _rope_table_kernel,
        out_shape=(jax.ShapeDtypeStruct((t, ATT_DIM), F32), jax.ShapeDtypeStruct((t, ATT_DIM), F32)),
        grid=(t // tm,),
        in_specs=[pl.BlockSpec((tm, 1), lambda i: (i, 0)),
                  pl.BlockSpec((1, ATT_DIM), lambda i: (0, 0)),
                  pl.BlockSpec((1, ATT_DIM), lambda i: (0, 0))],
        out_specs=(pl.BlockSpec((tm, ATT_DIM), lambda i: (i, 0)),
                   pl.BlockSpec((tm, ATT_DIM), lambda i: (i, 0))),
        compiler_params=_params("parallel"),
        name="rope_tables",
    )(positions.reshape(t, 1), inv, sign)


def _cumsum_rows(x):
    n = x.shape[0]
    row = lax.broadcasted_iota(I32, x.shape, 0)
    s = 1
    while s < n:
        x = x + jnp.where(row >= s, pltpu.roll(x, s, 0), 0.0)
        s *= 2
    return x


def _rows_from(cum, offsets):
    parts = []
    for o in offsets:
        if o is None:
            parts.append(jnp.zeros((HG_SUB, cum.shape[1]), F32))
        else:
            parts.append(jnp.broadcast_to(cum[o:o + 1, :], (HG_SUB, cum.shape[1])))
    return jnp.concatenate(parts, axis=0)


def _hgrn_kernel(hq_ref, hf_ref, hi_ref, hg_ref, lb_ref, l1m_ref, ng_ref, o_ref, st_ref, *, n_chunks):
    c = HG_CHUNK
    nsub = c // HG_SUB

    @pl.when(pl.program_id(2) == 0)
    def _():
        st_ref[...] = jnp.zeros_like(st_ref)

    lb = lb_ref[...]
    one_m_lb = 1.0 - lb
    log1m_lb = l1m_ref[...]
    norm_g = ng_ref[...]
    row = lax.broadcasted_iota(I32, (c, HG_DIM), 0)
    blk = row // HG_SUB
    ti = lax.broadcasted_iota(I32, (c, c), 0)
    si = lax.broadcasted_iota(I32, (c, c), 1)
    diag_mask = ((ti // HG_SUB) == (si // HG_SUB)) & (si <= ti)

    def chunk(ci, carry):
        r0 = pl.multiple_of(ci * c, c)
        z = hf_ref[pl.ds(r0, c), :].astype(F32)
        qraw = hq_ref[pl.ds(r0, c), :].astype(F32)
        v = hi_ref[pl.ds(r0, c), :]
        g = hg_ref[pl.ds(r0, c), :].astype(F32)

        e = jnp.exp(-jnp.abs(z))
        r = 1.0 / (1.0 + e)
        sig = jnp.where(z >= 0, r, e * r)
        sig_neg = jnp.where(z >= 0, e * r, r)
        log_sig = jnp.minimum(z, 0.0) - jnp.log(1.0 + e)
        log_f = jnp.maximum(jnp.log(lb + one_m_lb * sig), log1m_lb + log_sig)
        k = one_m_lb * sig_neg
        q = qraw * _sigmoid(qraw)

        cum = _cumsum_rows(log_f)
        start = _rows_from(cum, [None] + [HG_SUB * i - 1 for i in range(1, nsub)])
        mid = _rows_from(cum, [HG_SUB * i + HG_SUB // 2 - 1 for i in range(nsub)])
        end = cum[c - 1:c, :]

        qd = (q * jnp.exp(cum - mid)).astype(BF16)
        kd = (k * jnp.exp(mid - cum)).astype(BF16)
        att = jnp.where(diag_mask, lax.dot_general(qd, kd, NT_DIMS, preferred_element_type=F32), 0.0)

        qs = q * jnp.exp(cum - start)
        q_slots, k_slots = [], []
        for i in range(1, nsub):
            q_slots.append(jnp.where(blk == i, qs, 0.0).astype(BF16))
            n_rows = HG_SUB * i
            b_i = cum[n_rows - 1:n_rows, :]
            k_i = k[:n_rows] * jnp.exp(b_i - cum[:n_rows])
            k_slots.append(jnp.concatenate([k_i, jnp.zeros((c - n_rows, HG_DIM), F32)], axis=0).astype(BF16))
        att = att + lax.dot_general(jnp.concatenate(q_slots, axis=1), jnp.concatenate(k_slots, axis=1),
                                    NT_DIMS, preferred_element_type=F32)
        intra = jnp.dot(att.astype(BF16), v, preferred_element_type=F32)

        st = st_ref[...]
        inter = lax.dot_general((q * jnp.exp(cum)).astype(BF16), st.astype(BF16), NT_DIMS,
                                preferred_element_type=F32)
        k_end = (k * jnp.exp(end - cum)).astype(BF16)
        st_ref[...] = st * jnp.exp(end) + lax.dot_general(v, k_end, TN_DIMS, preferred_element_type=F32)

        o = inter + intra
        o = o * lax.rsqrt(jnp.mean(o * o, axis=-1, keepdims=True) + NORM_EPS) * norm_g
        o_ref[pl.ds(r0, c), :] = (o * _sigmoid(g)).astype(o_ref.dtype)
        return carry

    lax.fori_loop(0, n_chunks, chunk, 0, unroll=8)


def _hgrn(h, lb, log1m_lb, norm_g, batch, seq):
    t = h.shape[0]
    tb = min(seq, 2048)
    nsb = seq // tb
    width = HG_HEADS * HG_DIM

    def col(off):
        return pl.BlockSpec((tb, HG_DIM), lambda b, hh, s: (b * nsb + s, off * HG_HEADS + hh))

    vec = pl.BlockSpec((1, HG_DIM), lambda b, hh, s: (0, hh))
    return pl.pallas_call(
        functools.partial(_hgrn_kernel, n_chunks=tb // HG_CHUNK),
        out_shape=jax.ShapeDtypeStruct((t, width), BF16),
        grid=(batch, HG_HEADS, nsb),
        in_specs=[col(0), col(1), col(2), col(3), vec, vec, vec],
        out_specs=pl.BlockSpec((tb, HG_DIM), lambda b, hh, s: (b * nsb + s, hh)),
        scratch_shapes=[pltpu.VMEM((HG_DIM, HG_DIM), F32)],
        compiler_params=_params("parallel", "parallel", "arbitrary"),
        name="hgrn2",
    )(h, h, h, h, lb, log1m_lb, norm_g)


def _attn_kernel(*refs, seq):
    qkv_refs = refs[:9]
    cos_ref, sin_ref, o_ref = refs[9:12]
    qf, kf, vf, og, lg = refs[12:]
    n_groups = len(ATT_GROUPS)
    scale = ATT_DIM ** -0.5
    rb = min(seq, 256)

    def rope_rows(i, carry):
        r0 = pl.multiple_of(i * rb, rb)
        cs = cos_ref[pl.ds(r0, rb), :]
        sn = sin_ref[pl.ds(r0, rb), :]
        for gi in range(n_groups):
            xq = qkv_refs[3 * gi][pl.ds(r0, rb), :].astype(F32)
            xk = qkv_refs[3 * gi + 1][pl.ds(r0, rb), :].astype(F32)
            qf[gi, pl.ds(r0, rb), :] = (xq * cs + pltpu.roll(xq, ATT_DIM // 2, 1) * sn) * scale
            kf[gi, pl.ds(r0, rb), :] = xk * cs + pltpu.roll(xk, ATT_DIM // 2, 1) * sn
            vf[gi, pl.ds(r0, rb), :] = qkv_refs[3 * gi + 2][pl.ds(r0, rb), :].astype(F32)
        return carry

    lax.fori_loop(0, seq // rb, rope_rows, 0)

    qb = ATT_BACK
    for gi, (_, dil) in enumerate(ATT_GROUPS):
        length = seq // dil
        nk = min(2 * qb, length)
        n_qblk = length // qb

        def block(it, carry, gi=gi, dil=dil, nk=nk, n_qblk=n_qblk):
            res = it // n_qblk
            iq = it % n_qblk
            q0 = iq * qb
            k0 = jnp.maximum(q0 - qb, 0)
            if dil == 1:
                q_rows = pl.ds(pl.multiple_of(q0, qb), qb)
                k_rows = pl.ds(pl.multiple_of(k0, qb), nk)
            else:
                q_rows = pl.ds(res + dil * q0, qb, stride=dil)
                k_rows = pl.ds(res + dil * k0, nk, stride=dil)
            q = qf[gi, q_rows, :].astype(BF16)
            k = kf[gi, k_rows, :].astype(BF16)
            v = vf[gi, k_rows, :].astype(BF16)
            s = lax.dot_general(q, k, NT_DIMS, preferred_element_type=F32)
            dist = (q0 + lax.broadcasted_iota(I32, (qb, nk), 0)) - (k0 + lax.broadcasted_iota(I32, (qb, nk), 1))
            s = jnp.where((dist >= 0) & (dist <= ATT_BACK), s, -jnp.inf)
            m = jnp.max(s, axis=-1, keepdims=True)
            p = jnp.exp(s - m)
            den = jnp.sum(p, axis=-1, keepdims=True)
            o = jnp.dot(p.astype(BF16), v, preferred_element_type=F32) / den
            og[gi, q_rows, :] = o
            lg[gi, q_rows, :] = jnp.broadcast_to(m + jnp.log(den), (qb, ATT_DIM))
            return carry

        lax.fori_loop(0, dil * n_qblk, block, 0, unroll=8)

    def merge_rows(i, carry):
        r0 = pl.multiple_of(i * rb, rb)
        ls = [lg[gi, pl.ds(r0, rb), :] for gi in range(n_groups)]
        m = functools.reduce(jnp.maximum, ls)
        ws = [jnp.exp(l - m) for l in ls]
        num = sum(w * og[gi, pl.ds(r0, rb), :] for gi, w in enumerate(ws))
        o_ref[pl.ds(r0, rb), :] = (num / sum(ws)).astype(o_ref.dtype)
        return carry

    lax.fori_loop(0, seq // rb, merge_rows, 0)


def _attention(h, cos, sin, batch, seq, col0, batch0):
    t = h.shape[0]
    n_groups = len(ATT_GROUPS)
    part = n_groups * ATT_HEADS * ATT_DIM
    blk0 = col0 // ATT_DIM
    in_specs = []
    for gi in range(n_groups):
        for p in range(3):
            off = blk0 + (p * part) // ATT_DIM + gi * ATT_HEADS
            in_specs.append(pl.BlockSpec((seq, ATT_DIM), lambda b, hh, off=off: (b, off + hh)))
    tab = pl.BlockSpec((seq, ATT_DIM), lambda b, hh: (batch0 + b, 0))
    in_specs += [tab, tab]
    scr = pltpu.VMEM((n_groups, seq, ATT_DIM), F32)
    return pl.pallas_call(
        functools.partial(_attn_kernel, seq=seq),
        out_shape=jax.ShapeDtypeStruct((t, ATT_HEADS * ATT_DIM), BF16),
        grid=(batch, ATT_HEADS),
        in_specs=in_specs,
        out_specs=pl.BlockSpec((seq, ATT_DIM), lambda b, hh: (b, hh)),
        scratch_shapes=[scr] * 5,
        compiler_params=_params("parallel", "parallel"),
        name="dilated_attn",
    )(*([h] * 9), cos, sin)


def _layer_norm_rows(r, g, b):
    mu = jnp.mean(r, axis=-1, keepdims=True)
    d = r - mu
    var = jnp.mean(d * d, axis=-1, keepdims=True)
    return d * lax.rsqrt(var + LN_EPS) * g + b


def _pack_halves(y):
    n = y.shape[1] // 2
    return pltpu.pack_elementwise([y[:, :n], y[:, n:]], packed_dtype=BF16)


def _unpack_halves(w):
    first = pltpu.unpack_elementwise(w, index=0, packed_dtype=BF16, unpacked_dtype=F32)
    second = pltpu.unpack_elementwise(w, index=1, packed_dtype=BF16, unpacked_dtype=F32)
    return first, second


def _mix_out_kernel(*refs, alpha, n_col):
    x_ref, oh_ref, oa_ref = refs[:3]
    gh_refs = refs[3:3 + n_col]
    ga_refs = refs[3 + n_col:3 + 2 * n_col]
    whg_ref, wap_ref, wout_ref, g_ref, b_ref, x1_ref, x1p_ref = refs[3 + 2 * n_col:]
    tn = gh_refs[0].shape[1]
    oh = oh_ref[...]
    oa = oa_ref[...]
    merged = []
    for n in range(n_col):
        cols = slice(n * tn, (n + 1) * tn)
        y_h = jnp.dot(oh, whg_ref[:, cols], preferred_element_type=F32)
        y_a = jnp.dot(oa, wap_ref[:, cols], preferred_element_type=F32)
        m = _sigmoid(gh_refs[n][...].astype(F32)) * y_h + _sigmoid(ga_refs[n][...].astype(F32)) * y_a
        merged.append(m.astype(BF16))
    mix = jnp.dot(jnp.concatenate(merged, axis=1), wout_ref[...], preferred_element_type=F32)
    y = _layer_norm_rows(alpha * x_ref[...] + mix, g_ref[...], b_ref[...])
    x1_ref[...] = y
    x1p_ref[...] = _pack_halves(y)


def _mix_out(x, x_row0, h, o_h, o_a, w_hg, w_ap, w_out, ln_g, ln_b, gate_col0, alpha):
    t = h.shape[0]
    d = x.shape[1]
    tm = min(t, 256)
    x_blk0 = x_row0 // tm
    tn = 512
    n_col = d // tn
    gh0 = gate_col0 // tn
    ga0 = (gate_col0 + d) // tn
    rows = lambda i: (i, 0)
    const = lambda i: (0, 0)
    resident = dict(index_map=const, pipeline_mode=pl.Buffered(1))
    gate_specs = [pl.BlockSpec((tm, tn), lambda i, c=c0 + n: (i, c)) for c0 in (gh0, ga0) for n in range(n_col)]
    return pl.pallas_call(
        functools.partial(_mix_out_kernel, alpha=alpha, n_col=n_col),
        out_shape=(jax.ShapeDtypeStruct((t, d), F32), jax.ShapeDtypeStruct((t, d // 2), I32)),
        grid=(t // tm,),
        in_specs=[pl.BlockSpec((tm, d), lambda i: (x_blk0 + i, 0)),
                  pl.BlockSpec((tm, o_h.shape[1]), rows),
                  pl.BlockSpec((tm, o_a.shape[1]), rows),
                  *gate_specs,
                  pl.BlockSpec(w_hg.shape, **resident),
                  pl.BlockSpec(w_ap.shape, **resident),
                  pl.BlockSpec(w_out.shape, **resident),
                  pl.BlockSpec((1, d), const),
                  pl.BlockSpec((1, d), const)],
        out_specs=(pl.BlockSpec((tm, d), rows), pl.BlockSpec((tm, d // 2), rows)),
        compiler_params=_params("parallel"),
        name="mix_out",
    )(x, o_h, o_a, *([h] * (2 * n_col)), w_hg, w_ap, w_out, ln_g, ln_b)


def _split_bf16(x):
    hi = x.astype(BF16)
    lo = (x - hi.astype(F32)).astype(BF16)
    return hi, lo


def _router_kernel(x_ref, wr_ref, bias_ref, tri_ref, low_ref, rank_ref, eid_ref, wgt_ref, cnt_ref,
                   carry_ref, *, tr):
    ne = N_EXPERTS
    per = ne // N_GROUPS

    @pl.when(pl.program_id(0) == 0)
    def _():
        carry_ref[...] = jnp.zeros_like(carry_ref)

    xh, xl = _split_bf16(x_ref[...])
    wh, wl = _split_bf16(wr_ref[...])
    logits = (lax.dot_general(wh, xh, NT_DIMS, preferred_element_type=F32)
              + lax.dot_general(wh, xl, NT_DIMS, preferred_element_type=F32)
              + lax.dot_general(wl, xh, NT_DIMS, preferred_element_type=F32))
    scores = _sigmoid(logits)
    sel = scores + bias_ref[...]

    grp = sel.reshape(N_GROUPS, per, tr)
    sub = lax.broadcasted_iota(I32, grp.shape, 1)
    m1 = jnp.max(grp, axis=1, keepdims=True)
    first = jnp.min(jnp.where(grp == m1, sub, per), axis=1, keepdims=True)
    m2 = jnp.max(jnp.where(sub == first, -jnp.inf, grp), axis=1, keepdims=True)
    gs = m1 + m2
    gidx = lax.broadcasted_iota(I32, gs.shape, 0)
    grank = jnp.zeros(gs.shape, I32)
    for j in range(N_GROUPS):
        other = gs[j:j + 1]
        grank += ((other > gs) | ((other == gs) & (j < gidx))).astype(I32)
    masked = jnp.where(grank < TOPK_GROUPS, grp, -jnp.inf).reshape(ne, tr)

    eidx = lax.broadcasted_iota(I32, (ne, tr), 0)
    work = masked
    picked = jnp.zeros((ne, tr), F32)
    for _ in range(TOP_K):
        top = jnp.max(work, axis=0, keepdims=True)
        first = jnp.min(jnp.where(work == top, eidx, ne), axis=0, keepdims=True)
        hit = eidx == first
        picked = jnp.where(hit, 1.0, picked)
        work = jnp.where(hit, -jnp.inf, work)
    chosen = picked > 0.0
    w = jnp.where(chosen, scores, 0.0)
    gates = w / jnp.sum(w, axis=0, keepdims=True) * ROUTED_SCALE

    chosen_b = jnp.where(chosen, 1.0, 0.0).astype(BF16)
    incl = jnp.dot(chosen_b, tri_ref[...], preferred_element_type=F32)
    carry = carry_ref[...]
    rank_in_expert = (carry + incl - 1.0).astype(I32)
    carry_new = carry + incl[:, tr - 1:tr]
    carry_ref[...] = carry_new
    cnt_ref[...] = jnp.broadcast_to(carry_new, cnt_ref.shape).astype(I32)

    slot = jnp.dot(low_ref[...], chosen_b, preferred_element_type=F32).astype(I32)
    for j in range(TOP_K):
        pick = chosen & (slot == j)
        rank_ref[pl.ds(j, 1), :] = jnp.sum(jnp.where(pick, rank_in_expert, 0), axis=0, keepdims=True)
        eid_ref[pl.ds(j, 1), :] = jnp.sum(jnp.where(pick, eidx, 0), axis=0, keepdims=True)
        wgt_ref[pl.ds(j, 1), :] = jnp.sum(jnp.where(pick, gates, 0.0), axis=0, keepdims=True)


def _router(x1, w_router_t, bias_col, row0, t):
    d = x1.shape[1]
    ne = N_EXPERTS
    tr = min(t, 512)
    blk0 = row0 // tr
    tri = jnp.asarray(np.triu(np.ones((tr, tr), np.float32)), BF16)
    low = jnp.asarray(np.tril(np.ones((ne, ne), np.float32), -1), BF16)
    slot_shape = jax.ShapeDtypeStruct((TOP_K, t), I32)
    slot_spec = pl.BlockSpec((TOP_K, tr), lambda i: (0, i))
    return pl.pallas_call(
        functools.partial(_router_kernel, tr=tr),
        out_shape=(slot_shape, slot_shape, jax.ShapeDtypeStruct((TOP_K, t), F32),
                   jax.ShapeDtypeStruct((ne, LANES), I32)),
        grid=(t // tr,),
        in_specs=[pl.BlockSpec((tr, d), lambda i: (blk0 + i, 0)),
                  pl.BlockSpec((ne, d), lambda i: (0, 0)),
                  pl.BlockSpec((ne, 1), lambda i: (0, 0)),
                  pl.BlockSpec((tr, tr), lambda i: (0, 0)),
                  pl.BlockSpec((ne, ne), lambda i: (0, 0))],
        out_specs=(slot_spec, slot_spec, slot_spec, pl.BlockSpec((ne, LANES), lambda i: (0, 0))),
        scratch_shapes=[pltpu.VMEM((ne, 1), F32)],
        compiler_params=_params("arbitrary"),
        name="router",
    )(x1, w_router_t, bias_col, tri, low)


def _sc_worker_id():
    return lax.axis_index("s") * SC_CORES + lax.axis_index("c")


def _dispatch(pos, x1p, row0):
    t = pos.shape[1]
    dp = x1p.shape[1]
    t_per_w = t // SC_WORKERS
    chunk = min(SC_SCATTER_ROWS, t_per_w // 2)
    n_chunks = t_per_w // chunk
    mesh = plsc.VectorSubcoreMesh(core_axis_name="c", subcore_axis_name="s")

    @functools.partial(
        pl.kernel, mesh=mesh,
        out_type=jax.ShapeDtypeStruct((t * TOP_K, dp), x1p.dtype),
        scratch_types=[pltpu.VMEM((n_chunks * TOP_K, chunk), I32),
                       pltpu.VMEM((2, chunk, dp), x1p.dtype),
                       pltpu.SemaphoreType.DMA((2,)),
                       pltpu.SemaphoreType.DMA((2,))],
    )
    def scatter_rows(x_hbm, pos_hbm, out_hbm, idx_v, rows_v, load_sem, scat_sem):
        wid = _sc_worker_id()
        base = row0 + wid * t_per_w
        pltpu.sync_copy(pos_hbm.at[wid], idx_v)

        def load(c, b):
            return pltpu.make_async_copy(x_hbm.at[pl.ds(base + c * chunk, chunk)], rows_v.at[b], load_sem.at[b])

        def scat(c, b, j):
            return pltpu.make_async_copy(rows_v.at[b], out_hbm.at[idx_v.at[c * TOP_K + j]], scat_sem.at[b])

        load(0, 0).start()

        @pl.loop(0, n_chunks, step=2)
        def _(c0):
            for b in range(2):
                c = c0 + b
                load(c, b).wait()
                for j in range(TOP_K):
                    scat(c, b, j).start()

                @pl.when(c + 1 < n_chunks)
                def _():
                    @pl.when(c >= 1)
                    def _():
                        for j in range(TOP_K):
                            scat(c - 1, 1 - b, j).wait()
                    load(c + 1, 1 - b).start()

        for b in range(2):
            for j in range(TOP_K):
                scat(n_chunks - 2 + b, b, j).wait()

    idx = pos.reshape(TOP_K, SC_WORKERS, n_chunks, chunk).transpose(1, 2, 0, 3)
    return scatter_rows(x1p, idx.reshape(SC_WORKERS, n_chunks * TOP_K, chunk))


def _gather_rows(table, idx):
    n_rows = idx.shape[0]
    dp = table.shape[1]
    r_per_w = n_rows // SC_WORKERS
    chunk, nbuf = SC_GATHER_ROWS, SC_GATHER_BUFFERS
    n_chunks = r_per_w // chunk
    mesh = plsc.VectorSubcoreMesh(core_axis_name="c", subcore_axis_name="s")

    @functools.partial(
        pl.kernel, mesh=mesh,
        out_type=jax.ShapeDtypeStruct((n_rows, dp), table.dtype),
        scratch_types=[pltpu.VMEM((n_chunks, chunk), I32),
                       pltpu.VMEM((nbuf, chunk, dp), table.dtype),
                       pltpu.SemaphoreType.DMA((nbuf,)),
                       pltpu.SemaphoreType.DMA((nbuf,))],
    )
    def gather_rows(table_hbm, idx_hbm, out_hbm, idx_v, rows_v, gat_sem, store_sem):
        wid = _sc_worker_id()
        base = wid * r_per_w
        pltpu.sync_copy(idx_hbm.at[wid], idx_v)

        def gather(c, b):
            return pltpu.make_async_copy(table_hbm.at[idx_v.at[c]], rows_v.at[b], gat_sem.at[b])

        def store(c, b):
            return pltpu.make_async_copy(rows_v.at[b], out_hbm.at[pl.ds(base + c * chunk, chunk)], store_sem.at[b])

        for b in range(nbuf - 1):
            gather(b, b).start()

        @pl.loop(0, n_chunks, step=nbuf)
        def _(c0):
            for b in range(nbuf):
                c = c0 + b
                gather(c, b).wait()
                store(c, b).start()
                nb = (b + nbuf - 1) % nbuf

                @pl.when(c + nbuf - 1 < n_chunks)
                def _():
                    @pl.when(c >= 1)
                    def _():
                        store(c - 1, nb).wait()
                    gather(c + nbuf - 1, nb).start()

        for b in range(nbuf):
            store(n_chunks - nbuf + b, b).wait()

    return gather_rows(table, idx.reshape(SC_WORKERS, n_chunks, chunk))


def _experts_kernel(tile_ref, exp_ref, lo_ref, hi_ref, slot_ref, next_ref, nv_ref, xs_ref, wg_hbm, wu_hbm, wd_hbm,
                    ys_ref, wgf, wuf, wdf, wsem, wgb, wub, wdb, hid_ref, ybuf_ref, *, layer):
    it = pl.program_id(0)
    nv = nv_ref[0]
    cur = jnp.minimum(it, nv - 1)
    prev = jnp.maximum(it - 1, 0)

    def weight_copies(e):
        return [pltpu.make_async_copy(src.at[layer, e], dst, wsem.at[k])
                for k, (src, dst) in enumerate(((wg_hbm, wgf), (wu_hbm, wuf), (wd_hbm, wdf)))]

    @pl.when(it == 0)
    def _():
        hid_ref[...] = jnp.zeros_like(hid_ref)
        wdb[...] = jnp.zeros_like(wdb)
        ybuf_ref[...] = jnp.zeros_like(ybuf_ref)
        for cp in weight_copies(exp_ref[0]):
            cp.start()

    @pl.when((it < nv) & ((it == 0) | (exp_ref[cur] != exp_ref[prev])))
    def _():
        for cp in weight_copies(exp_ref[cur]):
            cp.wait()
        slot = slot_ref[cur]
        rows_g = wgf.shape[0] // EXPERT_CAST_STEPS
        rows_d = wdf.shape[0] // EXPERT_CAST_STEPS

        def cast_rows(i, carry):
            rg = pl.ds(pl.multiple_of(i * rows_g, rows_g), rows_g)
            rd = pl.ds(pl.multiple_of(i * rows_d, rows_d), rows_d)
            wgb[rg, :] = wgf[rg, :].astype(BF16)
            wub[rg, :] = wuf[rg, :].astype(BF16)
            wdb[slot, rd, :] = wdf[rd, :].astype(BF16)
            return carry

        lax.fori_loop(0, EXPERT_CAST_STEPS, cast_rows, 0)

        @pl.when(next_ref[cur] >= 0)
        def _():
            for cp in weight_copies(next_ref[cur]):
                cp.start()

    tm, half = xs_ref.shape
    y = _pack_halves(jnp.dot(hid_ref[(it + 1) % 2], wdb[slot_ref[prev]], preferred_element_type=F32))
    a, b = _unpack_halves(xs_ref[...])
    x = jnp.concatenate([a.astype(BF16), b.astype(BF16)], axis=1)
    gate = jnp.dot(x, wgb[...], preferred_element_type=F32)
    up = jnp.dot(x, wub[...], preferred_element_type=F32)
    hid_ref[it % 2] = (gate * _sigmoid(gate) * up).astype(BF16)

    done = (it >= 1) & (it <= nv)
    lo = jnp.where(done, lo_ref[prev], 0)
    hi = jnp.where(done, hi_ref[prev], 0)
    row = lax.broadcasted_iota(I32, (tm, half), 0)
    merged = jnp.where((row >= lo) & (row < hi), y, ybuf_ref[...])
    ybuf_ref[...] = merged
    ys_ref[...] = merged


def _experts(items, xs, w_gate, w_up, w_down, layer, tm):
    n_rows, dp = xs.shape
    _, ne, d, ff = w_gate.shape
    n_items = items[0].shape[0]

    def cur_map(i, tl, ex, lo, hi, sl, nx, nv):
        return (tl[jnp.minimum(i, nv[0] - 1)], 0)

    def prev_map(i, tl, ex, lo, hi, sl, nx, nv):
        return (tl[jnp.clip(i - 1, 0, nv[0] - 1)], 0)

    hbm = pl.BlockSpec(memory_space=pl.ANY)
    return pl.pallas_call(
        functools.partial(_experts_kernel, layer=layer),
        out_shape=jax.ShapeDtypeStruct((n_rows, dp), I32),
        grid_spec=pltpu.PrefetchScalarGridSpec(
            num_scalar_prefetch=7,
            grid=(n_items + 1,),
            in_specs=[pl.BlockSpec((tm, dp), cur_map), hbm, hbm, hbm],
            out_specs=pl.BlockSpec((tm, dp), prev_map),
            scratch_shapes=[pltpu.VMEM((d, ff), F32), pltpu.VMEM((d, ff), F32), pltpu.VMEM((ff, d), F32),
                            pltpu.SemaphoreType.DMA((3,)),
                            pltpu.VMEM((d, ff), BF16), pltpu.VMEM((d, ff), BF16), pltpu.VMEM((2, ff, d), BF16),
                            pltpu.VMEM((2, tm, ff), BF16), pltpu.VMEM((tm, dp), I32)]),
        compiler_params=_params("arbitrary"),
        name="experts",
    )(*items, xs, w_gate, w_up, w_down)


def _combine_kernel(*refs, alpha):
    x_ref, wk_ref, yk_ref, wsg_ref, wsu_ref, wsd_ref, g_ref, b_ref = refs[:8]
    x2_ref, x2b_ref = refs[-2:]
    x = x_ref[...]
    xb = x.astype(BF16)
    gate = jnp.dot(xb, wsg_ref[...], preferred_element_type=F32)
    up = jnp.dot(xb, wsu_ref[...], preferred_element_type=F32)
    hid = (gate * _sigmoid(gate) * up).astype(BF16)
    shared = jnp.dot(hid, wsd_ref[...], preferred_element_type=F32)

    tc, half = yk_ref.shape[1:]
    wk = wk_ref[...]
    acc_hi = jnp.zeros((tc, half), F32)
    acc_lo = jnp.zeros((tc, half), F32)
    for j in range(TOP_K):
        hi, lo = _unpack_halves(yk_ref[j])
        wj = wk[:, j:j + 1]
        acc_hi += wj * hi
        acc_lo += wj * lo
    ffn = jnp.concatenate([acc_hi, acc_lo], axis=1) + shared
    y = _layer_norm_rows(alpha * x + ffn, g_ref[...], b_ref[...])
    x2_ref[...] = y
    x2b_ref[...] = y.astype(BF16)


def _combine(x1, wk, yk, ws_gate, ws_up, ws_down, ln_g, ln_b, alpha, out_rows, out_row0, prev):
    t, d = x1.shape
    t_all = out_rows
    dp = yk.shape[2]
    ff = ws_gate.shape[1]
    tc = min(t, 256)
    blk0 = out_row0 // tc
    const = lambda i: (0, 0)
    rows = lambda i: (i, 0)
    rows_all = lambda i: (blk0 + i, 0)
    in_specs = [pl.BlockSpec((tc, d), rows),
                pl.BlockSpec((tc, TOP_K), rows),
                pl.BlockSpec((TOP_K, tc, dp), lambda i: (0, i, 0)),
                pl.BlockSpec((d, ff), const),
                pl.BlockSpec((d, ff), const),
                pl.BlockSpec((ff, d), const),
                pl.BlockSpec((1, d), const),
                pl.BlockSpec((1, d), const)]
    args = [x1, wk, yk, ws_gate, ws_up, ws_down, ln_g, ln_b]
    aliases = {}
    if prev is not None:
        in_specs += [pl.BlockSpec(memory_space=pl.ANY)] * 2
        aliases = {len(args): 0, len(args) + 1: 1}
        args += list(prev)
    return pl.pallas_call(
        functools.partial(_combine_kernel, alpha=alpha),
        out_shape=(jax.ShapeDtypeStruct((t_all, d), F32), jax.ShapeDtypeStruct((t_all, d), BF16)),
        grid=(t // tc,),
        in_specs=in_specs,
        out_specs=(pl.BlockSpec((tc, d), rows_all), pl.BlockSpec((tc, d), rows_all)),
        input_output_aliases=aliases,
        compiler_params=_params("parallel"),
        name="combine",
    )(*args)


def _expert_work_items(counts, n_rows, tm):
    ne = counts.shape[0]
    end = jnp.cumsum(counts)
    start = end - counts
    first_tile = start // tm
    n_it = jnp.where(counts > 0, (end - 1) // tm - first_tile + 1, 0)
    it_end = jnp.cumsum(n_it)
    it_start = it_end - n_it
    n_items = n_rows // tm + ne
    k = jnp.arange(n_items, dtype=I32)
    expert = jnp.minimum(jnp.sum((it_end[None, :] <= k[:, None]).astype(I32), axis=1), ne - 1)
    onehot = expert[:, None] == jnp.arange(ne, dtype=I32)[None, :]
    pick = lambda v: jnp.sum(jnp.where(onehot, v[None, :], 0), axis=1)
    tile = jnp.clip(pick(first_tile) + k - pick(it_start), 0, n_rows // tm - 1)
    lo = jnp.maximum(pick(start) - tile * tm, 0)
    hi = jnp.minimum(pick(end) - tile * tm, tm)
    changed = jnp.concatenate([jnp.zeros((1,), I32), (expert[1:] != expert[:-1]).astype(I32)])
    slot = jnp.cumsum(changed) % 2
    later = (k[None, :] < it_end[-1]) & (expert[None, :] > expert[:, None])
    nxt = jnp.min(jnp.where(later, expert[None, :], ne), axis=1)
    nxt = jnp.where(nxt < ne, nxt, -1)
    return start.astype(I32), (tile.astype(I32), expert.astype(I32), lo.astype(I32), hi.astype(I32),
                               slot.astype(I32), nxt.astype(I32), it_end[-1:].astype(I32))


def _positions_kernel(start_ref, eid_ref, rank_ref, pos_ref):
    eid = eid_ref[...]
    pos = rank_ref[...]
    for e in range(N_EXPERTS):
        pos = pos + jnp.where(eid == e, start_ref[e], 0)
    pos_ref[...] = pos


def _positions(expert_start, eid, rank):
    full = pl.BlockSpec(eid.shape, lambda i, st: (0, 0))
    return pl.pallas_call(
        _positions_kernel,
        out_shape=jax.ShapeDtypeStruct(eid.shape, I32),
        grid_spec=pltpu.PrefetchScalarGridSpec(num_scalar_prefetch=1, grid=(1,), in_specs=[full, full],
                                               out_specs=full),
        compiler_params=_params("arbitrary"),
        name="positions",
    )(expert_start, eid, rank)


def _moe(x1, x1p, w_router_t, bias_col, w_gate, w_up, w_down, layer, ws_gate, ws_up, ws_down, ln_g, ln_b, alpha,
         out_rows, out_row0, prev):
    t = x1.shape[0]
    tm = 256
    rank, eid, wgt, cnt = _router(x1, w_router_t, bias_col, 0, t)
    expert_start, items = _expert_work_items(cnt[:, 0], t * TOP_K, tm)
    pos = _positions(expert_start, eid, rank)
    xs = _dispatch(pos, x1p, 0)
    ys = _experts(items, xs, w_gate, w_up, w_down, layer, tm)
    yk = _gather_rows(ys, pos.reshape(-1)).reshape(TOP_K, t, -1)
    return _combine(x1, wgt.T, yk, ws_gate, ws_up, ws_down, ln_g, ln_b, alpha, out_rows, out_row0, prev)


def kernel(x, positions, w_in, lb_logits, hg_norm_g, w_hg_proj, w_att_proj, w_out, ln1_g, ln1_b,
           w_router, router_bias, w_e_gate, w_e_up, w_e_down, w_s_gate, w_s_up, w_s_down, ln2_g, ln2_b):
    batch, seq, d = x.shape
    depth = w_in.shape[0]
    t = batch * seq
    alpha = float((2 * depth) ** 0.25)
    hg_width = HG_HEADS * HG_DIM
    att_col0 = 4 * hg_width
    gate_col0 = att_col0 + 3 * len(ATT_GROUPS) * ATT_HEADS * ATT_DIM

    p = jax.nn.softmax(lb_logits.astype(F32), axis=0)
    cs = jnp.cumsum(p, axis=0)
    lower = cs - cs[:1]
    log1m_lower = jnp.log1p(-lower)

    cos, sin = _rope_tables(positions)
    x_all = x.reshape(t, d)
    in_width = w_in.shape[2]
    tn = 1280 if in_width % 1280 == 0 else 512

    n_parts = BATCH_PIPELINES if batch % BATCH_PIPELINES == 0 else 1
    bp = batch // n_parts
    tp = bp * seq
    state = [((x_all, part * tp), x_all[part * tp:(part + 1) * tp].astype(BF16)) for part in range(n_parts)]
    final = None
    for layer in range(depth):
        last = layer == depth - 1
        for part in range(n_parts):
            (xf, x_row0), xb = state[part]
            h = _matmul(xb, w_in, layer, min(tp, 1024), tn, BF16)
            o_h = _hgrn(h, lower[layer].reshape(1, -1), log1m_lower[layer].reshape(1, -1),
                        hg_norm_g[layer].reshape(1, -1), bp, seq)
            o_a = _attention(h, cos, sin, bp, seq, att_col0, part * bp)
            x1, x1p = _mix_out(xf, x_row0, h, o_h, o_a, w_hg_proj[layer].astype(BF16),
                               w_att_proj[layer].astype(BF16), w_out[layer].astype(BF16),
                               ln1_g[layer].reshape(1, d), ln1_b[layer].reshape(1, d), gate_col0, alpha)
            out = _moe(x1, x1p, w_router[layer].T, router_bias[layer].reshape(-1, 1),
                       w_e_gate, w_e_up, w_e_down, layer,
                       w_s_gate[layer].astype(BF16), w_s_up[layer].astype(BF16), w_s_down[layer].astype(BF16),
                       ln2_g[layer].reshape(1, d), ln2_b[layer].reshape(1, d), alpha,
                       t if last else tp, part * tp if last else 0, final if last else None)
            if last:
                final = out
            else:
                state[part] = ((out[0], 0), out[1])
    return final[0].reshape(batch, seq, d)
```

```python
import functools

import numpy as np
import jax
import jax.numpy as jnp
from jax import lax
from jax.experimental import pallas as pl
from jax.experimental.pallas import tpu as pltpu
from jax.experimental.pallas import tpu_sc as plsc

F32 = jnp.float32
BF16 = jnp.bfloat16
I32 = jnp.int32

LANES = 128
VMEM_LIMIT_BYTES = 56 * 1024 * 1024
SC_CORES = 2
SC_SUBCORES = 16
SC_WORKERS = SC_CORES * SC_SUBCORES
SC_SCATTER_ROWS = 32
SC_GATHER_ROWS = 16
SC_GATHER_BUFFERS = 4
BATCH_PIPELINES = 1
EXPERT_CAST_STEPS = 16

HG_HEADS = 8
HG_DIM = 128
HG_CHUNK = 64
HG_SUB = 16
ATT_GROUPS = ((128, 1), (512, 4), (2048, 16))
ATT_HEADS = 4
ATT_DIM = 128
ATT_BACK = 128
ROPE_THETA = 10000.0
N_EXPERTS = 64
N_GROUPS = 8
TOPK_GROUPS = 4
TOP_K = 8
ROUTED_SCALE = 2.5
LN_EPS = 1e-5
NORM_EPS = 1e-6

NT_DIMS = (((1,), (1,)), ((), ()))
TN_DIMS = (((0,), (0,)), ((), ()))


def _params(*sem):
    return pltpu.CompilerParams(dimension_semantics=sem, vmem_limit_bytes=VMEM_LIMIT_BYTES)


def _sigmoid(x):
    return 1.0 / (1.0 + jnp.exp(-x))


def _matmul_kernel(x_ref, w_ref, o_ref, wb_ref):
    @pl.when(pl.program_id(1) == 0)
    def _():
        wb_ref[...] = w_ref[0].astype(BF16)

    o_ref[...] = jnp.dot(x_ref[...], wb_ref[...], preferred_element_type=F32).astype(o_ref.dtype)


def _matmul(x, w, layer, tm, tn, out_dtype):
    m, k = x.shape
    n = w.shape[2]
    return pl.pallas_call(
        _matmul_kernel,
        out_shape=jax.ShapeDtypeStruct((m, n), out_dtype),
        grid=(n // tn, m // tm),
        in_specs=[pl.BlockSpec((tm, k), lambda j, i: (i, 0)),
                  pl.BlockSpec((1, k, tn), lambda j, i: (layer, 0, j))],
        out_specs=pl.BlockSpec((tm, tn), lambda j, i: (i, j)),
        scratch_shapes=[pltpu.VMEM((k, tn), BF16)],
        compiler_params=_params("parallel", "arbitrary"),
        name="in_proj",
    )(x, w)


def _rope_table_kernel(pos_ref, inv_ref, sign_ref, cos_ref, sin_ref):
    ang = pos_ref[...].astype(F32) * inv_ref[...]
    cos_ref[...] = jnp.cos(ang)
    sin_ref[...] = jnp.sin(ang) * sign_ref[...]


def _rope_tables(positions):
    t = positions.size
    half = ATT_DIM // 2
    inv_half = ROPE_THETA ** (-np.arange(half, dtype=np.float32) * np.float32(2.0) / np.float32(ATT_DIM))
    inv = jnp.asarray(np.concatenate([inv_half, inv_half]).astype(np.float32).reshape(1, ATT_DIM))
    sign = jnp.asarray(np.concatenate([-np.ones(half), np.ones(half)]).astype(np.float32).reshape(1, ATT_DIM))
    tm = min(t, 2048)
    return pl.pallas_call(
        _rope_table_kernel,
        out_shape=(jax.ShapeDtypeStruct((t, ATT_DIM), F32), jax.ShapeDtypeStruct((t, ATT_DIM), F32)),
        grid=(t // tm,),
        in_specs=[pl.BlockSpec((tm, 1), lambda i: (i, 0)),
                  pl.BlockSpec((1, ATT_DIM), lambda i: (0, 0)),
                  pl.BlockSpec((1, ATT_DIM), lambda i: (0, 0))],
        out_specs=(pl.BlockSpec((tm, ATT_DIM), lambda i: (i, 0)),
                   pl.BlockSpec((tm, ATT_DIM), lambda i: (i, 0))),
        compiler_params=_params("parallel"),
        name="rope_tables",
    )(positions.reshape(t, 1), inv, sign)


def _cumsum_rows(x):
    n = x.shape[0]
    row = lax.broadcasted_iota(I32, x.shape, 0)
    s = 1
    while s < n:
        x = x + jnp.where(row >= s, pltpu.roll(x, s, 0), 0.0)
        s *= 2
    return x


def _rows_from(cum, offsets):
    parts = []
    for o in offsets:
        if o is None:
            parts.append(jnp.zeros((HG_SUB, cum.shape[1]), F32))
        else:
            parts.append(jnp.broadcast_to(cum[o:o + 1, :], (HG_SUB, cum.shape[1])))
    return jnp.concatenate(parts, axis=0)


def _hgrn_kernel(hq_ref, hf_ref, hi_ref, hg_ref, lb_ref, l1m_ref, ng_ref, o_ref, st_ref, *, n_chunks):
    c = HG_CHUNK
    nsub = c // HG_SUB

    @pl.when(pl.program_id(2) == 0)
    def _():
        st_ref[...] = jnp.zeros_like(st_ref)

    lb = lb_ref[...]
    one_m_lb = 1.0 - lb
    log1m_lb = l1m_ref[...]
    norm_g = ng_ref[...]
    row = lax.broadcasted_iota(I32, (c, HG_DIM), 0)
    blk = row // HG_SUB
    ti = lax.broadcasted_iota(I32, (c, c), 0)
    si = lax.broadcasted_iota(I32, (c, c), 1)
    diag_mask = ((ti // HG_SUB) == (si // HG_SUB)) & (si <= ti)

    def chunk(ci, carry):
        r0 = pl.multiple_of(ci * c, c)
        z = hf_ref[pl.ds(r0, c), :].astype(F32)
        qraw = hq_ref[pl.ds(r0, c), :].astype(F32)
        v = hi_ref[pl.ds(r0, c), :]
        g = hg_ref[pl.ds(r0, c), :].astype(F32)

        e = jnp.exp(-jnp.abs(z))
        r = 1.0 / (1.0 + e)
        sig = jnp.where(z >= 0, r, e * r)
        sig_neg = jnp.where(z >= 0, e * r, r)
        log_sig = jnp.minimum(z, 0.0) - jnp.log(1.0 + e)
        log_f = jnp.maximum(jnp.log(lb + one_m_lb * sig), log1m_lb + log_sig)
        k = one_m_lb * sig_neg
        q = qraw * _sigmoid(qraw)

        cum = _cumsum_rows(log_f)
        start = _rows_from(cum, [None] + [HG_SUB * i - 1 for i in range(1, nsub)])
        mid = _rows_from(cum, [HG_SUB * i + HG_SUB // 2 - 1 for i in range(nsub)])
        end = cum[c - 1:c, :]

        qd = (q * jnp.exp(cum - mid)).astype(BF16)
        kd = (k * jnp.exp(mid - cum)).astype(BF16)
        att = jnp.where(diag_mask, lax.dot_general(qd, kd, NT_DIMS, preferred_element_type=F32), 0.0)

        qs = q * jnp.exp(cum - start)
        q_slots, k_slots = [], []
        for i in range(1, nsub):
            q_slots.append(jnp.where(blk == i, qs, 0.0).astype(BF16))
            n_rows = HG_SUB * i
            b_i = cum[n_rows - 1:n_rows, :]
            k_i = k[:n_rows] * jnp.exp(b_i - cum[:n_rows])
            k_slots.append(jnp.concatenate([k_i, jnp.zeros((c - n_rows, HG_DIM), F32)], axis=0).astype(BF16))
        att = att + lax.dot_general(jnp.concatenate(q_slots, axis=1), jnp.concatenate(k_slots, axis=1),
                                    NT_DIMS, preferred_element_type=F32)
        intra = jnp.dot(att.astype(BF16), v, preferred_element_type=F32)

        st = st_ref[...]
        inter = lax.dot_general((q * jnp.exp(cum)).astype(BF16), st.astype(BF16), NT_DIMS,
                                preferred_element_type=F32)
        k_end = (k * jnp.exp(end - cum)).astype(BF16)
        st_ref[...] = st * jnp.exp(end) + lax.dot_general(v, k_end, TN_DIMS, preferred_element_type=F32)

        o = inter + intra
        o = o * lax.rsqrt(jnp.mean(o * o, axis=-1, keepdims=True) + NORM_EPS) * norm_g
        o_ref[pl.ds(r0, c), :] = (o * _sigmoid(g)).astype(o_ref.dtype)
        return carry

    lax.fori_loop(0, n_chunks, chunk, 0, unroll=8)


def _hgrn(h, lb, log1m_lb, norm_g, batch, seq):
    t = h.shape[0]
    tb = min(seq, 2048)
    nsb = seq // tb
    width = HG_HEADS * HG_DIM

    def col(off):
        return pl.BlockSpec((tb, HG_DIM), lambda b, hh, s: (b * nsb + s, off * HG_HEADS + hh))

    vec = pl.BlockSpec((1, HG_DIM), lambda b, hh, s: (0, hh))
    return pl.pallas_call(
        functools.partial(_hgrn_kernel, n_chunks=tb // HG_CHUNK),
        out_shape=jax.ShapeDtypeStruct((t, width), BF16),
        grid=(batch, HG_HEADS, nsb),
        in_specs=[col(0), col(1), col(2), col(3), vec, vec, vec],
        out_specs=pl.BlockSpec((tb, HG_DIM), lambda b, hh, s: (b * nsb + s, hh)),
        scratch_shapes=[pltpu.VMEM((HG_DIM, HG_DIM), F32)],
        compiler_params=_params("parallel", "parallel", "arbitrary"),
        name="hgrn2",
    )(h, h, h, h, lb, log1m_lb, norm_g)


def _attn_kernel(*refs, seq):
    qkv_refs = refs[:9]
    cos_ref, sin_ref, o_ref = refs[9:12]
    qf, kf, vf, og, lg = refs[12:]
    n_groups = len(ATT_GROUPS)
    scale = ATT_DIM ** -0.5
    rb = min(seq, 256)

    def rope_rows(i, carry):
        r0 = pl.multiple_of(i * rb, rb)
        cs = cos_ref[pl.ds(r0, rb), :]
        sn = sin_ref[pl.ds(r0, rb), :]
        for gi in range(n_groups):
            xq = qkv_refs[3 * gi][pl.ds(r0, rb), :].astype(F32)
            xk = qkv_refs[3 * gi + 1][pl.ds(r0, rb), :].astype(F32)
            qf[gi, pl.ds(r0, rb), :] = (xq * cs + pltpu.roll(xq, ATT_DIM // 2, 1) * sn) * scale
            kf[gi, pl.ds(r0, rb), :] = xk * cs + pltpu.roll(xk, ATT_DIM // 2, 1) * sn
            vf[gi, pl.ds(r0, rb), :] = qkv_refs[3 * gi + 2][pl.ds(r0, rb), :].astype(F32)
        return carry

    lax.fori_loop(0, seq // rb, rope_rows, 0)

    qb = ATT_BACK
    for gi, (_, dil) in enumerate(ATT_GROUPS):
        length = seq // dil
        nk = min(2 * qb, length)
        n_qblk = length // qb

        def block(it, carry, gi=gi, dil=dil, nk=nk, n_qblk=n_qblk):
            res = it // n_qblk
            iq = it % n_qblk
            q0 = iq * qb
            k0 = jnp.maximum(q0 - qb, 0)
            if dil == 1:
                q_rows = pl.ds(pl.multiple_of(q0, qb), qb)
                k_rows = pl.ds(pl.multiple_of(k0, qb), nk)
            else:
                q_rows = pl.ds(res + dil * q0, qb, stride=dil)
                k_rows = pl.ds(res + dil * k0, nk, stride=dil)
            q = qf[gi, q_rows, :].astype(BF16)
            k = kf[gi, k_rows, :].astype(BF16)
            v = vf[gi, k_rows, :].astype(BF16)
            wk = 2 * qb
            k = jnp.concatenate([k] * (wk // nk), axis=0)
            v = jnp.concatenate([v] * (wk // nk), axis=0)
            col = lax.broadcasted_iota(I32, (qb, wk), 1)
            dist = (q0 + lax.broadcasted_iota(I32, (qb, wk), 0)) - (k0 + col)
            if wk > nk:
                dist = jnp.where(col < nk, dist, -1)
            s = lax.dot_general(q, k, NT_DIMS, preferred_element_type=F32)
            s = jnp.where((dist >= 0) & (dist <= ATT_BACK), s, -jnp.inf)
            m = jnp.max(s, axis=-1, keepdims=True)
            p = jnp.exp(s - m)
            den = jnp.sum(p, axis=-1, keepdims=True)
            o = jnp.dot(p.astype(BF16), v, preferred_element_type=F32) / den
            og[gi, q_rows, :] = o
            lg[gi, q_rows, :] = jnp.broadcast_to(m + jnp.log(den), (qb, ATT_DIM))
            return carry

        lax.fori_loop(0, dil * n_qblk, block, 0, unroll=8)

    def merge_rows(i, carry):
        r0 = pl.multiple_of(i * rb, rb)
        ls = [lg[gi, pl.ds(r0, rb), :] for gi in range(n_groups)]
        m = functools.reduce(jnp.maximum, ls)
        ws = [jnp.exp(l - m) for l in ls]
        num = sum(w * og[gi, pl.ds(r0, rb), :] for gi, w in enumerate(ws))
        o_ref[pl.ds(r0, rb), :] = (num / sum(ws)).astype(o_ref.dtype)
        return carry

    lax.fori_loop(0, seq // rb, merge_rows, 0)


def _attention(h, cos, sin, batch, seq, col0, batch0):
    t = h.shape[0]
    n_groups = len(ATT_GROUPS)
    part = n_groups * ATT_HEADS * ATT_DIM
    blk0 = col0 // ATT_DIM
    in_specs = []
    for gi in range(n_groups):
        for p in range(3):
            off = blk0 + (p * part) // ATT_DIM + gi * ATT_HEADS
            in_specs.append(pl.BlockSpec((seq, ATT_DIM), lambda b, hh, off=off: (b, off + hh)))
    tab = pl.BlockSpec((seq, ATT_DIM), lambda b, hh: (batch0 + b, 0))
    in_specs += [tab, tab]
    scr = pltpu.VMEM((n_groups, seq, ATT_DIM), F32)
    return pl.pallas_call(
        functools.partial(_attn_kernel, seq=seq),
        out_shape=jax.ShapeDtypeStruct((t, ATT_HEADS * ATT_DIM), BF16),
        grid=(batch, ATT_HEADS),
        in_specs=in_specs,
        out_specs=pl.BlockSpec((seq, ATT_DIM), lambda b, hh: (b, hh)),
        scratch_shapes=[scr] * 5,
        compiler_params=_params("parallel", "parallel"),
        name="dilated_attn",
    )(*([h] * 9), cos, sin)


def _layer_norm_rows(r, g, b):
    mu = jnp.mean(r, axis=-1, keepdims=True)
    d = r - mu
    var = jnp.mean(d * d, axis=-1, keepdims=True)
    return d * lax.rsqrt(var + LN_EPS) * g + b


def _pack_halves(y):
    n = y.shape[1] // 2
    return pltpu.pack_elementwise([y[:, :n], y[:, n:]], packed_dtype=BF16)


def _unpack_halves(w):
    first = pltpu.unpack_elementwise(w, index=0, packed_dtype=BF16, unpacked_dtype=F32)
    second = pltpu.unpack_elementwise(w, index=1, packed_dtype=BF16, unpacked_dtype=F32)
    return first, second


def _mix_out_kernel(*refs, alpha, n_col):
    x_ref, oh_ref, oa_ref = refs[:3]
    gh_refs = refs[3:3 + n_col]
    ga_refs = refs[3 + n_col:3 + 2 * n_col]
    whg_ref, wap_ref, wout_ref, g_ref, b_ref, x1_ref, x1p_ref = refs[3 + 2 * n_col:]
    tn = gh_refs[0].shape[1]
    oh = oh_ref[...]
    oa = oa_ref[...]
    merged = []
    for n in range(n_col):
        cols = slice(n * tn, (n + 1) * tn)
        y_h = jnp.dot(oh, whg_ref[:, cols], preferred_element_type=F32)
        y_a = jnp.dot(oa, wap_ref[:, cols], preferred_element_type=F32)
        m = _sigmoid(gh_refs[n][...].astype(F32)) * y_h + _sigmoid(ga_refs[n][...].astype(F32)) * y_a
        merged.append(m.astype(BF16))
    mix = jnp.dot(jnp.concatenate(merged, axis=1), wout_ref[...], preferred_element_type=F32)
    y = _layer_norm_rows(alpha * x_ref[...] + mix, g_ref[...], b_ref[...])
    x1_ref[...] = y
    x1p_ref[...] = _pack_halves(y)


def _mix_out(x, x_row0, h, o_h, o_a, w_hg, w_ap, w_out, ln_g, ln_b, gate_col0, alpha):
    t = h.shape[0]
    d = x.shape[1]
    tm = min(t, 256)
    x_blk0 = x_row0 // tm
    tn = 512
    n_col = d // tn
    gh0 = gate_col0 // tn
    ga0 = (gate_col0 + d) // tn
    rows = lambda i: (i, 0)
    const = lambda i: (0, 0)
    resident = dict(index_map=const, pipeline_mode=pl.Buffered(1))
    gate_specs = [pl.BlockSpec((tm, tn), lambda i, c=c0 + n: (i, c)) for c0 in (gh0, ga0) for n in range(n_col)]
    return pl.pallas_call(
        functools.partial(_mix_out_kernel, alpha=alpha, n_col=n_col),
        out_shape=(jax.ShapeDtypeStruct((t, d), F32), jax.ShapeDtypeStruct((t, d // 2), I32)),
        grid=(t // tm,),
        in_specs=[pl.BlockSpec((tm, d), lambda i: (x_blk0 + i, 0)),
                  pl.BlockSpec((tm, o_h.shape[1]), rows),
                  pl.BlockSpec((tm, o_a.shape[1]), rows),
                  *gate_specs,
                  pl.BlockSpec(w_hg.shape, **resident),
                  pl.BlockSpec(w_ap.shape, **resident),
                  pl.BlockSpec(w_out.shape, **resident),
                  pl.BlockSpec((1, d), const),
                  pl.BlockSpec((1, d), const)],
        out_specs=(pl.BlockSpec((tm, d), rows), pl.BlockSpec((tm, d // 2), rows)),
        compiler_params=_params("parallel"),
        name="mix_out",
    )(x, o_h, o_a, *([h] * (2 * n_col)), w_hg, w_ap, w_out, ln_g, ln_b)


def _split_bf16(x):
    hi = x.astype(BF16)
    lo = (x - hi.astype(F32)).astype(BF16)
    return hi, lo


def _router_kernel(x_ref, wr_ref, bias_ref, tri_ref, low_ref, rank_ref, eid_ref, wgt_ref, cnt_ref,
                   carry_ref, *, tr):
    ne = N_EXPERTS
    per = ne // N_GROUPS

    @pl.when(pl.program_id(0) == 0)
    def _():
        carry_ref[...] = jnp.zeros_like(carry_ref)

    xh, xl = _split_bf16(x_ref[...])
    wh, wl = _split_bf16(wr_ref[...])
    logits = (lax.dot_general(wh, xh, NT_DIMS, preferred_element_type=F32)
              + lax.dot_general(wh, xl, NT_DIMS, preferred_element_type=F32)
              + lax.dot_general(wl, xh, NT_DIMS, preferred_element_type=F32))
    scores = _sigmoid(logits)
    sel = scores + bias_ref[...]

    grp = sel.reshape(N_GROUPS, per, tr)
    sub = lax.broadcasted_iota(I32, grp.shape, 1)
    m1 = jnp.max(grp, axis=1, keepdims=True)
    first = jnp.min(jnp.where(grp == m1, sub, per), axis=1, keepdims=True)
    m2 = jnp.max(jnp.where(sub == first, -jnp.inf, grp), axis=1, keepdims=True)
    gs = m1 + m2
    gidx = lax.broadcasted_iota(I32, gs.shape, 0)
    grank = jnp.zeros(gs.shape, I32)
    for j in range(N_GROUPS):
        other = gs[j:j + 1]
        grank += ((other > gs) | ((other == gs) & (j < gidx))).astype(I32)
    masked = jnp.where(grank < TOPK_GROUPS, grp, -jnp.inf).reshape(ne, tr)

    eidx = lax.broadcasted_iota(I32, (ne, tr), 0)
    work = masked
    picked = jnp.zeros((ne, tr), F32)
    for _ in range(TOP_K):
        top = jnp.max(work, axis=0, keepdims=True)
        first = jnp.min(jnp.where(work == top, eidx, ne), axis=0, keepdims=True)
        hit = eidx == first
        picked = jnp.where(hit, 1.0, picked)
        work = jnp.where(hit, -jnp.inf, work)
    chosen = picked > 0.0
    w = jnp.where(chosen, scores, 0.0)
    gates = w / jnp.sum(w, axis=0, keepdims=True) * ROUTED_SCALE

    chosen_b = jnp.where(chosen, 1.0, 0.0).astype(BF16)
    incl = jnp.dot(chosen_b, tri_ref[...], preferred_element_type=F32)
    carry = carry_ref[...]
    rank_in_expert = (carry + incl - 1.0).astype(I32)
    carry_new = carry + incl[:, tr - 1:tr]
    carry_ref[...] = carry_new
    cnt_ref[...] = jnp.broadcast_to(carry_new, cnt_ref.shape).astype(I32)

    slot = jnp.dot(low_ref[...], chosen_b, preferred_element_type=F32).astype(I32)
    for j in range(TOP_K):
        pick = chosen & (slot == j)
        rank_ref[pl.ds(j, 1), :] = jnp.sum(jnp.where(pick, rank_in_expert, 0), axis=0, keepdims=True)
        eid_ref[pl.ds(j, 1), :] = jnp.sum(jnp.where(pick, eidx, 0), axis=0, keepdims=True)
        wgt_ref[pl.ds(j, 1), :] = jnp.sum(jnp.where(pick, gates, 0.0), axis=0, keepdims=True)


def _router(x1, w_router_t, bias_col, row0, t):
    d = x1.shape[1]
    ne = N_EXPERTS
    tr = min(t, 512)
    blk0 = row0 // tr
    tri = jnp.asarray(np.triu(np.ones((tr, tr), np.float32)), BF16)
    low = jnp.asarray(np.tril(np.ones((ne, ne), np.float32), -1), BF16)
    slot_shape = jax.ShapeDtypeStruct((TOP_K, t), I32)
    slot_spec = pl.BlockSpec((TOP_K, tr), lambda i: (0, i))
    return pl.pallas_call(
        functools.partial(_router_kernel, tr=tr),
        out_shape=(slot_shape, slot_shape, jax.ShapeDtypeStruct((TOP_K, t), F32),
                   jax.ShapeDtypeStruct((ne, LANES), I32)),
        grid=(t // tr,),
        in_specs=[pl.BlockSpec((tr, d), lambda i: (blk0 + i, 0)),
                  pl.BlockSpec((ne, d), lambda i: (0, 0)),
                  pl.BlockSpec((ne, 1), lambda i: (0, 0)),
                  pl.BlockSpec((tr, tr), lambda i: (0, 0)),
                  pl.BlockSpec((ne, ne), lambda i: (0, 0))],
        out_specs=(slot_spec, slot_spec, slot_spec, pl.BlockSpec((ne, LANES), lambda i: (0, 0))),
        scratch_shapes=[pltpu.VMEM((ne, 1), F32)],
        compiler_params=_params("arbitrary"),
        name="router",
    )(x1, w_router_t, bias_col, tri, low)


def _sc_worker_id():
    return lax.axis_index("s") * SC_CORES + lax.axis_index("c")


def _dispatch(pos, x1p, row0):
    t = pos.shape[1]
    dp = x1p.shape[1]
    t_per_w = t // SC_WORKERS
    chunk = min(SC_SCATTER_ROWS, t_per_w // 2)
    n_chunks = t_per_w // chunk
    mesh = plsc.VectorSubcoreMesh(core_axis_name="c", subcore_axis_name="s")

    @functools.partial(
        pl.kernel, mesh=mesh,
        out_type=jax.ShapeDtypeStruct((t * TOP_K, dp), x1p.dtype),
        scratch_types=[pltpu.VMEM((n_chunks * TOP_K, chunk), I32),
                       pltpu.VMEM((2, chunk, dp), x1p.dtype),
                       pltpu.SemaphoreType.DMA((2,)),
                       pltpu.SemaphoreType.DMA((2,))],
    )
    def scatter_rows(x_hbm, pos_hbm, out_hbm, idx_v, rows_v, load_sem, scat_sem):
        wid = _sc_worker_id()
        base = row0 + wid * t_per_w
        pltpu.sync_copy(pos_hbm.at[wid], idx_v)

        def load(c, b):
            return pltpu.make_async_copy(x_hbm.at[pl.ds(base + c * chunk, chunk)], rows_v.at[b], load_sem.at[b])

        def scat(c, b, j):
            return pltpu.make_async_copy(rows_v.at[b], out_hbm.at[idx_v.at[c * TOP_K + j]], scat_sem.at[b])

        load(0, 0).start()

        @pl.loop(0, n_chunks, step=2)
        def _(c0):
            for b in range(2):
                c = c0 + b
                load(c, b).wait()
                for j in range(TOP_K):
                    scat(c, b, j).start()

                @pl.when(c + 1 < n_chunks)
                def _():
                    @pl.when(c >= 1)
                    def _():
                        for j in range(TOP_K):
                            scat(c - 1, 1 - b, j).wait()
                    load(c + 1, 1 - b).start()

        for b in range(2):
            for j in range(TOP_K):
                scat(n_chunks - 2 + b, b, j).wait()

    idx = pos.reshape(TOP_K, SC_WORKERS, n_chunks, chunk).transpose(1, 2, 0, 3)
    return scatter_rows(x1p, idx.reshape(SC_WORKERS, n_chunks * TOP_K, chunk))


def _gather_rows(table, idx):
    n_rows = idx.shape[0]
    dp = table.shape[1]
    r_per_w = n_rows // SC_WORKERS
    chunk, nbuf = SC_GATHER_ROWS, SC_GATHER_BUFFERS
    n_chunks = r_per_w // chunk
    mesh = plsc.VectorSubcoreMesh(core_axis_name="c", subcore_axis_name="s")

    @functools.partial(
        pl.kernel, mesh=mesh,
        out_type=jax.ShapeDtypeStruct((n_rows, dp), table.dtype),
        scratch_types=[pltpu.VMEM((n_chunks, chunk), I32),
                       pltpu.VMEM((nbuf, chunk, dp), table.dtype),
                       pltpu.SemaphoreType.DMA((nbuf,)),
                       pltpu.SemaphoreType.DMA((nbuf,))],
    )
    def gather_rows(table_hbm, idx_hbm, out_hbm, idx_v, rows_v, gat_sem, store_sem):
        wid = _sc_worker_id()
        base = wid * r_per_w
        pltpu.sync_copy(idx_hbm.at[wid], idx_v)

        def gather(c, b):
            return pltpu.make_async_copy(table_hbm.at[idx_v.at[c]], rows_v.at[b], gat_sem.at[b])

        def store(c, b):
            return pltpu.make_async_copy(rows_v.at[b], out_hbm.at[pl.ds(base + c * chunk, chunk)], store_sem.at[b])

        for b in range(nbuf - 1):
            gather(b, b).start()

        @pl.loop(0, n_chunks, step=nbuf)
        def _(c0):
            for b in range(nbuf):
                c = c0 + b
                gather(c, b).wait()
                store(c, b).start()
                nb = (b + nbuf - 1) % nbuf

                @pl.when(c + nbuf - 1 < n_chunks)
                def _():
                    @pl.when(c >= 1)
                    def _():
                        store(c - 1, nb).wait()
                    gather(c + nbuf - 1, nb).start()

        for b in range(nbuf):
            store(n_chunks - nbuf + b, b).wait()

    return gather_rows(table, idx.reshape(SC_WORKERS, n_chunks, chunk))


def _experts_kernel(tile_ref, exp_ref, lo_ref, hi_ref, slot_ref, next_ref, nv_ref, xs_ref, wg_hbm, wu_hbm, wd_hbm,
                    ys_ref, wgf, wuf, wdf, wsem, wgb, wub, wdb, hid_ref, ybuf_ref, *, layer):
    it = pl.program_id(0)
    nv = nv_ref[0]
    cur = jnp.minimum(it, nv - 1)
    prev = jnp.maximum(it - 1, 0)

    def weight_copies(e):
        return [pltpu.make_async_copy(src.at[layer, e], dst, wsem.at[k])
                for k, (src, dst) in enumerate(((wg_hbm, wgf), (wu_hbm, wuf), (wd_hbm, wdf)))]

    @pl.when(it == 0)
    def _():
        hid_ref[...] = jnp.zeros_like(hid_ref)
        wdb[...] = jnp.zeros_like(wdb)
        ybuf_ref[...] = jnp.zeros_like(ybuf_ref)
        for cp in weight_copies(exp_ref[0]):
            cp.start()

    @pl.when((it < nv) & ((it == 0) | (exp_ref[cur] != exp_ref[prev])))
    def _():
        for cp in weight_copies(exp_ref[cur]):
            cp.wait()
        slot = slot_ref[cur]
        rows_g = wgf.shape[0] // EXPERT_CAST_STEPS
        rows_d = wdf.shape[0] // EXPERT_CAST_STEPS

        def cast_rows(i, carry):
            rg = pl.ds(pl.multiple_of(i * rows_g, rows_g), rows_g)
            rd = pl.ds(pl.multiple_of(i * rows_d, rows_d), rows_d)
            wgb[rg, :] = wgf[rg, :].astype(BF16)
            wub[rg, :] = wuf[rg, :].astype(BF16)
            wdb[slot, rd, :] = wdf[rd, :].astype(BF16)
            return carry

        lax.fori_loop(0, EXPERT_CAST_STEPS, cast_rows, 0)

        @pl.when(next_ref[cur] >= 0)
        def _():
            for cp in weight_copies(next_ref[cur]):
                cp.start()

    tm, half = xs_ref.shape
    y = _pack_halves(jnp.dot(hid_ref[(it + 1) % 2], wdb[slot_ref[prev]], preferred_element_type=F32))
    a, b = _unpack_halves(xs_ref[...])
    x = jnp.concatenate([a.astype(BF16), b.astype(BF16)], axis=1)
    gate = jnp.dot(x, wgb[...], preferred_element_type=F32)
    up = jnp.dot(x, wub[...], preferred_element_type=F32)
    hid_ref[it % 2] = (gate * _sigmoid(gate) * up).astype(BF16)

    done = (it >= 1) & (it <= nv)
    lo = jnp.where(done, lo_ref[prev], 0)
    hi = jnp.where(done, hi_ref[prev], 0)
    row = lax.broadcasted_iota(I32, (tm, half), 0)
    merged = jnp.where((row >= lo) & (row < hi), y, ybuf_ref[...])
    ybuf_ref[...] = merged
    ys_ref[...] = merged


def _experts(items, xs, w_gate, w_up, w_down, layer, tm):
    n_rows, dp = xs.shape
    _, ne, d, ff = w_gate.shape
    n_items = items[0].shape[0]

    def cur_map(i, tl, ex, lo, hi, sl, nx, nv):
        return (tl[jnp.minimum(i, nv[0] - 1)], 0)

    def prev_map(i, tl, ex, lo, hi, sl, nx, nv):
        return (tl[jnp.clip(i - 1, 0, nv[0] - 1)], 0)

    hbm = pl.BlockSpec(memory_space=pl.ANY)
    return pl.pallas_call(
        functools.partial(_experts_kernel, layer=layer),
        out_shape=jax.ShapeDtypeStruct((n_rows, dp), I32),
        grid_spec=pltpu.PrefetchScalarGridSpec(
            num_scalar_prefetch=7,
            grid=(n_items + 1,),
            in_specs=[pl.BlockSpec((tm, dp), cur_map), hbm, hbm, hbm],
            out_specs=pl.BlockSpec((tm, dp), prev_map),
            scratch_shapes=[pltpu.VMEM((d, ff), F32), pltpu.VMEM((d, ff), F32), pltpu.VMEM((ff, d), F32),
                            pltpu.SemaphoreType.DMA((3,)),
                            pltpu.VMEM((d, ff), BF16), pltpu.VMEM((d, ff), BF16), pltpu.VMEM((2, ff, d), BF16),
                            pltpu.VMEM((2, tm, ff), BF16), pltpu.VMEM((tm, dp), I32)]),
        compiler_params=_params("arbitrary"),
        name="experts",
    )(*items, xs, w_gate, w_up, w_down)


def _combine_kernel(*refs, alpha):
    x_ref, wk_ref, yk_ref, wsg_ref, wsu_ref, wsd_ref, g_ref, b_ref = refs[:8]
    x2_ref, x2b_ref = refs[-2:]
    x = x_ref[...]
    xb = x.astype(BF16)
    gate = jnp.dot(xb, wsg_ref[...], preferred_element_type=F32)
    up = jnp.dot(xb, wsu_ref[...], preferred_element_type=F32)
    hid = (gate * _sigmoid(gate) * up).astype(BF16)
    shared = jnp.dot(hid, wsd_ref[...], preferred_element_type=F32)

    tc, half = yk_ref.shape[1:]
    wk = wk_ref[...]
    acc_hi = jnp.zeros((tc, half), F32)
    acc_lo = jnp.zeros((tc, half), F32)
    for j in range(TOP_K):
        hi, lo = _unpack_halves(yk_ref[j])
        wj = wk[:, j:j + 1]
        acc_hi += wj * hi
        acc_lo += wj * lo
    ffn = jnp.concatenate([acc_hi, acc_lo], axis=1) + shared
    y = _layer_norm_rows(alpha * x + ffn, g_ref[...], b_ref[...])
    x2_ref[...] = y
    x2b_ref[...] = y.astype(BF16)


def _combine(x1, wk, yk, ws_gate, ws_up, ws_down, ln_g, ln_b, alpha, out_rows, out_row0, prev):
    t, d = x1.shape
    t_all = out_rows
    dp = yk.shape[2]
    ff = ws_gate.shape[1]
    tc = min(t, 256)
    blk0 = out_row0 // tc
    const = lambda i: (0, 0)
    rows = lambda i: (i, 0)
    rows_all = lambda i: (blk0 + i, 0)
    in_specs = [pl.BlockSpec((tc, d), rows),
                pl.BlockSpec((tc, TOP_K), rows),
                pl.BlockSpec((TOP_K, tc, dp), lambda i: (0, i, 0)),
                pl.BlockSpec((d, ff), const),
                pl.BlockSpec((d, ff), const),
                pl.BlockSpec((ff, d), const),
                pl.BlockSpec((1, d), const),
                pl.BlockSpec((1, d), const)]
    args = [x1, wk, yk, ws_gate, ws_up, ws_down, ln_g, ln_b]
    aliases = {}
    if prev is not None:
        in_specs += [pl.BlockSpec(memory_space=pl.ANY)] * 2
        aliases = {len(args): 0, len(args) + 1: 1}
        args += list(prev)
    return pl.pallas_call(
        functools.partial(_combine_kernel, alpha=alpha),
        out_shape=(jax.ShapeDtypeStruct((t_all, d), F32), jax.ShapeDtypeStruct((t_all, d), BF16)),
        grid=(t // tc,),
        in_specs=in_specs,
        out_specs=(pl.BlockSpec((tc, d), rows_all), pl.BlockSpec((tc, d), rows_all)),
        input_output_aliases=aliases,
        compiler_params=_params("parallel"),
        name="combine",
    )(*args)


def _expert_work_items(counts, n_rows, tm):
    ne = counts.shape[0]
    end = jnp.cumsum(counts)
    start = end - counts
    first_tile = start // tm
    n_it = jnp.where(counts > 0, (end - 1) // tm - first_tile + 1, 0)
    it_end = jnp.cumsum(n_it)
    it_start = it_end - n_it
    n_items = n_rows // tm + ne
    k = jnp.arange(n_items, dtype=I32)
    expert = jnp.minimum(jnp.sum((it_end[None, :] <= k[:, None]).astype(I32), axis=1), ne - 1)
    onehot = expert[:, None] == jnp.arange(ne, dtype=I32)[None, :]
    pick = lambda v: jnp.sum(jnp.where(onehot, v[None, :], 0), axis=1)
    tile = jnp.clip(pick(first_tile) + k - pick(it_start), 0, n_rows // tm - 1)
    lo = jnp.maximum(pick(start) - tile * tm, 0)
    hi = jnp.minimum(pick(end) - tile * tm, tm)
    changed = jnp.concatenate([jnp.zeros((1,), I32), (expert[1:] != expert[:-1]).astype(I32)])
    slot = jnp.cumsum(changed) % 2
    later = (k[None, :] < it_end[-1]) & (expert[None, :] > expert[:, None])
    nxt = jnp.min(jnp.where(later, expert[None, :], ne), axis=1)
    nxt = jnp.where(nxt < ne, nxt, -1)
    return start.astype(I32), (tile.astype(I32), expert.astype(I32), lo.astype(I32), hi.astype(I32),
                               slot.astype(I32), nxt.astype(I32), it_end[-1:].astype(I32))


def _positions_kernel(start_ref, eid_ref, rank_ref, pos_ref):
    eid = eid_ref[...]
    pos = rank_ref[...]
    for e in range(N_EXPERTS):
        pos = pos + jnp.where(eid == e, start_ref[e], 0)
    pos_ref[...] = pos


def _positions(expert_start, eid, rank):
    full = pl.BlockSpec(eid.shape, lambda i, st: (0, 0))
    return pl.pallas_call(
        _positions_kernel,
        out_shape=jax.ShapeDtypeStruct(eid.shape, I32),
        grid_spec=pltpu.PrefetchScalarGridSpec(num_scalar_prefetch=1, grid=(1,), in_specs=[full, full],
                                               out_specs=full),
        compiler_params=_params("arbitrary"),
        name="positions",
    )(expert_start, eid, rank)


def _moe(x1, x1p, w_router_t, bias_col, w_gate, w_up, w_down, layer, ws_gate, ws_up, ws_down, ln_g, ln_b, alpha,
         out_rows, out_row0, prev):
    t = x1.shape[0]
    tm = 256
    rank, eid, wgt, cnt = _router(x1, w_router_t, bias_col, 0, t)
    expert_start, items = _expert_work_items(cnt[:, 0], t * TOP_K, tm)
    pos = _positions(expert_start, eid, rank)
    xs = _dispatch(pos, x1p, 0)
    ys = _experts(items, xs, w_gate, w_up, w_down, layer, tm)
    yk = _gather_rows(ys, pos.reshape(-1)).reshape(TOP_K, t, -1)
    return _combine(x1, wgt.T, yk, ws_gate, ws_up, ws_down, ln_g, ln_b, alpha, out_rows, out_row0, prev)


def kernel(x, positions, w_in, lb_logits, hg_norm_g, w_hg_proj, w_att_proj, w_out, ln1_g, ln1_b,
           w_router, router_bias, w_e_gate, w_e_up, w_e_down, w_s_gate, w_s_up, w_s_down, ln2_g, ln2_b):
    batch, seq, d = x.shape
    depth = w_in.shape[0]
    t = batch * seq
    alpha = float((2 * depth) ** 0.25)
    hg_width = HG_HEADS * HG_DIM
    att_col0 = 4 * hg_width
    gate_col0 = att_col0 + 3 * len(ATT_GROUPS) * ATT_HEADS * ATT_DIM

    p = jax.nn.softmax(lb_logits.astype(F32), axis=0)
    cs = jnp.cumsum(p, axis=0)
    lower = cs - cs[:1]
    log1m_lower = jnp.log1p(-lower)

    cos, sin = _rope_tables(positions)
    x_all = x.reshape(t, d)
    in_width = w_in.shape[2]
    tn = 1280 if in_width % 1280 == 0 else 512

    n_parts = BATCH_PIPELINES if batch % BATCH_PIPELINES == 0 else 1
    bp = batch // n_parts
    tp = bp * seq
    state = [((x_all, part * tp), x_all[part * tp:(part + 1) * tp].astype(BF16)) for part in range(n_parts)]
    final = None
    for layer in range(depth):
        last = layer == depth - 1
        for part in range(n_parts):
            (xf, x_row0), xb = state[part]
            h = _matmul(xb, w_in, layer, min(tp, 1024), tn, BF16)
            o_h = _hgrn(h, lower[layer].reshape(1, -1), log1m_lower[layer].reshape(1, -1),
                        hg_norm_g[layer].reshape(1, -1), bp, seq)
            o_a = _attention(h, cos, sin, bp, seq, att_col0, part * bp)
            x1, x1p = _mix_out(xf, x_row0, h, o_h, o_a, w_hg_proj[layer].astype(BF16),
                               w_att_proj[layer].astype(BF16), w_out[layer].astype(BF16),
                               ln1_g[layer].reshape(1, d), ln1_b[layer].reshape(1, d), gate_col0, alpha)
            out = _moe(x1, x1p, w_router[layer].T, router_bias[layer].reshape(-1, 1),
                       w_e_gate, w_e_up, w_e_down, layer,
                       w_s_gate[layer].astype(BF16), w_s_up[layer].astype(BF16), w_s_down[layer].astype(BF16),
                       ln2_g[layer].reshape(1, d), ln2_b[layer].reshape(1, d), alpha,
                       t if last else tp, part * tp if last else 0, final if last else None)
            if last:
                final = out
            else:
                state[part] = ((out[0], 0), out[1])
    return final[0].reshape(batch, seq, d)
```

```python
import functools

import numpy as np
import jax
import jax.numpy as jnp
from jax import lax
from jax.experimental import pallas as pl
from jax.experimental.pallas import tpu as pltpu
from jax.experimental.pallas import tpu_sc as plsc

F32 = jnp.float32
BF16 = jnp.bfloat16
I32 = jnp.int32

LANES = 128
VMEM_LIMIT_BYTES = 56 * 1024 * 1024
SC_CORES = 2
SC_SUBCORES = 16
SC_WORKERS = SC_CORES * SC_SUBCORES
SC_SCATTER_ROWS = 32
SC_GATHER_ROWS = 16
SC_GATHER_BUFFERS = 4
EXPERT_ROW_TILE = 256
EXPERT_CAST_STEPS = 16

HG_HEADS = 8
HG_DIM = 128
HG_CHUNK = 64
HG_SUB = 16
ATT_GROUPS = ((128, 1), (512, 4), (2048, 16))
ATT_HEADS = 4
ATT_DIM = 128
ATT_BACK = 128
ROPE_THETA = 10000.0
N_EXPERTS = 64
N_GROUPS = 8
TOPK_GROUPS = 4
TOP_K = 8
ROUTED_SCALE = 2.5
LN_EPS = 1e-5
NORM_EPS = 1e-6

NT_DIMS = (((1,), (1,)), ((), ()))
TN_DIMS = (((0,), (0,)), ((), ()))


def _params(*sem):
    return pltpu.CompilerParams(dimension_semantics=sem, vmem_limit_bytes=VMEM_LIMIT_BYTES)


def _sigmoid(x):
    return 1.0 / (1.0 + jnp.exp(-x))


def _matmul_kernel(x_ref, w_ref, o_ref, wb_ref):
    @pl.when(pl.program_id(1) == 0)
    def _():
        wb_ref[...] = w_ref[0].astype(BF16)

    o_ref[...] = jnp.dot(x_ref[...], wb_ref[...], preferred_element_type=F32).astype(o_ref.dtype)


def _matmul(x, w, layer, tm, tn, out_dtype):
    m, k = x.shape
    n = w.shape[2]
    return pl.pallas_call(
        _matmul_kernel,
        out_shape=jax.ShapeDtypeStruct((m, n), out_dtype),
        grid=(n // tn, m // tm),
        in_specs=[pl.BlockSpec((tm, k), lambda j, i: (i, 0)),
                  pl.BlockSpec((1, k, tn), lambda j, i: (layer, 0, j))],
        out_specs=pl.BlockSpec((tm, tn), lambda j, i: (i, j)),
        scratch_shapes=[pltpu.VMEM((k, tn), BF16)],
        compiler_params=_params("parallel", "arbitrary"),
        name="in_proj",
    )(x, w)


def _rope_table_kernel(pos_ref, inv_ref, sign_ref, cos_ref, sin_ref):
    ang = pos_ref[...].astype(F32) * inv_ref[...]
    cos_ref[...] = jnp.cos(ang)
    sin_ref[...] = jnp.sin(ang) * sign_ref[...]


def _rope_tables(positions):
    t = positions.size
    half = ATT_DIM // 2
    inv_half = ROPE_THETA ** (-np.arange(half, dtype=np.float32) * np.float32(2.0) / np.float32(ATT_DIM))
    inv = jnp.asarray(np.concatenate([inv_half, inv_half]).astype(np.float32).reshape(1, ATT_DIM))
    sign = jnp.asarray(np.concatenate([-np.ones(half), np.ones(half)]).astype(np.float32).reshape(1, ATT_DIM))
    tm = min(t, 2048)
    return pl.pallas_call(
        _rope_table_kernel,
        out_shape=(jax.ShapeDtypeStruct((t, ATT_DIM), F32), jax.ShapeDtypeStruct((t, ATT_DIM), F32)),
        grid=(t // tm,),
        in_specs=[pl.BlockSpec((tm, 1), lambda i: (i, 0)),
                  pl.BlockSpec((1, ATT_DIM), lambda i: (0, 0)),
                  pl.BlockSpec((1, ATT_DIM), lambda i: (0, 0))],
        out_specs=(pl.BlockSpec((tm, ATT_DIM), lambda i: (i, 0)),
                   pl.BlockSpec((tm, ATT_DIM), lambda i: (i, 0))),
        compiler_params=_params("parallel"),
        name="rope_tables",
    )(positions.reshape(t, 1), inv, sign)


def _cumsum_rows(x):
    n = x.shape[0]
    row = lax.broadcasted_iota(I32, x.shape, 0)
    s = 1
    while s < n:
        x = x + jnp.where(row >= s, pltpu.roll(x, s, 0), 0.0)
        s *= 2
    return x


def _rows_from(cum, offsets):
    parts = []
    for o in offsets:
        if o is None:
            parts.append(jnp.zeros((HG_SUB, cum.shape[1]), F32))
        else:
            parts.append(jnp.broadcast_to(cum[o:o + 1, :], (HG_SUB, cum.shape[1])))
    return jnp.concatenate(parts, axis=0)


def _hgrn_kernel(hq_ref, hf_ref, hi_ref, hg_ref, lb_ref, l1m_ref, ng_ref, o_ref, st_ref, *, n_chunks):
    c = HG_CHUNK
    nsub = c // HG_SUB

    @pl.when(pl.program_id(2) == 0)
    def _():
        st_ref[...] = jnp.zeros_like(st_ref)

    lb = lb_ref[...]
    one_m_lb = 1.0 - lb
    log1m_lb = l1m_ref[...]
    norm_g = ng_ref[...]
    row = lax.broadcasted_iota(I32, (c, HG_DIM), 0)
    blk = row // HG_SUB
    ti = lax.broadcasted_iota(I32, (c, c), 0)
    si = lax.broadcasted_iota(I32, (c, c), 1)
    diag_mask = ((ti // HG_SUB) == (si // HG_SUB)) & (si <= ti)

    def chunk(ci, carry):
        r0 = pl.multiple_of(ci * c, c)
        z = hf_ref[pl.ds(r0, c), :].astype(F32)
        qraw = hq_ref[pl.ds(r0, c), :].astype(F32)
        v = hi_ref[pl.ds(r0, c), :]
        g = hg_ref[pl.ds(r0, c), :].astype(F32)

        e = jnp.exp(-jnp.abs(z))
        r = 1.0 / (1.0 + e)
        sig = jnp.where(z >= 0, r, e * r)
        sig_neg = jnp.where(z >= 0, e * r, r)
        log_sig = jnp.minimum(z, 0.0) - jnp.log(1.0 + e)
        log_f = jnp.maximum(jnp.log(lb + one_m_lb * sig), log1m_lb + log_sig)
        k = one_m_lb * sig_neg
        q = qraw * _sigmoid(qraw)

        cum = _cumsum_rows(log_f)
        start = _rows_from(cum, [None] + [HG_SUB * i - 1 for i in range(1, nsub)])
        mid = _rows_from(cum, [HG_SUB * i + HG_SUB // 2 - 1 for i in range(nsub)])
        end = cum[c - 1:c, :]

        qd = (q * jnp.exp(cum - mid)).astype(BF16)
        kd = (k * jnp.exp(mid - cum)).astype(BF16)
        att = jnp.where(diag_mask, lax.dot_general(qd, kd, NT_DIMS, preferred_element_type=F32), 0.0)

        qs = q * jnp.exp(cum - start)
        q_slots, k_slots = [], []
        for i in range(1, nsub):
            q_slots.append(jnp.where(blk == i, qs, 0.0).astype(BF16))
            n_rows = HG_SUB * i
            b_i = cum[n_rows - 1:n_rows, :]
            k_i = k[:n_rows] * jnp.exp(b_i - cum[:n_rows])
            k_slots.append(jnp.concatenate([k_i, jnp.zeros((c - n_rows, HG_DIM), F32)], axis=0).astype(BF16))
        att = att + lax.dot_general(jnp.concatenate(q_slots, axis=1), jnp.concatenate(k_slots, axis=1),
                                    NT_DIMS, preferred_element_type=F32)
        intra = jnp.dot(att.astype(BF16), v, preferred_element_type=F32)

        st = st_ref[...]
        inter = lax.dot_general((q * jnp.exp(cum)).astype(BF16), st.astype(BF16), NT_DIMS,
                                preferred_element_type=F32)
        k_end = (k * jnp.exp(end - cum)).astype(BF16)
        st_ref[...] = st * jnp.exp(end) + lax.dot_general(v, k_end, TN_DIMS, preferred_element_type=F32)

        o = inter + intra
        o = o * lax.rsqrt(jnp.mean(o * o, axis=-1, keepdims=True) + NORM_EPS) * norm_g
        o_ref[pl.ds(r0, c), :] = (o * _sigmoid(g)).astype(o_ref.dtype)
        return carry

    lax.fori_loop(0, n_chunks, chunk, 0, unroll=8)


def _hgrn(h, lb, log1m_lb, norm_g, batch, seq):
    t = h.shape[0]
    tb = min(seq, 2048)
    nsb = seq // tb
    width = HG_HEADS * HG_DIM

    def col(off):
        return pl.BlockSpec((tb, HG_DIM), lambda b, hh, s: (b * nsb + s, off * HG_HEADS + hh))

    vec = pl.BlockSpec((1, HG_DIM), lambda b, hh, s: (0, hh))
    return pl.pallas_call(
        functools.partial(_hgrn_kernel, n_chunks=tb // HG_CHUNK),
        out_shape=jax.ShapeDtypeStruct((t, width), BF16),
        grid=(batch, HG_HEADS, nsb),
        in_specs=[col(0), col(1), col(2), col(3), vec, vec, vec],
        out_specs=pl.BlockSpec((tb, HG_DIM), lambda b, hh, s: (b * nsb + s, hh)),
        scratch_shapes=[pltpu.VMEM((HG_DIM, HG_DIM), F32)],
        compiler_params=_params("parallel", "parallel", "arbitrary"),
        name="hgrn2",
    )(h, h, h, h, lb, log1m_lb, norm_g)


def _attn_kernel(*refs, seq):
    qkv_refs = refs[:9]
    cos_ref, sin_ref, o_ref = refs[9:12]
    qf, kf, vf, og, lg = refs[12:]
    n_groups = len(ATT_GROUPS)
    scale = ATT_DIM ** -0.5
    rb = min(seq, 256)

    def rope_rows(i, carry):
        r0 = pl.multiple_of(i * rb, rb)
        cs = cos_ref[pl.ds(r0, rb), :]
        sn = sin_ref[pl.ds(r0, rb), :]
        for gi in range(n_groups):
            xq = qkv_refs[3 * gi][pl.ds(r0, rb), :].astype(F32)
            xk = qkv_refs[3 * gi + 1][pl.ds(r0, rb), :].astype(F32)
            qf[gi, pl.ds(r0, rb), :] = (xq * cs + pltpu.roll(xq, ATT_DIM // 2, 1) * sn) * scale
            kf[gi, pl.ds(r0, rb), :] = xk * cs + pltpu.roll(xk, ATT_DIM // 2, 1) * sn
            vf[gi, pl.ds(r0, rb), :] = qkv_refs[3 * gi + 2][pl.ds(r0, rb), :].astype(F32)
        return carry

    lax.fori_loop(0, seq // rb, rope_rows, 0)

    qb = ATT_BACK
    for gi, (_, dil) in enumerate(ATT_GROUPS):
        length = seq // dil
        nk = min(2 * qb, length)
        n_qblk = length // qb

        def block(it, carry, gi=gi, dil=dil, nk=nk, n_qblk=n_qblk):
            res = it // n_qblk
            iq = it % n_qblk
            q0 = iq * qb
            k0 = jnp.maximum(q0 - qb, 0)
            if dil == 1:
                q_rows = pl.ds(pl.multiple_of(q0, qb), qb)
                k_rows = pl.ds(pl.multiple_of(k0, qb), nk)
            else:
                q_rows = pl.ds(res + dil * q0, qb, stride=dil)
                k_rows = pl.ds(res + dil * k0, nk, stride=dil)
            q = qf[gi, q_rows, :].astype(BF16)
            k = kf[gi, k_rows, :].astype(BF16)
            v = vf[gi, k_rows, :].astype(BF16)
            wk = 2 * qb
            k = jnp.concatenate([k] * (wk // nk), axis=0)
            v = jnp.concatenate([v] * (wk // nk), axis=0)
            col = lax.broadcasted_iota(I32, (qb, wk), 1)
            dist = (q0 + lax.broadcasted_iota(I32, (qb, wk), 0)) - (k0 + col)
            if wk > nk:
                dist = jnp.where(col < nk, dist, -1)
            s = lax.dot_general(q, k, NT_DIMS, preferred_element_type=F32)
            s = jnp.where((dist >= 0) & (dist <= ATT_BACK), s, -jnp.inf)
            m = jnp.max(s, axis=-1, keepdims=True)
            p = jnp.exp(s - m)
            den = jnp.sum(p, axis=-1, keepdims=True)
            o = jnp.dot(p.astype(BF16), v, preferred_element_type=F32) / den
            og[gi, q_rows, :] = o
            lg[gi, q_rows, :] = jnp.broadcast_to(m + jnp.log(den), (qb, ATT_DIM))
            return carry

        lax.fori_loop(0, dil * n_qblk, block, 0, unroll=8)

    def merge_rows(i, carry):
        r0 = pl.multiple_of(i * rb, rb)
        ls = [lg[gi, pl.ds(r0, rb), :] for gi in range(n_groups)]
        m = functools.reduce(jnp.maximum, ls)
        ws = [jnp.exp(l - m) for l in ls]
        num = sum(w * og[gi, pl.ds(r0, rb), :] for gi, w in enumerate(ws))
        o_ref[pl.ds(r0, rb), :] = (num / sum(ws)).astype(o_ref.dtype)
        return carry

    lax.fori_loop(0, seq // rb, merge_rows, 0)


def _attention(h, cos, sin, batch, seq, col0):
    t = h.shape[0]
    n_groups = len(ATT_GROUPS)
    part = n_groups * ATT_HEADS * ATT_DIM
    blk0 = col0 // ATT_DIM
    in_specs = []
    for gi in range(n_groups):
        for p in range(3):
            off = blk0 + (p * part) // ATT_DIM + gi * ATT_HEADS
            in_specs.append(pl.BlockSpec((seq, ATT_DIM), lambda b, hh, off=off: (b, off + hh)))
    tab = pl.BlockSpec((seq, ATT_DIM), lambda b, hh: (b, 0))
    in_specs += [tab, tab]
    scr = pltpu.VMEM((n_groups, seq, ATT_DIM), F32)
    return pl.pallas_call(
        functools.partial(_attn_kernel, seq=seq),
        out_shape=jax.ShapeDtypeStruct((t, ATT_HEADS * ATT_DIM), BF16),
        grid=(batch, ATT_HEADS),
        in_specs=in_specs,
        out_specs=pl.BlockSpec((seq, ATT_DIM), lambda b, hh: (b, hh)),
        scratch_shapes=[scr] * 5,
        compiler_params=_params("parallel", "parallel"),
        name="dilated_attn",
    )(*([h] * 9), cos, sin)


def _layer_norm_rows(r, g, b):
    mu = jnp.mean(r, axis=-1, keepdims=True)
    d = r - mu
    var = jnp.mean(d * d, axis=-1, keepdims=True)
    return d * lax.rsqrt(var + LN_EPS) * g + b


def _pack_halves(y):
    n = y.shape[1] // 2
    return pltpu.pack_elementwise([y[:, :n], y[:, n:]], packed_dtype=BF16)


def _unpack_halves(w):
    first = pltpu.unpack_elementwise(w, index=0, packed_dtype=BF16, unpacked_dtype=F32)
    second = pltpu.unpack_elementwise(w, index=1, packed_dtype=BF16, unpacked_dtype=F32)
    return first, second


def _mix_out_kernel(*refs, alpha, n_col):
    x_ref, oh_ref, oa_ref = refs[:3]
    gh_refs = refs[3:3 + n_col]
    ga_refs = refs[3 + n_col:3 + 2 * n_col]
    whg_ref, wap_ref, wout_ref, g_ref, b_ref, x1_ref, x1p_ref = refs[3 + 2 * n_col:]
    tn = gh_refs[0].shape[1]
    oh = oh_ref[...]
    oa = oa_ref[...]
    merged = []
    for n in range(n_col):
        cols = slice(n * tn, (n + 1) * tn)
        y_h = jnp.dot(oh, whg_ref[:, cols], preferred_element_type=F32)
        y_a = jnp.dot(oa, wap_ref[:, cols], preferred_element_type=F32)
        m = _sigmoid(gh_refs[n][...].astype(F32)) * y_h + _sigmoid(ga_refs[n][...].astype(F32)) * y_a
        merged.append(m.astype(BF16))
    mix = jnp.dot(jnp.concatenate(merged, axis=1), wout_ref[...], preferred_element_type=F32)
    y = _layer_norm_rows(alpha * x_ref[...] + mix, g_ref[...], b_ref[...])
    x1_ref[...] = y
    x1p_ref[...] = _pack_halves(y)


def _mix_out(x, h, o_h, o_a, w_hg, w_ap, w_out, ln_g, ln_b, gate_col0, alpha):
    t, d = x.shape
    tm = min(t, 256)
    tn = 512
    n_col = d // tn
    gh0 = gate_col0 // tn
    ga0 = (gate_col0 + d) // tn
    rows = lambda i: (i, 0)
    const = lambda i: (0, 0)
    resident = dict(index_map=const, pipeline_mode=pl.Buffered(1))
    gate_specs = [pl.BlockSpec((tm, tn), lambda i, c=c0 + n: (i, c)) for c0 in (gh0, ga0) for n in range(n_col)]
    return pl.pallas_call(
        functools.partial(_mix_out_kernel, alpha=alpha, n_col=n_col),
        out_shape=(jax.ShapeDtypeStruct((t, d), F32), jax.ShapeDtypeStruct((t, d // 2), I32)),
        grid=(t // tm,),
        in_specs=[pl.BlockSpec((tm, d), rows),
                  pl.BlockSpec((tm, o_h.shape[1]), rows),
                  pl.BlockSpec((tm, o_a.shape[1]), rows),
                  *gate_specs,
                  pl.BlockSpec(w_hg.shape, **resident),
                  pl.BlockSpec(w_ap.shape, **resident),
                  pl.BlockSpec(w_out.shape, **resident),
                  pl.BlockSpec((1, d), const),
                  pl.BlockSpec((1, d), const)],
        out_specs=(pl.BlockSpec((tm, d), rows), pl.BlockSpec((tm, d // 2), rows)),
        compiler_params=_params("parallel"),
        name="mix_out",
    )(x, o_h, o_a, *([h] * (2 * n_col)), w_hg, w_ap, w_out, ln_g, ln_b)


def _split_bf16(x):
    hi = x.astype(BF16)
    lo = (x - hi.astype(F32)).astype(BF16)
    return hi, lo


def _router_kernel(x_ref, wr_ref, bias_ref, tri_ref, low_ref, rank_ref, eid_ref, wgt_ref, cnt_ref,
                   carry_ref, *, tr):
    ne = N_EXPERTS
    per = ne // N_GROUPS

    @pl.when(pl.program_id(0) == 0)
    def _():
        carry_ref[...] = jnp.zeros_like(carry_ref)

    xh, xl = _split_bf16(x_ref[...])
    wh, wl = _split_bf16(wr_ref[...])
    logits = (lax.dot_general(wh, xh, NT_DIMS, preferred_element_type=F32)
              + lax.dot_general(wh, xl, NT_DIMS, preferred_element_type=F32)
              + lax.dot_general(wl, xh, NT_DIMS, preferred_element_type=F32))
    scores = _sigmoid(logits)
    sel = scores + bias_ref[...]

    grp = sel.reshape(N_GROUPS, per, tr)
    sub = lax.broadcasted_iota(I32, grp.shape, 1)
    m1 = jnp.max(grp, axis=1, keepdims=True)
    first = jnp.min(jnp.where(grp == m1, sub, per), axis=1, keepdims=True)
    m2 = jnp.max(jnp.where(sub == first, -jnp.inf, grp), axis=1, keepdims=True)
    gs = m1 + m2
    gidx = lax.broadcasted_iota(I32, gs.shape, 0)
    grank = jnp.zeros(gs.shape, I32)
    for j in range(N_GROUPS):
        other = gs[j:j + 1]
        grank += ((other > gs) | ((other == gs) & (j < gidx))).astype(I32)
    masked = jnp.where(grank < TOPK_GROUPS, grp, -jnp.inf).reshape(ne, tr)

    eidx = lax.broadcasted_iota(I32, (ne, tr), 0)
    work = masked
    picked = jnp.zeros((ne, tr), F32)
    for _ in range(TOP_K):
        top = jnp.max(work, axis=0, keepdims=True)
        first = jnp.min(jnp.where(work == top, eidx, ne), axis=0, keepdims=True)
        hit = eidx == first
        picked = jnp.where(hit, 1.0, picked)
        work = jnp.where(hit, -jnp.inf, work)
    chosen = picked > 0.0
    w = jnp.where(chosen, scores, 0.0)
    gates = w / jnp.sum(w, axis=0, keepdims=True) * ROUTED_SCALE

    chosen_b = jnp.where(chosen, 1.0, 0.0).astype(BF16)
    incl = jnp.dot(chosen_b, tri_ref[...], preferred_element_type=F32)
    carry = carry_ref[...]
    rank_in_expert = (carry + incl - 1.0).astype(I32)
    carry_new = carry + incl[:, tr - 1:tr]
    carry_ref[...] = carry_new
    cnt_ref[...] = jnp.broadcast_to(carry_new, cnt_ref.shape).astype(I32)

    slot = jnp.dot(low_ref[...], chosen_b, preferred_element_type=F32).astype(I32)
    for j in range(TOP_K):
        pick = chosen & (slot == j)
        rank_ref[pl.ds(j, 1), :] = jnp.sum(jnp.where(pick, rank_in_expert, 0), axis=0, keepdims=True)
        eid_ref[pl.ds(j, 1), :] = jnp.sum(jnp.where(pick, eidx, 0), axis=0, keepdims=True)
        wgt_ref[pl.ds(j, 1), :] = jnp.sum(jnp.where(pick, gates, 0.0), axis=0, keepdims=True)


def _router(x1, w_router_t, bias_col):
    t, d = x1.shape
    ne = N_EXPERTS
    tr = min(t, 512)
    tri = jnp.asarray(np.triu(np.ones((tr, tr), np.float32)), BF16)
    low = jnp.asarray(np.tril(np.ones((ne, ne), np.float32), -1), BF16)
    slot_shape = jax.ShapeDtypeStruct((TOP_K, t), I32)
    slot_spec = pl.BlockSpec((TOP_K, tr), lambda i: (0, i))
    return pl.pallas_call(
        functools.partial(_router_kernel, tr=tr),
        out_shape=(slot_shape, slot_shape, jax.ShapeDtypeStruct((TOP_K, t), F32),
                   jax.ShapeDtypeStruct((ne, LANES), I32)),
        grid=(t // tr,),
        in_specs=[pl.BlockSpec((tr, d), lambda i: (i, 0)),
                  pl.BlockSpec((ne, d), lambda i: (0, 0)),
                  pl.BlockSpec((ne, 1), lambda i: (0, 0)),
                  pl.BlockSpec((tr, tr), lambda i: (0, 0)),
                  pl.BlockSpec((ne, ne), lambda i: (0, 0))],
        out_specs=(slot_spec, slot_spec, slot_spec, pl.BlockSpec((ne, LANES), lambda i: (0, 0))),
        scratch_shapes=[pltpu.VMEM((ne, 1), F32)],
        compiler_params=_params("arbitrary"),
        name="router",
    )(x1, w_router_t, bias_col, tri, low)


def _sc_worker_id():
    return lax.axis_index("s") * SC_CORES + lax.axis_index("c")


def _dispatch(pos, x1p):
    t, dp = x1p.shape
    t_per_w = t // SC_WORKERS
    chunk = min(SC_SCATTER_ROWS, t_per_w // 2)
    n_chunks = t_per_w // chunk
    mesh = plsc.VectorSubcoreMesh(core_axis_name="c", subcore_axis_name="s")

    @functools.partial(
        pl.kernel, mesh=mesh,
        out_type=jax.ShapeDtypeStruct((t * TOP_K, dp), x1p.dtype),
        scratch_types=[pltpu.VMEM((n_chunks * TOP_K, chunk), I32),
                       pltpu.VMEM((2, chunk, dp), x1p.dtype),
                       pltpu.SemaphoreType.DMA((2,)),
                       pltpu.SemaphoreType.DMA((2,))],
    )
    def scatter_rows(x_hbm, pos_hbm, out_hbm, idx_v, rows_v, load_sem, scat_sem):
        wid = _sc_worker_id()
        base = wid * t_per_w
        pltpu.sync_copy(pos_hbm.at[wid], idx_v)

        def load(c, b):
            return pltpu.make_async_copy(x_hbm.at[pl.ds(base + c * chunk, chunk)], rows_v.at[b], load_sem.at[b])

        def scat(c, b, j):
            return pltpu.make_async_copy(rows_v.at[b], out_hbm.at[idx_v.at[c * TOP_K + j]], scat_sem.at[b])

        load(0, 0).start()

        @pl.loop(0, n_chunks, step=2)
        def _(c0):
            for b in range(2):
                c = c0 + b
                load(c, b).wait()
                for j in range(TOP_K):
                    scat(c, b, j).start()

                @pl.when(c + 1 < n_chunks)
                def _():
                    @pl.when(c >= 1)
                    def _():
                        for j in range(TOP_K):
                            scat(c - 1, 1 - b, j).wait()
                    load(c + 1, 1 - b).start()

        for b in range(2):
            for j in range(TOP_K):
                scat(n_chunks - 2 + b, b, j).wait()

    idx = pos.reshape(TOP_K, SC_WORKERS, n_chunks, chunk).transpose(1, 2, 0, 3)
    return scatter_rows(x1p, idx.reshape(SC_WORKERS, n_chunks * TOP_K, chunk))


def _gather_rows(table, idx):
    n_rows = idx.shape[0]
    dp = table.shape[1]
    r_per_w = n_rows // SC_WORKERS
    chunk, nbuf = SC_GATHER_ROWS, SC_GATHER_BUFFERS
    n_chunks = r_per_w // chunk
    mesh = plsc.VectorSubcoreMesh(core_axis_name="c", subcore_axis_name="s")

    @functools.partial(
        pl.kernel, mesh=mesh,
        out_type=jax.ShapeDtypeStruct((n_rows, dp), table.dtype),
        scratch_types=[pltpu.VMEM((n_chunks, chunk), I32),
                       pltpu.VMEM((nbuf, chunk, dp), table.dtype),
                       pltpu.SemaphoreType.DMA((nbuf,)),
                       pltpu.SemaphoreType.DMA((nbuf,))],
    )
    def gather_rows(table_hbm, idx_hbm, out_hbm, idx_v, rows_v, gat_sem, store_sem):
        wid = _sc_worker_id()
        base = wid * r_per_w
        pltpu.sync_copy(idx_hbm.at[wid], idx_v)

        def gather(c, b):
            return pltpu.make_async_copy(table_hbm.at[idx_v.at[c]], rows_v.at[b], gat_sem.at[b])

        def store(c, b):
            return pltpu.make_async_copy(rows_v.at[b], out_hbm.at[pl.ds(base + c * chunk, chunk)], store_sem.at[b])

        for b in range(nbuf - 1):
            gather(b, b).start()

        @pl.loop(0, n_chunks, step=nbuf)
        def _(c0):
            for b in range(nbuf):
                c = c0 + b
                gather(c, b).wait()
                store(c, b).start()
                nb = (b + nbuf - 1) % nbuf

                @pl.when(c + nbuf - 1 < n_chunks)
                def _():
                    @pl.when(c >= 1)
                    def _():
                        store(c - 1, nb).wait()
                    gather(c + nbuf - 1, nb).start()

        for b in range(nbuf):
            store(n_chunks - nbuf + b, b).wait()

    return gather_rows(table, idx.reshape(SC_WORKERS, n_chunks, chunk))


def _experts_kernel(tile_ref, exp_ref, lo_ref, hi_ref, slot_ref, next_ref, nv_ref, xs_ref, wg_hbm, wu_hbm, wd_hbm,
                    ys_ref, wgf, wuf, wdf, wsem, wgb, wub, wdb, hid_ref, ybuf_ref, *, layer):
    it = pl.program_id(0)
    nv = nv_ref[0]
    cur = jnp.minimum(it, nv - 1)
    prev = jnp.maximum(it - 1, 0)

    def weight_copies(e):
        return [pltpu.make_async_copy(src.at[layer, e], dst, wsem.at[k])
                for k, (src, dst) in enumerate(((wg_hbm, wgf), (wu_hbm, wuf), (wd_hbm, wdf)))]

    @pl.when(it == 0)
    def _():
        hid_ref[...] = jnp.zeros_like(hid_ref)
        wdb[...] = jnp.zeros_like(wdb)
        ybuf_ref[...] = jnp.zeros_like(ybuf_ref)
        for cp in weight_copies(exp_ref[0]):
            cp.start()

    @pl.when((it < nv) & ((it == 0) | (exp_ref[cur] != exp_ref[prev])))
    def _():
        for cp in weight_copies(exp_ref[cur]):
            cp.wait()
        slot = slot_ref[cur]
        rows_g = wgf.shape[0] // EXPERT_CAST_STEPS
        rows_d = wdf.shape[0] // EXPERT_CAST_STEPS

        def cast_rows(i, carry):
            rg = pl.ds(pl.multiple_of(i * rows_g, rows_g), rows_g)
            rd = pl.ds(pl.multiple_of(i * rows_d, rows_d), rows_d)
            wgb[rg, :] = wgf[rg, :].astype(BF16)
            wub[rg, :] = wuf[rg, :].astype(BF16)
            wdb[slot, rd, :] = wdf[rd, :].astype(BF16)
            return carry

        lax.fori_loop(0, EXPERT_CAST_STEPS, cast_rows, 0)

        @pl.when(next_ref[cur] >= 0)
        def _():
            for cp in weight_copies(next_ref[cur]):
                cp.start()

    tm, half = xs_ref.shape
    y = _pack_halves(jnp.dot(hid_ref[(it + 1) % 2], wdb[slot_ref[prev]], preferred_element_type=F32))
    a, b = _unpack_halves(xs_ref[...])
    x = jnp.concatenate([a.astype(BF16), b.astype(BF16)], axis=1)
    gate = jnp.dot(x, wgb[...], preferred_element_type=F32)
    up = jnp.dot(x, wub[...], preferred_element_type=F32)
    hid_ref[it % 2] = (gate * _sigmoid(gate) * up).astype(BF16)

    done = (it >= 1) & (it <= nv)
    lo = jnp.where(done, lo_ref[prev], 0)
    hi = jnp.where(done, hi_ref[prev], 0)
    row = lax.broadcasted_iota(I32, (tm, half), 0)
    merged = jnp.where((row >= lo) & (row < hi), y, ybuf_ref[...])
    ybuf_ref[...] = merged
    ys_ref[...] = merged


def _experts(items, xs, w_gate, w_up, w_down, layer, tm):
    n_rows, dp = xs.shape
    _, ne, d, ff = w_gate.shape
    n_items = items[0].shape[0]

    def cur_map(i, tl, ex, lo, hi, sl, nx, nv):
        return (tl[jnp.minimum(i, nv[0] - 1)], 0)

    def prev_map(i, tl, ex, lo, hi, sl, nx, nv):
        return (tl[jnp.clip(i - 1, 0, nv[0] - 1)], 0)

    hbm = pl.BlockSpec(memory_space=pl.ANY)
    return pl.pallas_call(
        functools.partial(_experts_kernel, layer=layer),
        out_shape=jax.ShapeDtypeStruct((n_rows, dp), I32),
        grid_spec=pltpu.PrefetchScalarGridSpec(
            num_scalar_prefetch=7,
            grid=(n_items + 1,),
            in_specs=[pl.BlockSpec((tm, dp), cur_map), hbm, hbm, hbm],
            out_specs=pl.BlockSpec((tm, dp), prev_map),
            scratch_shapes=[pltpu.VMEM((d, ff), F32), pltpu.VMEM((d, ff), F32), pltpu.VMEM((ff, d), F32),
                            pltpu.SemaphoreType.DMA((3,)),
                            pltpu.VMEM((d, ff), BF16), pltpu.VMEM((d, ff), BF16), pltpu.VMEM((2, ff, d), BF16),
                            pltpu.VMEM((2, tm, ff), BF16), pltpu.VMEM((tm, dp), I32)]),
        compiler_params=_params("arbitrary"),
        name="experts",
    )(*items, xs, w_gate, w_up, w_down)


def _combine_kernel(x_ref, wk_ref, yk_ref, wsg_ref, wsu_ref, wsd_ref, g_ref, b_ref, x2_ref, x2b_ref, *, alpha):
    x = x_ref[...]
    xb = x.astype(BF16)
    gate = jnp.dot(xb, wsg_ref[...], preferred_element_type=F32)
    up = jnp.dot(xb, wsu_ref[...], preferred_element_type=F32)
    hid = (gate * _sigmoid(gate) * up).astype(BF16)
    shared = jnp.dot(hid, wsd_ref[...], preferred_element_type=F32)

    tc, half = yk_ref.shape[1:]
    wk = wk_ref[...]
    acc_hi = jnp.zeros((tc, half), F32)
    acc_lo = jnp.zeros((tc, half), F32)
    for j in range(TOP_K):
        hi, lo = _unpack_halves(yk_ref[j])
        wj = wk[:, j:j + 1]
        acc_hi += wj * hi
        acc_lo += wj * lo
    ffn = jnp.concatenate([acc_hi, acc_lo], axis=1) + shared
    y = _layer_norm_rows(alpha * x + ffn, g_ref[...], b_ref[...])
    x2_ref[...] = y
    x2b_ref[...] = y.astype(BF16)


def _combine(x1, wk, yk, ws_gate, ws_up, ws_down, ln_g, ln_b, alpha):
    t, d = x1.shape
    dp = yk.shape[2]
    ff = ws_gate.shape[1]
    tc = min(t, 256)
    const = lambda i: (0, 0)
    rows = lambda i: (i, 0)
    return pl.pallas_call(
        functools.partial(_combine_kernel, alpha=alpha),
        out_shape=(jax.ShapeDtypeStruct((t, d), F32), jax.ShapeDtypeStruct((t, d), BF16)),
        grid=(t // tc,),
        in_specs=[pl.BlockSpec((tc, d), rows),
                  pl.BlockSpec((tc, TOP_K), rows),
                  pl.BlockSpec((TOP_K, tc, dp), lambda i: (0, i, 0)),
                  pl.BlockSpec((d, ff), const),
                  pl.BlockSpec((d, ff), const),
                  pl.BlockSpec((ff, d), const),
                  pl.BlockSpec((1, d), const),
                  pl.BlockSpec((1, d), const)],
        out_specs=(pl.BlockSpec((tc, d), rows), pl.BlockSpec((tc, d), rows)),
        compiler_params=_params("parallel"),
        name="combine",
    )(x1, wk, yk, ws_gate, ws_up, ws_down, ln_g, ln_b)


def _expert_work_items(counts, n_rows, tm):
    ne = counts.shape[0]
    end = jnp.cumsum(counts)
    start = end - counts
    first_tile = start // tm
    n_it = jnp.where(counts > 0, (end - 1) // tm - first_tile + 1, 0)
    it_end = jnp.cumsum(n_it)
    it_start = it_end - n_it
    n_items = n_rows // tm + ne
    k = jnp.arange(n_items, dtype=I32)
    expert = jnp.minimum(jnp.sum((it_end[None, :] <= k[:, None]).astype(I32), axis=1), ne - 1)
    onehot = expert[:, None] == jnp.arange(ne, dtype=I32)[None, :]
    pick = lambda v: jnp.sum(jnp.where(onehot, v[None, :], 0), axis=1)
    tile = jnp.clip(pick(first_tile) + k - pick(it_start), 0, n_rows // tm - 1)
    lo = jnp.maximum(pick(start) - tile * tm, 0)
    hi = jnp.minimum(pick(end) - tile * tm, tm)
    changed = jnp.concatenate([jnp.zeros((1,), I32), (expert[1:] != expert[:-1]).astype(I32)])
    slot = jnp.cumsum(changed) % 2
    later = (k[None, :] < it_end[-1]) & (expert[None, :] > expert[:, None])
    nxt = jnp.min(jnp.where(later, expert[None, :], ne), axis=1)
    nxt = jnp.where(nxt < ne, nxt, -1)
    return start.astype(I32), (tile.astype(I32), expert.astype(I32), lo.astype(I32), hi.astype(I32),
                               slot.astype(I32), nxt.astype(I32), it_end[-1:].astype(I32))


def _positions_kernel(start_ref, eid_ref, rank_ref, pos_ref):
    eid = eid_ref[...]
    pos = rank_ref[...]
    for e in range(N_EXPERTS):
        pos = pos + jnp.where(eid == e, start_ref[e], 0)
    pos_ref[...] = pos


def _positions(expert_start, eid, rank):
    full = pl.BlockSpec(eid.shape, lambda i, st: (0, 0))
    return pl.pallas_call(
        _positions_kernel,
        out_shape=jax.ShapeDtypeStruct(eid.shape, I32),
        grid_spec=pltpu.PrefetchScalarGridSpec(num_scalar_prefetch=1, grid=(1,), in_specs=[full, full],
                                               out_specs=full),
        compiler_params=_params("arbitrary"),
        name="positions",
    )(expert_start, eid, rank)


def _moe(x1, x1p, w_router_t, bias_col, w_gate, w_up, w_down, layer, ws_gate, ws_up, ws_down, ln_g, ln_b, alpha):
    t = x1.shape[0]
    tm = EXPERT_ROW_TILE
    rank, eid, wgt, cnt = _router(x1, w_router_t, bias_col)
    expert_start, items = _expert_work_items(cnt[:, 0], t * TOP_K, tm)
    pos = _positions(expert_start, eid, rank)
    xs = _dispatch(pos, x1p)
    ys = _experts(items, xs, w_gate, w_up, w_down, layer, tm)
    yk = _gather_rows(ys, pos.reshape(-1)).reshape(TOP_K, t, -1)
    return _combine(x1, wgt.T, yk, ws_gate, ws_up, ws_down, ln_g, ln_b, alpha)


def kernel(x, positions, w_in, lb_logits, hg_norm_g, w_hg_proj, w_att_proj, w_out, ln1_g, ln1_b,
           w_router, router_bias, w_e_gate, w_e_up, w_e_down, w_s_gate, w_s_up, w_s_down, ln2_g, ln2_b):
    batch, seq, d = x.shape
    depth = w_in.shape[0]
    t = batch * seq
    alpha = float((2 * depth) ** 0.25)
    hg_width = HG_HEADS * HG_DIM
    att_col0 = 4 * hg_width
    gate_col0 = att_col0 + 3 * len(ATT_GROUPS) * ATT_HEADS * ATT_DIM

    p = jax.nn.softmax(lb_logits.astype(F32), axis=0)
    cs = jnp.cumsum(p, axis=0)
    lower = cs - cs[:1]
    log1m_lower = jnp.log1p(-lower)

    cos, sin = _rope_tables(positions)
    xf = x.reshape(t, d)
    xb = xf.astype(BF16)
    in_width = w_in.shape[2]
    tn = 1280 if in_width % 1280 == 0 else 512
    for layer in range(depth):
        h = _matmul(xb, w_in, layer, min(t, 1024), tn, BF16)
        o_h = _hgrn(h, lower[layer].reshape(1, -1), log1m_lower[layer].reshape(1, -1),
                    hg_norm_g[layer].reshape(1, -1), batch, seq)
        o_a = _attention(h, cos, sin, batch, seq, att_col0)
        x1, x1p = _mix_out(xf, h, o_h, o_a, w_hg_proj[layer].astype(BF16), w_att_proj[layer].astype(BF16),
                           w_out[layer].astype(BF16), ln1_g[layer].reshape(1, d), ln1_b[layer].reshape(1, d),
                           gate_col0, alpha)
        xf, xb = _moe(x1, x1p, w_router[layer].T, router_bias[layer].reshape(-1, 1),
                      w_e_gate, w_e_up, w_e_down, layer,
                      w_s_gate[layer].astype(BF16), w_s_up[layer].astype(BF16), w_s_down[layer].astype(BF16),
                      ln2_g[layer].reshape(1, d), ln2_b[layer].reshape(1, d), alpha)
    return xf.reshape(batch, seq, d)
```

```python
import functools

import numpy as np
import jax
import jax.numpy as jnp
from jax import lax
from jax.experimental import pallas as pl
from jax.experimental.pallas import tpu as pltpu
from jax.experimental.pallas import tpu_sc as plsc

F32 = jnp.float32
BF16 = jnp.bfloat16
I32 = jnp.int32

LANES = 128
VMEM_LIMIT_BYTES = 56 * 1024 * 1024
SC_CORES = 2
SC_SUBCORES = 16
SC_WORKERS = SC_CORES * SC_SUBCORES
SC_SCATTER_ROWS = 32
SC_GATHER_ROWS = 16
SC_GATHER_BUFFERS = 4
EXPERT_ROW_TILE = 256
EXPERT_CAST_STEPS = 16

HG_HEADS = 8
HG_DIM = 128
HG_CHUNK = 64
HG_SUB = 16
ATT_GROUPS = ((128, 1), (512, 4), (2048, 16))
ATT_HEADS = 4
ATT_DIM = 128
ATT_BACK = 128
ROPE_THETA = 10000.0
N_EXPERTS = 64
N_GROUPS = 8
TOPK_GROUPS = 4
TOP_K = 8
ROUTED_SCALE = 2.5
LN_EPS = 1e-5
NORM_EPS = 1e-6

NT_DIMS = (((1,), (1,)), ((), ()))
TN_DIMS = (((0,), (0,)), ((), ()))


def _params(*sem):
    return pltpu.CompilerParams(dimension_semantics=sem, vmem_limit_bytes=VMEM_LIMIT_BYTES)


def _sigmoid(x):
    return 1.0 / (1.0 + jnp.exp(-x))


def _matmul_kernel(x_ref, w_ref, o_ref, wb_ref):
    @pl.when(pl.program_id(1) == 0)
    def _():
        wb_ref[...] = w_ref[0].astype(BF16)

    o_ref[...] = jnp.dot(x_ref[...], wb_ref[...], preferred_element_type=F32).astype(o_ref.dtype)


def _matmul(x, w, layer, tm, tn, out_dtype):
    m, k = x.shape
    n = w.shape[2]
    return pl.pallas_call(
        _matmul_kernel,
        out_shape=jax.ShapeDtypeStruct((m, n), out_dtype),
        grid=(n // tn, m // tm),
        in_specs=[pl.BlockSpec((tm, k), lambda j, i: (i, 0)),
                  pl.BlockSpec((1, k, tn), lambda j, i: (layer, 0, j))],
        out_specs=pl.BlockSpec((tm, tn), lambda j, i: (i, j)),
        scratch_shapes=[pltpu.VMEM((k, tn), BF16)],
        compiler_params=_params("parallel", "arbitrary"),
        name="in_proj",
    )(x, w)


def _rope_table_kernel(pos_ref, inv_ref, sign_ref, cos_ref, sin_ref):
    ang = pos_ref[...].astype(F32) * inv_ref[...]
    cos_ref[...] = jnp.cos(ang)
    sin_ref[...] = jnp.sin(ang) * sign_ref[...]


def _rope_tables(positions):
    t = positions.size
    half = ATT_DIM // 2
    inv_half = ROPE_THETA ** (-np.arange(half, dtype=np.float32) * np.float32(2.0) / np.float32(ATT_DIM))
    inv = jnp.asarray(np.concatenate([inv_half, inv_half]).astype(np.float32).reshape(1, ATT_DIM))
    sign = jnp.asarray(np.concatenate([-np.ones(half), np.ones(half)]).astype(np.float32).reshape(1, ATT_DIM))
    tm = min(t, 2048)
    return pl.pallas_call(
        _rope_table_kernel,
        out_shape=(jax.ShapeDtypeStruct((t, ATT_DIM), F32), jax.ShapeDtypeStruct((t, ATT_DIM), F32)),
        grid=(t // tm,),
        in_specs=[pl.BlockSpec((tm, 1), lambda i: (i, 0)),
                  pl.BlockSpec((1, ATT_DIM), lambda i: (0, 0)),
                  pl.BlockSpec((1, ATT_DIM), lambda i: (0, 0))],
        out_specs=(pl.BlockSpec((tm, ATT_DIM), lambda i: (i, 0)),
                   pl.BlockSpec((tm, ATT_DIM), lambda i: (i, 0))),
        compiler_params=_params("parallel"),
        name="rope_tables",
    )(positions.reshape(t, 1), inv, sign)


def _cumsum_rows(x):
    n = x.shape[0]
    row = lax.broadcasted_iota(I32, x.shape, 0)
    s = 1
    while s < n:
        x = x + jnp.where(row >= s, pltpu.roll(x, s, 0), 0.0)
        s *= 2
    return x


def _rows_from(cum, offsets):
    parts = []
    for o in offsets:
        if o is None:
            parts.append(jnp.zeros((HG_SUB, cum.shape[1]), F32))
        else:
            parts.append(jnp.broadcast_to(cum[o:o + 1, :], (HG_SUB, cum.shape[1])))
    return jnp.concatenate(parts, axis=0)


def _hgrn_kernel(hq_ref, hf_ref, hi_ref, hg_ref, lb_ref, l1m_ref, ng_ref, o_ref, st_ref, *, n_chunks):
    c = HG_CHUNK
    nsub = c // HG_SUB

    @pl.when(pl.program_id(2) == 0)
    def _():
        st_ref[...] = jnp.zeros_like(st_ref)

    lb = lb_ref[...]
    one_m_lb = 1.0 - lb
    log1m_lb = l1m_ref[...]
    norm_g = ng_ref[...]
    row = lax.broadcasted_iota(I32, (c, HG_DIM), 0)
    blk = row // HG_SUB
    ti = lax.broadcasted_iota(I32, (c, c), 0)
    si = lax.broadcasted_iota(I32, (c, c), 1)
    diag_mask = ((ti // HG_SUB) == (si // HG_SUB)) & (si <= ti)

    def chunk(ci, carry):
        r0 = pl.multiple_of(ci * c, c)
        z = hf_ref[pl.ds(r0, c), :].astype(F32)
        qraw = hq_ref[pl.ds(r0, c), :].astype(F32)
        v = hi_ref[pl.ds(r0, c), :]
        g = hg_ref[pl.ds(r0, c), :].astype(F32)

        e = jnp.exp(-jnp.abs(z))
        r = 1.0 / (1.0 + e)
        sig = jnp.where(z >= 0, r, e * r)
        sig_neg = jnp.where(z >= 0, e * r, r)
        log_sig = jnp.minimum(z, 0.0) - jnp.log(1.0 + e)
        log_f = jnp.maximum(jnp.log(lb + one_m_lb * sig), log1m_lb + log_sig)
        k = one_m_lb * sig_neg
        q = qraw * _sigmoid(qraw)

        cum = _cumsum_rows(log_f)
        start = _rows_from(cum, [None] + [HG_SUB * i - 1 for i in range(1, nsub)])
        mid = _rows_from(cum, [HG_SUB * i + HG_SUB // 2 - 1 for i in range(nsub)])
        end = cum[c - 1:c, :]

        qd = (q * jnp.exp(cum - mid)).astype(BF16)
        kd = (k * jnp.exp(mid - cum)).astype(BF16)
        att = jnp.where(diag_mask, lax.dot_general(qd, kd, NT_DIMS, preferred_element_type=F32), 0.0)

        qs = q * jnp.exp(cum - start)
        q_slots, k_slots = [], []
        for i in range(1, nsub):
            q_slots.append(jnp.where(blk == i, qs, 0.0).astype(BF16))
            n_rows = HG_SUB * i
            b_i = cum[n_rows - 1:n_rows, :]
            k_i = k[:n_rows] * jnp.exp(b_i - cum[:n_rows])
            k_slots.append(jnp.concatenate([k_i, jnp.zeros((c - n_rows, HG_DIM), F32)], axis=0).astype(BF16))
        att = att + lax.dot_general(jnp.concatenate(q_slots, axis=1), jnp.concatenate(k_slots, axis=1),
                                    NT_DIMS, preferred_element_type=F32)
        intra = jnp.dot(att.astype(BF16), v, preferred_element_type=F32)

        st = st_ref[...]
        inter = lax.dot_general((q * jnp.exp(cum)).astype(BF16), st.astype(BF16), NT_DIMS,
                                preferred_element_type=F32)
        k_end = (k * jnp.exp(end - cum)).astype(BF16)
        st_ref[...] = st * jnp.exp(end) + lax.dot_general(v, k_end, TN_DIMS, preferred_element_type=F32)

        o = inter + intra
        o = o * lax.rsqrt(jnp.mean(o * o, axis=-1, keepdims=True) + NORM_EPS) * norm_g
        o_ref[pl.ds(r0, c), :] = (o * _sigmoid(g)).astype(o_ref.dtype)
        return carry

    lax.fori_loop(0, n_chunks, chunk, 0, unroll=32)


def _hgrn(h, lb, log1m_lb, norm_g, batch, seq):
    t = h.shape[0]
    tb = min(seq, 2048)
    nsb = seq // tb
    width = HG_HEADS * HG_DIM

    def col(off):
        return pl.BlockSpec((tb, HG_DIM), lambda b, hh, s: (b * nsb + s, off * HG_HEADS + hh))

    vec = pl.BlockSpec((1, HG_DIM), lambda b, hh, s: (0, hh))
    return pl.pallas_call(
        functools.partial(_hgrn_kernel, n_chunks=tb // HG_CHUNK),
        out_shape=jax.ShapeDtypeStruct((t, width), BF16),
        grid=(batch, HG_HEADS, nsb),
        in_specs=[col(0), col(1), col(2), col(3), vec, vec, vec],
        out_specs=pl.BlockSpec((tb, HG_DIM), lambda b, hh, s: (b * nsb + s, hh)),
        scratch_shapes=[pltpu.VMEM((HG_DIM, HG_DIM), F32)],
        compiler_params=_params("parallel", "parallel", "arbitrary"),
        name="hgrn2",
    )(h, h, h, h, lb, log1m_lb, norm_g)


def _attn_kernel(*refs, seq):
    qkv_refs = refs[:9]
    cos_ref, sin_ref, o_ref = refs[9:12]
    qf, kf, vf, og, lg = refs[12:]
    n_groups = len(ATT_GROUPS)
    scale = ATT_DIM ** -0.5
    rb = min(seq, 256)

    def rope_rows(i, carry):
        r0 = pl.multiple_of(i * rb, rb)
        cs = cos_ref[pl.ds(r0, rb), :]
        sn = sin_ref[pl.ds(r0, rb), :]
        for gi in range(n_groups):
            xq = qkv_refs[3 * gi][pl.ds(r0, rb), :].astype(F32)
            xk = qkv_refs[3 * gi + 1][pl.ds(r0, rb), :].astype(F32)
            qf[gi, pl.ds(r0, rb), :] = (xq * cs + pltpu.roll(xq, ATT_DIM // 2, 1) * sn) * scale
            kf[gi, pl.ds(r0, rb), :] = xk * cs + pltpu.roll(xk, ATT_DIM // 2, 1) * sn
            vf[gi, pl.ds(r0, rb), :] = qkv_refs[3 * gi + 2][pl.ds(r0, rb), :].astype(F32)
        return carry

    lax.fori_loop(0, seq // rb, rope_rows, 0)

    qb = ATT_BACK
    for gi, (_, dil) in enumerate(ATT_GROUPS):
        length = seq // dil
        nk = min(2 * qb, length)
        n_qblk = length // qb

        def block(it, carry, gi=gi, dil=dil, nk=nk, n_qblk=n_qblk):
            res = it // n_qblk
            iq = it % n_qblk
            q0 = iq * qb
            k0 = jnp.maximum(q0 - qb, 0)
            if dil == 1:
                q_rows = pl.ds(pl.multiple_of(q0, qb), qb)
                k_rows = pl.ds(pl.multiple_of(k0, qb), nk)
            else:
                q_rows = pl.ds(res + dil * q0, qb, stride=dil)
                k_rows = pl.ds(res + dil * k0, nk, stride=dil)
            q = qf[gi, q_rows, :].astype(BF16)
            k = kf[gi, k_rows, :].astype(BF16)
            v = vf[gi, k_rows, :].astype(BF16)
            wk = 2 * qb
            k = jnp.concatenate([k] * (wk // nk), axis=0)
            v = jnp.concatenate([v] * (wk // nk), axis=0)
            col = lax.broadcasted_iota(I32, (qb, wk), 1)
            dist = (q0 + lax.broadcasted_iota(I32, (qb, wk), 0)) - (k0 + col)
            if wk > nk:
                dist = jnp.where(col < nk, dist, -1)
            s = lax.dot_general(q, k, NT_DIMS, preferred_element_type=F32)
            s = jnp.where((dist >= 0) & (dist <= ATT_BACK), s, -jnp.inf)
            m = jnp.max(s, axis=-1, keepdims=True)
            p = jnp.exp(s - m)
            den = jnp.sum(p, axis=-1, keepdims=True)
            o = jnp.dot(p.astype(BF16), v, preferred_element_type=F32) / den
            og[gi, q_rows, :] = o
            lg[gi, q_rows, :] = jnp.broadcast_to(m + jnp.log(den), (qb, ATT_DIM))
            return carry

        lax.fori_loop(0, dil * n_qblk, block, 0, unroll=16)

    def merge_rows(i, carry):
        r0 = pl.multiple_of(i * rb, rb)
        ls = [lg[gi, pl.ds(r0, rb), :] for gi in range(n_groups)]
        m = functools.reduce(jnp.maximum, ls)
        ws = [jnp.exp(l - m) for l in ls]
        num = sum(w * og[gi, pl.ds(r0, rb), :] for gi, w in enumerate(ws))
        o_ref[pl.ds(r0, rb), :] = (num / sum(ws)).astype(o_ref.dtype)
        return carry

    lax.fori_loop(0, seq // rb, merge_rows, 0)


def _attention(h, cos, sin, batch, seq, col0):
    t = h.shape[0]
    n_groups = len(ATT_GROUPS)
    part = n_groups * ATT_HEADS * ATT_DIM
    blk0 = col0 // ATT_DIM
    in_specs = []
    for gi in range(n_groups):
        for p in range(3):
            off = blk0 + (p * part) // ATT_DIM + gi * ATT_HEADS
            in_specs.append(pl.BlockSpec((seq, ATT_DIM), lambda b, hh, off=off: (b, off + hh)))
    tab = pl.BlockSpec((seq, ATT_DIM), lambda b, hh: (b, 0))
    in_specs += [tab, tab]
    scr = pltpu.VMEM((n_groups, seq, ATT_DIM), F32)
    return pl.pallas_call(
        functools.partial(_attn_kernel, seq=seq),
        out_shape=jax.ShapeDtypeStruct((t, ATT_HEADS * ATT_DIM), BF16),
        grid=(batch, ATT_HEADS),
        in_specs=in_specs,
        out_specs=pl.BlockSpec((seq, ATT_DIM), lambda b, hh: (b, hh)),
        scratch_shapes=[scr] * 5,
        compiler_params=_params("parallel", "parallel"),
        name="dilated_attn",
    )(*([h] * 9), cos, sin)


def _layer_norm_rows(r, g, b):
    mu = jnp.mean(r, axis=-1, keepdims=True)
    d = r - mu
    var = jnp.mean(d * d, axis=-1, keepdims=True)
    return d * lax.rsqrt(var + LN_EPS) * g + b


def _pack_halves(y):
    n = y.shape[1] // 2
    return pltpu.pack_elementwise([y[:, :n], y[:, n:]], packed_dtype=BF16)


def _unpack_halves(w):
    first = pltpu.unpack_elementwise(w, index=0, packed_dtype=BF16, unpacked_dtype=F32)
    second = pltpu.unpack_elementwise(w, index=1, packed_dtype=BF16, unpacked_dtype=F32)
    return first, second


def _mix_out_kernel(*refs, alpha, n_col):
    x_ref, oh_ref, oa_ref = refs[:3]
    gh_refs = refs[3:3 + n_col]
    ga_refs = refs[3 + n_col:3 + 2 * n_col]
    whg_ref, wap_ref, wout_ref, g_ref, b_ref, x1_ref, x1p_ref = refs[3 + 2 * n_col:]
    tn = gh_refs[0].shape[1]
    oh = oh_ref[...]
    oa = oa_ref[...]
    merged = []
    for n in range(n_col):
        cols = slice(n * tn, (n + 1) * tn)
        y_h = jnp.dot(oh, whg_ref[:, cols], preferred_element_type=F32)
        y_a = jnp.dot(oa, wap_ref[:, cols], preferred_element_type=F32)
        m = _sigmoid(gh_refs[n][...].astype(F32)) * y_h + _sigmoid(ga_refs[n][...].astype(F32)) * y_a
        merged.append(m.astype(BF16))
    mix = jnp.dot(jnp.concatenate(merged, axis=1), wout_ref[...], preferred_element_type=F32)
    y = _layer_norm_rows(alpha * x_ref[...] + mix, g_ref[...], b_ref[...])
    x1_ref[...] = y
    x1p_ref[...] = _pack_halves(y)


def _mix_out(x, h, o_h, o_a, w_hg, w_ap, w_out, ln_g, ln_b, gate_col0, alpha):
    t, d = x.shape
    tm = min(t, 256)
    tn = 512
    n_col = d // tn
    gh0 = gate_col0 // tn
    ga0 = (gate_col0 + d) // tn
    rows = lambda i: (i, 0)
    const = lambda i: (0, 0)
    resident = dict(index_map=const, pipeline_mode=pl.Buffered(1))
    gate_specs = [pl.BlockSpec((tm, tn), lambda i, c=c0 + n: (i, c)) for c0 in (gh0, ga0) for n in range(n_col)]
    return pl.pallas_call(
        functools.partial(_mix_out_kernel, alpha=alpha, n_col=n_col),
        out_shape=(jax.ShapeDtypeStruct((t, d), F32), jax.ShapeDtypeStruct((t, d // 2), I32)),
        grid=(t // tm,),
        in_specs=[pl.BlockSpec((tm, d), rows),
                  pl.BlockSpec((tm, o_h.shape[1]), rows),
                  pl.BlockSpec((tm, o_a.shape[1]), rows),
                  *gate_specs,
                  pl.BlockSpec(w_hg.shape, **resident),
                  pl.BlockSpec(w_ap.shape, **resident),
                  pl.BlockSpec(w_out.shape, **resident),
                  pl.BlockSpec((1, d), const),
                  pl.BlockSpec((1, d), const)],
        out_specs=(pl.BlockSpec((tm, d), rows), pl.BlockSpec((tm, d // 2), rows)),
        compiler_params=_params("parallel"),
        name="mix_out",
    )(x, o_h, o_a, *([h] * (2 * n_col)), w_hg, w_ap, w_out, ln_g, ln_b)


def _split_bf16(x):
    hi = x.astype(BF16)
    lo = (x - hi.astype(F32)).astype(BF16)
    return hi, lo


def _router_kernel(x_ref, wr_ref, bias_ref, tri_ref, low_ref, rank_ref, eid_ref, wgt_ref, cnt_ref,
                   carry_ref, *, tr):
    ne = N_EXPERTS
    per = ne // N_GROUPS

    @pl.when(pl.program_id(0) == 0)
    def _():
        carry_ref[...] = jnp.zeros_like(carry_ref)

    xh, xl = _split_bf16(x_ref[...])
    wh, wl = _split_bf16(wr_ref[...])
    logits = (lax.dot_general(wh, xh, NT_DIMS, preferred_element_type=F32)
              + lax.dot_general(wh, xl, NT_DIMS, preferred_element_type=F32)
              + lax.dot_general(wl, xh, NT_DIMS, preferred_element_type=F32))
    scores = _sigmoid(logits)
    sel = scores + bias_ref[...]

    grp = sel.reshape(N_GROUPS, per, tr)
    sub = lax.broadcasted_iota(I32, grp.shape, 1)
    m1 = jnp.max(grp, axis=1, keepdims=True)
    first = jnp.min(jnp.where(grp == m1, sub, per), axis=1, keepdims=True)
    m2 = jnp.max(jnp.where(sub == first, -jnp.inf, grp), axis=1, keepdims=True)
    gs = m1 + m2
    gidx = lax.broadcasted_iota(I32, gs.shape, 0)
    grank = jnp.zeros(gs.shape, I32)
    for j in range(N_GROUPS):
        other = gs[j:j + 1]
        grank += ((other > gs) | ((other == gs) & (j < gidx))).astype(I32)
    masked = jnp.where(grank < TOPK_GROUPS, grp, -jnp.inf).reshape(ne, tr)

    eidx = lax.broadcasted_iota(I32, (ne, tr), 0)
    work = masked
    picked = jnp.zeros((ne, tr), F32)
    for _ in range(TOP_K):
        top = jnp.max(work, axis=0, keepdims=True)
        first = jnp.min(jnp.where(work == top, eidx, ne), axis=0, keepdims=True)
        hit = eidx == first
        picked = jnp.where(hit, 1.0, picked)
        work = jnp.where(hit, -jnp.inf, work)
    chosen = picked > 0.0
    w = jnp.where(chosen, scores, 0.0)
    gates = w / jnp.sum(w, axis=0, keepdims=True) * ROUTED_SCALE

    chosen_b = jnp.where(chosen, 1.0, 0.0).astype(BF16)
    incl = jnp.dot(chosen_b, tri_ref[...], preferred_element_type=F32)
    carry = carry_ref[...]
    rank_in_expert = (carry + incl - 1.0).astype(I32)
    carry_new = carry + incl[:, tr - 1:tr]
    carry_ref[...] = carry_new
    cnt_ref[...] = jnp.broadcast_to(carry_new, cnt_ref.shape).astype(I32)

    slot = jnp.dot(low_ref[...], chosen_b, preferred_element_type=F32).astype(I32)
    for j in range(TOP_K):
        pick = chosen & (slot == j)
        rank_ref[pl.ds(j, 1), :] = jnp.sum(jnp.where(pick, rank_in_expert, 0), axis=0, keepdims=True)
        eid_ref[pl.ds(j, 1), :] = jnp.sum(jnp.where(pick, eidx, 0), axis=0, keepdims=True)
        wgt_ref[pl.ds(j, 1), :] = jnp.sum(jnp.where(pick, gates, 0.0), axis=0, keepdims=True)


def _router(x1, w_router_t, bias_col):
    t, d = x1.shape
    ne = N_EXPERTS
    tr = min(t, 512)
    tri = jnp.asarray(np.triu(np.ones((tr, tr), np.float32)), BF16)
    low = jnp.asarray(np.tril(np.ones((ne, ne), np.float32), -1), BF16)
    slot_shape = jax.ShapeDtypeStruct((TOP_K, t), I32)
    slot_spec = pl.BlockSpec((TOP_K, tr), lambda i: (0, i))
    return pl.pallas_call(
        functools.partial(_router_kernel, tr=tr),
        out_shape=(slot_shape, slot_shape, jax.ShapeDtypeStruct((TOP_K, t), F32),
                   jax.ShapeDtypeStruct((ne, LANES), I32)),
        grid=(t // tr,),
        in_specs=[pl.BlockSpec((tr, d), lambda i: (i, 0)),
                  pl.BlockSpec((ne, d), lambda i: (0, 0)),
                  pl.BlockSpec((ne, 1), lambda i: (0, 0)),
                  pl.BlockSpec((tr, tr), lambda i: (0, 0)),
                  pl.BlockSpec((ne, ne), lambda i: (0, 0))],
        out_specs=(slot_spec, slot_spec, slot_spec, pl.BlockSpec((ne, LANES), lambda i: (0, 0))),
        scratch_shapes=[pltpu.VMEM((ne, 1), F32)],
        compiler_params=_params("arbitrary"),
        name="router",
    )(x1, w_router_t, bias_col, tri, low)


def _sc_worker_id():
    return lax.axis_index("s") * SC_CORES + lax.axis_index("c")


def _dispatch(pos, x1p):
    t, dp = x1p.shape
    t_per_w = t // SC_WORKERS
    chunk = min(SC_SCATTER_ROWS, t_per_w // 2)
    n_chunks = t_per_w // chunk
    mesh = plsc.VectorSubcoreMesh(core_axis_name="c", subcore_axis_name="s")

    @functools.partial(
        pl.kernel, mesh=mesh,
        out_type=jax.ShapeDtypeStruct((t * TOP_K, dp), x1p.dtype),
        scratch_types=[pltpu.VMEM((n_chunks * TOP_K, chunk), I32),
                       pltpu.VMEM((2, chunk, dp), x1p.dtype),
                       pltpu.SemaphoreType.DMA((2,)),
                       pltpu.SemaphoreType.DMA((2,))],
    )
    def scatter_rows(x_hbm, pos_hbm, out_hbm, idx_v, rows_v, load_sem, scat_sem):
        wid = _sc_worker_id()
        base = wid * t_per_w
        pltpu.sync_copy(pos_hbm.at[wid], idx_v)

        def load(c, b):
            return pltpu.make_async_copy(x_hbm.at[pl.ds(base + c * chunk, chunk)], rows_v.at[b], load_sem.at[b])

        def scat(c, b, j):
            return pltpu.make_async_copy(rows_v.at[b], out_hbm.at[idx_v.at[c * TOP_K + j]], scat_sem.at[b])

        load(0, 0).start()

        @pl.loop(0, n_chunks, step=2)
        def _(c0):
            for b in range(2):
                c = c0 + b
                load(c, b).wait()
                for j in range(TOP_K):
                    scat(c, b, j).start()

                @pl.when(c + 1 < n_chunks)
                def _():
                    @pl.when(c >= 1)
                    def _():
                        for j in range(TOP_K):
                            scat(c - 1, 1 - b, j).wait()
                    load(c + 1, 1 - b).start()

        for b in range(2):
            for j in range(TOP_K):
                scat(n_chunks - 2 + b, b, j).wait()

    idx = pos.reshape(TOP_K, SC_WORKERS, n_chunks, chunk).transpose(1, 2, 0, 3)
    return scatter_rows(x1p, idx.reshape(SC_WORKERS, n_chunks * TOP_K, chunk))


def _gather_rows(table, idx):
    n_rows = idx.shape[0]
    dp = table.shape[1]
    r_per_w = n_rows // SC_WORKERS
    chunk, nbuf = SC_GATHER_ROWS, SC_GATHER_BUFFERS
    n_chunks = r_per_w // chunk
    mesh = plsc.VectorSubcoreMesh(core_axis_name="c", subcore_axis_name="s")

    @functools.partial(
        pl.kernel, mesh=mesh,
        out_type=jax.ShapeDtypeStruct((n_rows, dp), table.dtype),
        scratch_types=[pltpu.VMEM((n_chunks, chunk), I32),
                       pltpu.VMEM((nbuf, chunk, dp), table.dtype),
                       pltpu.SemaphoreType.DMA((nbuf,)),
                       pltpu.SemaphoreType.DMA((nbuf,))],
    )
    def gather_rows(table_hbm, idx_hbm, out_hbm, idx_v, rows_v, gat_sem, store_sem):
        wid = _sc_worker_id()
        base = wid * r_per_w
        pltpu.sync_copy(idx_hbm.at[wid], idx_v)

        def gather(c, b):
            return pltpu.make_async_copy(table_hbm.at[idx_v.at[c]], rows_v.at[b], gat_sem.at[b])

        def store(c, b):
            return pltpu.make_async_copy(rows_v.at[b], out_hbm.at[pl.ds(base + c * chunk, chunk)], store_sem.at[b])

        for b in range(nbuf - 1):
            gather(b, b).start()

        @pl.loop(0, n_chunks, step=nbuf)
        def _(c0):
            for b in range(nbuf):
                c = c0 + b
                gather(c, b).wait()
                store(c, b).start()
                nb = (b + nbuf - 1) % nbuf

                @pl.when(c + nbuf - 1 < n_chunks)
                def _():
                    @pl.when(c >= 1)
                    def _():
                        store(c - 1, nb).wait()
                    gather(c + nbuf - 1, nb).start()

        for b in range(nbuf):
            store(n_chunks - nbuf + b, b).wait()

    return gather_rows(table, idx.reshape(SC_WORKERS, n_chunks, chunk))


def _experts_kernel(tile_ref, exp_ref, lo_ref, hi_ref, slot_ref, next_ref, nv_ref, xs_ref, wg_hbm, wu_hbm, wd_hbm,
                    ys_ref, wgf, wuf, wdf, wsem, wgb, wub, wdb, hid_ref, ybuf_ref, *, layer):
    it = pl.program_id(0)
    nv = nv_ref[0]
    cur = jnp.minimum(it, nv - 1)
    prev = jnp.maximum(it - 1, 0)

    def weight_copies(e):
        return [pltpu.make_async_copy(src.at[layer, e], dst, wsem.at[k])
                for k, (src, dst) in enumerate(((wg_hbm, wgf), (wu_hbm, wuf), (wd_hbm, wdf)))]

    @pl.when(it == 0)
    def _():
        hid_ref[...] = jnp.zeros_like(hid_ref)
        wdb[...] = jnp.zeros_like(wdb)
        ybuf_ref[...] = jnp.zeros_like(ybuf_ref)
        for cp in weight_copies(exp_ref[0]):
            cp.start()

    @pl.when((it < nv) & ((it == 0) | (exp_ref[cur] != exp_ref[prev])))
    def _():
        for cp in weight_copies(exp_ref[cur]):
            cp.wait()
        slot = slot_ref[cur]
        rows_g = wgf.shape[0] // EXPERT_CAST_STEPS
        rows_d = wdf.shape[0] // EXPERT_CAST_STEPS

        def cast_rows(i, carry):
            rg = pl.ds(pl.multiple_of(i * rows_g, rows_g), rows_g)
            rd = pl.ds(pl.multiple_of(i * rows_d, rows_d), rows_d)
            wgb[rg, :] = wgf[rg, :].astype(BF16)
            wub[rg, :] = wuf[rg, :].astype(BF16)
            wdb[slot, rd, :] = wdf[rd, :].astype(BF16)
            return carry

        lax.fori_loop(0, EXPERT_CAST_STEPS, cast_rows, 0)

        @pl.when(next_ref[cur] >= 0)
        def _():
            for cp in weight_copies(next_ref[cur]):
                cp.start()

    tm, half = xs_ref.shape
    y = _pack_halves(jnp.dot(hid_ref[(it + 1) % 2], wdb[slot_ref[prev]], preferred_element_type=F32))
    a, b = _unpack_halves(xs_ref[...])
    x = jnp.concatenate([a.astype(BF16), b.astype(BF16)], axis=1)
    gate = jnp.dot(x, wgb[...], preferred_element_type=F32)
    up = jnp.dot(x, wub[...], preferred_element_type=F32)
    hid_ref[it % 2] = (gate * _sigmoid(gate) * up).astype(BF16)

    done = (it >= 1) & (it <= nv)
    lo = jnp.where(done, lo_ref[prev], 0)
    hi = jnp.where(done, hi_ref[prev], 0)
    row = lax.broadcasted_iota(I32, (tm, half), 0)
    merged = jnp.where((row >= lo) & (row < hi), y, ybuf_ref[...])
    ybuf_ref[...] = merged
    ys_ref[...] = merged


def _experts(items, xs, w_gate, w_up, w_down, layer, tm):
    n_rows, dp = xs.shape
    _, ne, d, ff = w_gate.shape
    n_items = items[0].shape[0]

    def cur_map(i, tl, ex, lo, hi, sl, nx, nv):
        return (tl[jnp.minimum(i, nv[0] - 1)], 0)

    def prev_map(i, tl, ex, lo, hi, sl, nx, nv):
        return (tl[jnp.clip(i - 1, 0, nv[0] - 1)], 0)

    hbm = pl.BlockSpec(memory_space=pl.ANY)
    return pl.pallas_call(
        functools.partial(_experts_kernel, layer=layer),
        out_shape=jax.ShapeDtypeStruct((n_rows, dp), I32),
        grid_spec=pltpu.PrefetchScalarGridSpec(
            num_scalar_prefetch=7,
            grid=(n_items + 1,),
            in_specs=[pl.BlockSpec((tm, dp), cur_map), hbm, hbm, hbm],
            out_specs=pl.BlockSpec((tm, dp), prev_map),
            scratch_shapes=[pltpu.VMEM((d, ff), F32), pltpu.VMEM((d, ff), F32), pltpu.VMEM((ff, d), F32),
                            pltpu.SemaphoreType.DMA((3,)),
                            pltpu.VMEM((d, ff), BF16), pltpu.VMEM((d, ff), BF16), pltpu.VMEM((2, ff, d), BF16),
                            pltpu.VMEM((2, tm, ff), BF16), pltpu.VMEM((tm, dp), I32)]),
        compiler_params=_params("arbitrary"),
        name="experts",
    )(*items, xs, w_gate, w_up, w_down)


def _combine_kernel(x_ref, wk_ref, yk_ref, wsg_ref, wsu_ref, wsd_ref, g_ref, b_ref, x2_ref, x2b_ref, *, alpha):
    x = x_ref[...]
    xb = x.astype(BF16)
    gate = jnp.dot(xb, wsg_ref[...], preferred_element_type=F32)
    up = jnp.dot(xb, wsu_ref[...], preferred_element_type=F32)
    hid = (gate * _sigmoid(gate) * up).astype(BF16)
    shared = jnp.dot(hid, wsd_ref[...], preferred_element_type=F32)

    tc, half = yk_ref.shape[1:]
    wk = wk_ref[...]
    acc_hi = jnp.zeros((tc, half), F32)
    acc_lo = jnp.zeros((tc, half), F32)
    for j in range(TOP_K):
        hi, lo = _unpack_halves(yk_ref[j])
        wj = wk[:, j:j + 1]
        acc_hi += wj * hi
        acc_lo += wj * lo
    ffn = jnp.concatenate([acc_hi, acc_lo], axis=1) + shared
    y = _layer_norm_rows(alpha * x + ffn, g_ref[...], b_ref[...])
    x2_ref[...] = y
    x2b_ref[...] = y.astype(BF16)


def _combine(x1, wk, yk, ws_gate, ws_up, ws_down, ln_g, ln_b, alpha):
    t, d = x1.shape
    dp = yk.shape[2]
    ff = ws_gate.shape[1]
    tc = min(t, 256)
    const = lambda i: (0, 0)
    rows = lambda i: (i, 0)
    return pl.pallas_call(
        functools.partial(_combine_kernel, alpha=alpha),
        out_shape=(jax.ShapeDtypeStruct((t, d), F32), jax.ShapeDtypeStruct((t, d), BF16)),
        grid=(t // tc,),
        in_specs=[pl.BlockSpec((tc, d), rows),
                  pl.BlockSpec((tc, TOP_K), rows),
                  pl.BlockSpec((TOP_K, tc, dp), lambda i: (0, i, 0)),
                  pl.BlockSpec((d, ff), const),
                  pl.BlockSpec((d, ff), const),
                  pl.BlockSpec((ff, d), const),
                  pl.BlockSpec((1, d), const),
                  pl.BlockSpec((1, d), const)],
        out_specs=(pl.BlockSpec((tc, d), rows), pl.BlockSpec((tc, d), rows)),
        compiler_params=_params("parallel"),
        name="combine",
    )(x1, wk, yk, ws_gate, ws_up, ws_down, ln_g, ln_b)


def _expert_work_items(counts, n_rows, tm):
    ne = counts.shape[0]
    end = jnp.cumsum(counts)
    start = end - counts
    first_tile = start // tm
    n_it = jnp.where(counts > 0, (end - 1) // tm - first_tile + 1, 0)
    it_end = jnp.cumsum(n_it)
    it_start = it_end - n_it
    n_items = n_rows // tm + ne
    k = jnp.arange(n_items, dtype=I32)
    expert = jnp.minimum(jnp.sum((it_end[None, :] <= k[:, None]).astype(I32), axis=1), ne - 1)
    onehot = expert[:, None] == jnp.arange(ne, dtype=I32)[None, :]
    pick = lambda v: jnp.sum(jnp.where(onehot, v[None, :], 0), axis=1)
    tile = jnp.clip(pick(first_tile) + k - pick(it_start), 0, n_rows // tm - 1)
    lo = jnp.maximum(pick(start) - tile * tm, 0)
    hi = jnp.minimum(pick(end) - tile * tm, tm)
    changed = jnp.concatenate([jnp.zeros((1,), I32), (expert[1:] != expert[:-1]).astype(I32)])
    slot = jnp.cumsum(changed) % 2
    later = (k[None, :] < it_end[-1]) & (expert[None, :] > expert[:, None])
    nxt = jnp.min(jnp.where(later, expert[None, :], ne), axis=1)
    nxt = jnp.where(nxt < ne, nxt, -1)
    return start.astype(I32), (tile.astype(I32), expert.astype(I32), lo.astype(I32), hi.astype(I32),
                               slot.astype(I32), nxt.astype(I32), it_end[-1:].astype(I32))


def _positions_kernel(start_ref, eid_ref, rank_ref, pos_ref):
    eid = eid_ref[...]
    pos = rank_ref[...]
    for e in range(N_EXPERTS):
        pos = pos + jnp.where(eid == e, start_ref[e], 0)
    pos_ref[...] = pos


def _positions(expert_start, eid, rank):
    full = pl.BlockSpec(eid.shape, lambda i, st: (0, 0))
    return pl.pallas_call(
        _positions_kernel,
        out_shape=jax.ShapeDtypeStruct(eid.shape, I32),
        grid_spec=pltpu.PrefetchScalarGridSpec(num_scalar_prefetch=1, grid=(1,), in_specs=[full, full],
                                               out_specs=full),
        compiler_params=_params("arbitrary"),
        name="positions",
    )(expert_start, eid, rank)


def _moe(x1, x1p, w_router_t, bias_col, w_gate, w_up, w_down, layer, ws_gate, ws_up, ws_down, ln_g, ln_b, alpha):
    t = x1.shape[0]
    tm = EXPERT_ROW_TILE
    rank, eid, wgt, cnt = _router(x1, w_router_t, bias_col)
    expert_start, items = _expert_work_items(cnt[:, 0], t * TOP_K, tm)
    pos = _positions(expert_start, eid, rank)
    xs = _dispatch(pos, x1p)
    ys = _experts(items, xs, w_gate, w_up, w_down, layer, tm)
    yk = _gather_rows(ys, pos.reshape(-1)).reshape(TOP_K, t, -1)
    return _combine(x1, wgt.T, yk, ws_gate, ws_up, ws_down, ln_g, ln_b, alpha)


def kernel(x, positions, w_in, lb_logits, hg_norm_g, w_hg_proj, w_att_proj, w_out, ln1_g, ln1_b,
           w_router, router_bias, w_e_gate, w_e_up, w_e_down, w_s_gate, w_s_up, w_s_down, ln2_g, ln2_b):
    batch, seq, d = x.shape
    depth = w_in.shape[0]
    t = batch * seq
    alpha = float((2 * depth) ** 0.25)
    hg_width = HG_HEADS * HG_DIM
    att_col0 = 4 * hg_width
    gate_col0 = att_col0 + 3 * len(ATT_GROUPS) * ATT_HEADS * ATT_DIM

    p = jax.nn.softmax(lb_logits.astype(F32), axis=0)
    cs = jnp.cumsum(p, axis=0)
    lower = cs - cs[:1]
    log1m_lower = jnp.log1p(-lower)

    cos, sin = _rope_tables(positions)
    xf = x.reshape(t, d)
    xb = xf.astype(BF16)
    in_width = w_in.shape[2]
    tn = 1280 if in_width % 1280 == 0 else 512
    for layer in range(depth):
        h = _matmul(xb, w_in, layer, min(t, 1024), tn, BF16)
        o_h = _hgrn(h, lower[layer].reshape(1, -1), log1m_lower[layer].reshape(1, -1),
                    hg_norm_g[layer].reshape(1, -1), batch, seq)
        o_a = _attention(h, cos, sin, batch, seq, att_col0)
        x1, x1p = _mix_out(xf, h, o_h, o_a, w_hg_proj[layer].astype(BF16), w_att_proj[layer].astype(BF16),
                           w_out[layer].astype(BF16), ln1_g[layer].reshape(1, d), ln1_b[layer].reshape(1, d),
                           gate_col0, alpha)
        xf, xb = _moe(x1, x1p, w_router[layer].T, router_bias[layer].reshape(-1, 1),
                      w_e_gate, w_e_up, w_e_down, layer,
                      w_s_gate[layer].astype(BF16), w_s_up[layer].astype(BF16), w_s_down[layer].astype(BF16),
                      ln2_g[layer].reshape(1, d), ln2_b[layer].reshape(1, d), alpha)
    return xf.reshape(batch, seq, d)
```

```python
import functools

import numpy as np
import jax
import jax.numpy as jnp
from jax import lax
from jax.experimental import pallas as pl
from jax.experimental.pallas import tpu as pltpu
from jax.experimental.pallas import tpu_sc as plsc

F32 = jnp.float32
BF16 = jnp.bfloat16
I32 = jnp.int32

LANES = 128
VMEM_LIMIT_BYTES = 56 * 1024 * 1024
SC_CORES = 2
SC_SUBCORES = 16
SC_WORKERS = SC_CORES * SC_SUBCORES
SC_SCATTER_ROWS = 32
SC_GATHER_ROWS = 16
SC_GATHER_BUFFERS = 4
EXPERT_ROW_TILE = 128
EXPERT_CAST_STEPS = 16

HG_HEADS = 8
HG_DIM = 128
HG_CHUNK = 64
HG_SUB = 16
ATT_GROUPS = ((128, 1), (512, 4), (2048, 16))
ATT_HEADS = 4
ATT_DIM = 128
ATT_BACK = 128
ROPE_THETA = 10000.0
N_EXPERTS = 64
N_GROUPS = 8
TOPK_GROUPS = 4
TOP_K = 8
ROUTED_SCALE = 2.5
LN_EPS = 1e-5
NORM_EPS = 1e-6

NT_DIMS = (((1,), (1,)), ((), ()))
TN_DIMS = (((0,), (0,)), ((), ()))


def _params(*sem):
    return pltpu.CompilerParams(dimension_semantics=sem, vmem_limit_bytes=VMEM_LIMIT_BYTES)


def _sigmoid(x):
    return 1.0 / (1.0 + jnp.exp(-x))


def _matmul_kernel(x_ref, w_ref, o_ref, wb_ref):
    @pl.when(pl.program_id(1) == 0)
    def _():
        wb_ref[...] = w_ref[0].astype(BF16)

    o_ref[...] = jnp.dot(x_ref[...], wb_ref[...], preferred_element_type=F32).astype(o_ref.dtype)


def _matmul(x, w, layer, tm, tn, out_dtype):
    m, k = x.shape
    n = w.shape[2]
    return pl.pallas_call(
        _matmul_kernel,
        out_shape=jax.ShapeDtypeStruct((m, n), out_dtype),
        grid=(n // tn, m // tm),
        in_specs=[pl.BlockSpec((tm, k), lambda j, i: (i, 0)),
                  pl.BlockSpec((1, k, tn), lambda j, i: (layer, 0, j))],
        out_specs=pl.BlockSpec((tm, tn), lambda j, i: (i, j)),
        scratch_shapes=[pltpu.VMEM((k, tn), BF16)],
        compiler_params=_params("parallel", "arbitrary"),
        name="in_proj",
    )(x, w)


def _rope_table_kernel(pos_ref, inv_ref, sign_ref, cos_ref, sin_ref):
    ang = pos_ref[...].astype(F32) * inv_ref[...]
    cos_ref[...] = jnp.cos(ang)
    sin_ref[...] = jnp.sin(ang) * sign_ref[...]


def _rope_tables(positions):
    t = positions.size
    half = ATT_DIM // 2
    inv_half = ROPE_THETA ** (-np.arange(half, dtype=np.float32) * np.float32(2.0) / np.float32(ATT_DIM))
    inv = jnp.asarray(np.concatenate([inv_half, inv_half]).astype(np.float32).reshape(1, ATT_DIM))
    sign = jnp.asarray(np.concatenate([-np.ones(half), np.ones(half)]).astype(np.float32).reshape(1, ATT_DIM))
    tm = min(t, 2048)
    return pl.pallas_call(
        _rope_table_kernel,
        out_shape=(jax.ShapeDtypeStruct((t, ATT_DIM), F32), jax.ShapeDtypeStruct((t, ATT_DIM), F32)),
        grid=(t // tm,),
        in_specs=[pl.BlockSpec((tm, 1), lambda i: (i, 0)),
                  pl.BlockSpec((1, ATT_DIM), lambda i: (0, 0)),
                  pl.BlockSpec((1, ATT_DIM), lambda i: (0, 0))],
        out_specs=(pl.BlockSpec((tm, ATT_DIM), lambda i: (i, 0)),
                   pl.BlockSpec((tm, ATT_DIM), lambda i: (i, 0))),
        compiler_params=_params("parallel"),
        name="rope_tables",
    )(positions.reshape(t, 1), inv, sign)


def _cumsum_rows(x):
    n = x.shape[0]
    row = lax.broadcasted_iota(I32, x.shape, 0)
    s = 1
    while s < n:
        x = x + jnp.where(row >= s, pltpu.roll(x, s, 0), 0.0)
        s *= 2
    return x


def _rows_from(cum, offsets):
    parts = []
    for o in offsets:
        if o is None:
            parts.append(jnp.zeros((HG_SUB, cum.shape[1]), F32))
        else:
            parts.append(jnp.broadcast_to(cum[o:o + 1, :], (HG_SUB, cum.shape[1])))
    return jnp.concatenate(parts, axis=0)


def _hgrn_kernel(hq_ref, hf_ref, hi_ref, hg_ref, lb_ref, l1m_ref, ng_ref, o_ref, st_ref, *, n_chunks):
    c = HG_CHUNK
    nsub = c // HG_SUB

    @pl.when(pl.program_id(2) == 0)
    def _():
        st_ref[...] = jnp.zeros_like(st_ref)

    lb = lb_ref[...]
    one_m_lb = 1.0 - lb
    log1m_lb = l1m_ref[...]
    norm_g = ng_ref[...]
    row = lax.broadcasted_iota(I32, (c, HG_DIM), 0)
    blk = row // HG_SUB
    ti = lax.broadcasted_iota(I32, (c, c), 0)
    si = lax.broadcasted_iota(I32, (c, c), 1)
    diag_mask = ((ti // HG_SUB) == (si // HG_SUB)) & (si <= ti)

    def chunk(ci, carry):
        r0 = pl.multiple_of(ci * c, c)
        z = hf_ref[pl.ds(r0, c), :].astype(F32)
        qraw = hq_ref[pl.ds(r0, c), :].astype(F32)
        v = hi_ref[pl.ds(r0, c), :]
        g = hg_ref[pl.ds(r0, c), :].astype(F32)

        e = jnp.exp(-jnp.abs(z))
        r = 1.0 / (1.0 + e)
        sig = jnp.where(z >= 0, r, e * r)
        sig_neg = jnp.where(z >= 0, e * r, r)
        log_sig = jnp.minimum(z, 0.0) - jnp.log(1.0 + e)
        log_f = jnp.maximum(jnp.log(lb + one_m_lb * sig), log1m_lb + log_sig)
        k = one_m_lb * sig_neg
        q = qraw * _sigmoid(qraw)

        cum = _cumsum_rows(log_f)
        start = _rows_from(cum, [None] + [HG_SUB * i - 1 for i in range(1, nsub)])
        mid = _rows_from(cum, [HG_SUB * i + HG_SUB // 2 - 1 for i in range(nsub)])
        end = cum[c - 1:c, :]

        qd = (q * jnp.exp(cum - mid)).astype(BF16)
        kd = (k * jnp.exp(mid - cum)).astype(BF16)
        att = jnp.where(diag_mask, lax.dot_general(qd, kd, NT_DIMS, preferred_element_type=F32), 0.0)

        qs = q * jnp.exp(cum - start)
        q_slots, k_slots = [], []
        for i in range(1, nsub):
            q_slots.append(jnp.where(blk == i, qs, 0.0).astype(BF16))
            n_rows = HG_SUB * i
            b_i = cum[n_rows - 1:n_rows, :]
            k_i = k[:n_rows] * jnp.exp(b_i - cum[:n_rows])
            k_slots.append(jnp.concatenate([k_i, jnp.zeros((c - n_rows, HG_DIM), F32)], axis=0).astype(BF16))
        att = att + lax.dot_general(jnp.concatenate(q_slots, axis=1), jnp.concatenate(k_slots, axis=1),
                                    NT_DIMS, preferred_element_type=F32)
        intra = jnp.dot(att.astype(BF16), v, preferred_element_type=F32)

        st = st_ref[...]
        inter = lax.dot_general((q * jnp.exp(cum)).astype(BF16), st.astype(BF16), NT_DIMS,
                                preferred_element_type=F32)
        k_end = (k * jnp.exp(end - cum)).astype(BF16)
        st_ref[...] = st * jnp.exp(end) + lax.dot_general(v, k_end, TN_DIMS, preferred_element_type=F32)

        o = inter + intra
        o = o * lax.rsqrt(jnp.mean(o * o, axis=-1, keepdims=True) + NORM_EPS) * norm_g
        o_ref[pl.ds(r0, c), :] = (o * _sigmoid(g)).astype(o_ref.dtype)
        return carry

    lax.fori_loop(0, n_chunks, chunk, 0, unroll=32)


def _hgrn(h, lb, log1m_lb, norm_g, batch, seq):
    t = h.shape[0]
    tb = min(seq, 2048)
    nsb = seq // tb
    width = HG_HEADS * HG_DIM

    def col(off):
        return pl.BlockSpec((tb, HG_DIM), lambda b, hh, s: (b * nsb + s, off * HG_HEADS + hh))

    vec = pl.BlockSpec((1, HG_DIM), lambda b, hh, s: (0, hh))
    return pl.pallas_call(
        functools.partial(_hgrn_kernel, n_chunks=tb // HG_CHUNK),
        out_shape=jax.ShapeDtypeStruct((t, width), BF16),
        grid=(batch, HG_HEADS, nsb),
        in_specs=[col(0), col(1), col(2), col(3), vec, vec, vec],
        out_specs=pl.BlockSpec((tb, HG_DIM), lambda b, hh, s: (b * nsb + s, hh)),
        scratch_shapes=[pltpu.VMEM((HG_DIM, HG_DIM), F32)],
        compiler_params=_params("parallel", "parallel", "arbitrary"),
        name="hgrn2",
    )(h, h, h, h, lb, log1m_lb, norm_g)


def _attn_kernel(*refs, seq):
    qkv_refs = refs[:9]
    cos_ref, sin_ref, o_ref = refs[9:12]
    qf, kf, vf, og, lg = refs[12:]
    n_groups = len(ATT_GROUPS)
    scale = ATT_DIM ** -0.5
    rb = min(seq, 256)

    def rope_rows(i, carry):
        r0 = pl.multiple_of(i * rb, rb)
        cs = cos_ref[pl.ds(r0, rb), :]
        sn = sin_ref[pl.ds(r0, rb), :]
        for gi in range(n_groups):
            xq = qkv_refs[3 * gi][pl.ds(r0, rb), :].astype(F32)
            xk = qkv_refs[3 * gi + 1][pl.ds(r0, rb), :].astype(F32)
            qf[gi, pl.ds(r0, rb), :] = (xq * cs + pltpu.roll(xq, ATT_DIM // 2, 1) * sn) * scale
            kf[gi, pl.ds(r0, rb), :] = xk * cs + pltpu.roll(xk, ATT_DIM // 2, 1) * sn
            vf[gi, pl.ds(r0, rb), :] = qkv_refs[3 * gi + 2][pl.ds(r0, rb), :].astype(F32)
        return carry

    lax.fori_loop(0, seq // rb, rope_rows, 0)

    qb = ATT_BACK
    for gi, (_, dil) in enumerate(ATT_GROUPS):
        length = seq // dil
        nk = min(2 * qb, length)
        n_qblk = length // qb

        def block(it, carry, gi=gi, dil=dil, nk=nk, n_qblk=n_qblk):
            res = it // n_qblk
            iq = it % n_qblk
            q0 = iq * qb
            k0 = jnp.maximum(q0 - qb, 0)
            if dil == 1:
                q_rows = pl.ds(pl.multiple_of(q0, qb), qb)
                k_rows = pl.ds(pl.multiple_of(k0, qb), nk)
            else:
                q_rows = pl.ds(res + dil * q0, qb, stride=dil)
                k_rows = pl.ds(res + dil * k0, nk, stride=dil)
            q = qf[gi, q_rows, :].astype(BF16)
            k = kf[gi, k_rows, :].astype(BF16)
            v = vf[gi, k_rows, :].astype(BF16)
            wk = 2 * qb
            k = jnp.concatenate([k] * (wk // nk), axis=0)
            v = jnp.concatenate([v] * (wk // nk), axis=0)
            col = lax.broadcasted_iota(I32, (qb, wk), 1)
            dist = (q0 + lax.broadcasted_iota(I32, (qb, wk), 0)) - (k0 + col)
            if wk > nk:
                dist = jnp.where(col < nk, dist, -1)
            s = lax.dot_general(q, k, NT_DIMS, preferred_element_type=F32)
            s = jnp.where((dist >= 0) & (dist <= ATT_BACK), s, -jnp.inf)
            m = jnp.max(s, axis=-1, keepdims=True)
            p = jnp.exp(s - m)
            den = jnp.sum(p, axis=-1, keepdims=True)
            o = jnp.dot(p.astype(BF16), v, preferred_element_type=F32) / den
            og[gi, q_rows, :] = o
            lg[gi, q_rows, :] = jnp.broadcast_to(m + jnp.log(den), (qb, ATT_DIM))
            return carry

        lax.fori_loop(0, dil * n_qblk, block, 0, unroll=16)

    def merge_rows(i, carry):
        r0 = pl.multiple_of(i * rb, rb)
        ls = [lg[gi, pl.ds(r0, rb), :] for gi in range(n_groups)]
        m = functools.reduce(jnp.maximum, ls)
        ws = [jnp.exp(l - m) for l in ls]
        num = sum(w * og[gi, pl.ds(r0, rb), :] for gi, w in enumerate(ws))
        o_ref[pl.ds(r0, rb), :] = (num / sum(ws)).astype(o_ref.dtype)
        return carry

    lax.fori_loop(0, seq // rb, merge_rows, 0)


def _attention(h, cos, sin, batch, seq, col0):
    t = h.shape[0]
    n_groups = len(ATT_GROUPS)
    part = n_groups * ATT_HEADS * ATT_DIM
    blk0 = col0 // ATT_DIM
    in_specs = []
    for gi in range(n_groups):
        for p in range(3):
            off = blk0 + (p * part) // ATT_DIM + gi * ATT_HEADS
            in_specs.append(pl.BlockSpec((seq, ATT_DIM), lambda b, hh, off=off: (b, off + hh)))
    tab = pl.BlockSpec((seq, ATT_DIM), lambda b, hh: (b, 0))
    in_specs += [tab, tab]
    scr = pltpu.VMEM((n_groups, seq, ATT_DIM), F32)
    return pl.pallas_call(
        functools.partial(_attn_kernel, seq=seq),
        out_shape=jax.ShapeDtypeStruct((t, ATT_HEADS * ATT_DIM), BF16),
        grid=(batch, ATT_HEADS),
        in_specs=in_specs,
        out_specs=pl.BlockSpec((seq, ATT_DIM), lambda b, hh: (b, hh)),
        scratch_shapes=[scr] * 5,
        compiler_params=_params("parallel", "parallel"),
        name="dilated_attn",
    )(*([h] * 9), cos, sin)


def _layer_norm_rows(r, g, b):
    mu = jnp.mean(r, axis=-1, keepdims=True)
    d = r - mu
    var = jnp.mean(d * d, axis=-1, keepdims=True)
    return d * lax.rsqrt(var + LN_EPS) * g + b


def _pack_halves(y):
    n = y.shape[1] // 2
    return pltpu.pack_elementwise([y[:, :n], y[:, n:]], packed_dtype=BF16)


def _unpack_halves(w):
    first = pltpu.unpack_elementwise(w, index=0, packed_dtype=BF16, unpacked_dtype=F32)
    second = pltpu.unpack_elementwise(w, index=1, packed_dtype=BF16, unpacked_dtype=F32)
    return first, second


def _mix_out_kernel(*refs, alpha, n_col):
    x_ref, oh_ref, oa_ref = refs[:3]
    gh_refs = refs[3:3 + n_col]
    ga_refs = refs[3 + n_col:3 + 2 * n_col]
    whg_ref, wap_ref, wout_ref, g_ref, b_ref, x1_ref, x1p_ref = refs[3 + 2 * n_col:]
    tn = gh_refs[0].shape[1]
    oh = oh_ref[...]
    oa = oa_ref[...]
    merged = []
    for n in range(n_col):
        cols = slice(n * tn, (n + 1) * tn)
        y_h = jnp.dot(oh, whg_ref[:, cols], preferred_element_type=F32)
        y_a = jnp.dot(oa, wap_ref[:, cols], preferred_element_type=F32)
        m = _sigmoid(gh_refs[n][...].astype(F32)) * y_h + _sigmoid(ga_refs[n][...].astype(F32)) * y_a
        merged.append(m.astype(BF16))
    mix = jnp.dot(jnp.concatenate(merged, axis=1), wout_ref[...], preferred_element_type=F32)
    y = _layer_norm_rows(alpha * x_ref[...] + mix, g_ref[...], b_ref[...])
    x1_ref[...] = y
    x1p_ref[...] = _pack_halves(y)


def _mix_out(x, h, o_h, o_a, w_hg, w_ap, w_out, ln_g, ln_b, gate_col0, alpha):
    t, d = x.shape
    tm = min(t, 256)
    tn = 512
    n_col = d // tn
    gh0 = gate_col0 // tn
    ga0 = (gate_col0 + d) // tn
    rows = lambda i: (i, 0)
    const = lambda i: (0, 0)
    resident = dict(index_map=const, pipeline_mode=pl.Buffered(1))
    gate_specs = [pl.BlockSpec((tm, tn), lambda i, c=c0 + n: (i, c)) for c0 in (gh0, ga0) for n in range(n_col)]
    return pl.pallas_call(
        functools.partial(_mix_out_kernel, alpha=alpha, n_col=n_col),
        out_shape=(jax.ShapeDtypeStruct((t, d), F32), jax.ShapeDtypeStruct((t, d // 2), I32)),
        grid=(t // tm,),
        in_specs=[pl.BlockSpec((tm, d), rows),
                  pl.BlockSpec((tm, o_h.shape[1]), rows),
                  pl.BlockSpec((tm, o_a.shape[1]), rows),
                  *gate_specs,
                  pl.BlockSpec(w_hg.shape, **resident),
                  pl.BlockSpec(w_ap.shape, **resident),
                  pl.BlockSpec(w_out.shape, **resident),
                  pl.BlockSpec((1, d), const),
                  pl.BlockSpec((1, d), const)],
        out_specs=(pl.BlockSpec((tm, d), rows), pl.BlockSpec((tm, d // 2), rows)),
        compiler_params=_params("parallel"),
        name="mix_out",
    )(x, o_h, o_a, *([h] * (2 * n_col)), w_hg, w_ap, w_out, ln_g, ln_b)


def _split_bf16(x):
    hi = x.astype(BF16)
    lo = (x - hi.astype(F32)).astype(BF16)
    return hi, lo


def _router_kernel(x_ref, wr_ref, bias_ref, tri_ref, low_ref, rank_ref, eid_ref, wgt_ref, cnt_ref,
                   carry_ref, *, tr):
    ne = N_EXPERTS
    per = ne // N_GROUPS

    @pl.when(pl.program_id(0) == 0)
    def _():
        carry_ref[...] = jnp.zeros_like(carry_ref)

    xh, xl = _split_bf16(x_ref[...])
    wh, wl = _split_bf16(wr_ref[...])
    logits = (lax.dot_general(wh, xh, NT_DIMS, preferred_element_type=F32)
              + lax.dot_general(wh, xl, NT_DIMS, preferred_element_type=F32)
              + lax.dot_general(wl, xh, NT_DIMS, preferred_element_type=F32))
    scores = _sigmoid(logits)
    sel = scores + bias_ref[...]

    grp = sel.reshape(N_GROUPS, per, tr)
    sub = lax.broadcasted_iota(I32, grp.shape, 1)
    m1 = jnp.max(grp, axis=1, keepdims=True)
    first = jnp.min(jnp.where(grp == m1, sub, per), axis=1, keepdims=True)
    m2 = jnp.max(jnp.where(sub == first, -jnp.inf, grp), axis=1, keepdims=True)
    gs = m1 + m2
    gidx = lax.broadcasted_iota(I32, gs.shape, 0)
    grank = jnp.zeros(gs.shape, I32)
    for j in range(N_GROUPS):
        other = gs[j:j + 1]
        grank += ((other > gs) | ((other == gs) & (j < gidx))).astype(I32)
    masked = jnp.where(grank < TOPK_GROUPS, grp, -jnp.inf).reshape(ne, tr)

    eidx = lax.broadcasted_iota(I32, (ne, tr), 0)
    work = masked
    picked = jnp.zeros((ne, tr), F32)
    for _ in range(TOP_K):
        top = jnp.max(work, axis=0, keepdims=True)
        first = jnp.min(jnp.where(work == top, eidx, ne), axis=0, keepdims=True)
        hit = eidx == first
        picked = jnp.where(hit, 1.0, picked)
        work = jnp.where(hit, -jnp.inf, work)
    chosen = picked > 0.0
    w = jnp.where(chosen, scores, 0.0)
    gates = w / jnp.sum(w, axis=0, keepdims=True) * ROUTED_SCALE

    chosen_b = jnp.where(chosen, 1.0, 0.0).astype(BF16)
    incl = jnp.dot(chosen_b, tri_ref[...], preferred_element_type=F32)
    carry = carry_ref[...]
    rank_in_expert = (carry + incl - 1.0).astype(I32)
    carry_new = carry + incl[:, tr - 1:tr]
    carry_ref[...] = carry_new
    cnt_ref[...] = jnp.broadcast_to(carry_new, cnt_ref.shape).astype(I32)

    slot = jnp.dot(low_ref[...], chosen_b, preferred_element_type=F32).astype(I32)
    for j in range(TOP_K):
        pick = chosen & (slot == j)
        rank_ref[pl.ds(j, 1), :] = jnp.sum(jnp.where(pick, rank_in_expert, 0), axis=0, keepdims=True)
        eid_ref[pl.ds(j, 1), :] = jnp.sum(jnp.where(pick, eidx, 0), axis=0, keepdims=True)
        wgt_ref[pl.ds(j, 1), :] = jnp.sum(jnp.where(pick, gates, 0.0), axis=0, keepdims=True)


def _router(x1, w_router_t, bias_col):
    t, d = x1.shape
    ne = N_EXPERTS
    tr = min(t, 512)
    tri = jnp.asarray(np.triu(np.ones((tr, tr), np.float32)), BF16)
    low = jnp.asarray(np.tril(np.ones((ne, ne), np.float32), -1), BF16)
    slot_shape = jax.ShapeDtypeStruct((TOP_K, t), I32)
    slot_spec = pl.BlockSpec((TOP_K, tr), lambda i: (0, i))
    return pl.pallas_call(
        functools.partial(_router_kernel, tr=tr),
        out_shape=(slot_shape, slot_shape, jax.ShapeDtypeStruct((TOP_K, t), F32),
                   jax.ShapeDtypeStruct((ne, LANES), I32)),
        grid=(t // tr,),
        in_specs=[pl.BlockSpec((tr, d), lambda i: (i, 0)),
                  pl.BlockSpec((ne, d), lambda i: (0, 0)),
                  pl.BlockSpec((ne, 1), lambda i: (0, 0)),
                  pl.BlockSpec((tr, tr), lambda i: (0, 0)),
                  pl.BlockSpec((ne, ne), lambda i: (0, 0))],
        out_specs=(slot_spec, slot_spec, slot_spec, pl.BlockSpec((ne, LANES), lambda i: (0, 0))),
        scratch_shapes=[pltpu.VMEM((ne, 1), F32)],
        compiler_params=_params("arbitrary"),
        name="router",
    )(x1, w_router_t, bias_col, tri, low)


def _sc_worker_id():
    return lax.axis_index("s") * SC_CORES + lax.axis_index("c")


def _dispatch(pos, x1p):
    t, dp = x1p.shape
    t_per_w = t // SC_WORKERS
    chunk = min(SC_SCATTER_ROWS, t_per_w // 2)
    n_chunks = t_per_w // chunk
    mesh = plsc.VectorSubcoreMesh(core_axis_name="c", subcore_axis_name="s")

    @functools.partial(
        pl.kernel, mesh=mesh,
        out_type=jax.ShapeDtypeStruct((t * TOP_K, dp), x1p.dtype),
        scratch_types=[pltpu.VMEM((n_chunks * TOP_K, chunk), I32),
                       pltpu.VMEM((2, chunk, dp), x1p.dtype),
                       pltpu.SemaphoreType.DMA((2,)),
                       pltpu.SemaphoreType.DMA((2,))],
    )
    def scatter_rows(x_hbm, pos_hbm, out_hbm, idx_v, rows_v, load_sem, scat_sem):
        wid = _sc_worker_id()
        base = wid * t_per_w
        pltpu.sync_copy(pos_hbm.at[wid], idx_v)

        def load(c, b):
            return pltpu.make_async_copy(x_hbm.at[pl.ds(base + c * chunk, chunk)], rows_v.at[b], load_sem.at[b])

        def scat(c, b, j):
            return pltpu.make_async_copy(rows_v.at[b], out_hbm.at[idx_v.at[c * TOP_K + j]], scat_sem.at[b])

        load(0, 0).start()

        @pl.loop(0, n_chunks, step=2)
        def _(c0):
            for b in range(2):
                c = c0 + b
                load(c, b).wait()
                for j in range(TOP_K):
                    scat(c, b, j).start()

                @pl.when(c + 1 < n_chunks)
                def _():
                    @pl.when(c >= 1)
                    def _():
                        for j in range(TOP_K):
                            scat(c - 1, 1 - b, j).wait()
                    load(c + 1, 1 - b).start()

        for b in range(2):
            for j in range(TOP_K):
                scat(n_chunks - 2 + b, b, j).wait()

    idx = pos.reshape(TOP_K, SC_WORKERS, n_chunks, chunk).transpose(1, 2, 0, 3)
    return scatter_rows(x1p, idx.reshape(SC_WORKERS, n_chunks * TOP_K, chunk))


def _gather_rows(table, idx):
    n_rows = idx.shape[0]
    dp = table.shape[1]
    r_per_w = n_rows // SC_WORKERS
    chunk, nbuf = SC_GATHER_ROWS, SC_GATHER_BUFFERS
    n_chunks = r_per_w // chunk
    mesh = plsc.VectorSubcoreMesh(core_axis_name="c", subcore_axis_name="s")

    @functools.partial(
        pl.kernel, mesh=mesh,
        out_type=jax.ShapeDtypeStruct((n_rows, dp), table.dtype),
        scratch_types=[pltpu.VMEM((n_chunks, chunk), I32),
                       pltpu.VMEM((nbuf, chunk, dp), table.dtype),
                       pltpu.SemaphoreType.DMA((nbuf,)),
                       pltpu.SemaphoreType.DMA((nbuf,))],
    )
    def gather_rows(table_hbm, idx_hbm, out_hbm, idx_v, rows_v, gat_sem, store_sem):
        wid = _sc_worker_id()
        base = wid * r_per_w
        pltpu.sync_copy(idx_hbm.at[wid], idx_v)

        def gather(c, b):
            return pltpu.make_async_copy(table_hbm.at[idx_v.at[c]], rows_v.at[b], gat_sem.at[b])

        def store(c, b):
            return pltpu.make_async_copy(rows_v.at[b], out_hbm.at[pl.ds(base + c * chunk, chunk)], store_sem.at[b])

        for b in range(nbuf - 1):
            gather(b, b).start()

        @pl.loop(0, n_chunks, step=nbuf)
        def _(c0):
            for b in range(nbuf):
                c = c0 + b
                gather(c, b).wait()
                store(c, b).start()
                nb = (b + nbuf - 1) % nbuf

                @pl.when(c + nbuf - 1 < n_chunks)
                def _():
                    @pl.when(c >= 1)
                    def _():
                        store(c - 1, nb).wait()
                    gather(c + nbuf - 1, nb).start()

        for b in range(nbuf):
            store(n_chunks - nbuf + b, b).wait()

    return gather_rows(table, idx.reshape(SC_WORKERS, n_chunks, chunk))


def _experts_kernel(tile_ref, exp_ref, lo_ref, hi_ref, slot_ref, next_ref, nv_ref, xs_ref, wg_hbm, wu_hbm, wd_hbm,
                    ys_ref, wgf, wuf, wdf, wsem, wgb, wub, wdb, hid_ref, ybuf_ref, *, layer):
    it = pl.program_id(0)
    nv = nv_ref[0]
    cur = jnp.minimum(it, nv - 1)
    prev = jnp.maximum(it - 1, 0)

    def weight_copies(e):
        return [pltpu.make_async_copy(src.at[layer, e], dst, wsem.at[k])
                for k, (src, dst) in enumerate(((wg_hbm, wgf), (wu_hbm, wuf), (wd_hbm, wdf)))]

    @pl.when(it == 0)
    def _():
        hid_ref[...] = jnp.zeros_like(hid_ref)
        wdb[...] = jnp.zeros_like(wdb)
        ybuf_ref[...] = jnp.zeros_like(ybuf_ref)
        for cp in weight_copies(exp_ref[0]):
            cp.start()

    @pl.when((it < nv) & ((it == 0) | (exp_ref[cur] != exp_ref[prev])))
    def _():
        for cp in weight_copies(exp_ref[cur]):
            cp.wait()
        slot = slot_ref[cur]
        rows_g = wgf.shape[0] // EXPERT_CAST_STEPS
        rows_d = wdf.shape[0] // EXPERT_CAST_STEPS

        def cast_rows(i, carry):
            rg = pl.ds(pl.multiple_of(i * rows_g, rows_g), rows_g)
            rd = pl.ds(pl.multiple_of(i * rows_d, rows_d), rows_d)
            wgb[rg, :] = wgf[rg, :].astype(BF16)
            wub[rg, :] = wuf[rg, :].astype(BF16)
            wdb[slot, rd, :] = wdf[rd, :].astype(BF16)
            return carry

        lax.fori_loop(0, EXPERT_CAST_STEPS, cast_rows, 0)

        @pl.when(next_ref[cur] >= 0)
        def _():
            for cp in weight_copies(next_ref[cur]):
                cp.start()

    tm, half = xs_ref.shape
    y = _pack_halves(jnp.dot(hid_ref[(it + 1) % 2], wdb[slot_ref[prev]], preferred_element_type=F32))
    a, b = _unpack_halves(xs_ref[...])
    x = jnp.concatenate([a.astype(BF16), b.astype(BF16)], axis=1)
    gate = jnp.dot(x, wgb[...], preferred_element_type=F32)
    up = jnp.dot(x, wub[...], preferred_element_type=F32)
    hid_ref[it % 2] = (gate * _sigmoid(gate) * up).astype(BF16)

    done = (it >= 1) & (it <= nv)
    lo = jnp.where(done, lo_ref[prev], 0)
    hi = jnp.where(done, hi_ref[prev], 0)
    row = lax.broadcasted_iota(I32, (tm, half), 0)
    merged = jnp.where((row >= lo) & (row < hi), y, ybuf_ref[...])
    ybuf_ref[...] = merged
    ys_ref[...] = merged


def _experts(items, xs, w_gate, w_up, w_down, layer, tm):
    n_rows, dp = xs.shape
    _, ne, d, ff = w_gate.shape
    n_items = items[0].shape[0]

    def cur_map(i, tl, ex, lo, hi, sl, nx, nv):
        return (tl[jnp.minimum(i, nv[0] - 1)], 0)

    def prev_map(i, tl, ex, lo, hi, sl, nx, nv):
        return (tl[jnp.clip(i - 1, 0, nv[0] - 1)], 0)

    hbm = pl.BlockSpec(memory_space=pl.ANY)
    return pl.pallas_call(
        functools.partial(_experts_kernel, layer=layer),
        out_shape=jax.ShapeDtypeStruct((n_rows, dp), I32),
        grid_spec=pltpu.PrefetchScalarGridSpec(
            num_scalar_prefetch=7,
            grid=(n_items + 1,),
            in_specs=[pl.BlockSpec((tm, dp), cur_map), hbm, hbm, hbm],
            out_specs=pl.BlockSpec((tm, dp), prev_map),
            scratch_shapes=[pltpu.VMEM((d, ff), F32), pltpu.VMEM((d, ff), F32), pltpu.VMEM((ff, d), F32),
                            pltpu.SemaphoreType.DMA((3,)),
                            pltpu.VMEM((d, ff), BF16), pltpu.VMEM((d, ff), BF16), pltpu.VMEM((2, ff, d), BF16),
                            pltpu.VMEM((2, tm, ff), BF16), pltpu.VMEM((tm, dp), I32)]),
        compiler_params=_params("arbitrary"),
        name="experts",
    )(*items, xs, w_gate, w_up, w_down)


def _combine_kernel(x_ref, wk_ref, yk_ref, wsg_ref, wsu_ref, wsd_ref, g_ref, b_ref, x2_ref, x2b_ref, *, alpha):
    x = x_ref[...]
    xb = x.astype(BF16)
    gate = jnp.dot(xb, wsg_ref[...], preferred_element_type=F32)
    up = jnp.dot(xb, wsu_ref[...], preferred_element_type=F32)
    hid = (gate * _sigmoid(gate) * up).astype(BF16)
    shared = jnp.dot(hid, wsd_ref[...], preferred_element_type=F32)

    tc, half = yk_ref.shape[1:]
    wk = wk_ref[...]
    acc_hi = jnp.zeros((tc, half), F32)
    acc_lo = jnp.zeros((tc, half), F32)
    for j in range(TOP_K):
        hi, lo = _unpack_halves(yk_ref[j])
        wj = wk[:, j:j + 1]
        acc_hi += wj * hi
        acc_lo += wj * lo
    ffn = jnp.concatenate([acc_hi, acc_lo], axis=1) + shared
    y = _layer_norm_rows(alpha * x + ffn, g_ref[...], b_ref[...])
    x2_ref[...] = y
    x2b_ref[...] = y.astype(BF16)


def _combine(x1, wk, yk, ws_gate, ws_up, ws_down, ln_g, ln_b, alpha):
    t, d = x1.shape
    dp = yk.shape[2]
    ff = ws_gate.shape[1]
    tc = min(t, 256)
    const = lambda i: (0, 0)
    rows = lambda i: (i, 0)
    return pl.pallas_call(
        functools.partial(_combine_kernel, alpha=alpha),
        out_shape=(jax.ShapeDtypeStruct((t, d), F32), jax.ShapeDtypeStruct((t, d), BF16)),
        grid=(t // tc,),
        in_specs=[pl.BlockSpec((tc, d), rows),
                  pl.BlockSpec((tc, TOP_K), rows),
                  pl.BlockSpec((TOP_K, tc, dp), lambda i: (0, i, 0)),
                  pl.BlockSpec((d, ff), const),
                  pl.BlockSpec((d, ff), const),
                  pl.BlockSpec((ff, d), const),
                  pl.BlockSpec((1, d), const),
                  pl.BlockSpec((1, d), const)],
        out_specs=(pl.BlockSpec((tc, d), rows), pl.BlockSpec((tc, d), rows)),
        compiler_params=_params("parallel"),
        name="combine",
    )(x1, wk, yk, ws_gate, ws_up, ws_down, ln_g, ln_b)


def _expert_work_items(counts, n_rows, tm):
    ne = counts.shape[0]
    end = jnp.cumsum(counts)
    start = end - counts
    first_tile = start // tm
    n_it = jnp.where(counts > 0, (end - 1) // tm - first_tile + 1, 0)
    it_end = jnp.cumsum(n_it)
    it_start = it_end - n_it
    n_items = n_rows // tm + ne
    k = jnp.arange(n_items, dtype=I32)
    expert = jnp.minimum(jnp.sum((it_end[None, :] <= k[:, None]).astype(I32), axis=1), ne - 1)
    onehot = expert[:, None] == jnp.arange(ne, dtype=I32)[None, :]
    pick = lambda v: jnp.sum(jnp.where(onehot, v[None, :], 0), axis=1)
    tile = jnp.clip(pick(first_tile) + k - pick(it_start), 0, n_rows // tm - 1)
    lo = jnp.maximum(pick(start) - tile * tm, 0)
    hi = jnp.minimum(pick(end) - tile * tm, tm)
    changed = jnp.concatenate([jnp.zeros((1,), I32), (expert[1:] != expert[:-1]).astype(I32)])
    slot = jnp.cumsum(changed) % 2
    later = (k[None, :] < it_end[-1]) & (expert[None, :] > expert[:, None])
    nxt = jnp.min(jnp.where(later, expert[None, :], ne), axis=1)
    nxt = jnp.where(nxt < ne, nxt, -1)
    return start.astype(I32), (tile.astype(I32), expert.astype(I32), lo.astype(I32), hi.astype(I32),
                               slot.astype(I32), nxt.astype(I32), it_end[-1:].astype(I32))


def _positions_kernel(start_ref, eid_ref, rank_ref, pos_ref):
    eid = eid_ref[...]
    pos = rank_ref[...]
    for e in range(N_EXPERTS):
        pos = pos + jnp.where(eid == e, start_ref[e], 0)
    pos_ref[...] = pos


def _positions(expert_start, eid, rank):
    full = pl.BlockSpec(eid.shape, lambda i, st: (0, 0))
    return pl.pallas_call(
        _positions_kernel,
        out_shape=jax.ShapeDtypeStruct(eid.shape, I32),
        grid_spec=pltpu.PrefetchScalarGridSpec(num_scalar_prefetch=1, grid=(1,), in_specs=[full, full],
                                               out_specs=full),
        compiler_params=_params("arbitrary"),
        name="positions",
    )(expert_start, eid, rank)


def _moe(x1, x1p, w_router_t, bias_col, w_gate, w_up, w_down, layer, ws_gate, ws_up, ws_down, ln_g, ln_b, alpha):
    t = x1.shape[0]
    tm = EXPERT_ROW_TILE
    rank, eid, wgt, cnt = _router(x1, w_router_t, bias_col)
    expert_start, items = _expert_work_items(cnt[:, 0], t * TOP_K, tm)
    pos = _positions(expert_start, eid, rank)
    xs = _dispatch(pos, x1p)
    ys = _experts(items, xs, w_gate, w_up, w_down, layer, tm)
    yk = _gather_rows(ys, pos.reshape(-1)).reshape(TOP_K, t, -1)
    return _combine(x1, wgt.T, yk, ws_gate, ws_up, ws_down, ln_g, ln_b, alpha)


def kernel(x, positions, w_in, lb_logits, hg_norm_g, w_hg_proj, w_att_proj, w_out, ln1_g, ln1_b,
           w_router, router_bias, w_e_gate, w_e_up, w_e_down, w_s_gate, w_s_up, w_s_down, ln2_g, ln2_b):
    batch, seq, d = x.shape
    depth = w_in.shape[0]
    t = batch * seq
    alpha = float((2 * depth) ** 0.25)
    hg_width = HG_HEADS * HG_DIM
    att_col0 = 4 * hg_width
    gate_col0 = att_col0 + 3 * len(ATT_GROUPS) * ATT_HEADS * ATT_DIM

    p = jax.nn.softmax(lb_logits.astype(F32), axis=0)
    cs = jnp.cumsum(p, axis=0)
    lower = cs - cs[:1]
    log1m_lower = jnp.log1p(-lower)

    cos, sin = _rope_tables(positions)
    xf = x.reshape(t, d)
    xb = xf.astype(BF16)
    in_width = w_in.shape[2]
    tn = 1280 if in_width % 1280 == 0 else 512
    for layer in range(depth):
        h = _matmul(xb, w_in, layer, min(t, 1024), tn, BF16)
        o_h = _hgrn(h, lower[layer].reshape(1, -1), log1m_lower[layer].reshape(1, -1),
                    hg_norm_g[layer].reshape(1, -1), batch, seq)
        o_a = _attention(h, cos, sin, batch, seq, att_col0)
        x1, x1p = _mix_out(xf, h, o_h, o_a, w_hg_proj[layer].astype(BF16), w_att_proj[layer].astype(BF16),
                           w_out[layer].astype(BF16), ln1_g[layer].reshape(1, d), ln1_b[layer].reshape(1, d),
                           gate_col0, alpha)
        xf, xb = _moe(x1, x1p, w_router[layer].T, router_bias[layer].reshape(-1, 1),
                      w_e_gate, w_e_up, w_e_down, layer,
                      w_s_gate[layer].astype(BF16), w_s_up[layer].astype(BF16), w_s_down[layer].astype(BF16),
                      ln2_g[layer].reshape(1, d), ln2_b[layer].reshape(1, d), alpha)
    return xf.reshape(batch, seq, d)
```

```python
import functools

import numpy as np
import jax
import jax.numpy as jnp
from jax import lax
from jax.experimental import pallas as pl
from jax.experimental.pallas import tpu as pltpu
from jax.experimental.pallas import tpu_sc as plsc

F32 = jnp.float32
BF16 = jnp.bfloat16
I32 = jnp.int32

LANES = 128
VMEM_LIMIT_BYTES = 56 * 1024 * 1024
SC_CORES = 2
SC_SUBCORES = 16
SC_WORKERS = SC_CORES * SC_SUBCORES
SC_SCATTER_ROWS = 32
SC_GATHER_ROWS = 16
SC_GATHER_BUFFERS = 4
EXPERT_ROW_TILE = 256
EXPERT_CAST_STEPS = 16

HG_HEADS = 8
HG_DIM = 128
HG_CHUNK = 64
HG_SUB = 16
ATT_GROUPS = ((128, 1), (512, 4), (2048, 16))
ATT_HEADS = 4
ATT_DIM = 128
ATT_BACK = 128
ROPE_THETA = 10000.0
N_EXPERTS = 64
N_GROUPS = 8
TOPK_GROUPS = 4
TOP_K = 8
ROUTED_SCALE = 2.5
LN_EPS = 1e-5
NORM_EPS = 1e-6

NT_DIMS = (((1,), (1,)), ((), ()))
TN_DIMS = (((0,), (0,)), ((), ()))


def _params(*sem):
    return pltpu.CompilerParams(dimension_semantics=sem, vmem_limit_bytes=VMEM_LIMIT_BYTES)


def _sigmoid(x):
    return 1.0 / (1.0 + jnp.exp(-x))


def _matmul_kernel(x_ref, w_ref, o_ref, wb_ref):
    @pl.when(pl.program_id(1) == 0)
    def _():
        wb_ref[...] = w_ref[0].astype(BF16)

    o_ref[...] = jnp.dot(x_ref[...].astype(BF16), wb_ref[...], preferred_element_type=F32).astype(o_ref.dtype)


def _matmul(x, w, layer, tm, tn, out_dtype):
    m, k = x.shape
    n = w.shape[2]
    return pl.pallas_call(
        _matmul_kernel,
        out_shape=jax.ShapeDtypeStruct((m, n), out_dtype),
        grid=(n // tn, m // tm),
        in_specs=[pl.BlockSpec((tm, k), lambda j, i: (i, 0)),
                  pl.BlockSpec((1, k, tn), lambda j, i: (layer, 0, j))],
        out_specs=pl.BlockSpec((tm, tn), lambda j, i: (i, j)),
        scratch_shapes=[pltpu.VMEM((k, tn), BF16)],
        compiler_params=_params("parallel", "arbitrary"),
        name="in_proj",
    )(x, w)


def _rope_table_kernel(pos_ref, inv_ref, sign_ref, cos_ref, sin_ref):
    ang = pos_ref[...].astype(F32) * inv_ref[...]
    cos_ref[...] = jnp.cos(ang)
    sin_ref[...] = jnp.sin(ang) * sign_ref[...]


def _rope_tables(positions):
    t = positions.size
    half = ATT_DIM // 2
    inv_half = ROPE_THETA ** (-np.arange(half, dtype=np.float32) * np.float32(2.0) / np.float32(ATT_DIM))
    inv = jnp.asarray(np.concatenate([inv_half, inv_half]).astype(np.float32).reshape(1, ATT_DIM))
    sign = jnp.asarray(np.concatenate([-np.ones(half), np.ones(half)]).astype(np.float32).reshape(1, ATT_DIM))
    tm = min(t, 2048)
    return pl.pallas_call(
        _rope_table_kernel,
        out_shape=(jax.ShapeDtypeStruct((t, ATT_DIM), F32), jax.ShapeDtypeStruct((t, ATT_DIM), F32)),
        grid=(t // tm,),
        in_specs=[pl.BlockSpec((tm, 1), lambda i: (i, 0)),
                  pl.BlockSpec((1, ATT_DIM), lambda i: (0, 0)),
                  pl.BlockSpec((1, ATT_DIM), lambda i: (0, 0))],
        out_specs=(pl.BlockSpec((tm, ATT_DIM), lambda i: (i, 0)),
                   pl.BlockSpec((tm, ATT_DIM), lambda i: (i, 0))),
        compiler_params=_params("parallel"),
        name="rope_tables",
    )(positions.reshape(t, 1), inv, sign)


def _cumsum_rows(x):
    n = x.shape[0]
    row = lax.broadcasted_iota(I32, x.shape, 0)
    s = 1
    while s < n:
        x = x + jnp.where(row >= s, pltpu.roll(x, s, 0), 0.0)
        s *= 2
    return x


def _rows_from(cum, offsets):
    parts = []
    for o in offsets:
        if o is None:
            parts.append(jnp.zeros((HG_SUB, cum.shape[1]), F32))
        else:
            parts.append(jnp.broadcast_to(cum[o:o + 1, :], (HG_SUB, cum.shape[1])))
    return jnp.concatenate(parts, axis=0)


def _hgrn_kernel(hq_ref, hf_ref, hi_ref, hg_ref, lb_ref, l1m_ref, ng_ref, o_ref, st_ref, *, n_chunks):
    c = HG_CHUNK
    nsub = c // HG_SUB

    @pl.when(pl.program_id(2) == 0)
    def _():
        st_ref[...] = jnp.zeros_like(st_ref)

    lb = lb_ref[...]
    one_m_lb = 1.0 - lb
    log1m_lb = l1m_ref[...]
    norm_g = ng_ref[...]
    row = lax.broadcasted_iota(I32, (c, HG_DIM), 0)
    blk = row // HG_SUB
    ti = lax.broadcasted_iota(I32, (c, c), 0)
    si = lax.broadcasted_iota(I32, (c, c), 1)
    diag_mask = ((ti // HG_SUB) == (si // HG_SUB)) & (si <= ti)

    def chunk(ci, carry):
        r0 = pl.multiple_of(ci * c, c)
        z = hf_ref[pl.ds(r0, c), :].astype(F32)
        qraw = hq_ref[pl.ds(r0, c), :].astype(F32)
        v = hi_ref[pl.ds(r0, c), :]
        g = hg_ref[pl.ds(r0, c), :].astype(F32)

        e = jnp.exp(-jnp.abs(z))
        r = 1.0 / (1.0 + e)
        sig = jnp.where(z >= 0, r, e * r)
        sig_neg = jnp.where(z >= 0, e * r, r)
        log_sig = jnp.minimum(z, 0.0) - jnp.log(1.0 + e)
        log_f = jnp.maximum(jnp.log(lb + one_m_lb * sig), log1m_lb + log_sig)
        k = one_m_lb * sig_neg
        q = qraw * _sigmoid(qraw)

        cum = _cumsum_rows(log_f)
        start = _rows_from(cum, [None] + [HG_SUB * i - 1 for i in range(1, nsub)])
        mid = _rows_from(cum, [HG_SUB * i + HG_SUB // 2 - 1 for i in range(nsub)])
        end = cum[c - 1:c, :]

        qd = (q * jnp.exp(cum - mid)).astype(BF16)
        kd = (k * jnp.exp(mid - cum)).astype(BF16)
        att = jnp.where(diag_mask, lax.dot_general(qd, kd, NT_DIMS, preferred_element_type=F32), 0.0)

        qs = q * jnp.exp(cum - start)
        q_slots, k_slots = [], []
        for i in range(1, nsub):
            q_slots.append(jnp.where(blk == i, qs, 0.0).astype(BF16))
            n_rows = HG_SUB * i
            b_i = cum[n_rows - 1:n_rows, :]
            k_i = k[:n_rows] * jnp.exp(b_i - cum[:n_rows])
            k_slots.append(jnp.concatenate([k_i, jnp.zeros((c - n_rows, HG_DIM), F32)], axis=0).astype(BF16))
        att = att + lax.dot_general(jnp.concatenate(q_slots, axis=1), jnp.concatenate(k_slots, axis=1),
                                    NT_DIMS, preferred_element_type=F32)
        intra = jnp.dot(att.astype(BF16), v, preferred_element_type=F32)

        st = st_ref[...]
        inter = lax.dot_general((q * jnp.exp(cum)).astype(BF16), st.astype(BF16), NT_DIMS,
                                preferred_element_type=F32)
        k_end = (k * jnp.exp(end - cum)).astype(BF16)
        st_ref[...] = st * jnp.exp(end) + lax.dot_general(v, k_end, TN_DIMS, preferred_element_type=F32)

        o = inter + intra
        o = o * lax.rsqrt(jnp.mean(o * o, axis=-1, keepdims=True) + NORM_EPS) * norm_g
        o_ref[pl.ds(r0, c), :] = (o * _sigmoid(g)).astype(o_ref.dtype)
        return carry

    lax.fori_loop(0, n_chunks, chunk, 0, unroll=32)


def _hgrn(h, lb, log1m_lb, norm_g, batch, seq):
    t = h.shape[0]
    tb = min(seq, 2048)
    nsb = seq // tb
    width = HG_HEADS * HG_DIM

    def col(off):
        return pl.BlockSpec((tb, HG_DIM), lambda b, hh, s: (b * nsb + s, off * HG_HEADS + hh))

    vec = pl.BlockSpec((1, HG_DIM), lambda b, hh, s: (0, hh))
    return pl.pallas_call(
        functools.partial(_hgrn_kernel, n_chunks=tb // HG_CHUNK),
        out_shape=jax.ShapeDtypeStruct((t, width), BF16),
        grid=(batch, HG_HEADS, nsb),
        in_specs=[col(0), col(1), col(2), col(3), vec, vec, vec],
        out_specs=pl.BlockSpec((tb, HG_DIM), lambda b, hh, s: (b * nsb + s, hh)),
        scratch_shapes=[pltpu.VMEM((HG_DIM, HG_DIM), F32)],
        compiler_params=_params("parallel", "parallel", "arbitrary"),
        name="hgrn2",
    )(h, h, h, h, lb, log1m_lb, norm_g)


def _attn_kernel(*refs, seq):
    qkv_refs = refs[:9]
    cos_ref, sin_ref, o_ref = refs[9:12]
    qf, kf, vf, og, lg = refs[12:]
    n_groups = len(ATT_GROUPS)
    scale = ATT_DIM ** -0.5
    rb = min(seq, 256)

    def rope_rows(i, carry):
        r0 = pl.multiple_of(i * rb, rb)
        cs = cos_ref[pl.ds(r0, rb), :]
        sn = sin_ref[pl.ds(r0, rb), :]
        for gi in range(n_groups):
            xq = qkv_refs[3 * gi][pl.ds(r0, rb), :].astype(F32)
            xk = qkv_refs[3 * gi + 1][pl.ds(r0, rb), :].astype(F32)
            qf[gi, pl.ds(r0, rb), :] = (xq * cs + pltpu.roll(xq, ATT_DIM // 2, 1) * sn) * scale
            kf[gi, pl.ds(r0, rb), :] = xk * cs + pltpu.roll(xk, ATT_DIM // 2, 1) * sn
            vf[gi, pl.ds(r0, rb), :] = qkv_refs[3 * gi + 2][pl.ds(r0, rb), :].astype(F32)
        return carry

    lax.fori_loop(0, seq // rb, rope_rows, 0)

    qb = ATT_BACK
    for gi, (_, dil) in enumerate(ATT_GROUPS):
        length = seq // dil
        nk = min(2 * qb, length)
        n_qblk = length // qb

        def block(it, carry, gi=gi, dil=dil, nk=nk, n_qblk=n_qblk):
            res = it // n_qblk
            iq = it % n_qblk
            q0 = iq * qb
            k0 = jnp.maximum(q0 - qb, 0)
            if dil == 1:
                q_rows = pl.ds(pl.multiple_of(q0, qb), qb)
                k_rows = pl.ds(pl.multiple_of(k0, qb), nk)
            else:
                q_rows = pl.ds(res + dil * q0, qb, stride=dil)
                k_rows = pl.ds(res + dil * k0, nk, stride=dil)
            q = qf[gi, q_rows, :].astype(BF16)
            k = kf[gi, k_rows, :].astype(BF16)
            v = vf[gi, k_rows, :].astype(BF16)
            wk = 2 * qb
            k = jnp.concatenate([k] * (wk // nk), axis=0)
            v = jnp.concatenate([v] * (wk // nk), axis=0)
            col = lax.broadcasted_iota(I32, (qb, wk), 1)
            dist = (q0 + lax.broadcasted_iota(I32, (qb, wk), 0)) - (k0 + col)
            if wk > nk:
                dist = jnp.where(col < nk, dist, -1)
            s = lax.dot_general(q, k, NT_DIMS, preferred_element_type=F32)
            s = jnp.where((dist >= 0) & (dist <= ATT_BACK), s, -jnp.inf)
            m = jnp.max(s, axis=-1, keepdims=True)
            p = jnp.exp(s - m)
            den = jnp.sum(p, axis=-1, keepdims=True)
            o = jnp.dot(p.astype(BF16), v, preferred_element_type=F32) / den
            og[gi, q_rows, :] = o
            lg[gi, q_rows, :] = jnp.broadcast_to(m + jnp.log(den), (qb, ATT_DIM))
            return carry

        lax.fori_loop(0, dil * n_qblk, block, 0, unroll=16)

    def merge_rows(i, carry):
        r0 = pl.multiple_of(i * rb, rb)
        ls = [lg[gi, pl.ds(r0, rb), :] for gi in range(n_groups)]
        m = functools.reduce(jnp.maximum, ls)
        ws = [jnp.exp(l - m) for l in ls]
        num = sum(w * og[gi, pl.ds(r0, rb), :] for gi, w in enumerate(ws))
        o_ref[pl.ds(r0, rb), :] = (num / sum(ws)).astype(o_ref.dtype)
        return carry

    lax.fori_loop(0, seq // rb, merge_rows, 0)


def _attention(h, cos, sin, batch, seq, col0):
    t = h.shape[0]
    n_groups = len(ATT_GROUPS)
    part = n_groups * ATT_HEADS * ATT_DIM
    blk0 = col0 // ATT_DIM
    in_specs = []
    for gi in range(n_groups):
        for p in range(3):
            off = blk0 + (p * part) // ATT_DIM + gi * ATT_HEADS
            in_specs.append(pl.BlockSpec((seq, ATT_DIM), lambda b, hh, off=off: (b, off + hh)))
    tab = pl.BlockSpec((seq, ATT_DIM), lambda b, hh: (b, 0))
    in_specs += [tab, tab]
    scr = pltpu.VMEM((n_groups, seq, ATT_DIM), F32)
    return pl.pallas_call(
        functools.partial(_attn_kernel, seq=seq),
        out_shape=jax.ShapeDtypeStruct((t, ATT_HEADS * ATT_DIM), BF16),
        grid=(batch, ATT_HEADS),
        in_specs=in_specs,
        out_specs=pl.BlockSpec((seq, ATT_DIM), lambda b, hh: (b, hh)),
        scratch_shapes=[scr] * 5,
        compiler_params=_params("parallel", "parallel"),
        name="dilated_attn",
    )(*([h] * 9), cos, sin)


def _layer_norm_rows(r, g, b):
    mu = jnp.mean(r, axis=-1, keepdims=True)
    d = r - mu
    var = jnp.mean(d * d, axis=-1, keepdims=True)
    return d * lax.rsqrt(var + LN_EPS) * g + b


def _pack_halves(y):
    n = y.shape[1] // 2
    return pltpu.pack_elementwise([y[:, :n], y[:, n:]], packed_dtype=BF16)


def _unpack_halves(w):
    first = pltpu.unpack_elementwise(w, index=0, packed_dtype=BF16, unpacked_dtype=F32)
    second = pltpu.unpack_elementwise(w, index=1, packed_dtype=BF16, unpacked_dtype=F32)
    return first, second


def _mix_out_kernel(*refs, alpha, n_col):
    x_ref, oh_ref, oa_ref = refs[:3]
    gh_refs = refs[3:3 + n_col]
    ga_refs = refs[3 + n_col:3 + 2 * n_col]
    whg_ref, wap_ref, wout_ref, g_ref, b_ref, x1_ref, x1p_ref = refs[3 + 2 * n_col:]
    tn = gh_refs[0].shape[1]
    oh = oh_ref[...]
    oa = oa_ref[...]
    merged = []
    for n in range(n_col):
        cols = slice(n * tn, (n + 1) * tn)
        y_h = jnp.dot(oh, whg_ref[:, cols], preferred_element_type=F32)
        y_a = jnp.dot(oa, wap_ref[:, cols], preferred_element_type=F32)
        m = _sigmoid(gh_refs[n][...].astype(F32)) * y_h + _sigmoid(ga_refs[n][...].astype(F32)) * y_a
        merged.append(m.astype(BF16))
    mix = jnp.dot(jnp.concatenate(merged, axis=1), wout_ref[...], preferred_element_type=F32)
    y = _layer_norm_rows(alpha * x_ref[...] + mix, g_ref[...], b_ref[...])
    x1_ref[...] = y
    x1p_ref[...] = _pack_halves(y)


def _mix_out(x, h, o_h, o_a, w_hg, w_ap, w_out, ln_g, ln_b, gate_col0, alpha):
    t, d = x.shape
    tm = min(t, 256)
    tn = 512
    n_col = d // tn
    gh0 = gate_col0 // tn
    ga0 = (gate_col0 + d) // tn
    rows = lambda i: (i, 0)
    const = lambda i: (0, 0)
    resident = dict(index_map=const, pipeline_mode=pl.Buffered(1))
    gate_specs = [pl.BlockSpec((tm, tn), lambda i, c=c0 + n: (i, c)) for c0 in (gh0, ga0) for n in range(n_col)]
    return pl.pallas_call(
        functools.partial(_mix_out_kernel, alpha=alpha, n_col=n_col),
        out_shape=(jax.ShapeDtypeStruct((t, d), F32), jax.ShapeDtypeStruct((t, d // 2), I32)),
        grid=(t // tm,),
        in_specs=[pl.BlockSpec((tm, d), rows),
                  pl.BlockSpec((tm, o_h.shape[1]), rows),
                  pl.BlockSpec((tm, o_a.shape[1]), rows),
                  *gate_specs,
                  pl.BlockSpec(w_hg.shape, **resident),
                  pl.BlockSpec(w_ap.shape, **resident),
                  pl.BlockSpec(w_out.shape, **resident),
                  pl.BlockSpec((1, d), const),
                  pl.BlockSpec((1, d), const)],
        out_specs=(pl.BlockSpec((tm, d), rows), pl.BlockSpec((tm, d // 2), rows)),
        compiler_params=_params("parallel"),
        name="mix_out",
    )(x, o_h, o_a, *([h] * (2 * n_col)), w_hg, w_ap, w_out, ln_g, ln_b)


def _split_bf16(x):
    hi = x.astype(BF16)
    lo = (x - hi.astype(F32)).astype(BF16)
    return hi, lo


def _router_kernel(x_ref, wr_ref, bias_ref, tri_ref, low_ref, rank_ref, eid_ref, wgt_ref, cnt_ref,
                   carry_ref, *, tr):
    ne = N_EXPERTS
    per = ne // N_GROUPS

    @pl.when(pl.program_id(0) == 0)
    def _():
        carry_ref[...] = jnp.zeros_like(carry_ref)

    xh, xl = _split_bf16(x_ref[...])
    wh, wl = _split_bf16(wr_ref[...])
    logits = (lax.dot_general(wh, xh, NT_DIMS, preferred_element_type=F32)
              + lax.dot_general(wh, xl, NT_DIMS, preferred_element_type=F32)
              + lax.dot_general(wl, xh, NT_DIMS, preferred_element_type=F32))
    scores = _sigmoid(logits)
    sel = scores + bias_ref[...]

    grp = sel.reshape(N_GROUPS, per, tr)
    sub = lax.broadcasted_iota(I32, grp.shape, 1)
    m1 = jnp.max(grp, axis=1, keepdims=True)
    first = jnp.min(jnp.where(grp == m1, sub, per), axis=1, keepdims=True)
    m2 = jnp.max(jnp.where(sub == first, -jnp.inf, grp), axis=1, keepdims=True)
    gs = m1 + m2
    gidx = lax.broadcasted_iota(I32, gs.shape, 0)
    grank = jnp.zeros(gs.shape, I32)
    for j in range(N_GROUPS):
        other = gs[j:j + 1]
        grank += ((other > gs) | ((other == gs) & (j < gidx))).astype(I32)
    masked = jnp.where(grank < TOPK_GROUPS, grp, -jnp.inf).reshape(ne, tr)

    eidx = lax.broadcasted_iota(I32, (ne, tr), 0)
    work = masked
    picked = jnp.zeros((ne, tr), F32)
    for _ in range(TOP_K):
        top = jnp.max(work, axis=0, keepdims=True)
        first = jnp.min(jnp.where(work == top, eidx, ne), axis=0, keepdims=True)
        hit = eidx == first
        picked = jnp.where(hit, 1.0, picked)
        work = jnp.where(hit, -jnp.inf, work)
    chosen = picked > 0.0
    w = jnp.where(chosen, scores, 0.0)
    gates = w / jnp.sum(w, axis=0, keepdims=True) * ROUTED_SCALE

    chosen_b = jnp.where(chosen, 1.0, 0.0).astype(BF16)
    incl = jnp.dot(chosen_b, tri_ref[...], preferred_element_type=F32)
    carry = carry_ref[...]
    rank_in_expert = (carry + incl - 1.0).astype(I32)
    carry_new = carry + incl[:, tr - 1:tr]
    carry_ref[...] = carry_new
    cnt_ref[...] = jnp.broadcast_to(carry_new, cnt_ref.shape).astype(I32)

    slot = jnp.dot(low_ref[...], chosen_b, preferred_element_type=F32).astype(I32)
    for j in range(TOP_K):
        pick = chosen & (slot == j)
        rank_ref[pl.ds(j, 1), :] = jnp.sum(jnp.where(pick, rank_in_expert, 0), axis=0, keepdims=True)
        eid_ref[pl.ds(j, 1), :] = jnp.sum(jnp.where(pick, eidx, 0), axis=0, keepdims=True)
        wgt_ref[pl.ds(j, 1), :] = jnp.sum(jnp.where(pick, gates, 0.0), axis=0, keepdims=True)


def _router(x1, w_router_t, bias_col):
    t, d = x1.shape
    ne = N_EXPERTS
    tr = min(t, 512)
    tri = jnp.asarray(np.triu(np.ones((tr, tr), np.float32)), BF16)
    low = jnp.asarray(np.tril(np.ones((ne, ne), np.float32), -1), BF16)
    slot_shape = jax.ShapeDtypeStruct((TOP_K, t), I32)
    slot_spec = pl.BlockSpec((TOP_K, tr), lambda i: (0, i))
    return pl.pallas_call(
        functools.partial(_router_kernel, tr=tr),
        out_shape=(slot_shape, slot_shape, jax.ShapeDtypeStruct((TOP_K, t), F32),
                   jax.ShapeDtypeStruct((ne, LANES), I32)),
        grid=(t // tr,),
        in_specs=[pl.BlockSpec((tr, d), lambda i: (i, 0)),
                  pl.BlockSpec((ne, d), lambda i: (0, 0)),
                  pl.BlockSpec((ne, 1), lambda i: (0, 0)),
                  pl.BlockSpec((tr, tr), lambda i: (0, 0)),
                  pl.BlockSpec((ne, ne), lambda i: (0, 0))],
        out_specs=(slot_spec, slot_spec, slot_spec, pl.BlockSpec((ne, LANES), lambda i: (0, 0))),
        scratch_shapes=[pltpu.VMEM((ne, 1), F32)],
        compiler_params=_params("arbitrary"),
        name="router",
    )(x1, w_router_t, bias_col, tri, low)


def _sc_worker_id():
    return lax.axis_index("s") * SC_CORES + lax.axis_index("c")


def _dispatch(pos, x1p):
    t, dp = x1p.shape
    t_per_w = t // SC_WORKERS
    chunk = min(SC_SCATTER_ROWS, t_per_w // 2)
    n_chunks = t_per_w // chunk
    mesh = plsc.VectorSubcoreMesh(core_axis_name="c", subcore_axis_name="s")

    @functools.partial(
        pl.kernel, mesh=mesh,
        out_type=jax.ShapeDtypeStruct((t * TOP_K, dp), x1p.dtype),
        scratch_types=[pltpu.VMEM((n_chunks * TOP_K, chunk), I32),
                       pltpu.VMEM((2, chunk, dp), x1p.dtype),
                       pltpu.SemaphoreType.DMA((2,)),
                       pltpu.SemaphoreType.DMA((2,))],
    )
    def scatter_rows(x_hbm, pos_hbm, out_hbm, idx_v, rows_v, load_sem, scat_sem):
        wid = _sc_worker_id()
        base = wid * t_per_w
        pltpu.sync_copy(pos_hbm.at[wid], idx_v)

        def load(c, b):
            return pltpu.make_async_copy(x_hbm.at[pl.ds(base + c * chunk, chunk)], rows_v.at[b], load_sem.at[b])

        def scat(c, b, j):
            return pltpu.make_async_copy(rows_v.at[b], out_hbm.at[idx_v.at[c * TOP_K + j]], scat_sem.at[b])

        load(0, 0).start()

        @pl.loop(0, n_chunks, step=2)
        def _(c0):
            for b in range(2):
                c = c0 + b
                load(c, b).wait()
                for j in range(TOP_K):
                    scat(c, b, j).start()

                @pl.when(c + 1 < n_chunks)
                def _():
                    @pl.when(c >= 1)
                    def _():
                        for j in range(TOP_K):
                            scat(c - 1, 1 - b, j).wait()
                    load(c + 1, 1 - b).start()

        for b in range(2):
            for j in range(TOP_K):
                scat(n_chunks - 2 + b, b, j).wait()

    idx = pos.reshape(TOP_K, SC_WORKERS, n_chunks, chunk).transpose(1, 2, 0, 3)
    return scatter_rows(x1p, idx.reshape(SC_WORKERS, n_chunks * TOP_K, chunk))


def _gather_rows(table, idx):
    n_rows = idx.shape[0]
    dp = table.shape[1]
    r_per_w = n_rows // SC_WORKERS
    chunk, nbuf = SC_GATHER_ROWS, SC_GATHER_BUFFERS
    n_chunks = r_per_w // chunk
    mesh = plsc.VectorSubcoreMesh(core_axis_name="c", subcore_axis_name="s")

    @functools.partial(
        pl.kernel, mesh=mesh,
        out_type=jax.ShapeDtypeStruct((n_rows, dp), table.dtype),
        scratch_types=[pltpu.VMEM((n_chunks, chunk), I32),
                       pltpu.VMEM((nbuf, chunk, dp), table.dtype),
                       pltpu.SemaphoreType.DMA((nbuf,)),
                       pltpu.SemaphoreType.DMA((nbuf,))],
    )
    def gather_rows(table_hbm, idx_hbm, out_hbm, idx_v, rows_v, gat_sem, store_sem):
        wid = _sc_worker_id()
        base = wid * r_per_w
        pltpu.sync_copy(idx_hbm.at[wid], idx_v)

        def gather(c, b):
            return pltpu.make_async_copy(table_hbm.at[idx_v.at[c]], rows_v.at[b], gat_sem.at[b])

        def store(c, b):
            return pltpu.make_async_copy(rows_v.at[b], out_hbm.at[pl.ds(base + c * chunk, chunk)], store_sem.at[b])

        for b in range(nbuf - 1):
            gather(b, b).start()

        @pl.loop(0, n_chunks, step=nbuf)
        def _(c0):
            for b in range(nbuf):
                c = c0 + b
                gather(c, b).wait()
                store(c, b).start()
                nb = (b + nbuf - 1) % nbuf

                @pl.when(c + nbuf - 1 < n_chunks)
                def _():
                    @pl.when(c >= 1)
                    def _():
                        store(c - 1, nb).wait()
                    gather(c + nbuf - 1, nb).start()

        for b in range(nbuf):
            store(n_chunks - nbuf + b, b).wait()

    return gather_rows(table, idx.reshape(SC_WORKERS, n_chunks, chunk))


def _experts_kernel(tile_ref, exp_ref, lo_ref, hi_ref, slot_ref, next_ref, nv_ref, xs_ref, wg_hbm, wu_hbm, wd_hbm,
                    ys_ref, wgf, wuf, wdf, wsem, wgb, wub, wdb, hid_ref, ybuf_ref, *, layer):
    it = pl.program_id(0)
    nv = nv_ref[0]
    cur = jnp.minimum(it, nv - 1)
    prev = jnp.maximum(it - 1, 0)

    def weight_copies(e):
        return [pltpu.make_async_copy(src.at[layer, e], dst, wsem.at[k])
                for k, (src, dst) in enumerate(((wg_hbm, wgf), (wu_hbm, wuf), (wd_hbm, wdf)))]

    @pl.when(it == 0)
    def _():
        hid_ref[...] = jnp.zeros_like(hid_ref)
        wdb[...] = jnp.zeros_like(wdb)
        ybuf_ref[...] = jnp.zeros_like(ybuf_ref)
        for cp in weight_copies(exp_ref[0]):
            cp.start()

    @pl.when((it < nv) & ((it == 0) | (exp_ref[cur] != exp_ref[prev])))
    def _():
        for cp in weight_copies(exp_ref[cur]):
            cp.wait()
        slot = slot_ref[cur]
        rows_g = wgf.shape[0] // EXPERT_CAST_STEPS
        rows_d = wdf.shape[0] // EXPERT_CAST_STEPS

        def cast_rows(i, carry):
            rg = pl.ds(pl.multiple_of(i * rows_g, rows_g), rows_g)
            rd = pl.ds(pl.multiple_of(i * rows_d, rows_d), rows_d)
            wgb[rg, :] = wgf[rg, :].astype(BF16)
            wub[rg, :] = wuf[rg, :].astype(BF16)
            wdb[slot, rd, :] = wdf[rd, :].astype(BF16)
            return carry

        lax.fori_loop(0, EXPERT_CAST_STEPS, cast_rows, 0)

        @pl.when(next_ref[cur] >= 0)
        def _():
            for cp in weight_copies(next_ref[cur]):
                cp.start()

    tm, half = xs_ref.shape
    y = _pack_halves(jnp.dot(hid_ref[(it + 1) % 2], wdb[slot_ref[prev]], preferred_element_type=F32))
    a, b = _unpack_halves(xs_ref[...])
    x = jnp.concatenate([a.astype(BF16), b.astype(BF16)], axis=1)
    gate = jnp.dot(x, wgb[...], preferred_element_type=F32)
    up = jnp.dot(x, wub[...], preferred_element_type=F32)
    hid_ref[it % 2] = (gate * _sigmoid(gate) * up).astype(BF16)

    done = (it >= 1) & (it <= nv)
    lo = jnp.where(done, lo_ref[prev], 0)
    hi = jnp.where(done, hi_ref[prev], 0)
    row = lax.broadcasted_iota(I32, (tm, half), 0)
    merged = jnp.where((row >= lo) & (row < hi), y, ybuf_ref[...])
    ybuf_ref[...] = merged
    ys_ref[...] = merged


def _experts(items, xs, w_gate, w_up, w_down, layer, tm):
    n_rows, dp = xs.shape
    _, ne, d, ff = w_gate.shape
    n_items = items[0].shape[0]

    def cur_map(i, tl, ex, lo, hi, sl, nx, nv):
        return (tl[jnp.minimum(i, nv[0] - 1)], 0)

    def prev_map(i, tl, ex, lo, hi, sl, nx, nv):
        return (tl[jnp.clip(i - 1, 0, nv[0] - 1)], 0)

    hbm = pl.BlockSpec(memory_space=pl.ANY)
    return pl.pallas_call(
        functools.partial(_experts_kernel, layer=layer),
        out_shape=jax.ShapeDtypeStruct((n_rows, dp), I32),
        grid_spec=pltpu.PrefetchScalarGridSpec(
            num_scalar_prefetch=7,
            grid=(n_items + 1,),
            in_specs=[pl.BlockSpec((tm, dp), cur_map), hbm, hbm, hbm],
            out_specs=pl.BlockSpec((tm, dp), prev_map),
            scratch_shapes=[pltpu.VMEM((d, ff), F32), pltpu.VMEM((d, ff), F32), pltpu.VMEM((ff, d), F32),
                            pltpu.SemaphoreType.DMA((3,)),
                            pltpu.VMEM((d, ff), BF16), pltpu.VMEM((d, ff), BF16), pltpu.VMEM((2, ff, d), BF16),
                            pltpu.VMEM((2, tm, ff), BF16), pltpu.VMEM((tm, dp), I32)]),
        compiler_params=_params("arbitrary"),
        name="experts",
    )(*items, xs, w_gate, w_up, w_down)


def _combine_kernel(x_ref, wk_ref, yk_ref, wsg_ref, wsu_ref, wsd_ref, g_ref, b_ref, x2_ref, *, alpha):
    x = x_ref[...]
    xb = x.astype(BF16)
    gate = jnp.dot(xb, wsg_ref[...], preferred_element_type=F32)
    up = jnp.dot(xb, wsu_ref[...], preferred_element_type=F32)
    hid = (gate * _sigmoid(gate) * up).astype(BF16)
    shared = jnp.dot(hid, wsd_ref[...], preferred_element_type=F32)

    tc, half = yk_ref.shape[1:]
    wk = wk_ref[...]
    acc_hi = jnp.zeros((tc, half), F32)
    acc_lo = jnp.zeros((tc, half), F32)
    for j in range(TOP_K):
        hi, lo = _unpack_halves(yk_ref[j])
        wj = wk[:, j:j + 1]
        acc_hi += wj * hi
        acc_lo += wj * lo
    ffn = jnp.concatenate([acc_hi, acc_lo], axis=1) + shared
    y = _layer_norm_rows(alpha * x + ffn, g_ref[...], b_ref[...])
    x2_ref[...] = y


def _combine(x1, wk, yk, ws_gate, ws_up, ws_down, ln_g, ln_b, alpha):
    t, d = x1.shape
    dp = yk.shape[2]
    ff = ws_gate.shape[1]
    tc = min(t, 256)
    const = lambda i: (0, 0)
    rows = lambda i: (i, 0)
    return pl.pallas_call(
        functools.partial(_combine_kernel, alpha=alpha),
        out_shape=jax.ShapeDtypeStruct((t, d), F32),
        grid=(t // tc,),
        in_specs=[pl.BlockSpec((tc, d), rows),
                  pl.BlockSpec((tc, TOP_K), rows),
                  pl.BlockSpec((TOP_K, tc, dp), lambda i: (0, i, 0)),
                  pl.BlockSpec((d, ff), const),
                  pl.BlockSpec((d, ff), const),
                  pl.BlockSpec((ff, d), const),
                  pl.BlockSpec((1, d), const),
                  pl.BlockSpec((1, d), const)],
        out_specs=pl.BlockSpec((tc, d), rows),
        compiler_params=_params("parallel"),
        name="combine",
    )(x1, wk, yk, ws_gate, ws_up, ws_down, ln_g, ln_b)


def _expert_work_items(counts, n_rows, tm):
    ne = counts.shape[0]
    end = jnp.cumsum(counts)
    start = end - counts
    first_tile = start // tm
    n_it = jnp.where(counts > 0, (end - 1) // tm - first_tile + 1, 0)
    it_end = jnp.cumsum(n_it)
    it_start = it_end - n_it
    n_items = n_rows // tm + ne
    k = jnp.arange(n_items, dtype=I32)
    expert = jnp.minimum(jnp.sum((it_end[None, :] <= k[:, None]).astype(I32), axis=1), ne - 1)
    onehot = expert[:, None] == jnp.arange(ne, dtype=I32)[None, :]
    pick = lambda v: jnp.sum(jnp.where(onehot, v[None, :], 0), axis=1)
    tile = jnp.clip(pick(first_tile) + k - pick(it_start), 0, n_rows // tm - 1)
    lo = jnp.maximum(pick(start) - tile * tm, 0)
    hi = jnp.minimum(pick(end) - tile * tm, tm)
    changed = jnp.concatenate([jnp.zeros((1,), I32), (expert[1:] != expert[:-1]).astype(I32)])
    slot = jnp.cumsum(changed) % 2
    later = (k[None, :] < it_end[-1]) & (expert[None, :] > expert[:, None])
    nxt = jnp.min(jnp.where(later, expert[None, :], ne), axis=1)
    nxt = jnp.where(nxt < ne, nxt, -1)
    return start.astype(I32), (tile.astype(I32), expert.astype(I32), lo.astype(I32), hi.astype(I32),
                               slot.astype(I32), nxt.astype(I32), it_end[-1:].astype(I32))


def _positions_kernel(start_ref, eid_ref, rank_ref, pos_ref):
    eid = eid_ref[...]
    pos = rank_ref[...]
    for e in range(N_EXPERTS):
        pos = pos + jnp.where(eid == e, start_ref[e], 0)
    pos_ref[...] = pos


def _positions(expert_start, eid, rank):
    full = pl.BlockSpec(eid.shape, lambda i, st: (0, 0))
    return pl.pallas_call(
        _positions_kernel,
        out_shape=jax.ShapeDtypeStruct(eid.shape, I32),
        grid_spec=pltpu.PrefetchScalarGridSpec(num_scalar_prefetch=1, grid=(1,), in_specs=[full, full],
                                               out_specs=full),
        compiler_params=_params("arbitrary"),
        name="positions",
    )(expert_start, eid, rank)


def _moe(x1, x1p, w_router_t, bias_col, w_gate, w_up, w_down, layer, ws_gate, ws_up, ws_down, ln_g, ln_b, alpha):
    t = x1.shape[0]
    tm = EXPERT_ROW_TILE
    rank, eid, wgt, cnt = _router(x1, w_router_t, bias_col)
    expert_start, items = _expert_work_items(cnt[:, 0], t * TOP_K, tm)
    pos = _positions(expert_start, eid, rank)
    xs = _dispatch(pos, x1p)
    ys = _experts(items, xs, w_gate, w_up, w_down, layer, tm)
    yk = _gather_rows(ys, pos.reshape(-1)).reshape(TOP_K, t, -1)
    return _combine(x1, wgt.T, yk, ws_gate, ws_up, ws_down, ln_g, ln_b, alpha)


def kernel(x, positions, w_in, lb_logits, hg_norm_g, w_hg_proj, w_att_proj, w_out, ln1_g, ln1_b,
           w_router, router_bias, w_e_gate, w_e_up, w_e_down, w_s_gate, w_s_up, w_s_down, ln2_g, ln2_b):
    batch, seq, d = x.shape
    depth = w_in.shape[0]
    t = batch * seq
    alpha = float((2 * depth) ** 0.25)
    hg_width = HG_HEADS * HG_DIM
    att_col0 = 4 * hg_width
    gate_col0 = att_col0 + 3 * len(ATT_GROUPS) * ATT_HEADS * ATT_DIM

    p = jax.nn.softmax(lb_logits.astype(F32), axis=0)
    cs = jnp.cumsum(p, axis=0)
    lower = cs - cs[:1]
    log1m_lower = jnp.log1p(-lower)

    cos, sin = _rope_tables(positions)
    xf = x.reshape(t, d)
    in_width = w_in.shape[2]
    tn = 1280 if in_width % 1280 == 0 else 512
    for layer in range(depth):
        h = _matmul(xf, w_in, layer, min(t, 1024), tn, BF16)
        o_h = _hgrn(h, lower[layer].reshape(1, -1), log1m_lower[layer].reshape(1, -1),
                    hg_norm_g[layer].reshape(1, -1), batch, seq)
        o_a = _attention(h, cos, sin, batch, seq, att_col0)
        x1, x1p = _mix_out(xf, h, o_h, o_a, w_hg_proj[layer].astype(BF16), w_att_proj[layer].astype(BF16),
                           w_out[layer].astype(BF16), ln1_g[layer].reshape(1, d), ln1_b[layer].reshape(1, d),
                           gate_col0, alpha)
        xf = _moe(x1, x1p, w_router[layer].T, router_bias[layer].reshape(-1, 1),
                  w_e_gate, w_e_up, w_e_down, layer,
                  w_s_gate[layer].astype(BF16), w_s_up[layer].astype(BF16), w_s_down[layer].astype(BF16),
                  ln2_g[layer].reshape(1, d), ln2_b[layer].reshape(1, d), alpha)
    return xf.reshape(batch, seq, d)
```

```python
import functools

import numpy as np
import jax
import jax.numpy as jnp
from jax import lax
from jax.experimental import pallas as pl
from jax.experimental.pallas import tpu as pltpu
from jax.experimental.pallas import tpu_sc as plsc

F32 = jnp.float32
BF16 = jnp.bfloat16
I32 = jnp.int32

LANES = 128
VMEM_LIMIT_BYTES = 56 * 1024 * 1024
SC_CORES = 2
SC_SUBCORES = 16
SC_WORKERS = SC_CORES * SC_SUBCORES
SC_SCATTER_ROWS = 32
SC_GATHER_ROWS = 16
SC_GATHER_BUFFERS = 4
EXPERT_ROW_TILE = 256
EXPERT_CAST_STEPS = 16

HG_HEADS = 8
HG_DIM = 128
HG_CHUNK = 64
HG_SUB = 16
ATT_GROUPS = ((128, 1), (512, 4), (2048, 16))
ATT_HEADS = 4
ATT_DIM = 128
ATT_BACK = 128
ROPE_THETA = 10000.0
N_EXPERTS = 64
N_GROUPS = 8
TOPK_GROUPS = 4
TOP_K = 8
ROUTED_SCALE = 2.5
LN_EPS = 1e-5
NORM_EPS = 1e-6

NT_DIMS = (((1,), (1,)), ((), ()))
TN_DIMS = (((0,), (0,)), ((), ()))


def _params(*sem, fuse_inputs=None):
    return pltpu.CompilerParams(dimension_semantics=sem, vmem_limit_bytes=VMEM_LIMIT_BYTES,
                                allow_input_fusion=fuse_inputs)


def _sigmoid(x):
    return 1.0 / (1.0 + jnp.exp(-x))


def _matmul_kernel(x_ref, w_ref, o_ref, wb_ref):
    @pl.when(pl.program_id(1) == 0)
    def _():
        wb_ref[...] = w_ref[0].astype(BF16)

    o_ref[...] = jnp.dot(x_ref[...].astype(BF16), wb_ref[...], preferred_element_type=F32).astype(o_ref.dtype)


def _matmul(x, w, layer, tm, tn, out_dtype):
    m, k = x.shape
    n = w.shape[2]
    return pl.pallas_call(
        _matmul_kernel,
        out_shape=jax.ShapeDtypeStruct((m, n), out_dtype),
        grid=(n // tn, m // tm),
        in_specs=[pl.BlockSpec((tm, k), lambda j, i: (i, 0)),
                  pl.BlockSpec((1, k, tn), lambda j, i: (layer, 0, j))],
        out_specs=pl.BlockSpec((tm, tn), lambda j, i: (i, j)),
        scratch_shapes=[pltpu.VMEM((k, tn), BF16)],
        compiler_params=_params("parallel", "arbitrary"),
        name="in_proj",
    )(x, w)


def _rope_table_kernel(pos_ref, inv_ref, sign_ref, cos_ref, sin_ref):
    ang = pos_ref[...].astype(F32) * inv_ref[...]
    cos_ref[...] = jnp.cos(ang)
    sin_ref[...] = jnp.sin(ang) * sign_ref[...]


def _rope_tables(positions):
    t = positions.size
    half = ATT_DIM // 2
    inv_half = ROPE_THETA ** (-np.arange(half, dtype=np.float32) * np.float32(2.0) / np.float32(ATT_DIM))
    inv = jnp.asarray(np.concatenate([inv_half, inv_half]).astype(np.float32).reshape(1, ATT_DIM))
    sign = jnp.asarray(np.concatenate([-np.ones(half), np.ones(half)]).astype(np.float32).reshape(1, ATT_DIM))
    tm = min(t, 2048)
    return pl.pallas_call(
        _rope_table_kernel,
        out_shape=(jax.ShapeDtypeStruct((t, ATT_DIM), F32), jax.ShapeDtypeStruct((t, ATT_DIM), F32)),
        grid=(t // tm,),
        in_specs=[pl.BlockSpec((tm, 1), lambda i: (i, 0)),
                  pl.BlockSpec((1, ATT_DIM), lambda i: (0, 0)),
                  pl.BlockSpec((1, ATT_DIM), lambda i: (0, 0))],
        out_specs=(pl.BlockSpec((tm, ATT_DIM), lambda i: (i, 0)),
                   pl.BlockSpec((tm, ATT_DIM), lambda i: (i, 0))),
        compiler_params=_params("parallel"),
        name="rope_tables",
    )(positions.reshape(t, 1), inv, sign)


def _cumsum_rows(x):
    n = x.shape[0]
    row = lax.broadcasted_iota(I32, x.shape, 0)
    s = 1
    while s < n:
        x = x + jnp.where(row >= s, pltpu.roll(x, s, 0), 0.0)
        s *= 2
    return x


def _rows_from(cum, offsets):
    parts = []
    for o in offsets:
        if o is None:
            parts.append(jnp.zeros((HG_SUB, cum.shape[1]), F32))
        else:
            parts.append(jnp.broadcast_to(cum[o:o + 1, :], (HG_SUB, cum.shape[1])))
    return jnp.concatenate(parts, axis=0)


def _hgrn_kernel(hq_ref, hf_ref, hi_ref, hg_ref, lb_ref, l1m_ref, ng_ref, o_ref, st_ref, *, n_chunks):
    c = HG_CHUNK
    nsub = c // HG_SUB

    @pl.when(pl.program_id(2) == 0)
    def _():
        st_ref[...] = jnp.zeros_like(st_ref)

    lb = lb_ref[...]
    one_m_lb = 1.0 - lb
    log1m_lb = l1m_ref[...]
    norm_g = ng_ref[...]
    row = lax.broadcasted_iota(I32, (c, HG_DIM), 0)
    blk = row // HG_SUB
    ti = lax.broadcasted_iota(I32, (c, c), 0)
    si = lax.broadcasted_iota(I32, (c, c), 1)
    diag_mask = ((ti // HG_SUB) == (si // HG_SUB)) & (si <= ti)

    def chunk(ci, carry):
        r0 = pl.multiple_of(ci * c, c)
        z = hf_ref[pl.ds(r0, c), :].astype(F32)
        qraw = hq_ref[pl.ds(r0, c), :].astype(F32)
        v = hi_ref[pl.ds(r0, c), :]
        g = hg_ref[pl.ds(r0, c), :].astype(F32)

        e = jnp.exp(-jnp.abs(z))
        r = 1.0 / (1.0 + e)
        sig = jnp.where(z >= 0, r, e * r)
        sig_neg = jnp.where(z >= 0, e * r, r)
        log_sig = jnp.minimum(z, 0.0) - jnp.log(1.0 + e)
        log_f = jnp.maximum(jnp.log(lb + one_m_lb * sig), log1m_lb + log_sig)
        k = one_m_lb * sig_neg
        q = qraw * _sigmoid(qraw)

        cum = _cumsum_rows(log_f)
        start = _rows_from(cum, [None] + [HG_SUB * i - 1 for i in range(1, nsub)])
        mid = _rows_from(cum, [HG_SUB * i + HG_SUB // 2 - 1 for i in range(nsub)])
        end = cum[c - 1:c, :]

        qd = (q * jnp.exp(cum - mid)).astype(BF16)
        kd = (k * jnp.exp(mid - cum)).astype(BF16)
        att = jnp.where(diag_mask, lax.dot_general(qd, kd, NT_DIMS, preferred_element_type=F32), 0.0)

        qs = q * jnp.exp(cum - start)
        q_slots, k_slots = [], []
        for i in range(1, nsub):
            q_slots.append(jnp.where(blk == i, qs, 0.0).astype(BF16))
            n_rows = HG_SUB * i
            b_i = cum[n_rows - 1:n_rows, :]
            k_i = k[:n_rows] * jnp.exp(b_i - cum[:n_rows])
            k_slots.append(jnp.concatenate([k_i, jnp.zeros((c - n_rows, HG_DIM), F32)], axis=0).astype(BF16))
        att = att + lax.dot_general(jnp.concatenate(q_slots, axis=1), jnp.concatenate(k_slots, axis=1),
                                    NT_DIMS, preferred_element_type=F32)
        intra = jnp.dot(att.astype(BF16), v, preferred_element_type=F32)

        st = st_ref[...]
        inter = lax.dot_general((q * jnp.exp(cum)).astype(BF16), st.astype(BF16), NT_DIMS,
                                preferred_element_type=F32)
        k_end = (k * jnp.exp(end - cum)).astype(BF16)
        st_ref[...] = st * jnp.exp(end) + lax.dot_general(v, k_end, TN_DIMS, preferred_element_type=F32)

        o = inter + intra
        o = o * lax.rsqrt(jnp.mean(o * o, axis=-1, keepdims=True) + NORM_EPS) * norm_g
        o_ref[pl.ds(r0, c), :] = (o * _sigmoid(g)).astype(o_ref.dtype)
        return carry

    lax.fori_loop(0, n_chunks, chunk, 0, unroll=32)


def _hgrn(h, lb, log1m_lb, norm_g, batch, seq):
    t = h.shape[0]
    tb = min(seq, 2048)
    nsb = seq // tb
    width = HG_HEADS * HG_DIM

    def col(off):
        return pl.BlockSpec((tb, HG_DIM), lambda b, hh, s: (b * nsb + s, off * HG_HEADS + hh))

    vec = pl.BlockSpec((1, HG_DIM), lambda b, hh, s: (0, hh))
    return pl.pallas_call(
        functools.partial(_hgrn_kernel, n_chunks=tb // HG_CHUNK),
        out_shape=jax.ShapeDtypeStruct((t, width), BF16),
        grid=(batch, HG_HEADS, nsb),
        in_specs=[col(0), col(1), col(2), col(3), vec, vec, vec],
        out_specs=pl.BlockSpec((tb, HG_DIM), lambda b, hh, s: (b * nsb + s, hh)),
        scratch_shapes=[pltpu.VMEM((HG_DIM, HG_DIM), F32)],
        compiler_params=_params("parallel", "parallel", "arbitrary"),
        name="hgrn2",
    )(h, h, h, h, lb, log1m_lb, norm_g)


def _attn_kernel(*refs, seq):
    qkv_refs = refs[:9]
    cos_ref, sin_ref, o_ref = refs[9:12]
    qf, kf, vf, og, lg = refs[12:]
    n_groups = len(ATT_GROUPS)
    scale = ATT_DIM ** -0.5
    rb = min(seq, 256)

    def rope_rows(i, carry):
        r0 = pl.multiple_of(i * rb, rb)
        cs = cos_ref[pl.ds(r0, rb), :]
        sn = sin_ref[pl.ds(r0, rb), :]
        for gi in range(n_groups):
            xq = qkv_refs[3 * gi][pl.ds(r0, rb), :].astype(F32)
            xk = qkv_refs[3 * gi + 1][pl.ds(r0, rb), :].astype(F32)
            qf[gi, pl.ds(r0, rb), :] = (xq * cs + pltpu.roll(xq, ATT_DIM // 2, 1) * sn) * scale
            kf[gi, pl.ds(r0, rb), :] = xk * cs + pltpu.roll(xk, ATT_DIM // 2, 1) * sn
            vf[gi, pl.ds(r0, rb), :] = qkv_refs[3 * gi + 2][pl.ds(r0, rb), :].astype(F32)
        return carry

    lax.fori_loop(0, seq // rb, rope_rows, 0)

    qb = ATT_BACK
    for gi, (_, dil) in enumerate(ATT_GROUPS):
        length = seq // dil
        nk = min(2 * qb, length)
        n_qblk = length // qb

        def block(it, carry, gi=gi, dil=dil, nk=nk, n_qblk=n_qblk):
            res = it // n_qblk
            iq = it % n_qblk
            q0 = iq * qb
            k0 = jnp.maximum(q0 - qb, 0)
            if dil == 1:
                q_rows = pl.ds(pl.multiple_of(q0, qb), qb)
                k_rows = pl.ds(pl.multiple_of(k0, qb), nk)
            else:
                q_rows = pl.ds(res + dil * q0, qb, stride=dil)
                k_rows = pl.ds(res + dil * k0, nk, stride=dil)
            q = qf[gi, q_rows, :].astype(BF16)
            k = kf[gi, k_rows, :].astype(BF16)
            v = vf[gi, k_rows, :].astype(BF16)
            wk = 2 * qb
            k = jnp.concatenate([k] * (wk // nk), axis=0)
            v = jnp.concatenate([v] * (wk // nk), axis=0)
            col = lax.broadcasted_iota(I32, (qb, wk), 1)
            dist = (q0 + lax.broadcasted_iota(I32, (qb, wk), 0)) - (k0 + col)
            if wk > nk:
                dist = jnp.where(col < nk, dist, -1)
            s = lax.dot_general(q, k, NT_DIMS, preferred_element_type=F32)
            s = jnp.where((dist >= 0) & (dist <= ATT_BACK), s, -jnp.inf)
            m = jnp.max(s, axis=-1, keepdims=True)
            p = jnp.exp(s - m)
            den = jnp.sum(p, axis=-1, keepdims=True)
            o = jnp.dot(p.astype(BF16), v, preferred_element_type=F32) / den
            og[gi, q_rows, :] = o
            lg[gi, q_rows, :] = jnp.broadcast_to(m + jnp.log(den), (qb, ATT_DIM))
            return carry

        lax.fori_loop(0, dil * n_qblk, block, 0, unroll=16)

    def merge_rows(i, carry):
        r0 = pl.multiple_of(i * rb, rb)
        ls = [lg[gi, pl.ds(r0, rb), :] for gi in range(n_groups)]
        m = functools.reduce(jnp.maximum, ls)
        ws = [jnp.exp(l - m) for l in ls]
        num = sum(w * og[gi, pl.ds(r0, rb), :] for gi, w in enumerate(ws))
        o_ref[pl.ds(r0, rb), :] = (num / sum(ws)).astype(o_ref.dtype)
        return carry

    lax.fori_loop(0, seq // rb, merge_rows, 0)


def _attention(h, cos, sin, batch, seq, col0):
    t = h.shape[0]
    n_groups = len(ATT_GROUPS)
    part = n_groups * ATT_HEADS * ATT_DIM
    blk0 = col0 // ATT_DIM
    in_specs = []
    for gi in range(n_groups):
        for p in range(3):
            off = blk0 + (p * part) // ATT_DIM + gi * ATT_HEADS
            in_specs.append(pl.BlockSpec((seq, ATT_DIM), lambda b, hh, off=off: (b, off + hh)))
    tab = pl.BlockSpec((seq, ATT_DIM), lambda b, hh: (b, 0))
    in_specs += [tab, tab]
    scr = pltpu.VMEM((n_groups, seq, ATT_DIM), F32)
    return pl.pallas_call(
        functools.partial(_attn_kernel, seq=seq),
        out_shape=jax.ShapeDtypeStruct((t, ATT_HEADS * ATT_DIM), BF16),
        grid=(batch, ATT_HEADS),
        in_specs=in_specs,
        out_specs=pl.BlockSpec((seq, ATT_DIM), lambda b, hh: (b, hh)),
        scratch_shapes=[scr] * 5,
        compiler_params=_params("parallel", "parallel"),
        name="dilated_attn",
    )(*([h] * 9), cos, sin)


def _layer_norm_rows(r, g, b):
    mu = jnp.mean(r, axis=-1, keepdims=True)
    d = r - mu
    var = jnp.mean(d * d, axis=-1, keepdims=True)
    return d * lax.rsqrt(var + LN_EPS) * g + b


def _pack_halves(y):
    n = y.shape[1] // 2
    return pltpu.pack_elementwise([y[:, :n], y[:, n:]], packed_dtype=BF16)


def _unpack_halves(w):
    first = pltpu.unpack_elementwise(w, index=0, packed_dtype=BF16, unpacked_dtype=F32)
    second = pltpu.unpack_elementwise(w, index=1, packed_dtype=BF16, unpacked_dtype=F32)
    return first, second


def _mix_out_kernel(*refs, alpha, n_col):
    x_ref, oh_ref, oa_ref = refs[:3]
    gh_refs = refs[3:3 + n_col]
    ga_refs = refs[3 + n_col:3 + 2 * n_col]
    whg_ref, wap_ref, wout_ref, g_ref, b_ref, x1_ref, x1p_ref = refs[3 + 2 * n_col:]
    tn = gh_refs[0].shape[1]
    oh = oh_ref[...]
    oa = oa_ref[...]
    merged = []
    for n in range(n_col):
        cols = slice(n * tn, (n + 1) * tn)
        y_h = jnp.dot(oh, whg_ref[:, cols], preferred_element_type=F32)
        y_a = jnp.dot(oa, wap_ref[:, cols], preferred_element_type=F32)
        m = _sigmoid(gh_refs[n][...].astype(F32)) * y_h + _sigmoid(ga_refs[n][...].astype(F32)) * y_a
        merged.append(m.astype(BF16))
    mix = jnp.dot(jnp.concatenate(merged, axis=1), wout_ref[...], preferred_element_type=F32)
    y = _layer_norm_rows(alpha * x_ref[...] + mix, g_ref[...], b_ref[...])
    x1_ref[...] = y
    x1p_ref[...] = _pack_halves(y)


def _mix_out(x, h, o_h, o_a, w_hg, w_ap, w_out, ln_g, ln_b, gate_col0, alpha):
    t, d = x.shape
    tm = min(t, 256)
    tn = 512
    n_col = d // tn
    gh0 = gate_col0 // tn
    ga0 = (gate_col0 + d) // tn
    rows = lambda i: (i, 0)
    const = lambda i: (0, 0)
    resident = dict(index_map=const, pipeline_mode=pl.Buffered(1))
    gate_specs = [pl.BlockSpec((tm, tn), lambda i, c=c0 + n: (i, c)) for c0 in (gh0, ga0) for n in range(n_col)]
    return pl.pallas_call(
        functools.partial(_mix_out_kernel, alpha=alpha, n_col=n_col),
        out_shape=(jax.ShapeDtypeStruct((t, d), F32), jax.ShapeDtypeStruct((t, d // 2), I32)),
        grid=(t // tm,),
        in_specs=[pl.BlockSpec((tm, d), rows),
                  pl.BlockSpec((tm, o_h.shape[1]), rows),
                  pl.BlockSpec((tm, o_a.shape[1]), rows),
                  *gate_specs,
                  pl.BlockSpec(w_hg.shape, **resident),
                  pl.BlockSpec(w_ap.shape, **resident),
                  pl.BlockSpec(w_out.shape, **resident),
                  pl.BlockSpec((1, d), const),
                  pl.BlockSpec((1, d), const)],
        out_specs=(pl.BlockSpec((tm, d), rows), pl.BlockSpec((tm, d // 2), rows)),
        compiler_params=_params("parallel", fuse_inputs=[False] * (3 + 2 * n_col) + [True] * 3 + [False] * 2),
        name="mix_out",
    )(x, o_h, o_a, *([h] * (2 * n_col)), w_hg, w_ap, w_out, ln_g, ln_b)


def _split_bf16(x):
    hi = x.astype(BF16)
    lo = (x - hi.astype(F32)).astype(BF16)
    return hi, lo


def _router_kernel(x_ref, wr_ref, bias_ref, tri_ref, low_ref, rank_ref, eid_ref, wgt_ref, cnt_ref,
                   carry_ref, *, tr):
    ne = N_EXPERTS
    per = ne // N_GROUPS

    @pl.when(pl.program_id(0) == 0)
    def _():
        carry_ref[...] = jnp.zeros_like(carry_ref)

    xh, xl = _split_bf16(x_ref[...])
    wh, wl = _split_bf16(wr_ref[...])
    logits = (lax.dot_general(wh, xh, NT_DIMS, preferred_element_type=F32)
              + lax.dot_general(wh, xl, NT_DIMS, preferred_element_type=F32)
              + lax.dot_general(wl, xh, NT_DIMS, preferred_element_type=F32))
    scores = _sigmoid(logits)
    sel = scores + bias_ref[...]

    grp = sel.reshape(N_GROUPS, per, tr)
    sub = lax.broadcasted_iota(I32, grp.shape, 1)
    m1 = jnp.max(grp, axis=1, keepdims=True)
    first = jnp.min(jnp.where(grp == m1, sub, per), axis=1, keepdims=True)
    m2 = jnp.max(jnp.where(sub == first, -jnp.inf, grp), axis=1, keepdims=True)
    gs = m1 + m2
    gidx = lax.broadcasted_iota(I32, gs.shape, 0)
    grank = jnp.zeros(gs.shape, I32)
    for j in range(N_GROUPS):
        other = gs[j:j + 1]
        grank += ((other > gs) | ((other == gs) & (j < gidx))).astype(I32)
    masked = jnp.where(grank < TOPK_GROUPS, grp, -jnp.inf).reshape(ne, tr)

    eidx = lax.broadcasted_iota(I32, (ne, tr), 0)
    work = masked
    picked = jnp.zeros((ne, tr), F32)
    for _ in range(TOP_K):
        top = jnp.max(work, axis=0, keepdims=True)
        first = jnp.min(jnp.where(work == top, eidx, ne), axis=0, keepdims=True)
        hit = eidx == first
        picked = jnp.where(hit, 1.0, picked)
        work = jnp.where(hit, -jnp.inf, work)
    chosen = picked > 0.0
    w = jnp.where(chosen, scores, 0.0)
    gates = w / jnp.sum(w, axis=0, keepdims=True) * ROUTED_SCALE

    chosen_b = jnp.where(chosen, 1.0, 0.0).astype(BF16)
    incl = jnp.dot(chosen_b, tri_ref[...], preferred_element_type=F32)
    carry = carry_ref[...]
    rank_in_expert = (carry + incl - 1.0).astype(I32)
    carry_new = carry + incl[:, tr - 1:tr]
    carry_ref[...] = carry_new
    cnt_ref[...] = jnp.broadcast_to(carry_new, cnt_ref.shape).astype(I32)

    slot = jnp.dot(low_ref[...], chosen_b, preferred_element_type=F32).astype(I32)
    for j in range(TOP_K):
        pick = chosen & (slot == j)
        rank_ref[pl.ds(j, 1), :] = jnp.sum(jnp.where(pick, rank_in_expert, 0), axis=0, keepdims=True)
        eid_ref[pl.ds(j, 1), :] = jnp.sum(jnp.where(pick, eidx, 0), axis=0, keepdims=True)
        wgt_ref[pl.ds(j, 1), :] = jnp.sum(jnp.where(pick, gates, 0.0), axis=0, keepdims=True)


def _router(x1, w_router_t, bias_col):
    t, d = x1.shape
    ne = N_EXPERTS
    tr = min(t, 512)
    tri = jnp.asarray(np.triu(np.ones((tr, tr), np.float32)), BF16)
    low = jnp.asarray(np.tril(np.ones((ne, ne), np.float32), -1), BF16)
    slot_shape = jax.ShapeDtypeStruct((TOP_K, t), I32)
    slot_spec = pl.BlockSpec((TOP_K, tr), lambda i: (0, i))
    return pl.pallas_call(
        functools.partial(_router_kernel, tr=tr),
        out_shape=(slot_shape, slot_shape, jax.ShapeDtypeStruct((TOP_K, t), F32),
                   jax.ShapeDtypeStruct((ne, LANES), I32)),
        grid=(t // tr,),
        in_specs=[pl.BlockSpec((tr, d), lambda i: (i, 0)),
                  pl.BlockSpec((ne, d), lambda i: (0, 0)),
                  pl.BlockSpec((ne, 1), lambda i: (0, 0)),
                  pl.BlockSpec((tr, tr), lambda i: (0, 0)),
                  pl.BlockSpec((ne, ne), lambda i: (0, 0))],
        out_specs=(slot_spec, slot_spec, slot_spec, pl.BlockSpec((ne, LANES), lambda i: (0, 0))),
        scratch_shapes=[pltpu.VMEM((ne, 1), F32)],
        compiler_params=_params("arbitrary"),
        name="router",
    )(x1, w_router_t, bias_col, tri, low)


def _sc_worker_id():
    return lax.axis_index("s") * SC_CORES + lax.axis_index("c")


def _dispatch(pos, x1p):
    t, dp = x1p.shape
    t_per_w = t // SC_WORKERS
    chunk = min(SC_SCATTER_ROWS, t_per_w // 2)
    n_chunks = t_per_w // chunk
    mesh = plsc.VectorSubcoreMesh(core_axis_name="c", subcore_axis_name="s")

    @functools.partial(
        pl.kernel, mesh=mesh,
        out_type=jax.ShapeDtypeStruct((t * TOP_K, dp), x1p.dtype),
        scratch_types=[pltpu.VMEM((n_chunks * TOP_K, chunk), I32),
                       pltpu.VMEM((2, chunk, dp), x1p.dtype),
                       pltpu.SemaphoreType.DMA((2,)),
                       pltpu.SemaphoreType.DMA((2,))],
    )
    def scatter_rows(x_hbm, pos_hbm, out_hbm, idx_v, rows_v, load_sem, scat_sem):
        wid = _sc_worker_id()
        base = wid * t_per_w
        pltpu.sync_copy(pos_hbm.at[wid], idx_v)

        def load(c, b):
            return pltpu.make_async_copy(x_hbm.at[pl.ds(base + c * chunk, chunk)], rows_v.at[b], load_sem.at[b])

        def scat(c, b, j):
            return pltpu.make_async_copy(rows_v.at[b], out_hbm.at[idx_v.at[c * TOP_K + j]], scat_sem.at[b])

        load(0, 0).start()

        @pl.loop(0, n_chunks, step=2)
        def _(c0):
            for b in range(2):
                c = c0 + b
                load(c, b).wait()
                for j in range(TOP_K):
                    scat(c, b, j).start()

                @pl.when(c + 1 < n_chunks)
                def _():
                    @pl.when(c >= 1)
                    def _():
                        for j in range(TOP_K):
                            scat(c - 1, 1 - b, j).wait()
                    load(c + 1, 1 - b).start()

        for b in range(2):
            for j in range(TOP_K):
                scat(n_chunks - 2 + b, b, j).wait()

    idx = pos.reshape(TOP_K, SC_WORKERS, n_chunks, chunk).transpose(1, 2, 0, 3)
    return scatter_rows(x1p, idx.reshape(SC_WORKERS, n_chunks * TOP_K, chunk))


def _gather_rows(table, idx):
    n_rows = idx.shape[0]
    dp = table.shape[1]
    r_per_w = n_rows // SC_WORKERS
    chunk, nbuf = SC_GATHER_ROWS, SC_GATHER_BUFFERS
    n_chunks = r_per_w // chunk
    mesh = plsc.VectorSubcoreMesh(core_axis_name="c", subcore_axis_name="s")

    @functools.partial(
        pl.kernel, mesh=mesh,
        out_type=jax.ShapeDtypeStruct((n_rows, dp), table.dtype),
        scratch_types=[pltpu.VMEM((n_chunks, chunk), I32),
                       pltpu.VMEM((nbuf, chunk, dp), table.dtype),
                       pltpu.SemaphoreType.DMA((nbuf,)),
                       pltpu.SemaphoreType.DMA((nbuf,))],
    )
    def gather_rows(table_hbm, idx_hbm, out_hbm, idx_v, rows_v, gat_sem, store_sem):
        wid = _sc_worker_id()
        base = wid * r_per_w
        pltpu.sync_copy(idx_hbm.at[wid], idx_v)

        def gather(c, b):
            return pltpu.make_async_copy(table_hbm.at[idx_v.at[c]], rows_v.at[b], gat_sem.at[b])

        def store(c, b):
            return pltpu.make_async_copy(rows_v.at[b], out_hbm.at[pl.ds(base + c * chunk, chunk)], store_sem.at[b])

        for b in range(nbuf - 1):
            gather(b, b).start()

        @pl.loop(0, n_chunks, step=nbuf)
        def _(c0):
            for b in range(nbuf):
                c = c0 + b
                gather(c, b).wait()
                store(c, b).start()
                nb = (b + nbuf - 1) % nbuf

                @pl.when(c + nbuf - 1 < n_chunks)
                def _():
                    @pl.when(c >= 1)
                    def _():
                        store(c - 1, nb).wait()
                    gather(c + nbuf - 1, nb).start()

        for b in range(nbuf):
            store(n_chunks - nbuf + b, b).wait()

    return gather_rows(table, idx.reshape(SC_WORKERS, n_chunks, chunk))


def _experts_kernel(tile_ref, exp_ref, lo_ref, hi_ref, slot_ref, next_ref, nv_ref, xs_ref, wg_hbm, wu_hbm, wd_hbm,
                    ys_ref, wgf, wuf, wdf, wsem, wgb, wub, wdb, hid_ref, ybuf_ref, *, layer):
    it = pl.program_id(0)
    nv = nv_ref[0]
    cur = jnp.minimum(it, nv - 1)
    prev = jnp.maximum(it - 1, 0)

    def weight_copies(e):
        return [pltpu.make_async_copy(src.at[layer, e], dst, wsem.at[k])
                for k, (src, dst) in enumerate(((wg_hbm, wgf), (wu_hbm, wuf), (wd_hbm, wdf)))]

    @pl.when(it == 0)
    def _():
        hid_ref[...] = jnp.zeros_like(hid_ref)
        wdb[...] = jnp.zeros_like(wdb)
        ybuf_ref[...] = jnp.zeros_like(ybuf_ref)
        for cp in weight_copies(exp_ref[0]):
            cp.start()

    @pl.when((it < nv) & ((it == 0) | (exp_ref[cur] != exp_ref[prev])))
    def _():
        for cp in weight_copies(exp_ref[cur]):
            cp.wait()
        slot = slot_ref[cur]
        rows_g = wgf.shape[0] // EXPERT_CAST_STEPS
        rows_d = wdf.shape[0] // EXPERT_CAST_STEPS

        def cast_rows(i, carry):
            rg = pl.ds(pl.multiple_of(i * rows_g, rows_g), rows_g)
            rd = pl.ds(pl.multiple_of(i * rows_d, rows_d), rows_d)
            wgb[rg, :] = wgf[rg, :].astype(BF16)
            wub[rg, :] = wuf[rg, :].astype(BF16)
            wdb[slot, rd, :] = wdf[rd, :].astype(BF16)
            return carry

        lax.fori_loop(0, EXPERT_CAST_STEPS, cast_rows, 0)

        @pl.when(next_ref[cur] >= 0)
        def _():
            for cp in weight_copies(next_ref[cur]):
                cp.start()

    tm, half = xs_ref.shape
    y = _pack_halves(jnp.dot(hid_ref[(it + 1) % 2], wdb[slot_ref[prev]], preferred_element_type=F32))
    a, b = _unpack_halves(xs_ref[...])
    x = jnp.concatenate([a.astype(BF16), b.astype(BF16)], axis=1)
    gate = jnp.dot(x, wgb[...], preferred_element_type=F32)
    up = jnp.dot(x, wub[...], preferred_element_type=F32)
    hid_ref[it % 2] = (gate * _sigmoid(gate) * up).astype(BF16)

    done = (it >= 1) & (it <= nv)
    lo = jnp.where(done, lo_ref[prev], 0)
    hi = jnp.where(done, hi_ref[prev], 0)
    row = lax.broadcasted_iota(I32, (tm, half), 0)
    merged = jnp.where((row >= lo) & (row < hi), y, ybuf_ref[...])
    ybuf_ref[...] = merged
    ys_ref[...] = merged


def _experts(items, xs, w_gate, w_up, w_down, layer, tm):
    n_rows, dp = xs.shape
    _, ne, d, ff = w_gate.shape
    n_items = items[0].shape[0]

    def cur_map(i, tl, ex, lo, hi, sl, nx, nv):
        return (tl[jnp.minimum(i, nv[0] - 1)], 0)

    def prev_map(i, tl, ex, lo, hi, sl, nx, nv):
        return (tl[jnp.clip(i - 1, 0, nv[0] - 1)], 0)

    hbm = pl.BlockSpec(memory_space=pl.ANY)
    return pl.pallas_call(
        functools.partial(_experts_kernel, layer=layer),
        out_shape=jax.ShapeDtypeStruct((n_rows, dp), I32),
        grid_spec=pltpu.PrefetchScalarGridSpec(
            num_scalar_prefetch=7,
            grid=(n_items + 1,),
            in_specs=[pl.BlockSpec((tm, dp), cur_map), hbm, hbm, hbm],
            out_specs=pl.BlockSpec((tm, dp), prev_map),
            scratch_shapes=[pltpu.VMEM((d, ff), F32), pltpu.VMEM((d, ff), F32), pltpu.VMEM((ff, d), F32),
                            pltpu.SemaphoreType.DMA((3,)),
                            pltpu.VMEM((d, ff), BF16), pltpu.VMEM((d, ff), BF16), pltpu.VMEM((2, ff, d), BF16),
                            pltpu.VMEM((2, tm, ff), BF16), pltpu.VMEM((tm, dp), I32)]),
        compiler_params=_params("arbitrary"),
        name="experts",
    )(*items, xs, w_gate, w_up, w_down)


def _combine_kernel(x_ref, wk_ref, yk_ref, wsg_ref, wsu_ref, wsd_ref, g_ref, b_ref, x2_ref, *, alpha):
    x = x_ref[...]
    xb = x.astype(BF16)
    gate = jnp.dot(xb, wsg_ref[...], preferred_element_type=F32)
    up = jnp.dot(xb, wsu_ref[...], preferred_element_type=F32)
    hid = (gate * _sigmoid(gate) * up).astype(BF16)
    shared = jnp.dot(hid, wsd_ref[...], preferred_element_type=F32)

    tc, half = yk_ref.shape[1:]
    wk = wk_ref[...]
    acc_hi = jnp.zeros((tc, half), F32)
    acc_lo = jnp.zeros((tc, half), F32)
    for j in range(TOP_K):
        hi, lo = _unpack_halves(yk_ref[j])
        wj = wk[:, j:j + 1]
        acc_hi += wj * hi
        acc_lo += wj * lo
    ffn = jnp.concatenate([acc_hi, acc_lo], axis=1) + shared
    y = _layer_norm_rows(alpha * x + ffn, g_ref[...], b_ref[...])
    x2_ref[...] = y


def _combine(x1, wk, yk, ws_gate, ws_up, ws_down, ln_g, ln_b, alpha):
    t, d = x1.shape
    dp = yk.shape[2]
    ff = ws_gate.shape[1]
    tc = min(t, 256)
    const = lambda i: (0, 0)
    rows = lambda i: (i, 0)
    return pl.pallas_call(
        functools.partial(_combine_kernel, alpha=alpha),
        out_shape=jax.ShapeDtypeStruct((t, d), F32),
        grid=(t // tc,),
        in_specs=[pl.BlockSpec((tc, d), rows),
                  pl.BlockSpec((tc, TOP_K), rows),
                  pl.BlockSpec((TOP_K, tc, dp), lambda i: (0, i, 0)),
                  pl.BlockSpec((d, ff), const),
                  pl.BlockSpec((d, ff), const),
                  pl.BlockSpec((ff, d), const),
                  pl.BlockSpec((1, d), const),
                  pl.BlockSpec((1, d), const)],
        out_specs=pl.BlockSpec((tc, d), rows),
        compiler_params=_params("parallel", fuse_inputs=[False] * 3 + [True] * 3 + [False] * 2),
        name="combine",
    )(x1, wk, yk, ws_gate, ws_up, ws_down, ln_g, ln_b)


def _expert_work_items(counts, n_rows, tm):
    ne = counts.shape[0]
    end = jnp.cumsum(counts)
    start = end - counts
    first_tile = start // tm
    n_it = jnp.where(counts > 0, (end - 1) // tm - first_tile + 1, 0)
    it_end = jnp.cumsum(n_it)
    it_start = it_end - n_it
    n_items = n_rows // tm + ne
    k = jnp.arange(n_items, dtype=I32)
    expert = jnp.minimum(jnp.sum((it_end[None, :] <= k[:, None]).astype(I32), axis=1), ne - 1)
    onehot = expert[:, None] == jnp.arange(ne, dtype=I32)[None, :]
    pick = lambda v: jnp.sum(jnp.where(onehot, v[None, :], 0), axis=1)
    tile = jnp.clip(pick(first_tile) + k - pick(it_start), 0, n_rows // tm - 1)
    lo = jnp.maximum(pick(start) - tile * tm, 0)
    hi = jnp.minimum(pick(end) - tile * tm, tm)
    changed = jnp.concatenate([jnp.zeros((1,), I32), (expert[1:] != expert[:-1]).astype(I32)])
    slot = jnp.cumsum(changed) % 2
    later = (k[None, :] < it_end[-1]) & (expert[None, :] > expert[:, None])
    nxt = jnp.min(jnp.where(later, expert[None, :], ne), axis=1)
    nxt = jnp.where(nxt < ne, nxt, -1)
    return start.astype(I32), (tile.astype(I32), expert.astype(I32), lo.astype(I32), hi.astype(I32),
                               slot.astype(I32), nxt.astype(I32), it_end[-1:].astype(I32))


def _positions_kernel(start_ref, eid_ref, rank_ref, pos_ref):
    eid = eid_ref[...]
    pos = rank_ref[...]
    for e in range(N_EXPERTS):
        pos = pos + jnp.where(eid == e, start_ref[e], 0)
    pos_ref[...] = pos


def _positions(expert_start, eid, rank):
    full = pl.BlockSpec(eid.shape, lambda i, st: (0, 0))
    return pl.pallas_call(
        _positions_kernel,
        out_shape=jax.ShapeDtypeStruct(eid.shape, I32),
        grid_spec=pltpu.PrefetchScalarGridSpec(num_scalar_prefetch=1, grid=(1,), in_specs=[full, full],
                                               out_specs=full),
        compiler_params=_params("arbitrary"),
        name="positions",
    )(expert_start, eid, rank)


def _moe(x1, x1p, w_router_t, bias_col, w_gate, w_up, w_down, layer, ws_gate, ws_up, ws_down, ln_g, ln_b, alpha):
    t = x1.shape[0]
    tm = EXPERT_ROW_TILE
    rank, eid, wgt, cnt = _router(x1, w_router_t, bias_col)
    expert_start, items = _expert_work_items(cnt[:, 0], t * TOP_K, tm)
    pos = _positions(expert_start, eid, rank)
    xs = _dispatch(pos, x1p)
    ys = _experts(items, xs, w_gate, w_up, w_down, layer, tm)
    yk = _gather_rows(ys, pos.reshape(-1)).reshape(TOP_K, t, -1)
    return _combine(x1, wgt.T, yk, ws_gate, ws_up, ws_down, ln_g, ln_b, alpha)


def kernel(x, positions, w_in, lb_logits, hg_norm_g, w_hg_proj, w_att_proj, w_out, ln1_g, ln1_b,
           w_router, router_bias, w_e_gate, w_e_up, w_e_down, w_s_gate, w_s_up, w_s_down, ln2_g, ln2_b):
    batch, seq, d = x.shape
    depth = w_in.shape[0]
    t = batch * seq
    alpha = float((2 * depth) ** 0.25)
    hg_width = HG_HEADS * HG_DIM
    att_col0 = 4 * hg_width
    gate_col0 = att_col0 + 3 * len(ATT_GROUPS) * ATT_HEADS * ATT_DIM

    p = jax.nn.softmax(lb_logits.astype(F32), axis=0)
    cs = jnp.cumsum(p, axis=0)
    lower = cs - cs[:1]
    log1m_lower = jnp.log1p(-lower)

    cos, sin = _rope_tables(positions)
    xf = x.reshape(t, d)
    in_width = w_in.shape[2]
    tn = 1280 if in_width % 1280 == 0 else 512
    for layer in range(depth):
        h = _matmul(xf, w_in, layer, min(t, 1024), tn, BF16)
        o_h = _hgrn(h, lower[layer].reshape(1, -1), log1m_lower[layer].reshape(1, -1),
                    hg_norm_g[layer].reshape(1, -1), batch, seq)
        o_a = _attention(h, cos, sin, batch, seq, att_col0)
        x1, x1p = _mix_out(xf, h, o_h, o_a, w_hg_proj[layer].astype(BF16), w_att_proj[layer].astype(BF16),
                           w_out[layer].astype(BF16), ln1_g[layer].reshape(1, d), ln1_b[layer].reshape(1, d),
                           gate_col0, alpha)
        xf = _moe(x1, x1p, w_router[layer].T, router_bias[layer].reshape(-1, 1),
                  w_e_gate, w_e_up, w_e_down, layer,
                  w_s_gate[layer].astype(BF16), w_s_up[layer].astype(BF16), w_s_down[layer].astype(BF16),
                  ln2_g[layer].reshape(1, d), ln2_b[layer].reshape(1, d), alpha)
    return xf.reshape(batch, seq, d)
```
